```python
import jax, jax.numpy as jnp
from jax import lax
import numpy as np

D_MODEL = 2048
BATCH = 16
SEQ = 256
DEPTH = 1
DEC_BATCH = 2
DEC_SEQ = 1024
PAST_LEN = 256

GRID_W = 64
HEAD_DIM = 128
A_HEADS = 8
A_KV_HEADS = 2
A_GROUP = A_HEADS // A_KV_HEADS
A_WINDOW = 128
BLOCK = 128
B_HEADS = 8
NA_ROWS_MAX = 8
NA_COLS = 16
D_FF = 4 * D_MODEL
ROPE_THETA = 10000.0
EPS = 1e-6
NEG = -1e30
Q_BLOCK = 128
A_Q = A_HEADS * HEAD_DIM
A_KV = A_KV_HEADS * HEAD_DIM
B_QKV = B_HEADS * HEAD_DIM
SPLIT_POINTS = (A_Q, A_Q + A_KV, A_Q + 2 * A_KV, A_Q + 2 * A_KV + B_QKV,
                A_Q + 2 * A_KV + 2 * B_QKV, A_Q + 2 * A_KV + 3 * B_QKV,
                A_Q + 2 * A_KV + 3 * B_QKV + D_MODEL)
IN_WIDTH = A_Q + 2 * A_KV + 3 * B_QKV + 2 * D_MODEL
SCALE = HEAD_DIM ** -0.5

kernel_name = "hybrid_dit_window_gqa_natten_prefix_step"


def rms_norm(x, w):
    xf = x.astype(jnp.float32)
    y = xf * lax.rsqrt(jnp.mean(xf * xf, axis=-1, keepdims=True) + EPS)
    return (y * w.astype(jnp.float32)).astype(x.dtype)


def ada_mod(cvec, w_ada, b_ada):
    m = jax.nn.silu(cvec) @ w_ada + b_ada
    return jnp.split(m, 6, axis=-1)


def softmax_sink(logits, sink_col):
    if sink_col is None:
        return jax.nn.softmax(logits, axis=-1)
    full = jnp.concatenate([logits, jnp.broadcast_to(sink_col, logits.shape[:-1] + (1,)).astype(jnp.float32)], axis=-1)
    return jax.nn.softmax(full, axis=-1)[..., :-1]


def axial_rope(x, T):
    n_freq = HEAD_DIM // 4
    pos = jnp.arange(T)
    row = (pos // GRID_W).astype(jnp.float32)
    col = (pos % GRID_W).astype(jnp.float32)
    inv = ROPE_THETA ** (-jnp.arange(n_freq, dtype=jnp.float32) / n_freq)
    shape = (T,) + (1,) * (x.ndim - 3) + (n_freq,)
    xf = x.astype(jnp.float32)
    xr, xc = jnp.split(xf, 2, axis=-1)

    def rot(xh, ang):
        cos = jnp.cos(ang).reshape(shape)
        sin = jnp.sin(ang).reshape(shape)
        x1, x2 = jnp.split(xh, 2, axis=-1)
        return jnp.concatenate([x1 * cos - x2 * sin, x1 * sin + x2 * cos], axis=-1)

    out = jnp.concatenate([rot(xr, row[:, None] * inv), rot(xc, col[:, None] * inv)], axis=-1)
    return out.astype(x.dtype)


def project_heads(h, w_in, q_norm_a, k_norm_a, q_norm_b, k_norm_b):
    B, T, _ = h.shape
    p = h @ w_in
    qa, ka, va, qb, kb, vb, ga, gb = jnp.split(p, SPLIT_POINTS, axis=-1)
    qa = rms_norm(qa.reshape(B, T, A_KV_HEADS, A_GROUP, HEAD_DIM), q_norm_a)
    ka = rms_norm(ka.reshape(B, T, A_KV_HEADS, HEAD_DIM), k_norm_a)
    va = va.reshape(B, T, A_KV_HEADS, HEAD_DIM)
    qb = rms_norm(qb.reshape(B, T, B_HEADS, HEAD_DIM), q_norm_b)
    kb = rms_norm(kb.reshape(B, T, B_HEADS, HEAD_DIM), k_norm_b)
    vb = vb.reshape(B, T, B_HEADS, HEAD_DIM)
    return qa, ka, va, qb, kb, vb, ga, gb


def merge_branches(oa, ob, ga, gb, w_br_a, w_br_b, w_out):
    B, T = oa.shape[:2]
    ya = oa.reshape(B, T, A_Q) @ w_br_a
    yb = ob.reshape(B, T, B_QKV) @ w_br_b
    return (jax.nn.sigmoid(ga) * ya + jax.nn.sigmoid(gb) * yb) @ w_out


def sq_relu_mlp(h, w_up, w_down):
    return jnp.square(jax.nn.relu(h @ w_up)) @ w_down


def context_attention(q, k, v, sink):
    B, S, Hk, G, D = q.shape
    nb = S // Q_BLOCK
    qb = jnp.moveaxis(q.reshape(B, nb, Q_BLOCK, Hk, G, D), 1, 0)
    sink_col = None if sink is None else sink[None, :, :, None, None]

    def one_block(qblk):
        lg = jnp.einsum("bqhgd,bkhd->bhgqk", qblk, k).astype(jnp.float32) * SCALE
        pr = softmax_sink(lg, sink_col)
        return jnp.einsum("bhgqk,bkhd->bqhgd", pr.astype(v.dtype), v)

    out = lax.map(one_block, qb)
    return jnp.moveaxis(out, 0, 1).reshape(B, S, Hk, G, D)


def latent_window_attention(q, k, v, kc, vc, sink):
    B, T, Hk, G, D = q.shape
    nb = T // BLOCK
    qb = q.reshape(B, nb, BLOCK, Hk, G, D)

    def band(t):
        tp = jnp.pad(t, ((0, 0), (BLOCK, BLOCK), (0, 0), (0, 0)))
        tb = tp.reshape(B, nb + 2, BLOCK, Hk, D)
        return jnp.concatenate([tb[:, :-2], tb[:, 1:-1], tb[:, 2:]], axis=2)

    kband, vband = band(k), band(v)
    qpos = jnp.arange(nb)[:, None] * BLOCK + jnp.arange(BLOCK)[None, :]
    kpos = (jnp.arange(nb)[:, None] - 1) * BLOCK + jnp.arange(3 * BLOCK)[None, :]
    valid = ((kpos[:, None, :] >= 0) & (kpos[:, None, :] < T)
             & (jnp.abs(qpos[:, :, None] - kpos[:, None, :]) <= A_WINDOW))
    lg_loc = jnp.einsum("bnqhgd,bnkhd->bnhgqk", qb, kband).astype(jnp.float32) * SCALE
    lg_loc = jnp.where(valid[None, :, None, None], lg_loc, NEG)
    lg_ctx = jnp.einsum("bnqhgd,bchd->bnhgqc", qb, kc).astype(jnp.float32) * SCALE
    pr = softmax_sink(jnp.concatenate([lg_loc, lg_ctx], axis=-1), sink[None, None, :, :, None, None])
    p_loc = pr[..., :3 * BLOCK].astype(v.dtype)
    p_ctx = pr[..., 3 * BLOCK:].astype(v.dtype)
    out = (jnp.einsum("bnhgqk,bnkhd->bnqhgd", p_loc, vband)
           + jnp.einsum("bnhgqc,bchd->bnqhgd", p_ctx, vc))
    return out.reshape(B, T, Hk, G, D)


def latent_neighbourhood_attention(q, k, v, kc, vc, rpb):
    B, T, H, D = q.shape
    rows = T // GRID_W
    kr = min(NA_ROWS_MAX, rows)
    r = jnp.arange(rows)
    col = jnp.arange(GRID_W)
    rstart = jnp.clip(r - kr // 2, 0, rows - kr)
    row_idx = rstart[:, None] + jnp.arange(kr)[None, :]
    cstart = jnp.clip(col - NA_COLS // 2, 0, GRID_W - NA_COLS)
    K = kr * GRID_W
    qg = q.reshape(B, rows, GRID_W, H, D)
    kg = k.reshape(B, rows, GRID_W, H, D)[:, row_idx].reshape(B, rows, K, H, D)
    vg = v.reshape(B, rows, GRID_W, H, D)[:, row_idx].reshape(B, rows, K, H, D)
    key_row = jnp.broadcast_to(row_idx[:, :, None], (rows, kr, GRID_W)).reshape(rows, K)
    key_col = jnp.tile(col, kr)
    valid = (key_col[None, :] >= cstart[:, None]) & (key_col[None, :] < cstart[:, None] + NA_COLS)
    dr = key_row[:, None, :] - r[:, None, None] + NA_ROWS_MAX - 1
    dc = jnp.clip(key_col[None, :] - col[:, None] + NA_COLS - 1, 0, 2 * NA_COLS - 2)
    bias = rpb[:, dr, dc].astype(jnp.float32)
    lg_loc = jnp.einsum("brwhd,brkhd->bhrwk", qg, kg).astype(jnp.float32) * SCALE + bias[None]
    lg_loc = jnp.where(valid[None, None, None], lg_loc, NEG)
    lg_ctx = jnp.einsum("brwhd,bchd->bhrwc", qg, kc).astype(jnp.float32) * SCALE
    pr = jax.nn.softmax(jnp.concatenate([lg_loc, lg_ctx], axis=-1), axis=-1)
    p_loc = pr[..., :K].astype(v.dtype)
    p_ctx = pr[..., K:].astype(v.dtype)
    out = (jnp.einsum("bhrwk,brkhd->brwhd", p_loc, vg)
           + jnp.einsum("bhrwc,bchd->brwhd", p_ctx, vc))
    return out.reshape(B, T, H, D)


def setup_inputs(seed: int = 0) -> dict:
    key = jax.random.key(seed)
    ks = jax.random.split(key, 24)
    f32 = jnp.float32

    def nrm(k, shape, scale):
        return jax.random.normal(k, shape, f32) * scale

    return {
        "x_prompt": nrm(ks[0], (BATCH, SEQ, D_MODEL), 1.0),
        "x_sample": nrm(ks[1], (DEC_BATCH, DEC_SEQ, D_MODEL), 1.0),
        "cache_a_k": nrm(ks[2], (DEC_BATCH, DEPTH, PAST_LEN, A_KV_HEADS, HEAD_DIM), 1.0),
        "cache_a_v": nrm(ks[3], (DEC_BATCH, DEPTH, PAST_LEN, A_KV_HEADS, HEAD_DIM), 1.0),
        "cache_b_k": nrm(ks[4], (DEC_BATCH, DEPTH, PAST_LEN, B_HEADS, HEAD_DIM), 1.0),
        "cache_b_v": nrm(ks[5], (DEC_BATCH, DEPTH, PAST_LEN, B_HEADS, HEAD_DIM), 1.0),
        "c": nrm(ks[6], (DEC_BATCH, D_MODEL), 1.0),
        "c_ctx": nrm(ks[7], (D_MODEL,), 1.0),
        "norm1_w": 1.0 + nrm(ks[8], (DEPTH, D_MODEL), 0.01),
        "norm2_w": 1.0 + nrm(ks[9], (DEPTH, D_MODEL), 0.01),
        "w_ada": nrm(ks[10], (DEPTH, D_MODEL, 6 * D_MODEL), D_MODEL ** -0.5),
        "b_ada": nrm(ks[11], (DEPTH, 6 * D_MODEL), 0.01),
        "w_in": nrm(ks[12], (DEPTH, D_MODEL, IN_WIDTH), D_MODEL ** -0.5),
        "q_norm_a": 1.0 + nrm(ks[13], (DEPTH, HEAD_DIM), 0.01),
        "k_norm_a": 1.0 + nrm(ks[14], (DEPTH, HEAD_DIM), 0.01),
        "q_norm_b": 1.0 + nrm(ks[15], (DEPTH, HEAD_DIM), 0.01),
        "k_norm_b": 1.0 + nrm(ks[16], (DEPTH, HEAD_DIM), 0.01),
        "sink_a": nrm(ks[17], (DEPTH, A_HEADS), 0.5),
        "rpb_b": nrm(ks[18], (DEPTH, B_HEADS, 2 * NA_ROWS_MAX - 1, 2 * NA_COLS - 1), 0.1),
        "w_br_a": nrm(ks[19], (DEPTH, A_Q, D_MODEL), A_Q ** -0.5),
        "w_br_b": nrm(ks[20], (DEPTH, B_QKV, D_MODEL), B_QKV ** -0.5),
        "w_out": nrm(ks[21], (DEPTH, D_MODEL, D_MODEL), D_MODEL ** -0.5),
        "w_up": nrm(ks[22], (DEPTH, D_MODEL, D_FF), D_MODEL ** -0.5),
        "w_down": nrm(ks[23], (DEPTH, D_FF, D_MODEL), D_FF ** -0.5),
    }


def reference(x_prompt, x_sample, cache_a_k, cache_a_v, cache_b_k, cache_b_v, c, c_ctx,
              norm1_w, norm2_w, w_ada, b_ada, w_in, q_norm_a, k_norm_a, q_norm_b, k_norm_b,
              sink_a, rpb_b, w_br_a, w_br_b, w_out, w_up, w_down):
    xp = x_prompt
    new_ak, new_av, new_bk, new_bv = [], [], [], []
    for l in range(DEPTH):
        sh1, sc1, g1, sh2, sc2, g2 = ada_mod(c_ctx[None, None, :], w_ada[l], b_ada[l])
        h = rms_norm(xp, norm1_w[l]) * (1 + sc1) + sh1
        qa, ka, va, qb, kb, vb, ga, gb = project_heads(h, w_in[l], q_norm_a[l], k_norm_a[l],
                                                        q_norm_b[l], k_norm_b[l])
        sink = sink_a[l].reshape(A_KV_HEADS, A_GROUP)
        oa = context_attention(qa, ka, va, sink)
        ob = context_attention(qb[:, :, :, None, :], kb, vb, None)
        xp = xp + g1 * merge_branches(oa, ob, ga, gb, w_br_a[l], w_br_b[l], w_out[l])
        h2 = rms_norm(xp, norm2_w[l]) * (1 + sc2) + sh2
        xp = xp + g2 * sq_relu_mlp(h2, w_up[l], w_down[l])
        new_ak.append(ka)
        new_av.append(va)
        new_bk.append(kb)
        new_bv.append(vb)
    y_prompt = xp
    new_a_k = jnp.stack(new_ak, axis=1)
    new_a_v = jnp.stack(new_av, axis=1)
    new_b_k = jnp.stack(new_bk, axis=1)
    new_b_v = jnp.stack(new_bv, axis=1)

    xs = x_sample
    T = xs.shape[1]
    for l in range(DEPTH):
        sh1, sc1, g1, sh2, sc2, g2 = ada_mod(c[:, None, :], w_ada[l], b_ada[l])
        h = rms_norm(xs, norm1_w[l]) * (1 + sc1) + sh1
        qa, ka, va, qb, kb, vb, ga, gb = project_heads(h, w_in[l], q_norm_a[l], k_norm_a[l],
                                                        q_norm_b[l], k_norm_b[l])
        qa = axial_rope(qa, T)
        ka = axial_rope(ka, T)
        sink = sink_a[l].reshape(A_KV_HEADS, A_GROUP)
        oa = latent_window_attention(qa, ka, va, cache_a_k[:, l], cache_a_v[:, l], sink)
        ob = latent_neighbourhood_attention(qb, kb, vb, cache_b_k[:, l], cache_b_v[:, l], rpb_b[l])
        xs = xs + g1 * merge_branches(oa, ob, ga, gb, w_br_a[l], w_br_b[l], w_out[l])
        h2 = rms_norm(xs, norm2_w[l]) * (1 + sc2) + sh2
        xs = xs + g2 * sq_relu_mlp(h2, w_up[l], w_down[l])
    y_sample = xs
    return (y_prompt, y_sample, new_a_k, new_a_v, new_b_k, new_b_v)
```

```python
import functools

import numpy as np
import jax
import jax.numpy as jnp
from jax import lax
from jax.experimental import pallas as pl
from jax.experimental.pallas import tpu as pltpu

D_MODEL = 2048
BATCH = 16
SEQ = 256
DEC_BATCH = 2
DEC_SEQ = 1024
PAST_LEN = 256
GRID_W = 64
HEAD_DIM = 128
A_HEADS = 8
A_KV_HEADS = 2
A_GROUP = A_HEADS // A_KV_HEADS
A_WINDOW = 128
BLOCK = 128
B_HEADS = 8
NA_ROWS_MAX = 8
NA_COLS = 16
D_FF = 4 * D_MODEL
ROPE_THETA = 10000.0
EPS = 1e-6
NEG = -1e30
A_Q = A_HEADS * HEAD_DIM
A_KV = A_KV_HEADS * HEAD_DIM
B_QKV = B_HEADS * HEAD_DIM
IN_WIDTH = A_Q + 2 * A_KV + 3 * B_QKV + 2 * D_MODEL
SCALE = HEAD_DIM ** -0.5

M_PROMPT = BATCH * SEQ
M_SAMPLE = DEC_BATCH * DEC_SEQ
M_ALL = M_PROMPT + M_SAMPLE
GRID_ROWS = DEC_SEQ // GRID_W
NA_ROWS = min(NA_ROWS_MAX, GRID_ROWS)

COL_QA = 0
COL_KVA = A_Q
COL_QB = A_Q + 2 * A_KV
COL_KB = COL_QB + B_QKV
COL_VB = COL_KB + B_QKV
COL_G = COL_VB + B_QKV

TM = 1024
TN = 512
TM_NORM = 512
TF = 512
VMEM_LIMIT = 56 * 1024 * 1024

F32 = jnp.float32
BF16 = jnp.bfloat16


def _mod_row(i, tm):
    p_tiles = M_PROMPT // tm
    return jnp.where(i < p_tiles, 0, 1 + (i - p_tiles) // (DEC_SEQ // tm))


def _dot(a, b):
    return jnp.dot(a, b, preferred_element_type=F32)


def _dot_nt(a, b):
    return lax.dot_general(a, b, (((1,), (1,)), ((), ())), preferred_element_type=F32)


def _params(sem, vmem=None):
    return pltpu.CompilerParams(dimension_semantics=sem, vmem_limit_bytes=vmem)


def _ada_kernel(c_ref, w_ref, b_ref, o_ref):
    cv = c_ref[...]
    s = (cv * jax.nn.sigmoid(cv)).astype(BF16)
    o_ref[...] = _dot(s, w_ref[...].astype(BF16)) + b_ref[...]


def _ada_call(cvecs, w_ada, b_ada):
    tn = 1024
    n = 6 * D_MODEL
    return pl.pallas_call(
        _ada_kernel,
        grid=(n // tn,),
        in_specs=[pl.BlockSpec((8, D_MODEL), lambda j: (0, 0)),
                  pl.BlockSpec((D_MODEL, tn), lambda j: (0, j)),
                  pl.BlockSpec((1, tn), lambda j: (0, j))],
        out_specs=pl.BlockSpec((8, tn), lambda j: (0, j)),
        out_shape=jax.ShapeDtypeStruct((8, n), F32),
        compiler_params=_params(("arbitrary",), VMEM_LIMIT),
        name="ada_mod",
    )(cvecs, w_ada, b_ada)


def _modnorm(x, nw, sc, sh):
    y = x * lax.rsqrt(jnp.mean(x * x, axis=-1, keepdims=True) + EPS)
    return (y * nw) * (1.0 + sc) + sh


def _norm_kernel(xp_ref, xs_ref, nw_ref, sh_ref, sc_ref, h_ref):
    i = pl.program_id(0)
    p_tiles = M_PROMPT // TM_NORM

    @pl.when(i < p_tiles)
    def _():
        h_ref[...] = _modnorm(xp_ref[...], nw_ref[...], sc_ref[...], sh_ref[...]).astype(BF16)

    @pl.when(i >= p_tiles)
    def _():
        h_ref[...] = _modnorm(xs_ref[...], nw_ref[...], sc_ref[...], sh_ref[...]).astype(BF16)


def _norm_call(xp, xs, nw, mods3):
    tm = TM_NORM
    p_tiles = M_PROMPT // tm
    return pl.pallas_call(
        _norm_kernel,
        grid=(M_ALL // tm,),
        in_specs=[pl.BlockSpec((tm, D_MODEL), lambda i: (jnp.minimum(i, p_tiles - 1), 0)),
                  pl.BlockSpec((tm, D_MODEL), lambda i: (jnp.maximum(i - p_tiles, 0), 0)),
                  pl.BlockSpec((1, D_MODEL), lambda i: (0, 0)),
                  pl.BlockSpec((None, 1, D_MODEL), lambda i: (_mod_row(i, tm), 0, 0)),
                  pl.BlockSpec((None, 1, D_MODEL), lambda i: (_mod_row(i, tm), 0, 1))],
        out_specs=pl.BlockSpec((tm, D_MODEL), lambda i: (i, 0)),
        out_shape=jax.ShapeDtypeStruct((M_ALL, D_MODEL), BF16),
        compiler_params=_params(("arbitrary",), VMEM_LIMIT),
        name="norm1",
    )(xp, xs, nw, mods3, mods3)


def _head_norm(x, nw):
    return x * lax.rsqrt(jnp.mean(x * x, axis=-1, keepdims=True) + EPS) * nw


def _rope(x, cos, sin_signed):
    lane = lax.broadcasted_iota(jnp.int32, x.shape, 1)
    partner = jnp.where((lane % 64) < 32, pltpu.roll(x, 96, 1), pltpu.roll(x, 32, 1))
    return x * cos + partner * sin_signed


def _rope_tables():
    n_freq = HEAD_DIM // 4
    pos = np.arange(DEC_SEQ)
    row = (pos // GRID_W).astype(np.float64)
    col = (pos % GRID_W).astype(np.float64)
    inv = ROPE_THETA ** (-np.arange(n_freq, dtype=np.float64) / n_freq)
    ar = row[:, None] * inv
    ac = col[:, None] * inv
    cos = np.concatenate([np.cos(ar), np.cos(ar), np.cos(ac), np.cos(ac)], axis=-1)
    sin = np.concatenate([-np.sin(ar), np.sin(ar), -np.sin(ac), np.sin(ac)], axis=-1)
    return jnp.asarray(cos, F32), jnp.asarray(sin, F32)


def _proj_kernel(*refs, kind):
    h_ref, w_ref = refs[0], refs[1]
    wbf = refs[-1]
    i = pl.program_id(1)
    p_tiles = M_PROMPT // TM
    is_prompt = i < p_tiles

    @pl.when(i == 0)
    def _():
        wbf[...] = w_ref[...].astype(BF16)

    acc = _dot(h_ref[...], wbf[...])
    heads = [acc[:, a:a + HEAD_DIM] for a in range(0, TN, HEAD_DIM)]

    if kind == "qa":
        nw_ref, cos_ref, sin_ref, q_ref = refs[2:6]
        normed = [_head_norm(x, nw_ref[...]) for x in heads]

        @pl.when(is_prompt)
        def _():
            for k, y in enumerate(normed):
                q_ref[:, k * HEAD_DIM:(k + 1) * HEAD_DIM] = y.astype(BF16)

        @pl.when(jnp.logical_not(is_prompt))
        def _():
            for k, y in enumerate(normed):
                q_ref[:, k * HEAD_DIM:(k + 1) * HEAD_DIM] = _rope(y, cos_ref[...], sin_ref[...]).astype(BF16)

    elif kind == "kva":
        nw_ref, cos_ref, sin_ref, kv_ref, kp_ref, vp_ref = refs[2:8]
        ks = [_head_norm(x, nw_ref[...]) for x in heads[:A_KV_HEADS]]
        vs = heads[A_KV_HEADS:]
        for k, v in enumerate(vs):
            kv_ref[:, A_KV + k * HEAD_DIM:A_KV + (k + 1) * HEAD_DIM] = v.astype(BF16)

        @pl.when(is_prompt)
        def _():
            for k, (y, v) in enumerate(zip(ks, vs)):
                sl = slice(k * HEAD_DIM, (k + 1) * HEAD_DIM)
                kv_ref[:, sl] = y.astype(BF16)
                kp_ref[:, sl] = y
                vp_ref[:, sl] = v

        @pl.when(jnp.logical_not(is_prompt))
        def _():
            for k, y in enumerate(ks):
                sl = slice(k * HEAD_DIM, (k + 1) * HEAD_DIM)
                kv_ref[:, sl] = _rope(y, cos_ref[...], sin_ref[...]).astype(BF16)

    elif kind == "qb":
        nw_ref, q_ref = refs[2:4]
        for k, x in enumerate(heads):
            q_ref[:, k * HEAD_DIM:(k + 1) * HEAD_DIM] = _head_norm(x, nw_ref[...]).astype(BF16)

    elif kind == "kb":
        nw_ref, k_ref, kp_ref = refs[2:5]
        normed = [_head_norm(x, nw_ref[...]) for x in heads]
        for k, y in enumerate(normed):
            k_ref[:, k * HEAD_DIM:(k + 1) * HEAD_DIM] = y.astype(BF16)

        @pl.when(is_prompt)
        def _():
            for k, y in enumerate(normed):
                kp_ref[:, k * HEAD_DIM:(k + 1) * HEAD_DIM] = y

    elif kind == "vb":
        v_ref, vp_ref = refs[2:4]
        v_ref[...] = acc.astype(BF16)

        @pl.when(is_prompt)
        def _():
            vp_ref[...] = acc

    elif kind == "gate":
        g_ref = refs[2]
        g_ref[...] = jax.nn.sigmoid(acc)

    else:
        raise ValueError(kind)


def _proj_call(h, w_in, kind, col0, width, norm_w=None, rope=None):
    assert col0 % TN == 0 and width % TN == 0
    nj = width // TN
    c0 = col0 // TN
    p_tiles = M_PROMPT // TM
    s_tiles = DEC_SEQ // TM

    in_specs = [pl.BlockSpec((TM, D_MODEL), lambda j, i: (i, 0)),
                pl.BlockSpec((D_MODEL, TN), lambda j, i: (0, c0 + j))]
    args = [h, w_in]
    if norm_w is not None:
        in_specs.append(pl.BlockSpec((1, HEAD_DIM), lambda j, i: (0, 0)))
        args.append(norm_w)
    if rope is not None:
        rope_spec = pl.BlockSpec((TM, HEAD_DIM), lambda j, i: (jnp.maximum(i - p_tiles, 0) % s_tiles, 0))
        in_specs += [rope_spec, rope_spec]
        args += list(rope)

    full = pl.BlockSpec((TM, TN), lambda j, i: (i, j))
    parked = lambda w: pl.BlockSpec((TM, w), lambda j, i: (jnp.minimum(i, p_tiles - 1), j))
    if kind in ("qa", "qb"):
        out_specs = full
        out_shape = jax.ShapeDtypeStruct((M_ALL, width), BF16)
    elif kind == "kva":
        out_specs = [full, parked(A_KV), parked(A_KV)]
        out_shape = [jax.ShapeDtypeStruct((M_ALL, width), BF16),
                     jax.ShapeDtypeStruct((M_PROMPT, A_KV), F32),
                     jax.ShapeDtypeStruct((M_PROMPT, A_KV), F32)]
    elif kind in ("kb", "vb"):
        out_specs = [full, parked(TN)]
        out_shape = [jax.ShapeDtypeStruct((M_ALL, width), BF16),
                     jax.ShapeDtypeStruct((M_PROMPT, width), F32)]
    elif kind == "gate":
        out_specs = full
        out_shape = jax.ShapeDtypeStruct((M_ALL, width), F32)
    else:
        raise ValueError(kind)

    return pl.pallas_call(
        functools.partial(_proj_kernel, kind=kind),
        grid=(nj, M_ALL // TM),
        in_specs=in_specs,
        out_specs=out_specs,
        out_shape=out_shape,
        scratch_shapes=[pltpu.VMEM((D_MODEL, TN), BF16)],
        compiler_params=_params(("arbitrary", "arbitrary"), VMEM_LIMIT),
        name="proj_" + kind,
    )(*args)


def _softmax_pv(parts, sink):
    m = functools.reduce(jnp.maximum, [jnp.max(s, axis=-1, keepdims=True) for s, _ in parts])
    if sink is not None:
        m = jnp.maximum(m, sink)
    ps = [jnp.exp(s - m) for s, _ in parts]
    l = functools.reduce(jnp.add, [jnp.sum(p, axis=-1, keepdims=True) for p in ps])
    if sink is not None:
        l = l + jnp.exp(sink - m)
    o = functools.reduce(jnp.add, [_dot(p.astype(BF16), v) for p, (_, v) in zip(ps, parts)])
    return o / l


def _sink_column(sink_ref, hk, rows_per_head):
    n = A_GROUP * rows_per_head
    g = lax.broadcasted_iota(jnp.int32, (n, 1), 0) // rows_per_head
    col = jnp.full((n, 1), sink_ref[hk * A_GROUP], F32)
    for k in range(1, A_GROUP):
        col = jnp.where(g == k, sink_ref[hk * A_GROUP + k], col)
    return col


def _stack_group(q_ref, hk):
    return jnp.concatenate(
        [q_ref[:, (hk * A_GROUP + g) * HEAD_DIM:(hk * A_GROUP + g + 1) * HEAD_DIM] for g in range(A_GROUP)], axis=0)


def _unstack_group(o_ref, hk, o, rows):
    for g in range(A_GROUP):
        c = (hk * A_GROUP + g) * HEAD_DIM
        o_ref[:, c:c + HEAD_DIM] = o[g * rows:(g + 1) * rows].astype(BF16)


def _ctx_attn_kernel(sink_ref, qa_ref, kva_ref, qb_ref, kb_ref, vb_ref, oa_ref, ob_ref):
    for hk in range(A_KV_HEADS):
        k = kva_ref[:, hk * HEAD_DIM:(hk + 1) * HEAD_DIM]
        v = kva_ref[:, A_KV + hk * HEAD_DIM:A_KV + (hk + 1) * HEAD_DIM]
        q4 = _stack_group(qa_ref, hk)
        s = _dot_nt(q4, k) * SCALE
        o = _softmax_pv([(s, v)], _sink_column(sink_ref, hk, SEQ))
        _unstack_group(oa_ref, hk, o, SEQ)
    for h in range(B_HEADS):
        sl = slice(h * HEAD_DIM, (h + 1) * HEAD_DIM)
        s = _dot_nt(qb_ref[:, sl], kb_ref[:, sl]) * SCALE
        ob_ref[:, sl] = _softmax_pv([(s, vb_ref[:, sl])], None).astype(BF16)


def _ctx_attn_call(sink, qa, kva, qb, kb, vb):
    row = lambda w: pl.BlockSpec((SEQ, w), lambda b: (b, 0))
    return pl.pallas_call(
        _ctx_attn_kernel,
        grid=(BATCH,),
        in_specs=[pl.BlockSpec(memory_space=pltpu.SMEM),
                  row(A_Q), row(2 * A_KV), row(B_QKV), row(B_QKV), row(B_QKV)],
        out_specs=[row(A_Q), row(B_QKV)],
        out_shape=[jax.ShapeDtypeStruct((M_PROMPT, A_Q), BF16),
                   jax.ShapeDtypeStruct((M_PROMPT, B_QKV), BF16)],
        compiler_params=_params(("arbitrary",), VMEM_LIMIT),
        name="attn_ctx",
    )(sink, qa, kva, qb, kb, vb)


BAND = 3 * BLOCK


def _win_attn_kernel(sink_ref, q_ref, kv_ref, ck_ref, cv_ref, o_ref):
    n = pl.program_id(1)
    start = pl.multiple_of(jnp.clip((n - 1) * BLOCK, 0, DEC_SEQ - BAND), BLOCK)
    qpos = n * BLOCK + lax.broadcasted_iota(jnp.int32, (BLOCK, BAND), 0)
    kpos = start + lax.broadcasted_iota(jnp.int32, (BLOCK, BAND), 1)
    valid = jnp.abs(qpos - kpos) <= A_WINDOW
    valid = jnp.concatenate([valid.astype(jnp.int32)] * A_GROUP, axis=0) > 0
    for hk in range(A_KV_HEADS):
        sl = slice(hk * HEAD_DIM, (hk + 1) * HEAD_DIM)
        slv = slice(A_KV + hk * HEAD_DIM, A_KV + (hk + 1) * HEAD_DIM)
        k_loc = kv_ref[pl.ds(start, BAND), sl]
        v_loc = kv_ref[pl.ds(start, BAND), slv]
        k_ctx = ck_ref[:, sl].astype(BF16)
        v_ctx = cv_ref[:, sl].astype(BF16)
        q4 = _stack_group(q_ref, hk)
        s_loc = jnp.where(valid, _dot_nt(q4, k_loc) * SCALE, NEG)
        s_ctx = _dot_nt(q4, k_ctx) * SCALE
        o = _softmax_pv([(s_loc, v_loc), (s_ctx, v_ctx)], _sink_column(sink_ref, hk, BLOCK))
        _unstack_group(o_ref, hk, o, BLOCK)


def _win_attn_call(sink, qa, kva, cache_k, cache_v):
    nb = DEC_SEQ // BLOCK
    q0 = M_PROMPT // BLOCK
    b0 = M_PROMPT // DEC_SEQ
    return pl.pallas_call(
        _win_attn_kernel,
        grid=(DEC_BATCH, nb),
        in_specs=[pl.BlockSpec(memory_space=pltpu.SMEM),
                  pl.BlockSpec((BLOCK, A_Q), lambda b, n: (q0 + b * nb + n, 0)),
                  pl.BlockSpec((DEC_SEQ, 2 * A_KV), lambda b, n: (b0 + b, 0)),
                  pl.BlockSpec((None, PAST_LEN, A_KV), lambda b, n: (b, 0, 0)),
                  pl.BlockSpec((None, PAST_LEN, A_KV), lambda b, n: (b, 0, 0))],
        out_specs=pl.BlockSpec((BLOCK, A_Q), lambda b, n: (b * nb + n, 0)),
        out_shape=jax.ShapeDtypeStruct((M_SAMPLE, A_Q), BF16),
        compiler_params=_params(("arbitrary", "arbitrary"), VMEM_LIMIT),
        name="attn_window",
    )(sink, qa, kva, cache_k, cache_v)


NA_KEYS = NA_ROWS * GRID_W
NA_PIECES = NA_KEYS // 128


def _na_row_start(r):
    return jnp.clip(r - NA_ROWS // 2, 0, GRID_ROWS - NA_ROWS)


def _na_attn_kernel(q_ref, k_ref, v_ref, ck_ref, cv_ref, *rest):
    bias_refs, o_ref = rest[:NA_PIECES], rest[NA_PIECES]
    r = pl.program_id(1)
    k0 = pl.multiple_of(_na_row_start(r) * GRID_W, GRID_W)
    qcol = lax.broadcasted_iota(jnp.int32, (GRID_W, NA_KEYS), 0)
    kcol = lax.broadcasted_iota(jnp.int32, (GRID_W, NA_KEYS), 1) % GRID_W
    cstart = jnp.clip(qcol - NA_COLS // 2, 0, GRID_W - NA_COLS)
    valid = (kcol >= cstart) & (kcol < cstart + NA_COLS)
    for h in range(B_HEADS):
        sl = slice(h * HEAD_DIM, (h + 1) * HEAD_DIM)
        q = q_ref[:, sl]
        k_loc = k_ref[pl.ds(k0, NA_KEYS), sl]
        v_loc = v_ref[pl.ds(k0, NA_KEYS), sl]
        bias = jnp.concatenate([b[h] for b in bias_refs], axis=-1)
        s_loc = jnp.where(valid, _dot_nt(q, k_loc) * SCALE + bias, NEG)
        s_ctx = _dot_nt(q, ck_ref[:, sl].astype(BF16)) * SCALE
        o = _softmax_pv([(s_loc, v_loc), (s_ctx, cv_ref[:, sl].astype(BF16))], None)
        o_ref[:, sl] = o.astype(BF16)


def _na_bias_table(rpb):
    w = np.arange(GRID_W)
    dc = np.clip(w[None, :] - w[:, None] + NA_COLS - 1, 0, 2 * NA_COLS - 2)
    tz = jnp.take(rpb, jnp.asarray(dc), axis=2)
    tz2 = jnp.concatenate([tz[:, :-1], tz[:, 1:]], axis=-1)
    return jnp.transpose(tz2, (1, 0, 2, 3))


def _na_attn_call(qb, kb, vb, cache_k, cache_v, bias_tab):
    q0 = M_PROMPT // GRID_W
    b0 = M_PROMPT // DEC_SEQ

    def bias_spec(piece):
        def idx(b, r):
            dr0 = _na_row_start(r) - r + NA_ROWS_MAX - 1
            return (dr0 + 2 * piece, 0, 0, 0)
        return pl.BlockSpec((None, B_HEADS, GRID_W, 2 * GRID_W), idx)

    return pl.pallas_call(
        _na_attn_kernel,
        grid=(DEC_BATCH, GRID_ROWS),
        in_specs=[pl.BlockSpec((GRID_W, B_QKV), lambda b, r: (q0 + b * GRID_ROWS + r, 0)),
                  pl.BlockSpec((DEC_SEQ, B_QKV), lambda b, r: (b0 + b, 0)),
                  pl.BlockSpec((DEC_SEQ, B_QKV), lambda b, r: (b0 + b, 0)),
                  pl.BlockSpec((None, PAST_LEN, B_QKV), lambda b, r: (b, 0, 0)),
                  pl.BlockSpec((None, PAST_LEN, B_QKV), lambda b, r: (b, 0, 0))]
                 + [bias_spec(p) for p in range(NA_PIECES)],
        out_specs=pl.BlockSpec((GRID_W, B_QKV), lambda b, r: (b * GRID_ROWS + r, 0)),
        out_shape=jax.ShapeDtypeStruct((M_SAMPLE, B_QKV), BF16),
        compiler_params=_params(("arbitrary", "arbitrary"), VMEM_LIMIT),
        name="attn_neighbourhood",
    )(qb, kb, vb, cache_k, cache_v, *([bias_tab] * NA_PIECES))


def _merge_kernel(oap, oas, obp, obs, wa_ref, wb_ref, ga_ref, gb_ref, z_ref, wa_bf, wb_bf):
    i = pl.program_id(1)
    p_tiles = M_PROMPT // TM

    @pl.when(i == 0)
    def _():
        wa_bf[...] = wa_ref[...].astype(BF16)
        wb_bf[...] = wb_ref[...].astype(BF16)

    def body(oa, ob):
        ya = _dot(oa, wa_bf[...])
        yb = _dot(ob, wb_bf[...])
        z_ref[...] = (ga_ref[...] * ya + gb_ref[...] * yb).astype(BF16)

    @pl.when(i < p_tiles)
    def _():
        body(oap[...], obp[...])

    @pl.when(i >= p_tiles)
    def _():
        body(oas[...], obs[...])


def _merge_call(oa_p, oa_s, ob_p, ob_s, w_br_a, w_br_b, gates):
    p_tiles = M_PROMPT // TM
    nj = D_MODEL // TN
    pspec = pl.BlockSpec((TM, A_Q), lambda j, i: (jnp.minimum(i, p_tiles - 1), 0))
    sspec = pl.BlockSpec((TM, A_Q), lambda j, i: (jnp.maximum(i - p_tiles, 0), 0))
    return pl.pallas_call(
        _merge_kernel,
        grid=(nj, M_ALL // TM),
        in_specs=[pspec, sspec, pspec, sspec,
                  pl.BlockSpec((A_Q, TN), lambda j, i: (0, j)),
                  pl.BlockSpec((B_QKV, TN), lambda j, i: (0, j)),
                  pl.BlockSpec((TM, TN), lambda j, i: (i, j)),
                  pl.BlockSpec((TM, TN), lambda j, i: (i, nj + j))],
        out_specs=pl.BlockSpec((TM, TN), lambda j, i: (i, j)),
        out_shape=jax.ShapeDtypeStruct((M_ALL, D_MODEL), BF16),
        scratch_shapes=[pltpu.VMEM((A_Q, TN), BF16), pltpu.VMEM((B_QKV, TN), BF16)],
        compiler_params=_params(("arbitrary", "arbitrary"), VMEM_LIMIT),
        name="merge_branches",
    )(oa_p, oa_s, ob_p, ob_s, w_br_a, w_br_b, gates, gates)


def _out_kernel(z_ref, w_ref, xp_ref, xs_ref, g_ref, o_ref, wbf):
    i = pl.program_id(1)
    p_tiles = M_PROMPT // TM

    @pl.when(i == 0)
    def _():
        wbf[...] = w_ref[...].astype(BF16)

    y = g_ref[...] * _dot(z_ref[...], wbf[...])

    @pl.when(i < p_tiles)
    def _():
        o_ref[...] = xp_ref[...] + y

    @pl.when(i >= p_tiles)
    def _():
        o_ref[...] = xs_ref[...] + y


def _out_call(z, w_out, xp, xs, mods3):
    p_tiles = M_PROMPT // TM
    nj = D_MODEL // TN
    return pl.pallas_call(
        _out_kernel,
        grid=(nj, M_ALL // TM),
        in_specs=[pl.BlockSpec((TM, D_MODEL), lambda j, i: (i, 0)),
                  pl.BlockSpec((D_MODEL, TN), lambda j, i: (0, j)),
                  pl.BlockSpec((TM, TN), lambda j, i: (jnp.minimum(i, p_tiles - 1), j)),
                  pl.BlockSpec((TM, TN), lambda j, i: (jnp.maximum(i - p_tiles, 0), j)),
                  pl.BlockSpec((None, 1, TN), lambda j, i: (_mod_row(i, TM), 0, 2 * nj + j))],
        out_specs=pl.BlockSpec((TM, TN), lambda j, i: (i, j)),
        out_shape=jax.ShapeDtypeStruct((M_ALL, D_MODEL), F32),
        scratch_shapes=[pltpu.VMEM((D_MODEL, TN), BF16)],
        compiler_params=_params(("arbitrary", "arbitrary"), VMEM_LIMIT),
        name="out_proj_residual",
    )(z, w_out, xp, xs, mods3)


def _mlp_kernel(x_ref, nw_ref, sh_ref, sc_ref, g_ref, wu_ref, wd_ref, o_ref, h_ref):
    f = pl.program_id(1)

    @pl.when(f == 0)
    def _():
        h_ref[...] = _modnorm(x_ref[...], nw_ref[...], sc_ref[...], sh_ref[...]).astype(BF16)

    u = _dot(h_ref[...], wu_ref[...].astype(BF16))
    u = jnp.square(jnp.maximum(u, 0.0)).astype(BF16)

    for c0 in range(0, D_MODEL, TN):
        cols = slice(c0, c0 + TN)
        part = _dot(u, wd_ref[:, cols].astype(BF16))

        @pl.when(f == 0)
        def _():
            o_ref[:, cols] = part

        @pl.when(f > 0)
        def _():
            o_ref[:, cols] += part

    @pl.when(f == pl.num_programs(1) - 1)
    def _():
        o_ref[...] = x_ref[...] + g_ref[...] * o_ref[...]


def _mlp_call(x1, nw, mods3, w_up, w_down, tile0, n_tiles):
    mod = lambda k: pl.BlockSpec((None, 1, D_MODEL), lambda i, f: (_mod_row(tile0 + i, TM), 0, k))
    return pl.pallas_call(
        _mlp_kernel,
        grid=(n_tiles, D_FF // TF),
        in_specs=[pl.BlockSpec((TM, D_MODEL), lambda i, f: (tile0 + i, 0), pipeline_mode=pl.Buffered(1)),
                  pl.BlockSpec((1, D_MODEL), lambda i, f: (0, 0)),
                  mod(3), mod(4), mod(5),
                  pl.BlockSpec((D_MODEL, TF), lambda i, f: (0, f)),
                  pl.BlockSpec((TF, D_MODEL), lambda i, f: (f, 0))],
        out_specs=pl.BlockSpec((TM, D_MODEL), lambda i, f: (i, 0)),
        out_shape=jax.ShapeDtypeStruct((n_tiles * TM, D_MODEL), F32),
        scratch_shapes=[pltpu.VMEM((TM, D_MODEL), BF16)],
        compiler_params=_params(("arbitrary", "arbitrary"), VMEM_LIMIT),
        name="mlp",
    )(x1, nw, mods3, mods3, mods3, w_up, w_down)


def kernel(x_prompt, x_sample, cache_a_k, cache_a_v, cache_b_k, cache_b_v, c, c_ctx, norm1_w, norm2_w, w_ada, b_ada, w_in, q_norm_a, k_norm_a, q_norm_b, k_norm_b, sink_a, rpb_b, w_br_a, w_br_b, w_out, w_up, w_down):
    assert w_ada.shape[0] == 1, "one trunk layer"
    xp = x_prompt.reshape(M_PROMPT, D_MODEL)
    xs = x_sample.reshape(M_SAMPLE, D_MODEL)

    cvecs = jnp.concatenate([c_ctx[None, :], c, jnp.zeros((8 - 1 - DEC_BATCH, D_MODEL), F32)], axis=0)
    mods = _ada_call(cvecs, w_ada[0], b_ada)
    mods3 = mods.reshape(8, 1, 6 * D_MODEL)

    h = _norm_call(xp, xs, norm1_w, mods3)

    w = w_in[0]
    rope = _rope_tables()
    qa = _proj_call(h, w, "qa", COL_QA, A_Q, q_norm_a, rope)
    kva, ka_p, va_p = _proj_call(h, w, "kva", COL_KVA, 2 * A_KV, k_norm_a, rope)
    qb = _proj_call(h, w, "qb", COL_QB, B_QKV, q_norm_b)
    kb, kb_p = _proj_call(h, w, "kb", COL_KB, B_QKV, k_norm_b)
    vb, vb_p = _proj_call(h, w, "vb", COL_VB, B_QKV)
    gates = _proj_call(h, w, "gate", COL_G, 2 * D_MODEL)

    sink = sink_a[0]
    oa_p, ob_p = _ctx_attn_call(sink, qa, kva, qb, kb, vb)
    oa_s = _win_attn_call(sink, qa, kva,
                          cache_a_k.reshape(DEC_BATCH, PAST_LEN, A_KV), cache_a_v.reshape(DEC_BATCH, PAST_LEN, A_KV))
    ob_s = _na_attn_call(qb, kb, vb,
                         cache_b_k.reshape(DEC_BATCH, PAST_LEN, B_QKV), cache_b_v.reshape(DEC_BATCH, PAST_LEN, B_QKV),
                         _na_bias_table(rpb_b[0]))

    z = _merge_call(oa_p, oa_s, ob_p, ob_s, w_br_a[0], w_br_b[0], gates)
    x1 = _out_call(z, w_out[0], xp, xs, mods3)

    p_tiles = M_PROMPT // TM
    y_p = _mlp_call(x1, norm2_w, mods3, w_up[0], w_down[0], 0, p_tiles)
    y_s = _mlp_call(x1, norm2_w, mods3, w_up[0], w_down[0], p_tiles, M_SAMPLE // TM)

    return (y_p.reshape(BATCH, SEQ, D_MODEL),
            y_s.reshape(DEC_BATCH, DEC_SEQ, D_MODEL),
            ka_p.reshape(BATCH, 1, SEQ, A_KV_HEADS, HEAD_DIM),
            va_p.reshape(BATCH, 1, SEQ, A_KV_HEADS, HEAD_DIM),
            kb_p.reshape(BATCH, 1, SEQ, B_HEADS, HEAD_DIM),
            vb_p.reshape(BATCH, 1, SEQ, B_HEADS, HEAD_DIM))
```

```python
import functools

import numpy as np
import jax
import jax.numpy as jnp
from jax import lax
from jax.experimental import pallas as pl
from jax.experimental.pallas import tpu as pltpu

D_MODEL = 2048
BATCH = 16
SEQ = 256
DEC_BATCH = 2
DEC_SEQ = 1024
PAST_LEN = 256
GRID_W = 64
HEAD_DIM = 128
A_HEADS = 8
A_KV_HEADS = 2
A_GROUP = A_HEADS // A_KV_HEADS
A_WINDOW = 128
BLOCK = 128
B_HEADS = 8
NA_ROWS_MAX = 8
NA_COLS = 16
D_FF = 4 * D_MODEL
ROPE_THETA = 10000.0
EPS = 1e-6
NEG = -1e30
A_Q = A_HEADS * HEAD_DIM
A_KV = A_KV_HEADS * HEAD_DIM
B_QKV = B_HEADS * HEAD_DIM
IN_WIDTH = A_Q + 2 * A_KV + 3 * B_QKV + 2 * D_MODEL
SCALE = HEAD_DIM ** -0.5

M_PROMPT = BATCH * SEQ
M_SAMPLE = DEC_BATCH * DEC_SEQ
M_ALL = M_PROMPT + M_SAMPLE
GRID_ROWS = DEC_SEQ // GRID_W
NA_ROWS = min(NA_ROWS_MAX, GRID_ROWS)

COL_QA = 0
COL_KVA = A_Q
COL_QB = A_Q + 2 * A_KV
COL_KB = COL_QB + B_QKV
COL_VB = COL_KB + B_QKV
COL_G = COL_VB + B_QKV

TM = 1024
TN = 1024
ROW_CHUNK = 256
W_SUB = 512
TM_NORM = 512
TF = 512
MLP_COLS = 512
VMEM_LIMIT = 56 * 1024 * 1024

F32 = jnp.float32
BF16 = jnp.bfloat16


def _mod_row(i, tm):
    p_tiles = M_PROMPT // tm
    return jnp.where(i < p_tiles, 0, 1 + (i - p_tiles) // (DEC_SEQ // tm))


def _dot(a, b):
    return jnp.dot(a, b, preferred_element_type=F32)


def _dot_nt(a, b):
    return lax.dot_general(a, b, (((1,), (1,)), ((), ())), preferred_element_type=F32)


def _params(sem, vmem=None):
    return pltpu.CompilerParams(dimension_semantics=sem, vmem_limit_bytes=vmem)


def _ada_kernel(c_ref, w_ref, b_ref, o_ref):
    cv = c_ref[...]
    s = (cv * jax.nn.sigmoid(cv)).astype(BF16)
    o_ref[...] = _dot(s, w_ref[...].astype(BF16)) + b_ref[...]


def _ada_call(cvecs, w_ada, b_ada):
    tn = 1024
    n = 6 * D_MODEL
    return pl.pallas_call(
        _ada_kernel,
        grid=(n // tn,),
        in_specs=[pl.BlockSpec((8, D_MODEL), lambda j: (0, 0)),
                  pl.BlockSpec((D_MODEL, tn), lambda j: (0, j)),
                  pl.BlockSpec((1, tn), lambda j: (0, j))],
        out_specs=pl.BlockSpec((8, tn), lambda j: (0, j)),
        out_shape=jax.ShapeDtypeStruct((8, n), F32),
        compiler_params=_params(("arbitrary",), VMEM_LIMIT),
        name="ada_mod",
    )(cvecs, w_ada, b_ada)


def _modnorm(x, nw, sc, sh):
    y = x * lax.rsqrt(jnp.mean(x * x, axis=-1, keepdims=True) + EPS)
    return (y * nw) * (1.0 + sc) + sh


def _norm_kernel(xp_ref, xs_ref, nw_ref, sh_ref, sc_ref, h_ref):
    i = pl.program_id(0)
    p_tiles = M_PROMPT // TM_NORM

    @pl.when(i < p_tiles)
    def _():
        h_ref[...] = _modnorm(xp_ref[...], nw_ref[...], sc_ref[...], sh_ref[...]).astype(BF16)

    @pl.when(i >= p_tiles)
    def _():
        h_ref[...] = _modnorm(xs_ref[...], nw_ref[...], sc_ref[...], sh_ref[...]).astype(BF16)


def _norm_call(xp, xs, nw, mods3):
    tm = TM_NORM
    p_tiles = M_PROMPT // tm
    return pl.pallas_call(
        _norm_kernel,
        grid=(M_ALL // tm,),
        in_specs=[pl.BlockSpec((tm, D_MODEL), lambda i: (jnp.minimum(i, p_tiles - 1), 0)),
                  pl.BlockSpec((tm, D_MODEL), lambda i: (jnp.maximum(i - p_tiles, 0), 0)),
                  pl.BlockSpec((1, D_MODEL), lambda i: (0, 0)),
                  pl.BlockSpec((None, 1, D_MODEL), lambda i: (_mod_row(i, tm), 0, 0)),
                  pl.BlockSpec((None, 1, D_MODEL), lambda i: (_mod_row(i, tm), 0, 1))],
        out_specs=pl.BlockSpec((tm, D_MODEL), lambda i: (i, 0)),
        out_shape=jax.ShapeDtypeStruct((M_ALL, D_MODEL), BF16),
        compiler_params=_params(("arbitrary",), VMEM_LIMIT),
        name="norm1",
    )(xp, xs, nw, mods3, mods3)


def _head_norm(x, nw):
    return x * lax.rsqrt(jnp.mean(x * x, axis=-1, keepdims=True) + EPS) * nw


def _rope(x, cos, sin_signed):
    lane = lax.broadcasted_iota(jnp.int32, x.shape, 1)
    partner = jnp.where((lane % 64) < 32, pltpu.roll(x, 96, 1), pltpu.roll(x, 32, 1))
    return x * cos + partner * sin_signed


def _rope_tables():
    n_freq = HEAD_DIM // 4
    pos = np.arange(DEC_SEQ)
    row = (pos // GRID_W).astype(np.float64)
    col = (pos % GRID_W).astype(np.float64)
    inv = ROPE_THETA ** (-np.arange(n_freq, dtype=np.float64) / n_freq)
    ar = row[:, None] * inv
    ac = col[:, None] * inv
    cos = np.concatenate([np.cos(ar), np.cos(ar), np.cos(ac), np.cos(ac)], axis=-1)
    sin = np.concatenate([-np.sin(ar), np.sin(ar), -np.sin(ac), np.sin(ac)], axis=-1)
    return jnp.asarray(cos, F32), jnp.asarray(sin, F32)


def _proj_kernel(*refs, kind, tn):
    nsub = tn // W_SUB
    h_ref, w_refs, refs, wbf = refs[0], refs[1:1 + nsub], refs[1 + nsub:-1], refs[-1]
    i = pl.program_id(1)
    is_prompt = i < M_PROMPT // TM

    @pl.when(i == 0)
    def _():
        for k, w_ref in enumerate(w_refs):
            wbf[:, k * W_SUB:(k + 1) * W_SUB] = w_ref[...].astype(BF16)

    def chunks():
        for r0 in range(0, TM, ROW_CHUNK):
            rows = slice(r0, r0 + ROW_CHUNK)
            acc = _dot(h_ref[rows, :], wbf[...])
            yield rows, acc, [acc[:, a:a + HEAD_DIM] for a in range(0, tn, HEAD_DIM)]

    def head_cols(k, base=0):
        return slice(base + k * HEAD_DIM, base + (k + 1) * HEAD_DIM)

    def per_group(run):
        pl.when(is_prompt)(functools.partial(run, True))
        pl.when(jnp.logical_not(is_prompt))(functools.partial(run, False))

    if kind == "qa":
        nw_ref, cos_ref, sin_ref, q_ref = refs

        def run(prompt):
            for rows, _, heads in chunks():
                for k, x in enumerate(heads):
                    y = _head_norm(x, nw_ref[...])
                    if not prompt:
                        y = _rope(y, cos_ref[rows, :], sin_ref[rows, :])
                    q_ref[rows, head_cols(k)] = y.astype(BF16)
        per_group(run)

    elif kind == "kva":
        nw_ref, cos_ref, sin_ref, kv_ref, kp_ref, vp_ref = refs

        def run(prompt):
            for rows, _, heads in chunks():
                for k in range(A_KV_HEADS):
                    y = _head_norm(heads[k], nw_ref[...])
                    v = heads[A_KV_HEADS + k]
                    if prompt:
                        kp_ref[rows, head_cols(k)] = y
                        vp_ref[rows, head_cols(k)] = v
                    else:
                        y = _rope(y, cos_ref[rows, :], sin_ref[rows, :])
                    kv_ref[rows, head_cols(k)] = y.astype(BF16)
                    kv_ref[rows, head_cols(k, A_KV)] = v.astype(BF16)
        per_group(run)

    elif kind == "qb":
        nw_ref, q_ref = refs
        for rows, _, heads in chunks():
            for k, x in enumerate(heads):
                q_ref[rows, head_cols(k)] = _head_norm(x, nw_ref[...]).astype(BF16)

    elif kind == "kb":
        nw_ref, k_ref, kp_ref = refs

        def run(prompt):
            for rows, _, heads in chunks():
                for k, x in enumerate(heads):
                    y = _head_norm(x, nw_ref[...])
                    if prompt:
                        kp_ref[rows, head_cols(k)] = y
                    k_ref[rows, head_cols(k)] = y.astype(BF16)
        per_group(run)

    elif kind == "vb":
        v_ref, vp_ref = refs

        def run(prompt):
            for rows, acc, _ in chunks():
                if prompt:
                    vp_ref[rows, :] = acc
                v_ref[rows, :] = acc.astype(BF16)
        per_group(run)

    elif kind == "gate":
        (g_ref,) = refs
        for rows, acc, _ in chunks():
            g_ref[rows, :] = jax.nn.sigmoid(acc)

    else:
        raise ValueError(kind)


def _proj_call(h, w_in, kind, col0, width, tn, norm_w=None, rope=None):
    assert col0 % W_SUB == 0 and tn % W_SUB == 0 and width % tn == 0
    nj = width // tn
    nsub = tn // W_SUB
    c0 = col0 // W_SUB
    p_tiles = M_PROMPT // TM
    s_tiles = DEC_SEQ // TM

    w_mode = dict(pipeline_mode=pl.Buffered(1)) if nj == 1 else {}
    in_specs = [pl.BlockSpec((TM, D_MODEL), lambda j, i: (i, 0))]
    in_specs += [pl.BlockSpec((D_MODEL, W_SUB), lambda j, i, k=k: (0, c0 + nsub * j + k), **w_mode)
                 for k in range(nsub)]
    args = [h] + [w_in] * nsub
    if norm_w is not None:
        in_specs.append(pl.BlockSpec((1, HEAD_DIM), lambda j, i: (0, 0)))
        args.append(norm_w)
    if rope is not None:
        rope_spec = pl.BlockSpec((TM, HEAD_DIM), lambda j, i: (jnp.maximum(i - p_tiles, 0) % s_tiles, 0))
        in_specs += [rope_spec, rope_spec]
        args += list(rope)

    full = pl.BlockSpec((TM, tn), lambda j, i: (i, j))
    parked = lambda w: pl.BlockSpec((TM, w), lambda j, i: (jnp.minimum(i, p_tiles - 1), j))
    if kind in ("qa", "qb"):
        out_specs = full
        out_shape = jax.ShapeDtypeStruct((M_ALL, width), BF16)
    elif kind == "kva":
        out_specs = [full, parked(A_KV), parked(A_KV)]
        out_shape = [jax.ShapeDtypeStruct((M_ALL, width), BF16),
                     jax.ShapeDtypeStruct((M_PROMPT, A_KV), F32),
                     jax.ShapeDtypeStruct((M_PROMPT, A_KV), F32)]
    elif kind in ("kb", "vb"):
        out_specs = [full, parked(tn)]
        out_shape = [jax.ShapeDtypeStruct((M_ALL, width), BF16),
                     jax.ShapeDtypeStruct((M_PROMPT, width), F32)]
    elif kind == "gate":
        out_specs = full
        out_shape = jax.ShapeDtypeStruct((M_ALL, width), F32)
    else:
        raise ValueError(kind)

    return pl.pallas_call(
        functools.partial(_proj_kernel, kind=kind, tn=tn),
        grid=(nj, M_ALL // TM),
        in_specs=in_specs,
        out_specs=out_specs,
        out_shape=out_shape,
        scratch_shapes=[pltpu.VMEM((D_MODEL, tn), BF16)],
        compiler_params=_params(("arbitrary", "arbitrary"), VMEM_LIMIT),
        name="proj_" + kind,
    )(*args)


def _softmax_pv(parts, sink):
    m = functools.reduce(jnp.maximum, [jnp.max(s, axis=-1, keepdims=True) for s, _ in parts])
    if sink is not None:
        m = jnp.maximum(m, sink)
    ps = [jnp.exp(s - m) for s, _ in parts]
    l = functools.reduce(jnp.add, [jnp.sum(p, axis=-1, keepdims=True) for p in ps])
    if sink is not None:
        l = l + jnp.exp(sink - m)
    o = functools.reduce(jnp.add, [_dot(p.astype(BF16), v) for p, (_, v) in zip(ps, parts)])
    return o / l


def _sink_column(sink_ref, hk, rows_per_head):
    n = A_GROUP * rows_per_head
    g = lax.broadcasted_iota(jnp.int32, (n, 1), 0) // rows_per_head
    col = jnp.full((n, 1), sink_ref[hk * A_GROUP], F32)
    for k in range(1, A_GROUP):
        col = jnp.where(g == k, sink_ref[hk * A_GROUP + k], col)
    return col


def _stack_group(q_ref, hk):
    return jnp.concatenate(
        [q_ref[:, (hk * A_GROUP + g) * HEAD_DIM:(hk * A_GROUP + g + 1) * HEAD_DIM] for g in range(A_GROUP)], axis=0)


def _unstack_group(o_ref, hk, o, rows):
    for g in range(A_GROUP):
        c = (hk * A_GROUP + g) * HEAD_DIM
        o_ref[:, c:c + HEAD_DIM] = o[g * rows:(g + 1) * rows].astype(BF16)


def _ctx_attn_kernel(sink_ref, qa_ref, kva_ref, qb_ref, kb_ref, vb_ref, oa_ref, ob_ref):
    for hk in range(A_KV_HEADS):
        k = kva_ref[:, hk * HEAD_DIM:(hk + 1) * HEAD_DIM]
        v = kva_ref[:, A_KV + hk * HEAD_DIM:A_KV + (hk + 1) * HEAD_DIM]
        q4 = _stack_group(qa_ref, hk)
        s = _dot_nt(q4, k) * SCALE
        o = _softmax_pv([(s, v)], _sink_column(sink_ref, hk, SEQ))
        _unstack_group(oa_ref, hk, o, SEQ)
    for h in range(B_HEADS):
        sl = slice(h * HEAD_DIM, (h + 1) * HEAD_DIM)
        s = _dot_nt(qb_ref[:, sl], kb_ref[:, sl]) * SCALE
        ob_ref[:, sl] = _softmax_pv([(s, vb_ref[:, sl])], None).astype(BF16)


def _ctx_attn_call(sink, qa, kva, qb, kb, vb):
    row = lambda w: pl.BlockSpec((SEQ, w), lambda b: (b, 0))
    return pl.pallas_call(
        _ctx_attn_kernel,
        grid=(BATCH,),
        in_specs=[pl.BlockSpec(memory_space=pltpu.SMEM),
                  row(A_Q), row(2 * A_KV), row(B_QKV), row(B_QKV), row(B_QKV)],
        out_specs=[row(A_Q), row(B_QKV)],
        out_shape=[jax.ShapeDtypeStruct((M_PROMPT, A_Q), BF16),
                   jax.ShapeDtypeStruct((M_PROMPT, B_QKV), BF16)],
        compiler_params=_params(("arbitrary",), VMEM_LIMIT),
        name="attn_ctx",
    )(sink, qa, kva, qb, kb, vb)


BAND = 3 * BLOCK


def _win_attn_kernel(sink_ref, q_ref, kv_ref, ck_ref, cv_ref, o_ref):
    n = pl.program_id(1)
    start = pl.multiple_of(jnp.clip((n - 1) * BLOCK, 0, DEC_SEQ - BAND), BLOCK)
    qpos = n * BLOCK + lax.broadcasted_iota(jnp.int32, (BLOCK, BAND), 0)
    kpos = start + lax.broadcasted_iota(jnp.int32, (BLOCK, BAND), 1)
    valid = jnp.abs(qpos - kpos) <= A_WINDOW
    valid = jnp.concatenate([valid.astype(jnp.int32)] * A_GROUP, axis=0) > 0
    for hk in range(A_KV_HEADS):
        sl = slice(hk * HEAD_DIM, (hk + 1) * HEAD_DIM)
        slv = slice(A_KV + hk * HEAD_DIM, A_KV + (hk + 1) * HEAD_DIM)
        k_loc = kv_ref[pl.ds(start, BAND), sl]
        v_loc = kv_ref[pl.ds(start, BAND), slv]
        k_ctx = ck_ref[:, sl].astype(BF16)
        v_ctx = cv_ref[:, sl].astype(BF16)
        q4 = _stack_group(q_ref, hk)
        s_loc = jnp.where(valid, _dot_nt(q4, k_loc) * SCALE, NEG)
        s_ctx = _dot_nt(q4, k_ctx) * SCALE
        o = _softmax_pv([(s_loc, v_loc), (s_ctx, v_ctx)], _sink_column(sink_ref, hk, BLOCK))
        _unstack_group(o_ref, hk, o, BLOCK)


def _win_attn_call(sink, qa, kva, cache_k, cache_v):
    nb = DEC_SEQ // BLOCK
    q0 = M_PROMPT // BLOCK
    b0 = M_PROMPT // DEC_SEQ
    return pl.pallas_call(
        _win_attn_kernel,
        grid=(DEC_BATCH, nb),
        in_specs=[pl.BlockSpec(memory_space=pltpu.SMEM),
                  pl.BlockSpec((BLOCK, A_Q), lambda b, n: (q0 + b * nb + n, 0)),
                  pl.BlockSpec((DEC_SEQ, 2 * A_KV), lambda b, n: (b0 + b, 0)),
                  pl.BlockSpec((None, PAST_LEN, A_KV), lambda b, n: (b, 0, 0)),
                  pl.BlockSpec((None, PAST_LEN, A_KV), lambda b, n: (b, 0, 0))],
        out_specs=pl.BlockSpec((BLOCK, A_Q), lambda b, n: (b * nb + n, 0)),
        out_shape=jax.ShapeDtypeStruct((M_SAMPLE, A_Q), BF16),
        compiler_params=_params(("arbitrary", "arbitrary"), VMEM_LIMIT),
        name="attn_window",
    )(sink, qa, kva, cache_k, cache_v)


NA_KEYS = NA_ROWS * GRID_W
NA_PIECES = NA_KEYS // 128


def _na_row_start(r):
    return jnp.clip(r - NA_ROWS // 2, 0, GRID_ROWS - NA_ROWS)


def _na_attn_kernel(q_ref, k_ref, v_ref, ck_ref, cv_ref, *rest):
    bias_refs, o_ref = rest[:NA_PIECES], rest[NA_PIECES]
    r = pl.program_id(1)
    k0 = pl.multiple_of(_na_row_start(r) * GRID_W, GRID_W)
    qcol = lax.broadcasted_iota(jnp.int32, (GRID_W, NA_KEYS), 0)
    kcol = lax.broadcasted_iota(jnp.int32, (GRID_W, NA_KEYS), 1) % GRID_W
    cstart = jnp.clip(qcol - NA_COLS // 2, 0, GRID_W - NA_COLS)
    valid = (kcol >= cstart) & (kcol < cstart + NA_COLS)
    for h in range(B_HEADS):
        sl = slice(h * HEAD_DIM, (h + 1) * HEAD_DIM)
        q = q_ref[:, sl]
        k_loc = k_ref[pl.ds(k0, NA_KEYS), sl]
        v_loc = v_ref[pl.ds(k0, NA_KEYS), sl]
        bias = jnp.concatenate([b[h] for b in bias_refs], axis=-1)
        s_loc = jnp.where(valid, _dot_nt(q, k_loc) * SCALE + bias, NEG)
        s_ctx = _dot_nt(q, ck_ref[:, sl].astype(BF16)) * SCALE
        o = _softmax_pv([(s_loc, v_loc), (s_ctx, cv_ref[:, sl].astype(BF16))], None)
        o_ref[:, sl] = o.astype(BF16)


def _na_bias_table(rpb):
    w = np.arange(GRID_W)
    dc = np.clip(w[None, :] - w[:, None] + NA_COLS - 1, 0, 2 * NA_COLS - 2)
    tz = jnp.take(rpb, jnp.asarray(dc), axis=2)
    tz2 = jnp.concatenate([tz[:, :-1], tz[:, 1:]], axis=-1)
    return jnp.transpose(tz2, (1, 0, 2, 3))


def _na_attn_call(qb, kb, vb, cache_k, cache_v, bias_tab):
    q0 = M_PROMPT // GRID_W
    b0 = M_PROMPT // DEC_SEQ

    def bias_spec(piece):
        def idx(b, r):
            dr0 = _na_row_start(r) - r + NA_ROWS_MAX - 1
            return (dr0 + 2 * piece, 0, 0, 0)
        return pl.BlockSpec((None, B_HEADS, GRID_W, 2 * GRID_W), idx)

    return pl.pallas_call(
        _na_attn_kernel,
        grid=(DEC_BATCH, GRID_ROWS),
        in_specs=[pl.BlockSpec((GRID_W, B_QKV), lambda b, r: (q0 + b * GRID_ROWS + r, 0)),
                  pl.BlockSpec((DEC_SEQ, B_QKV), lambda b, r: (b0 + b, 0)),
                  pl.BlockSpec((DEC_SEQ, B_QKV), lambda b, r: (b0 + b, 0)),
                  pl.BlockSpec((None, PAST_LEN, B_QKV), lambda b, r: (b, 0, 0)),
                  pl.BlockSpec((None, PAST_LEN, B_QKV), lambda b, r: (b, 0, 0))]
                 + [bias_spec(p) for p in range(NA_PIECES)],
        out_specs=pl.BlockSpec((GRID_W, B_QKV), lambda b, r: (b * GRID_ROWS + r, 0)),
        out_shape=jax.ShapeDtypeStruct((M_SAMPLE, B_QKV), BF16),
        compiler_params=_params(("arbitrary", "arbitrary"), VMEM_LIMIT),
        name="attn_neighbourhood",
    )(qb, kb, vb, cache_k, cache_v, *([bias_tab] * NA_PIECES))


def _merge_kernel(oap, oas, obp, obs, wa_ref, wb_ref, ga_ref, gb_ref, z_ref, wa_bf, wb_bf):
    i = pl.program_id(1)
    p_tiles = M_PROMPT // TM

    @pl.when(i == 0)
    def _():
        wa_bf[...] = wa_ref[...].astype(BF16)
        wb_bf[...] = wb_ref[...].astype(BF16)

    def run(oa_ref, ob_ref):
        for r0 in range(0, TM, ROW_CHUNK):
            rows = slice(r0, r0 + ROW_CHUNK)
            ya = _dot(oa_ref[rows, :], wa_bf[...])
            yb = _dot(ob_ref[rows, :], wb_bf[...])
            z_ref[rows, :] = (ga_ref[rows, :] * ya + gb_ref[rows, :] * yb).astype(BF16)

    pl.when(i < p_tiles)(functools.partial(run, oap, obp))
    pl.when(i >= p_tiles)(functools.partial(run, oas, obs))


def _merge_call(oa_p, oa_s, ob_p, ob_s, w_br_a, w_br_b, gates):
    p_tiles = M_PROMPT // TM
    nj = D_MODEL // TN
    pspec = pl.BlockSpec((TM, A_Q), lambda j, i: (jnp.minimum(i, p_tiles - 1), 0))
    sspec = pl.BlockSpec((TM, A_Q), lambda j, i: (jnp.maximum(i - p_tiles, 0), 0))
    once = dict(pipeline_mode=pl.Buffered(1))
    return pl.pallas_call(
        _merge_kernel,
        grid=(nj, M_ALL // TM),
        in_specs=[pspec, sspec, pspec, sspec,
                  pl.BlockSpec((A_Q, TN), lambda j, i: (0, j), **once),
                  pl.BlockSpec((B_QKV, TN), lambda j, i: (0, j), **once),
                  pl.BlockSpec((TM, TN), lambda j, i: (i, j)),
                  pl.BlockSpec((TM, TN), lambda j, i: (i, nj + j))],
        out_specs=pl.BlockSpec((TM, TN), lambda j, i: (i, j)),
        out_shape=jax.ShapeDtypeStruct((M_ALL, D_MODEL), BF16),
        scratch_shapes=[pltpu.VMEM((A_Q, TN), BF16), pltpu.VMEM((B_QKV, TN), BF16)],
        compiler_params=_params(("arbitrary", "arbitrary"), VMEM_LIMIT),
        name="merge_branches",
    )(oa_p, oa_s, ob_p, ob_s, w_br_a, w_br_b, gates, gates)


def _out_kernel(z_ref, w_ref, xp_ref, xs_ref, g_ref, o_ref, wbf):
    i = pl.program_id(1)
    p_tiles = M_PROMPT // TM

    @pl.when(i == 0)
    def _():
        wbf[...] = w_ref[...].astype(BF16)

    def run(x_ref):
        for r0 in range(0, TM, ROW_CHUNK):
            rows = slice(r0, r0 + ROW_CHUNK)
            o_ref[rows, :] = x_ref[rows, :] + g_ref[...] * _dot(z_ref[rows, :], wbf[...])

    pl.when(i < p_tiles)(functools.partial(run, xp_ref))
    pl.when(i >= p_tiles)(functools.partial(run, xs_ref))


def _out_call(z, w_out, xp, xs, mods3):
    p_tiles = M_PROMPT // TM
    nj = D_MODEL // TN
    return pl.pallas_call(
        _out_kernel,
        grid=(nj, M_ALL // TM),
        in_specs=[pl.BlockSpec((TM, D_MODEL), lambda j, i: (i, 0)),
                  pl.BlockSpec((D_MODEL, TN), lambda j, i: (0, j), pipeline_mode=pl.Buffered(1)),
                  pl.BlockSpec((TM, TN), lambda j, i: (jnp.minimum(i, p_tiles - 1), j)),
                  pl.BlockSpec((TM, TN), lambda j, i: (jnp.maximum(i - p_tiles, 0), j)),
                  pl.BlockSpec((None, 1, TN), lambda j, i: (_mod_row(i, TM), 0, 2 * nj + j))],
        out_specs=pl.BlockSpec((TM, TN), lambda j, i: (i, j)),
        out_shape=jax.ShapeDtypeStruct((M_ALL, D_MODEL), F32),
        scratch_shapes=[pltpu.VMEM((D_MODEL, TN), BF16)],
        compiler_params=_params(("arbitrary", "arbitrary"), VMEM_LIMIT),
        name="out_proj_residual",
    )(z, w_out, xp, xs, mods3)


def _mlp_kernel(x_ref, nw_ref, sh_ref, sc_ref, g_ref, wu_ref, wd_ref, o_ref, h_ref):
    f = pl.program_id(1)

    @pl.when(f == 0)
    def _():
        h_ref[...] = _modnorm(x_ref[...], nw_ref[...], sc_ref[...], sh_ref[...]).astype(BF16)
        o_ref[...] = jnp.zeros_like(o_ref)

    u = _dot(h_ref[...], wu_ref[...].astype(BF16))
    u = jnp.square(jnp.maximum(u, 0.0)).astype(BF16)
    for c0 in range(0, D_MODEL, MLP_COLS):
        cols = slice(c0, c0 + MLP_COLS)
        o_ref[:, cols] += _dot(u, wd_ref[:, cols].astype(BF16))

    @pl.when(f == pl.num_programs(1) - 1)
    def _():
        o_ref[...] = x_ref[...] + g_ref[...] * o_ref[...]


def _mlp_call(x1, nw, mods3, w_up, w_down, tile0, n_tiles):
    mod = lambda k: pl.BlockSpec((None, 1, D_MODEL), lambda i, f: (_mod_row(tile0 + i, TM), 0, k))
    return pl.pallas_call(
        _mlp_kernel,
        grid=(n_tiles, D_FF // TF),
        in_specs=[pl.BlockSpec((TM, D_MODEL), lambda i, f: (tile0 + i, 0), pipeline_mode=pl.Buffered(1)),
                  pl.BlockSpec((1, D_MODEL), lambda i, f: (0, 0)),
                  mod(3), mod(4), mod(5),
                  pl.BlockSpec((D_MODEL, TF), lambda i, f: (0, f)),
                  pl.BlockSpec((TF, D_MODEL), lambda i, f: (f, 0))],
        out_specs=pl.BlockSpec((TM, D_MODEL), lambda i, f: (i, 0)),
        out_shape=jax.ShapeDtypeStruct((n_tiles * TM, D_MODEL), F32),
        scratch_shapes=[pltpu.VMEM((TM, D_MODEL), BF16)],
        compiler_params=_params(("arbitrary", "arbitrary"), VMEM_LIMIT),
        name="mlp",
    )(x1, nw, mods3, mods3, mods3, w_up, w_down)


def kernel(x_prompt, x_sample, cache_a_k, cache_a_v, cache_b_k, cache_b_v, c, c_ctx, norm1_w, norm2_w, w_ada, b_ada, w_in, q_norm_a, k_norm_a, q_norm_b, k_norm_b, sink_a, rpb_b, w_br_a, w_br_b, w_out, w_up, w_down):
    assert w_ada.shape[0] == 1, "one trunk layer"
    xp = x_prompt.reshape(M_PROMPT, D_MODEL)
    xs = x_sample.reshape(M_SAMPLE, D_MODEL)

    cvecs = jnp.concatenate([c_ctx[None, :], c, jnp.zeros((8 - 1 - DEC_BATCH, D_MODEL), F32)], axis=0)
    mods = _ada_call(cvecs, w_ada[0], b_ada)
    mods3 = mods.reshape(8, 1, 6 * D_MODEL)

    h = _norm_call(xp, xs, norm1_w, mods3)

    w = w_in[0]
    rope = _rope_tables()
    qa = _proj_call(h, w, "qa", COL_QA, A_Q, TN, q_norm_a, rope)
    kva, ka_p, va_p = _proj_call(h, w, "kva", COL_KVA, 2 * A_KV, 2 * A_KV, k_norm_a, rope)
    qb = _proj_call(h, w, "qb", COL_QB, B_QKV, TN, q_norm_b)
    kb, kb_p = _proj_call(h, w, "kb", COL_KB, B_QKV, TN, k_norm_b)
    vb, vb_p = _proj_call(h, w, "vb", COL_VB, B_QKV, TN)
    gates = _proj_call(h, w, "gate", COL_G, 2 * D_MODEL, TN)

    sink = sink_a[0]
    oa_p, ob_p = _ctx_attn_call(sink, qa, kva, qb, kb, vb)
    oa_s = _win_attn_call(sink, qa, kva,
                          cache_a_k.reshape(DEC_BATCH, PAST_LEN, A_KV), cache_a_v.reshape(DEC_BATCH, PAST_LEN, A_KV))
    ob_s = _na_attn_call(qb, kb, vb,
                         cache_b_k.reshape(DEC_BATCH, PAST_LEN, B_QKV), cache_b_v.reshape(DEC_BATCH, PAST_LEN, B_QKV),
                         _na_bias_table(rpb_b[0]))

    z = _merge_call(oa_p, oa_s, ob_p, ob_s, w_br_a[0], w_br_b[0], gates)
    x1 = _out_call(z, w_out[0], xp, xs, mods3)

    p_tiles = M_PROMPT // TM
    y_p = _mlp_call(x1, norm2_w, mods3, w_up[0], w_down[0], 0, p_tiles)
    y_s = _mlp_call(x1, norm2_w, mods3, w_up[0], w_down[0], p_tiles, M_SAMPLE // TM)

    return (y_p.reshape(BATCH, SEQ, D_MODEL),
            y_s.reshape(DEC_BATCH, DEC_SEQ, D_MODEL),
            ka_p.reshape(BATCH, 1, SEQ, A_KV_HEADS, HEAD_DIM),
            va_p.reshape(BATCH, 1, SEQ, A_KV_HEADS, HEAD_DIM),
            kb_p.reshape(BATCH, 1, SEQ, B_HEADS, HEAD_DIM),
            vb_p.reshape(BATCH, 1, SEQ, B_HEADS, HEAD_DIM))
```

```python
import functools

import numpy as np
import jax
import jax.numpy as jnp
from jax import lax
from jax.experimental import pallas as pl
from jax.experimental.pallas import tpu as pltpu

D_MODEL = 2048
BATCH = 16
SEQ = 256
DEC_BATCH = 2
DEC_SEQ = 1024
PAST_LEN = 256
GRID_W = 64
HEAD_DIM = 128
A_HEADS = 8
A_KV_HEADS = 2
A_GROUP = A_HEADS // A_KV_HEADS
A_WINDOW = 128
BLOCK = 128
B_HEADS = 8
NA_ROWS_MAX = 8
NA_COLS = 16
D_FF = 4 * D_MODEL
ROPE_THETA = 10000.0
EPS = 1e-6
NEG = -1e30
A_Q = A_HEADS * HEAD_DIM
A_KV = A_KV_HEADS * HEAD_DIM
B_QKV = B_HEADS * HEAD_DIM
IN_WIDTH = A_Q + 2 * A_KV + 3 * B_QKV + 2 * D_MODEL
SCALE = HEAD_DIM ** -0.5

M_PROMPT = BATCH * SEQ
M_SAMPLE = DEC_BATCH * DEC_SEQ
M_ALL = M_PROMPT + M_SAMPLE
GRID_ROWS = DEC_SEQ // GRID_W
NA_ROWS = min(NA_ROWS_MAX, GRID_ROWS)

COL_QA = 0
COL_KVA = A_Q
COL_QB = A_Q + 2 * A_KV
COL_KB = COL_QB + B_QKV
COL_VB = COL_KB + B_QKV
COL_G = COL_VB + B_QKV

TM = 1024
TN = 1024
ROW_CHUNK = 256
W_SUB = 512
TM_NORM = 512
TF = 512
MLP_COLS = 512
VMEM_LIMIT = 56 * 1024 * 1024

F32 = jnp.float32
BF16 = jnp.bfloat16


def _mod_row(i, tm):
    p_tiles = M_PROMPT // tm
    return jnp.where(i < p_tiles, 0, 1 + (i - p_tiles) // (DEC_SEQ // tm))


def _dot(a, b):
    return jnp.dot(a, b, preferred_element_type=F32)


def _dot_nt(a, b):
    return lax.dot_general(a, b, (((1,), (1,)), ((), ())), preferred_element_type=F32)


def _params(sem, vmem=None):
    return pltpu.CompilerParams(dimension_semantics=sem, vmem_limit_bytes=vmem)


def _ada_kernel(c_ref, w_ref, b_ref, o_ref):
    cv = c_ref[...]
    s = (cv * jax.nn.sigmoid(cv)).astype(BF16)
    o_ref[...] = _dot(s, w_ref[...].astype(BF16)) + b_ref[...]


def _ada_call(cvecs, w_ada, b_ada):
    tn = 1024
    n = 6 * D_MODEL
    return pl.pallas_call(
        _ada_kernel,
        grid=(n // tn,),
        in_specs=[pl.BlockSpec((8, D_MODEL), lambda j: (0, 0)),
                  pl.BlockSpec((D_MODEL, tn), lambda j: (0, j)),
                  pl.BlockSpec((1, tn), lambda j: (0, j))],
        out_specs=pl.BlockSpec((8, tn), lambda j: (0, j)),
        out_shape=jax.ShapeDtypeStruct((8, n), F32),
        compiler_params=_params(("arbitrary",), VMEM_LIMIT),
        name="ada_mod",
    )(cvecs, w_ada, b_ada)


def _modnorm(x, nw, sc, sh):
    y = x * lax.rsqrt(jnp.mean(x * x, axis=-1, keepdims=True) + EPS)
    return (y * nw) * (1.0 + sc) + sh


def _norm_kernel(xp_ref, xs_ref, nw_ref, sh_ref, sc_ref, h_ref):
    i = pl.program_id(0)
    p_tiles = M_PROMPT // TM_NORM

    @pl.when(i < p_tiles)
    def _():
        h_ref[...] = _modnorm(xp_ref[...], nw_ref[...], sc_ref[...], sh_ref[...]).astype(BF16)

    @pl.when(i >= p_tiles)
    def _():
        h_ref[...] = _modnorm(xs_ref[...], nw_ref[...], sc_ref[...], sh_ref[...]).astype(BF16)


def _norm_call(xp, xs, nw, mods3):
    tm = TM_NORM
    p_tiles = M_PROMPT // tm
    return pl.pallas_call(
        _norm_kernel,
        grid=(M_ALL // tm,),
        in_specs=[pl.BlockSpec((tm, D_MODEL), lambda i: (jnp.minimum(i, p_tiles - 1), 0)),
                  pl.BlockSpec((tm, D_MODEL), lambda i: (jnp.maximum(i - p_tiles, 0), 0)),
                  pl.BlockSpec((1, D_MODEL), lambda i: (0, 0)),
                  pl.BlockSpec((None, 1, D_MODEL), lambda i: (_mod_row(i, tm), 0, 0)),
                  pl.BlockSpec((None, 1, D_MODEL), lambda i: (_mod_row(i, tm), 0, 1))],
        out_specs=pl.BlockSpec((tm, D_MODEL), lambda i: (i, 0)),
        out_shape=jax.ShapeDtypeStruct((M_ALL, D_MODEL), BF16),
        compiler_params=_params(("arbitrary",), VMEM_LIMIT),
        name="norm1",
    )(xp, xs, nw, mods3, mods3)


def _head_norm(x, nw):
    return x * lax.rsqrt(jnp.mean(x * x, axis=-1, keepdims=True) + EPS) * nw


def _rope(x, cos, sin_signed):
    lane = lax.broadcasted_iota(jnp.int32, x.shape, 1)
    partner = jnp.where((lane % 64) < 32, pltpu.roll(x, 96, 1), pltpu.roll(x, 32, 1))
    return x * cos + partner * sin_signed


def _rope_tables():
    n_freq = HEAD_DIM // 4
    pos = np.arange(DEC_SEQ)
    row = (pos // GRID_W).astype(np.float64)
    col = (pos % GRID_W).astype(np.float64)
    inv = ROPE_THETA ** (-np.arange(n_freq, dtype=np.float64) / n_freq)
    ar = row[:, None] * inv
    ac = col[:, None] * inv
    cos = np.concatenate([np.cos(ar), np.cos(ar), np.cos(ac), np.cos(ac)], axis=-1)
    sin = np.concatenate([-np.sin(ar), np.sin(ar), -np.sin(ac), np.sin(ac)], axis=-1)
    return jnp.asarray(cos, F32), jnp.asarray(sin, F32)


def _proj_kernel(*refs, kind, tn):
    nsub = tn // W_SUB
    h_ref, w_refs, refs, wbf = refs[0], refs[1:1 + nsub], refs[1 + nsub:-1], refs[-1]
    i = pl.program_id(1)
    is_prompt = i < M_PROMPT // TM

    @pl.when(i == 0)
    def _():
        for k, w_ref in enumerate(w_refs):
            wbf[:, k * W_SUB:(k + 1) * W_SUB] = w_ref[...].astype(BF16)

    def chunks():
        for r0 in range(0, TM, ROW_CHUNK):
            rows = slice(r0, r0 + ROW_CHUNK)
            acc = _dot(h_ref[rows, :], wbf[...])
            yield rows, acc, [acc[:, a:a + HEAD_DIM] for a in range(0, tn, HEAD_DIM)]

    def head_cols(k, base=0):
        return slice(base + k * HEAD_DIM, base + (k + 1) * HEAD_DIM)

    def per_group(run):
        pl.when(is_prompt)(functools.partial(run, True))
        pl.when(jnp.logical_not(is_prompt))(functools.partial(run, False))

    if kind == "qa":
        nw_ref, cos_ref, sin_ref, q_ref = refs

        def run(prompt):
            for rows, _, heads in chunks():
                for k, x in enumerate(heads):
                    y = _head_norm(x, nw_ref[...])
                    if not prompt:
                        y = _rope(y, cos_ref[rows, :], sin_ref[rows, :])
                    q_ref[rows, head_cols(k)] = y.astype(BF16)
        per_group(run)

    elif kind == "kva":
        nw_ref, cos_ref, sin_ref, kv_ref, kp_ref, vp_ref = refs

        def run(prompt):
            for rows, _, heads in chunks():
                for k in range(A_KV_HEADS):
                    y = _head_norm(heads[k], nw_ref[...])
                    v = heads[A_KV_HEADS + k]
                    if prompt:
                        kp_ref[rows, head_cols(k)] = y
                        vp_ref[rows, head_cols(k)] = v
                    else:
                        y = _rope(y, cos_ref[rows, :], sin_ref[rows, :])
                    kv_ref[rows, head_cols(k)] = y.astype(BF16)
                    kv_ref[rows, head_cols(k, A_KV)] = v.astype(BF16)
        per_group(run)

    elif kind == "qb":
        nw_ref, q_ref = refs
        for rows, _, heads in chunks():
            for k, x in enumerate(heads):
                q_ref[rows, head_cols(k)] = _head_norm(x, nw_ref[...]).astype(BF16)

    elif kind == "kb":
        nw_ref, k_ref, kp_ref = refs

        def run(prompt):
            for rows, _, heads in chunks():
                for k, x in enumerate(heads):
                    y = _head_norm(x, nw_ref[...])
                    if prompt:
                        kp_ref[rows, head_cols(k)] = y
                    k_ref[rows, head_cols(k)] = y.astype(BF16)
        per_group(run)

    elif kind == "vb":
        v_ref, vp_ref = refs

        def run(prompt):
            for rows, acc, _ in chunks():
                if prompt:
                    vp_ref[rows, :] = acc
                v_ref[rows, :] = acc.astype(BF16)
        per_group(run)

    elif kind == "gate":
        (g_ref,) = refs
        for rows, acc, _ in chunks():
            g_ref[rows, :] = jax.nn.sigmoid(acc)

    else:
        raise ValueError(kind)


def _proj_call(h, w_in, kind, col0, width, tn, norm_w=None, rope=None):
    assert col0 % W_SUB == 0 and tn % W_SUB == 0 and width % tn == 0
    nj = width // tn
    nsub = tn // W_SUB
    c0 = col0 // W_SUB
    p_tiles = M_PROMPT // TM
    s_tiles = DEC_SEQ // TM

    w_mode = dict(pipeline_mode=pl.Buffered(1)) if nj == 1 else {}
    in_specs = [pl.BlockSpec((TM, D_MODEL), lambda j, i: (i, 0))]
    in_specs += [pl.BlockSpec((D_MODEL, W_SUB), lambda j, i, k=k: (0, c0 + nsub * j + k), **w_mode)
                 for k in range(nsub)]
    args = [h] + [w_in] * nsub
    if norm_w is not None:
        in_specs.append(pl.BlockSpec((1, HEAD_DIM), lambda j, i: (0, 0)))
        args.append(norm_w)
    if rope is not None:
        rope_spec = pl.BlockSpec((TM, HEAD_DIM), lambda j, i: (jnp.maximum(i - p_tiles, 0) % s_tiles, 0))
        in_specs += [rope_spec, rope_spec]
        args += list(rope)

    full = pl.BlockSpec((TM, tn), lambda j, i: (i, j))
    parked = lambda w: pl.BlockSpec((TM, w), lambda j, i: (jnp.minimum(i, p_tiles - 1), j))
    if kind in ("qa", "qb"):
        out_specs = full
        out_shape = jax.ShapeDtypeStruct((M_ALL, width), BF16)
    elif kind == "kva":
        out_specs = [full, parked(A_KV), parked(A_KV)]
        out_shape = [jax.ShapeDtypeStruct((M_ALL, width), BF16),
                     jax.ShapeDtypeStruct((M_PROMPT, A_KV), F32),
                     jax.ShapeDtypeStruct((M_PROMPT, A_KV), F32)]
    elif kind in ("kb", "vb"):
        out_specs = [full, parked(tn)]
        out_shape = [jax.ShapeDtypeStruct((M_ALL, width), BF16),
                     jax.ShapeDtypeStruct((M_PROMPT, width), F32)]
    elif kind == "gate":
        out_specs = full
        out_shape = jax.ShapeDtypeStruct((M_ALL, width), F32)
    else:
        raise ValueError(kind)

    return pl.pallas_call(
        functools.partial(_proj_kernel, kind=kind, tn=tn),
        grid=(nj, M_ALL // TM),
        in_specs=in_specs,
        out_specs=out_specs,
        out_shape=out_shape,
        scratch_shapes=[pltpu.VMEM((D_MODEL, tn), BF16)],
        compiler_params=_params(("arbitrary", "arbitrary"), VMEM_LIMIT),
        name="proj_" + kind,
    )(*args)


def _softmax_pv(parts, sink):
    m = functools.reduce(jnp.maximum, [jnp.max(s, axis=-1, keepdims=True) for s, _ in parts])
    if sink is not None:
        m = jnp.maximum(m, sink)
    ps = [jnp.exp(s - m) for s, _ in parts]
    l = functools.reduce(jnp.add, [jnp.sum(p, axis=-1, keepdims=True) for p in ps])
    if sink is not None:
        l = l + jnp.exp(sink - m)
    o = functools.reduce(jnp.add, [_dot(p.astype(BF16), v) for p, (_, v) in zip(ps, parts)])
    return o / l


def _sink_column(sink_ref, hk, rows_per_head):
    n = A_GROUP * rows_per_head
    g = lax.broadcasted_iota(jnp.int32, (n, 1), 0) // rows_per_head
    col = jnp.full((n, 1), sink_ref[hk * A_GROUP], F32)
    for k in range(1, A_GROUP):
        col = jnp.where(g == k, sink_ref[hk * A_GROUP + k], col)
    return col


def _stack_group(q_ref, hk):
    return jnp.concatenate(
        [q_ref[:, (hk * A_GROUP + g) * HEAD_DIM:(hk * A_GROUP + g + 1) * HEAD_DIM] for g in range(A_GROUP)], axis=0)


def _unstack_group(o_ref, hk, o, rows):
    for g in range(A_GROUP):
        c = (hk * A_GROUP + g) * HEAD_DIM
        o_ref[:, c:c + HEAD_DIM] = o[g * rows:(g + 1) * rows].astype(BF16)


def _ctx_attn_kernel(sink_ref, qa_ref, kva_ref, qb_ref, kb_ref, vb_ref, oa_ref, ob_ref):
    for hk in range(A_KV_HEADS):
        k = kva_ref[:, hk * HEAD_DIM:(hk + 1) * HEAD_DIM]
        v = kva_ref[:, A_KV + hk * HEAD_DIM:A_KV + (hk + 1) * HEAD_DIM]
        q4 = _stack_group(qa_ref, hk)
        s = _dot_nt(q4, k) * SCALE
        o = _softmax_pv([(s, v)], _sink_column(sink_ref, hk, SEQ))
        _unstack_group(oa_ref, hk, o, SEQ)
    for h in range(B_HEADS):
        sl = slice(h * HEAD_DIM, (h + 1) * HEAD_DIM)
        s = _dot_nt(qb_ref[:, sl], kb_ref[:, sl]) * SCALE
        ob_ref[:, sl] = _softmax_pv([(s, vb_ref[:, sl])], None).astype(BF16)


def _ctx_attn_call(sink, qa, kva, qb, kb, vb):
    row = lambda w: pl.BlockSpec((SEQ, w), lambda b: (b, 0))
    return pl.pallas_call(
        _ctx_attn_kernel,
        grid=(BATCH,),
        in_specs=[pl.BlockSpec(memory_space=pltpu.SMEM),
                  row(A_Q), row(2 * A_KV), row(B_QKV), row(B_QKV), row(B_QKV)],
        out_specs=[row(A_Q), row(B_QKV)],
        out_shape=[jax.ShapeDtypeStruct((M_PROMPT, A_Q), BF16),
                   jax.ShapeDtypeStruct((M_PROMPT, B_QKV), BF16)],
        compiler_params=_params(("arbitrary",), VMEM_LIMIT),
        name="attn_ctx",
    )(sink, qa, kva, qb, kb, vb)


BAND = 3 * BLOCK


def _win_attn_kernel(sink_ref, q_ref, kv_ref, ck_ref, cv_ref, o_ref):
    n = pl.program_id(1)
    start = pl.multiple_of(jnp.clip((n - 1) * BLOCK, 0, DEC_SEQ - BAND), BLOCK)
    qpos = n * BLOCK + lax.broadcasted_iota(jnp.int32, (BLOCK, BAND), 0)
    kpos = start + lax.broadcasted_iota(jnp.int32, (BLOCK, BAND), 1)
    valid = jnp.abs(qpos - kpos) <= A_WINDOW
    valid = jnp.concatenate([valid.astype(jnp.int32)] * A_GROUP, axis=0) > 0
    for hk in range(A_KV_HEADS):
        sl = slice(hk * HEAD_DIM, (hk + 1) * HEAD_DIM)
        slv = slice(A_KV + hk * HEAD_DIM, A_KV + (hk + 1) * HEAD_DIM)
        k_loc = kv_ref[pl.ds(start, BAND), sl]
        v_loc = kv_ref[pl.ds(start, BAND), slv]
        k_ctx = ck_ref[:, sl].astype(BF16)
        v_ctx = cv_ref[:, sl].astype(BF16)
        q4 = _stack_group(q_ref, hk)
        s_loc = jnp.where(valid, _dot_nt(q4, k_loc) * SCALE, NEG)
        s_ctx = _dot_nt(q4, k_ctx) * SCALE
        o = _softmax_pv([(s_loc, v_loc), (s_ctx, v_ctx)], _sink_column(sink_ref, hk, BLOCK))
        _unstack_group(o_ref, hk, o, BLOCK)


def _win_attn_call(sink, qa, kva, cache_k, cache_v):
    nb = DEC_SEQ // BLOCK
    q0 = M_PROMPT // BLOCK
    b0 = M_PROMPT // DEC_SEQ
    return pl.pallas_call(
        _win_attn_kernel,
        grid=(DEC_BATCH, nb),
        in_specs=[pl.BlockSpec(memory_space=pltpu.SMEM),
                  pl.BlockSpec((BLOCK, A_Q), lambda b, n: (q0 + b * nb + n, 0)),
                  pl.BlockSpec((DEC_SEQ, 2 * A_KV), lambda b, n: (b0 + b, 0)),
                  pl.BlockSpec((None, PAST_LEN, A_KV), lambda b, n: (b, 0, 0)),
                  pl.BlockSpec((None, PAST_LEN, A_KV), lambda b, n: (b, 0, 0))],
        out_specs=pl.BlockSpec((BLOCK, A_Q), lambda b, n: (b * nb + n, 0)),
        out_shape=jax.ShapeDtypeStruct((M_SAMPLE, A_Q), BF16),
        compiler_params=_params(("arbitrary", "arbitrary"), VMEM_LIMIT),
        name="attn_window",
    )(sink, qa, kva, cache_k, cache_v)


NA_QROWS = 4
NA_WIN_ROWS = 12
NA_Q = NA_QROWS * GRID_W
NA_KEYS = NA_WIN_ROWS * GRID_W
NA_DR = 2 * NA_ROWS_MAX - 1
NA_PAIRS = NA_DR + 1
RPB_W = 2 * NA_COLS - 1
RPB_PAD = GRID_W - NA_COLS


def _na_row_start(r):
    return jnp.clip(r - NA_ROWS // 2, 0, GRID_ROWS - NA_ROWS)


def _na_window_row0(blk):
    return jnp.clip(blk * NA_QROWS - NA_ROWS // 2, 0, GRID_ROWS - NA_WIN_ROWS)


def _check_na_windows():
    for blk in range(GRID_ROWS // NA_QROWS):
        w0 = int(np.clip(blk * NA_QROWS - NA_ROWS // 2, 0, GRID_ROWS - NA_WIN_ROWS))
        for r in range(blk * NA_QROWS, (blk + 1) * NA_QROWS):
            r0 = int(np.clip(r - NA_ROWS // 2, 0, GRID_ROWS - NA_ROWS))
            assert w0 <= r0 and r0 + NA_ROWS <= w0 + NA_WIN_ROWS, (blk, r)


_check_na_windows()


def _na_attn_kernel(q_ref, k_ref, v_ref, ck_ref, cv_ref, rpb_ref, o_ref, tab_ref):
    b, blk = pl.program_id(0), pl.program_id(1)
    lane = lax.broadcasted_iota(jnp.int32, (GRID_W, 2 * GRID_W), 1)

    @pl.when((b == 0) & (blk == 0))
    def _():
        for h in range(B_HEADS):
            for d in range(NA_PAIRS):
                lo = jnp.broadcast_to(rpb_ref[h, d:d + 1, :], (GRID_W, 2 * GRID_W))
                hi = jnp.broadcast_to(rpb_ref[h, d + 1:d + 2, :], (GRID_W, 2 * GRID_W))
                lo = pltpu.roll(lo, GRID_W + 1, 1, stride=1, stride_axis=0)
                hi = pltpu.roll(hi, 1, 1, stride=1, stride_axis=0)
                tab_ref[h, d] = jnp.where(lane < GRID_W, lo, hi)

    row0 = _na_window_row0(blk)
    k0 = pl.multiple_of(row0 * GRID_W, GRID_W)
    qi = lax.broadcasted_iota(jnp.int32, (NA_Q, NA_KEYS), 0)
    ki = lax.broadcasted_iota(jnp.int32, (NA_Q, NA_KEYS), 1)
    qrow, qcol = blk * NA_QROWS + qi // GRID_W, qi % GRID_W
    krow, kcol = row0 + ki // GRID_W, ki % GRID_W
    rstart = _na_row_start(qrow)
    cstart = jnp.clip(qcol - NA_COLS // 2, 0, GRID_W - NA_COLS)
    valid = (krow >= rstart) & (krow < rstart + NA_ROWS) & (kcol >= cstart) & (kcol < cstart + NA_COLS)

    for h in range(B_HEADS):
        sl = slice(h * HEAD_DIM, (h + 1) * HEAD_DIM)
        bias = jnp.concatenate(
            [jnp.concatenate(
                [tab_ref[h, jnp.clip(row0 + 2 * p - blk * NA_QROWS - q + NA_ROWS_MAX, 0, NA_PAIRS - 1)]
                 for p in range(NA_WIN_ROWS // 2)], axis=1)
             for q in range(NA_QROWS)], axis=0)
        k_loc = k_ref[pl.ds(k0, NA_KEYS), sl]
        v_loc = v_ref[pl.ds(k0, NA_KEYS), sl]
        s_loc = jnp.where(valid, _dot_nt(q_ref[:, sl], k_loc) * SCALE + bias, NEG)
        s_ctx = _dot_nt(q_ref[:, sl], ck_ref[:, sl].astype(BF16)) * SCALE
        o = _softmax_pv([(s_loc, v_loc), (s_ctx, cv_ref[:, sl].astype(BF16))], None)
        o_ref[:, sl] = o.astype(BF16)


def _na_bias_rows(rpb):
    rows = jnp.pad(rpb, ((0, 0), (1, 1), (0, 0)))
    left = jnp.broadcast_to(rows[..., :1], rows.shape[:2] + (RPB_PAD,))
    right = jnp.broadcast_to(rows[..., -1:], rows.shape[:2] + (2 * GRID_W - RPB_PAD - RPB_W,))
    return jnp.concatenate([left, rows, right], axis=-1)


def _na_attn_call(qb, kb, vb, cache_k, cache_v, rpb):
    nblk = GRID_ROWS // NA_QROWS
    q0 = M_PROMPT // NA_Q
    b0 = M_PROMPT // DEC_SEQ
    return pl.pallas_call(
        _na_attn_kernel,
        grid=(DEC_BATCH, nblk),
        in_specs=[pl.BlockSpec((NA_Q, B_QKV), lambda b, r: (q0 + b * nblk + r, 0)),
                  pl.BlockSpec((DEC_SEQ, B_QKV), lambda b, r: (b0 + b, 0)),
                  pl.BlockSpec((DEC_SEQ, B_QKV), lambda b, r: (b0 + b, 0)),
                  pl.BlockSpec((None, PAST_LEN, B_QKV), lambda b, r: (b, 0, 0)),
                  pl.BlockSpec((None, PAST_LEN, B_QKV), lambda b, r: (b, 0, 0)),
                  pl.BlockSpec((B_HEADS, NA_DR + 2, 2 * GRID_W), lambda b, r: (0, 0, 0))],
        out_specs=pl.BlockSpec((NA_Q, B_QKV), lambda b, r: (b * nblk + r, 0)),
        out_shape=jax.ShapeDtypeStruct((M_SAMPLE, B_QKV), BF16),
        scratch_shapes=[pltpu.VMEM((B_HEADS, NA_PAIRS, GRID_W, 2 * GRID_W), F32)],
        compiler_params=_params(("arbitrary", "arbitrary"), VMEM_LIMIT),
        name="attn_neighbourhood",
    )(qb, kb, vb, cache_k, cache_v, _na_bias_rows(rpb))


def _merge_kernel(oap, oas, obp, obs, wa_ref, wb_ref, ga_ref, gb_ref, z_ref, wa_bf, wb_bf):
    i = pl.program_id(1)
    p_tiles = M_PROMPT // TM

    @pl.when(i == 0)
    def _():
        wa_bf[...] = wa_ref[...].astype(BF16)
        wb_bf[...] = wb_ref[...].astype(BF16)

    def run(oa_ref, ob_ref):
        for r0 in range(0, TM, ROW_CHUNK):
            rows = slice(r0, r0 + ROW_CHUNK)
            ya = _dot(oa_ref[rows, :], wa_bf[...])
            yb = _dot(ob_ref[rows, :], wb_bf[...])
            z_ref[rows, :] = (ga_ref[rows, :] * ya + gb_ref[rows, :] * yb).astype(BF16)

    pl.when(i < p_tiles)(functools.partial(run, oap, obp))
    pl.when(i >= p_tiles)(functools.partial(run, oas, obs))


def _merge_call(oa_p, oa_s, ob_p, ob_s, w_br_a, w_br_b, gates):
    p_tiles = M_PROMPT // TM
    nj = D_MODEL // TN
    pspec = pl.BlockSpec((TM, A_Q), lambda j, i: (jnp.minimum(i, p_tiles - 1), 0))
    sspec = pl.BlockSpec((TM, A_Q), lambda j, i: (jnp.maximum(i - p_tiles, 0), 0))
    once = dict(pipeline_mode=pl.Buffered(1))
    return pl.pallas_call(
        _merge_kernel,
        grid=(nj, M_ALL // TM),
        in_specs=[pspec, sspec, pspec, sspec,
                  pl.BlockSpec((A_Q, TN), lambda j, i: (0, j), **once),
                  pl.BlockSpec((B_QKV, TN), lambda j, i: (0, j), **once),
                  pl.BlockSpec((TM, TN), lambda j, i: (i, j)),
                  pl.BlockSpec((TM, TN), lambda j, i: (i, nj + j))],
        out_specs=pl.BlockSpec((TM, TN), lambda j, i: (i, j)),
        out_shape=jax.ShapeDtypeStruct((M_ALL, D_MODEL), BF16),
        scratch_shapes=[pltpu.VMEM((A_Q, TN), BF16), pltpu.VMEM((B_QKV, TN), BF16)],
        compiler_params=_params(("arbitrary", "arbitrary"), VMEM_LIMIT),
        name="merge_branches",
    )(oa_p, oa_s, ob_p, ob_s, w_br_a, w_br_b, gates, gates)


def _out_kernel(z_ref, w_ref, xp_ref, xs_ref, g_ref, o_ref, wbf):
    i = pl.program_id(1)
    p_tiles = M_PROMPT // TM

    @pl.when(i == 0)
    def _():
        wbf[...] = w_ref[...].astype(BF16)

    def run(x_ref):
        for r0 in range(0, TM, ROW_CHUNK):
            rows = slice(r0, r0 + ROW_CHUNK)
            o_ref[rows, :] = x_ref[rows, :] + g_ref[...] * _dot(z_ref[rows, :], wbf[...])

    pl.when(i < p_tiles)(functools.partial(run, xp_ref))
    pl.when(i >= p_tiles)(functools.partial(run, xs_ref))


def _out_call(z, w_out, xp, xs, mods3):
    p_tiles = M_PROMPT // TM
    nj = D_MODEL // TN
    return pl.pallas_call(
        _out_kernel,
        grid=(nj, M_ALL // TM),
        in_specs=[pl.BlockSpec((TM, D_MODEL), lambda j, i: (i, 0)),
                  pl.BlockSpec((D_MODEL, TN), lambda j, i: (0, j), pipeline_mode=pl.Buffered(1)),
                  pl.BlockSpec((TM, TN), lambda j, i: (jnp.minimum(i, p_tiles - 1), j)),
                  pl.BlockSpec((TM, TN), lambda j, i: (jnp.maximum(i - p_tiles, 0), j)),
                  pl.BlockSpec((None, 1, TN), lambda j, i: (_mod_row(i, TM), 0, 2 * nj + j))],
        out_specs=pl.BlockSpec((TM, TN), lambda j, i: (i, j)),
        out_shape=jax.ShapeDtypeStruct((M_ALL, D_MODEL), F32),
        scratch_shapes=[pltpu.VMEM((D_MODEL, TN), BF16)],
        compiler_params=_params(("arbitrary", "arbitrary"), VMEM_LIMIT),
        name="out_proj_residual",
    )(z, w_out, xp, xs, mods3)


def _mlp_kernel(x_ref, nw_ref, sh_ref, sc_ref, g_ref, wu_ref, wd_ref, o_ref, h_ref):
    f = pl.program_id(1)

    @pl.when(f == 0)
    def _():
        h_ref[...] = _modnorm(x_ref[...], nw_ref[...], sc_ref[...], sh_ref[...]).astype(BF16)
        o_ref[...] = jnp.zeros_like(o_ref)

    u = _dot(h_ref[...], wu_ref[...].astype(BF16))
    u = jnp.square(jnp.maximum(u, 0.0)).astype(BF16)
    for c0 in range(0, D_MODEL, MLP_COLS):
        cols = slice(c0, c0 + MLP_COLS)
        o_ref[:, cols] += _dot(u, wd_ref[:, cols].astype(BF16))

    @pl.when(f == pl.num_programs(1) - 1)
    def _():
        o_ref[...] = x_ref[...] + g_ref[...] * o_ref[...]


def _mlp_call(x1, nw, mods3, w_up, w_down, tile0, n_tiles):
    mod = lambda k: pl.BlockSpec((None, 1, D_MODEL), lambda i, f: (_mod_row(tile0 + i, TM), 0, k))
    return pl.pallas_call(
        _mlp_kernel,
        grid=(n_tiles, D_FF // TF),
        in_specs=[pl.BlockSpec((TM, D_MODEL), lambda i, f: (tile0 + i, 0), pipeline_mode=pl.Buffered(1)),
                  pl.BlockSpec((1, D_MODEL), lambda i, f: (0, 0)),
                  mod(3), mod(4), mod(5),
                  pl.BlockSpec((D_MODEL, TF), lambda i, f: (0, f)),
                  pl.BlockSpec((TF, D_MODEL), lambda i, f: (f, 0))],
        out_specs=pl.BlockSpec((TM, D_MODEL), lambda i, f: (i, 0)),
        out_shape=jax.ShapeDtypeStruct((n_tiles * TM, D_MODEL), F32),
        scratch_shapes=[pltpu.VMEM((TM, D_MODEL), BF16)],
        compiler_params=_params(("arbitrary", "arbitrary"), VMEM_LIMIT),
        name="mlp",
    )(x1, nw, mods3, mods3, mods3, w_up, w_down)


def kernel(x_prompt, x_sample, cache_a_k, cache_a_v, cache_b_k, cache_b_v, c, c_ctx, norm1_w, norm2_w, w_ada, b_ada, w_in, q_norm_a, k_norm_a, q_norm_b, k_norm_b, sink_a, rpb_b, w_br_a, w_br_b, w_out, w_up, w_down):
    assert w_ada.shape[0] == 1, "one trunk layer"
    xp = x_prompt.reshape(M_PROMPT, D_MODEL)
    xs = x_sample.reshape(M_SAMPLE, D_MODEL)

    cvecs = jnp.concatenate([c_ctx[None, :], c, jnp.zeros((8 - 1 - DEC_BATCH, D_MODEL), F32)], axis=0)
    mods = _ada_call(cvecs, w_ada[0], b_ada)
    mods3 = mods.reshape(8, 1, 6 * D_MODEL)

    h = _norm_call(xp, xs, norm1_w, mods3)

    w = w_in[0]
    rope = _rope_tables()
    qa = _proj_call(h, w, "qa", COL_QA, A_Q, TN, q_norm_a, rope)
    kva, ka_p, va_p = _proj_call(h, w, "kva", COL_KVA, 2 * A_KV, 2 * A_KV, k_norm_a, rope)
    qb = _proj_call(h, w, "qb", COL_QB, B_QKV, TN, q_norm_b)
    kb, kb_p = _proj_call(h, w, "kb", COL_KB, B_QKV, TN, k_norm_b)
    vb, vb_p = _proj_call(h, w, "vb", COL_VB, B_QKV, TN)
    gates = _proj_call(h, w, "gate", COL_G, 2 * D_MODEL, TN)

    sink = sink_a[0]
    oa_p, ob_p = _ctx_attn_call(sink, qa, kva, qb, kb, vb)
    oa_s = _win_attn_call(sink, qa, kva,
                          cache_a_k.reshape(DEC_BATCH, PAST_LEN, A_KV), cache_a_v.reshape(DEC_BATCH, PAST_LEN, A_KV))
    ob_s = _na_attn_call(qb, kb, vb,
                         cache_b_k.reshape(DEC_BATCH, PAST_LEN, B_QKV), cache_b_v.reshape(DEC_BATCH, PAST_LEN, B_QKV),
                         rpb_b[0])

    z = _merge_call(oa_p, oa_s, ob_p, ob_s, w_br_a[0], w_br_b[0], gates)
    x1 = _out_call(z, w_out[0], xp, xs, mods3)

    p_tiles = M_PROMPT // TM
    y_p = _mlp_call(x1, norm2_w, mods3, w_up[0], w_down[0], 0, p_tiles)
    y_s = _mlp_call(x1, norm2_w, mods3, w_up[0], w_down[0], p_tiles, M_SAMPLE // TM)

    return (y_p.reshape(BATCH, SEQ, D_MODEL),
            y_s.reshape(DEC_BATCH, DEC_SEQ, D_MODEL),
            ka_p.reshape(BATCH, 1, SEQ, A_KV_HEADS, HEAD_DIM),
            va_p.reshape(BATCH, 1, SEQ, A_KV_HEADS, HEAD_DIM),
            kb_p.reshape(BATCH, 1, SEQ, B_HEADS, HEAD_DIM),
            vb_p.reshape(BATCH, 1, SEQ, B_HEADS, HEAD_DIM))
```

```python
import functools

import numpy as np
import jax
import jax.numpy as jnp
from jax import lax
from jax.experimental import pallas as pl
from jax.experimental.pallas import tpu as pltpu

D_MODEL = 2048
BATCH = 16
SEQ = 256
DEC_BATCH = 2
DEC_SEQ = 1024
PAST_LEN = 256
GRID_W = 64
HEAD_DIM = 128
A_HEADS = 8
A_KV_HEADS = 2
A_GROUP = A_HEADS // A_KV_HEADS
A_WINDOW = 128
BLOCK = 128
B_HEADS = 8
NA_ROWS_MAX = 8
NA_COLS = 16
D_FF = 4 * D_MODEL
ROPE_THETA = 10000.0
EPS = 1e-6
NEG = -1e30
A_Q = A_HEADS * HEAD_DIM
A_KV = A_KV_HEADS * HEAD_DIM
B_QKV = B_HEADS * HEAD_DIM
IN_WIDTH = A_Q + 2 * A_KV + 3 * B_QKV + 2 * D_MODEL
SCALE = HEAD_DIM ** -0.5

M_PROMPT = BATCH * SEQ
M_SAMPLE = DEC_BATCH * DEC_SEQ
M_ALL = M_PROMPT + M_SAMPLE
GRID_ROWS = DEC_SEQ // GRID_W
NA_ROWS = min(NA_ROWS_MAX, GRID_ROWS)

COL_QA = 0
COL_KVA = A_Q
COL_QB = A_Q + 2 * A_KV
COL_KB = COL_QB + B_QKV
COL_VB = COL_KB + B_QKV
COL_G = COL_VB + B_QKV

TM = 1024
TN = 1024
ROW_CHUNK = 256
W_SUB = 512
TM_NORM = 512
TF = 512
MLP_COLS = 512
VMEM_LIMIT = 56 * 1024 * 1024

F32 = jnp.float32
BF16 = jnp.bfloat16


def _mod_row(i, tm):
    p_tiles = M_PROMPT // tm
    return jnp.where(i < p_tiles, 0, 1 + (i - p_tiles) // (DEC_SEQ // tm))


def _dot(a, b):
    return jnp.dot(a, b, preferred_element_type=F32)


def _dot_nt(a, b):
    return lax.dot_general(a, b, (((1,), (1,)), ((), ())), preferred_element_type=F32)


def _params(sem, vmem=None):
    return pltpu.CompilerParams(dimension_semantics=sem, vmem_limit_bytes=vmem)


def _ada_kernel(c_ref, w_ref, b_ref, o_ref):
    cv = c_ref[...]
    s = (cv * jax.nn.sigmoid(cv)).astype(BF16)
    o_ref[...] = _dot(s, w_ref[...].astype(BF16)) + b_ref[...]


def _ada_call(cvecs, w_ada, b_ada):
    tn = 1024
    n = 6 * D_MODEL
    return pl.pallas_call(
        _ada_kernel,
        grid=(n // tn,),
        in_specs=[pl.BlockSpec((8, D_MODEL), lambda j: (0, 0)),
                  pl.BlockSpec((D_MODEL, tn), lambda j: (0, j)),
                  pl.BlockSpec((1, tn), lambda j: (0, j))],
        out_specs=pl.BlockSpec((8, tn), lambda j: (0, j)),
        out_shape=jax.ShapeDtypeStruct((8, n), F32),
        compiler_params=_params(("arbitrary",), VMEM_LIMIT),
        name="ada_mod",
    )(cvecs, w_ada, b_ada)


def _modnorm(x, nw, sc, sh):
    y = x * lax.rsqrt(jnp.mean(x * x, axis=-1, keepdims=True) + EPS)
    return (y * nw) * (1.0 + sc) + sh


def _head_norm(x, nw):
    return x * lax.rsqrt(jnp.mean(x * x, axis=-1, keepdims=True) + EPS) * nw


def _rope(x, cos, sin_signed):
    lane = lax.broadcasted_iota(jnp.int32, x.shape, 1)
    partner = jnp.where((lane % 64) < 32, pltpu.roll(x, 96, 1), pltpu.roll(x, 32, 1))
    return x * cos + partner * sin_signed


def _norm_kva_kernel(xp_ref, xs_ref, nw1_ref, sh_ref, sc_ref, w_ref, nwk_ref, cos_ref, sin_ref,
                     h_ref, kv_ref, kp_ref, vp_ref, wbf):
    i = pl.program_id(0)
    p_tiles = M_PROMPT // TM_NORM

    @pl.when(i == 0)
    def _():
        wbf[...] = w_ref[...].astype(BF16)

    def head_cols(k, base=0):
        return slice(base + k * HEAD_DIM, base + (k + 1) * HEAD_DIM)

    def run(prompt):
        x_ref = xp_ref if prompt else xs_ref
        for r0 in range(0, TM_NORM, ROW_CHUNK):
            rows = slice(r0, r0 + ROW_CHUNK)
            h = _modnorm(x_ref[rows, :], nw1_ref[...], sc_ref[...], sh_ref[...]).astype(BF16)
            h_ref[rows, :] = h
            acc = _dot(h, wbf[...])
            for k in range(A_KV_HEADS):
                y = _head_norm(acc[:, head_cols(k)], nwk_ref[...])
                v = acc[:, head_cols(k, A_KV)]
                if prompt:
                    kp_ref[rows, head_cols(k)] = y
                    vp_ref[rows, head_cols(k)] = v
                else:
                    y = _rope(y, cos_ref[rows, :], sin_ref[rows, :])
                kv_ref[rows, head_cols(k)] = y.astype(BF16)
                kv_ref[rows, head_cols(k, A_KV)] = v.astype(BF16)

    pl.when(i < p_tiles)(functools.partial(run, True))
    pl.when(i >= p_tiles)(functools.partial(run, False))


def _norm_kva_call(xp, xs, nw1, mods3, w_in, nwk, rope):
    tm = TM_NORM
    p_tiles = M_PROMPT // tm
    s_tiles = DEC_SEQ // tm
    assert COL_KVA % (2 * A_KV) == 0
    rope_spec = pl.BlockSpec((tm, HEAD_DIM), lambda i: (jnp.maximum(i - p_tiles, 0) % s_tiles, 0))
    parked = pl.BlockSpec((tm, A_KV), lambda i: (jnp.minimum(i, p_tiles - 1), 0))
    return pl.pallas_call(
        _norm_kva_kernel,
        grid=(M_ALL // tm,),
        in_specs=[pl.BlockSpec((tm, D_MODEL), lambda i: (jnp.minimum(i, p_tiles - 1), 0)),
                  pl.BlockSpec((tm, D_MODEL), lambda i: (jnp.maximum(i - p_tiles, 0), 0)),
                  pl.BlockSpec((1, D_MODEL), lambda i: (0, 0)),
                  pl.BlockSpec((None, 1, D_MODEL), lambda i: (_mod_row(i, tm), 0, 0)),
                  pl.BlockSpec((None, 1, D_MODEL), lambda i: (_mod_row(i, tm), 0, 1)),
                  pl.BlockSpec((D_MODEL, 2 * A_KV), lambda i: (0, COL_KVA // (2 * A_KV)),
                               pipeline_mode=pl.Buffered(1)),
                  pl.BlockSpec((1, HEAD_DIM), lambda i: (0, 0)),
                  rope_spec, rope_spec],
        out_specs=[pl.BlockSpec((tm, D_MODEL), lambda i: (i, 0)),
                   pl.BlockSpec((tm, 2 * A_KV), lambda i: (i, 0)),
                   parked, parked],
        out_shape=[jax.ShapeDtypeStruct((M_ALL, D_MODEL), BF16),
                   jax.ShapeDtypeStruct((M_ALL, 2 * A_KV), BF16),
                   jax.ShapeDtypeStruct((M_PROMPT, A_KV), F32),
                   jax.ShapeDtypeStruct((M_PROMPT, A_KV), F32)],
        scratch_shapes=[pltpu.VMEM((D_MODEL, 2 * A_KV), BF16)],
        compiler_params=_params(("arbitrary",), VMEM_LIMIT),
        name="norm1_proj_kva",
    )(xp, xs, nw1, mods3, mods3, w_in, nwk, *rope)


def _rope_tables():
    n_freq = HEAD_DIM // 4
    pos = np.arange(DEC_SEQ)
    row = (pos // GRID_W).astype(np.float64)
    col = (pos % GRID_W).astype(np.float64)
    inv = ROPE_THETA ** (-np.arange(n_freq, dtype=np.float64) / n_freq)
    ar = row[:, None] * inv
    ac = col[:, None] * inv
    cos = np.concatenate([np.cos(ar), np.cos(ar), np.cos(ac), np.cos(ac)], axis=-1)
    sin = np.concatenate([-np.sin(ar), np.sin(ar), -np.sin(ac), np.sin(ac)], axis=-1)
    return jnp.asarray(cos, F32), jnp.asarray(sin, F32)


def _proj_kernel(*refs, kind, tn):
    nsub = tn // W_SUB
    h_ref, w_refs, refs, wbf = refs[0], refs[1:1 + nsub], refs[1 + nsub:-1], refs[-1]
    i = pl.program_id(1)
    is_prompt = i < M_PROMPT // TM

    @pl.when(i == 0)
    def _():
        for k, w_ref in enumerate(w_refs):
            wbf[:, k * W_SUB:(k + 1) * W_SUB] = w_ref[...].astype(BF16)

    def chunks():
        for r0 in range(0, TM, ROW_CHUNK):
            rows = slice(r0, r0 + ROW_CHUNK)
            acc = _dot(h_ref[rows, :], wbf[...])
            yield rows, acc, [acc[:, a:a + HEAD_DIM] for a in range(0, tn, HEAD_DIM)]

    def head_cols(k, base=0):
        return slice(base + k * HEAD_DIM, base + (k + 1) * HEAD_DIM)

    def per_group(run):
        pl.when(is_prompt)(functools.partial(run, True))
        pl.when(jnp.logical_not(is_prompt))(functools.partial(run, False))

    if kind == "qa":
        nw_ref, cos_ref, sin_ref, q_ref = refs

        def run(prompt):
            for rows, _, heads in chunks():
                for k, x in enumerate(heads):
                    y = _head_norm(x, nw_ref[...])
                    if not prompt:
                        y = _rope(y, cos_ref[rows, :], sin_ref[rows, :])
                    q_ref[rows, head_cols(k)] = y.astype(BF16)
        per_group(run)

    elif kind == "qb":
        nw_ref, q_ref = refs
        for rows, _, heads in chunks():
            for k, x in enumerate(heads):
                q_ref[rows, head_cols(k)] = _head_norm(x, nw_ref[...]).astype(BF16)

    elif kind == "kb":
        nw_ref, k_ref, kp_ref = refs

        def run(prompt):
            for rows, _, heads in chunks():
                for k, x in enumerate(heads):
                    y = _head_norm(x, nw_ref[...])
                    if prompt:
                        kp_ref[rows, head_cols(k)] = y
                    k_ref[rows, head_cols(k)] = y.astype(BF16)
        per_group(run)

    elif kind == "vb":
        v_ref, vp_ref = refs

        def run(prompt):
            for rows, acc, _ in chunks():
                if prompt:
                    vp_ref[rows, :] = acc
                v_ref[rows, :] = acc.astype(BF16)
        per_group(run)

    elif kind == "gate":
        (g_ref,) = refs
        for rows, acc, _ in chunks():
            g_ref[rows, :] = jax.nn.sigmoid(acc)

    else:
        raise ValueError(kind)


def _proj_call(h, w_in, kind, col0, width, tn, norm_w=None, rope=None):
    assert col0 % W_SUB == 0 and tn % W_SUB == 0 and width % tn == 0
    nj = width // tn
    nsub = tn // W_SUB
    c0 = col0 // W_SUB
    p_tiles = M_PROMPT // TM
    s_tiles = DEC_SEQ // TM

    w_mode = dict(pipeline_mode=pl.Buffered(1)) if nj == 1 else {}
    in_specs = [pl.BlockSpec((TM, D_MODEL), lambda j, i: (i, 0))]
    in_specs += [pl.BlockSpec((D_MODEL, W_SUB), lambda j, i, k=k: (0, c0 + nsub * j + k), **w_mode)
                 for k in range(nsub)]
    args = [h] + [w_in] * nsub
    if norm_w is not None:
        in_specs.append(pl.BlockSpec((1, HEAD_DIM), lambda j, i: (0, 0)))
        args.append(norm_w)
    if rope is not None:
        rope_spec = pl.BlockSpec((TM, HEAD_DIM), lambda j, i: (jnp.maximum(i - p_tiles, 0) % s_tiles, 0))
        in_specs += [rope_spec, rope_spec]
        args += list(rope)

    full = pl.BlockSpec((TM, tn), lambda j, i: (i, j))
    parked = lambda w: pl.BlockSpec((TM, w), lambda j, i: (jnp.minimum(i, p_tiles - 1), j))
    if kind in ("qa", "qb"):
        out_specs = full
        out_shape = jax.ShapeDtypeStruct((M_ALL, width), BF16)
    elif kind in ("kb", "vb"):
        out_specs = [full, parked(tn)]
        out_shape = [jax.ShapeDtypeStruct((M_ALL, width), BF16),
                     jax.ShapeDtypeStruct((M_PROMPT, width), F32)]
    elif kind == "gate":
        out_specs = full
        out_shape = jax.ShapeDtypeStruct((M_ALL, width), F32)
    else:
        raise ValueError(kind)

    return pl.pallas_call(
        functools.partial(_proj_kernel, kind=kind, tn=tn),
        grid=(nj, M_ALL // TM),
        in_specs=in_specs,
        out_specs=out_specs,
        out_shape=out_shape,
        scratch_shapes=[pltpu.VMEM((D_MODEL, tn), BF16)],
        compiler_params=_params(("arbitrary", "arbitrary"), VMEM_LIMIT),
        name="proj_" + kind,
    )(*args)


def _softmax_pv(parts, sink):
    m = functools.reduce(jnp.maximum, [jnp.max(s, axis=-1, keepdims=True) for s, _ in parts])
    if sink is not None:
        m = jnp.maximum(m, sink)
    ps = [jnp.exp(s - m) for s, _ in parts]
    l = functools.reduce(jnp.add, [jnp.sum(p, axis=-1, keepdims=True) for p in ps])
    if sink is not None:
        l = l + jnp.exp(sink - m)
    o = functools.reduce(jnp.add, [_dot(p.astype(BF16), v) for p, (_, v) in zip(ps, parts)])
    return o / l


def _sink_column(sink_ref, hk, rows_per_head):
    n = A_GROUP * rows_per_head
    g = lax.broadcasted_iota(jnp.int32, (n, 1), 0) // rows_per_head
    col = jnp.full((n, 1), sink_ref[hk * A_GROUP], F32)
    for k in range(1, A_GROUP):
        col = jnp.where(g == k, sink_ref[hk * A_GROUP + k], col)
    return col


def _stack_group(q_ref, hk):
    return jnp.concatenate(
        [q_ref[:, (hk * A_GROUP + g) * HEAD_DIM:(hk * A_GROUP + g + 1) * HEAD_DIM] for g in range(A_GROUP)], axis=0)


def _unstack_group(o_ref, hk, o, rows):
    for g in range(A_GROUP):
        c = (hk * A_GROUP + g) * HEAD_DIM
        o_ref[:, c:c + HEAD_DIM] = o[g * rows:(g + 1) * rows].astype(BF16)


def _ctx_attn_kernel(sink_ref, qa_ref, kva_ref, qb_ref, kb_ref, vb_ref, oa_ref, ob_ref):
    for hk in range(A_KV_HEADS):
        k = kva_ref[:, hk * HEAD_DIM:(hk + 1) * HEAD_DIM]
        v = kva_ref[:, A_KV + hk * HEAD_DIM:A_KV + (hk + 1) * HEAD_DIM]
        q4 = _stack_group(qa_ref, hk)
        s = _dot_nt(q4, k) * SCALE
        o = _softmax_pv([(s, v)], _sink_column(sink_ref, hk, SEQ))
        _unstack_group(oa_ref, hk, o, SEQ)
    for h in range(B_HEADS):
        sl = slice(h * HEAD_DIM, (h + 1) * HEAD_DIM)
        s = _dot_nt(qb_ref[:, sl], kb_ref[:, sl]) * SCALE
        ob_ref[:, sl] = _softmax_pv([(s, vb_ref[:, sl])], None).astype(BF16)


def _ctx_attn_call(sink, qa, kva, qb, kb, vb):
    row = lambda w: pl.BlockSpec((SEQ, w), lambda b: (b, 0))
    return pl.pallas_call(
        _ctx_attn_kernel,
        grid=(BATCH,),
        in_specs=[pl.BlockSpec(memory_space=pltpu.SMEM),
                  row(A_Q), row(2 * A_KV), row(B_QKV), row(B_QKV), row(B_QKV)],
        out_specs=[row(A_Q), row(B_QKV)],
        out_shape=[jax.ShapeDtypeStruct((M_PROMPT, A_Q), BF16),
                   jax.ShapeDtypeStruct((M_PROMPT, B_QKV), BF16)],
        compiler_params=_params(("arbitrary",), VMEM_LIMIT),
        name="attn_ctx",
    )(sink, qa, kva, qb, kb, vb)


BAND = 3 * BLOCK


def _win_attn_kernel(sink_ref, q_ref, kv_ref, ck_ref, cv_ref, o_ref):
    n = pl.program_id(1)
    start = pl.multiple_of(jnp.clip((n - 1) * BLOCK, 0, DEC_SEQ - BAND), BLOCK)
    qpos = n * BLOCK + lax.broadcasted_iota(jnp.int32, (BLOCK, BAND), 0)
    kpos = start + lax.broadcasted_iota(jnp.int32, (BLOCK, BAND), 1)
    valid = jnp.abs(qpos - kpos) <= A_WINDOW
    valid = jnp.concatenate([valid.astype(jnp.int32)] * A_GROUP, axis=0) > 0
    for hk in range(A_KV_HEADS):
        sl = slice(hk * HEAD_DIM, (hk + 1) * HEAD_DIM)
        slv = slice(A_KV + hk * HEAD_DIM, A_KV + (hk + 1) * HEAD_DIM)
        k_loc = kv_ref[pl.ds(start, BAND), sl]
        v_loc = kv_ref[pl.ds(start, BAND), slv]
        k_ctx = ck_ref[:, sl].astype(BF16)
        v_ctx = cv_ref[:, sl].astype(BF16)
        q4 = _stack_group(q_ref, hk)
        s_loc = jnp.where(valid, _dot_nt(q4, k_loc) * SCALE, NEG)
        s_ctx = _dot_nt(q4, k_ctx) * SCALE
        o = _softmax_pv([(s_loc, v_loc), (s_ctx, v_ctx)], _sink_column(sink_ref, hk, BLOCK))
        _unstack_group(o_ref, hk, o, BLOCK)


def _win_attn_call(sink, qa, kva, cache_k, cache_v):
    nb = DEC_SEQ // BLOCK
    q0 = M_PROMPT // BLOCK
    b0 = M_PROMPT // DEC_SEQ
    return pl.pallas_call(
        _win_attn_kernel,
        grid=(DEC_BATCH, nb),
        in_specs=[pl.BlockSpec(memory_space=pltpu.SMEM),
                  pl.BlockSpec((BLOCK, A_Q), lambda b, n: (q0 + b * nb + n, 0)),
                  pl.BlockSpec((DEC_SEQ, 2 * A_KV), lambda b, n: (b0 + b, 0)),
                  pl.BlockSpec((None, PAST_LEN, A_KV), lambda b, n: (b, 0, 0)),
                  pl.BlockSpec((None, PAST_LEN, A_KV), lambda b, n: (b, 0, 0))],
        out_specs=pl.BlockSpec((BLOCK, A_Q), lambda b, n: (b * nb + n, 0)),
        out_shape=jax.ShapeDtypeStruct((M_SAMPLE, A_Q), BF16),
        compiler_params=_params(("arbitrary", "arbitrary"), VMEM_LIMIT),
        name="attn_window",
    )(sink, qa, kva, cache_k, cache_v)


NA_QROWS = 4
NA_WIN_ROWS = 12
NA_Q = NA_QROWS * GRID_W
NA_KEYS = NA_WIN_ROWS * GRID_W
NA_DR = 2 * NA_ROWS_MAX - 1
NA_PAIRS = NA_DR + 1
RPB_W = 2 * NA_COLS - 1
RPB_PAD = GRID_W - NA_COLS


def _na_row_start(r):
    return jnp.clip(r - NA_ROWS // 2, 0, GRID_ROWS - NA_ROWS)


def _na_window_row0(blk):
    return jnp.clip(blk * NA_QROWS - NA_ROWS // 2, 0, GRID_ROWS - NA_WIN_ROWS)


def _check_na_windows():
    for blk in range(GRID_ROWS // NA_QROWS):
        w0 = int(np.clip(blk * NA_QROWS - NA_ROWS // 2, 0, GRID_ROWS - NA_WIN_ROWS))
        for r in range(blk * NA_QROWS, (blk + 1) * NA_QROWS):
            r0 = int(np.clip(r - NA_ROWS // 2, 0, GRID_ROWS - NA_ROWS))
            assert w0 <= r0 and r0 + NA_ROWS <= w0 + NA_WIN_ROWS, (blk, r)


_check_na_windows()


def _na_attn_kernel(q_ref, k_ref, v_ref, ck_ref, cv_ref, rpb_ref, o_ref, tab_ref):
    b, blk = pl.program_id(0), pl.program_id(1)
    lane = lax.broadcasted_iota(jnp.int32, (GRID_W, 2 * GRID_W), 1)

    @pl.when((b == 0) & (blk == 0))
    def _():
        for h in range(B_HEADS):
            for d in range(NA_PAIRS):
                lo = jnp.broadcast_to(rpb_ref[h, d:d + 1, :], (GRID_W, 2 * GRID_W))
                hi = jnp.broadcast_to(rpb_ref[h, d + 1:d + 2, :], (GRID_W, 2 * GRID_W))
                lo = pltpu.roll(lo, GRID_W + 1, 1, stride=1, stride_axis=0)
                hi = pltpu.roll(hi, 1, 1, stride=1, stride_axis=0)
                tab_ref[h, d] = jnp.where(lane < GRID_W, lo, hi)

    row0 = _na_window_row0(blk)
    k0 = pl.multiple_of(row0 * GRID_W, GRID_W)
    qi = lax.broadcasted_iota(jnp.int32, (NA_Q, NA_KEYS), 0)
    ki = lax.broadcasted_iota(jnp.int32, (NA_Q, NA_KEYS), 1)
    qrow, qcol = blk * NA_QROWS + qi // GRID_W, qi % GRID_W
    krow, kcol = row0 + ki // GRID_W, ki % GRID_W
    rstart = _na_row_start(qrow)
    cstart = jnp.clip(qcol - NA_COLS // 2, 0, GRID_W - NA_COLS)
    valid = (krow >= rstart) & (krow < rstart + NA_ROWS) & (kcol >= cstart) & (kcol < cstart + NA_COLS)

    for h in range(B_HEADS):
        sl = slice(h * HEAD_DIM, (h + 1) * HEAD_DIM)
        bias = jnp.concatenate(
            [jnp.concatenate(
                [tab_ref[h, jnp.clip(row0 + 2 * p - blk * NA_QROWS - q + NA_ROWS_MAX, 0, NA_PAIRS - 1)]
                 for p in range(NA_WIN_ROWS // 2)], axis=1)
             for q in range(NA_QROWS)], axis=0)
        k_loc = k_ref[pl.ds(k0, NA_KEYS), sl]
        v_loc = v_ref[pl.ds(k0, NA_KEYS), sl]
        s_loc = jnp.where(valid, _dot_nt(q_ref[:, sl], k_loc) * SCALE + bias, NEG)
        s_ctx = _dot_nt(q_ref[:, sl], ck_ref[:, sl].astype(BF16)) * SCALE
        o = _softmax_pv([(s_loc, v_loc), (s_ctx, cv_ref[:, sl].astype(BF16))], None)
        o_ref[:, sl] = o.astype(BF16)


def _na_bias_rows(rpb):
    rows = jnp.pad(rpb, ((0, 0), (1, 1), (0, 0)))
    left = jnp.broadcast_to(rows[..., :1], rows.shape[:2] + (RPB_PAD,))
    right = jnp.broadcast_to(rows[..., -1:], rows.shape[:2] + (2 * GRID_W - RPB_PAD - RPB_W,))
    return jnp.concatenate([left, rows, right], axis=-1)


def _na_attn_call(qb, kb, vb, cache_k, cache_v, rpb):
    nblk = GRID_ROWS // NA_QROWS
    q0 = M_PROMPT // NA_Q
    b0 = M_PROMPT // DEC_SEQ
    return pl.pallas_call(
        _na_attn_kernel,
        grid=(DEC_BATCH, nblk),
        in_specs=[pl.BlockSpec((NA_Q, B_QKV), lambda b, r: (q0 + b * nblk + r, 0)),
                  pl.BlockSpec((DEC_SEQ, B_QKV), lambda b, r: (b0 + b, 0)),
                  pl.BlockSpec((DEC_SEQ, B_QKV), lambda b, r: (b0 + b, 0)),
                  pl.BlockSpec((None, PAST_LEN, B_QKV), lambda b, r: (b, 0, 0)),
                  pl.BlockSpec((None, PAST_LEN, B_QKV), lambda b, r: (b, 0, 0)),
                  pl.BlockSpec((B_HEADS, NA_DR + 2, 2 * GRID_W), lambda b, r: (0, 0, 0))],
        out_specs=pl.BlockSpec((NA_Q, B_QKV), lambda b, r: (b * nblk + r, 0)),
        out_shape=jax.ShapeDtypeStruct((M_SAMPLE, B_QKV), BF16),
        scratch_shapes=[pltpu.VMEM((B_HEADS, NA_PAIRS, GRID_W, 2 * GRID_W), F32)],
        compiler_params=_params(("arbitrary", "arbitrary"), VMEM_LIMIT),
        name="attn_neighbourhood",
    )(qb, kb, vb, cache_k, cache_v, _na_bias_rows(rpb))


def _merge_kernel(oap, oas, obp, obs, wa_ref, wb_ref, ga_ref, gb_ref, z_ref, wa_bf, wb_bf):
    i = pl.program_id(1)
    p_tiles = M_PROMPT // TM

    @pl.when(i == 0)
    def _():
        wa_bf[...] = wa_ref[...].astype(BF16)
        wb_bf[...] = wb_ref[...].astype(BF16)

    def run(oa_ref, ob_ref):
        for r0 in range(0, TM, ROW_CHUNK):
            rows = slice(r0, r0 + ROW_CHUNK)
            ya = _dot(oa_ref[rows, :], wa_bf[...])
            yb = _dot(ob_ref[rows, :], wb_bf[...])
            z_ref[rows, :] = (ga_ref[rows, :] * ya + gb_ref[rows, :] * yb).astype(BF16)

    pl.when(i < p_tiles)(functools.partial(run, oap, obp))
    pl.when(i >= p_tiles)(functools.partial(run, oas, obs))


def _merge_call(oa_p, oa_s, ob_p, ob_s, w_br_a, w_br_b, gates):
    p_tiles = M_PROMPT // TM
    nj = D_MODEL // TN
    pspec = pl.BlockSpec((TM, A_Q), lambda j, i: (jnp.minimum(i, p_tiles - 1), 0))
    sspec = pl.BlockSpec((TM, A_Q), lambda j, i: (jnp.maximum(i - p_tiles, 0), 0))
    once = dict(pipeline_mode=pl.Buffered(1))
    return pl.pallas_call(
        _merge_kernel,
        grid=(nj, M_ALL // TM),
        in_specs=[pspec, sspec, pspec, sspec,
                  pl.BlockSpec((A_Q, TN), lambda j, i: (0, j), **once),
                  pl.BlockSpec((B_QKV, TN), lambda j, i: (0, j), **once),
                  pl.BlockSpec((TM, TN), lambda j, i: (i, j)),
                  pl.BlockSpec((TM, TN), lambda j, i: (i, nj + j))],
        out_specs=pl.BlockSpec((TM, TN), lambda j, i: (i, j)),
        out_shape=jax.ShapeDtypeStruct((M_ALL, D_MODEL), BF16),
        scratch_shapes=[pltpu.VMEM((A_Q, TN), BF16), pltpu.VMEM((B_QKV, TN), BF16)],
        compiler_params=_params(("arbitrary", "arbitrary"), VMEM_LIMIT),
        name="merge_branches",
    )(oa_p, oa_s, ob_p, ob_s, w_br_a, w_br_b, gates, gates)


def _out_kernel(z_ref, w_ref, xp_ref, xs_ref, g_ref, o_ref, wbf):
    i = pl.program_id(1)
    p_tiles = M_PROMPT // TM

    @pl.when(i == 0)
    def _():
        wbf[...] = w_ref[...].astype(BF16)

    def run(x_ref):
        for r0 in range(0, TM, ROW_CHUNK):
            rows = slice(r0, r0 + ROW_CHUNK)
            o_ref[rows, :] = x_ref[rows, :] + g_ref[...] * _dot(z_ref[rows, :], wbf[...])

    pl.when(i < p_tiles)(functools.partial(run, xp_ref))
    pl.when(i >= p_tiles)(functools.partial(run, xs_ref))


def _out_call(z, w_out, xp, xs, mods3):
    p_tiles = M_PROMPT // TM
    nj = D_MODEL // TN
    return pl.pallas_call(
        _out_kernel,
        grid=(nj, M_ALL // TM),
        in_specs=[pl.BlockSpec((TM, D_MODEL), lambda j, i: (i, 0)),
                  pl.BlockSpec((D_MODEL, TN), lambda j, i: (0, j), pipeline_mode=pl.Buffered(1)),
                  pl.BlockSpec((TM, TN), lambda j, i: (jnp.minimum(i, p_tiles - 1), j)),
                  pl.BlockSpec((TM, TN), lambda j, i: (jnp.maximum(i - p_tiles, 0), j)),
                  pl.BlockSpec((None, 1, TN), lambda j, i: (_mod_row(i, TM), 0, 2 * nj + j))],
        out_specs=pl.BlockSpec((TM, TN), lambda j, i: (i, j)),
        out_shape=jax.ShapeDtypeStruct((M_ALL, D_MODEL), F32),
        scratch_shapes=[pltpu.VMEM((D_MODEL, TN), BF16)],
        compiler_params=_params(("arbitrary", "arbitrary"), VMEM_LIMIT),
        name="out_proj_residual",
    )(z, w_out, xp, xs, mods3)


def _mlp_kernel(x_ref, nw_ref, sh_ref, sc_ref, g_ref, wu_ref, wd_ref, o_ref, h_ref):
    f = pl.program_id(1)

    @pl.when(f == 0)
    def _():
        wu = wu_ref[...].astype(BF16)
        wd = wd_ref[...].astype(BF16)
        for r0 in range(0, TM, ROW_CHUNK):
            rows = slice(r0, r0 + ROW_CHUNK)
            h = _modnorm(x_ref[rows, :], nw_ref[...], sc_ref[...], sh_ref[...]).astype(BF16)
            h_ref[rows, :] = h
            u = jnp.square(jnp.maximum(_dot(h, wu), 0.0)).astype(BF16)
            o_ref[rows, :] = _dot(u, wd)

    @pl.when(f > 0)
    def _():
        u = _dot(h_ref[...], wu_ref[...].astype(BF16))
        u = jnp.square(jnp.maximum(u, 0.0)).astype(BF16)
        for c0 in range(0, D_MODEL, MLP_COLS):
            cols = slice(c0, c0 + MLP_COLS)
            o_ref[:, cols] += _dot(u, wd_ref[:, cols].astype(BF16))

    @pl.when(f == pl.num_programs(1) - 1)
    def _():
        o_ref[...] = x_ref[...] + g_ref[...] * o_ref[...]


def _mlp_call(x1, nw, mods3, w_up, w_down, tile0, n_tiles):
    mod = lambda k: pl.BlockSpec((None, 1, D_MODEL), lambda i, f: (_mod_row(tile0 + i, TM), 0, k))
    return pl.pallas_call(
        _mlp_kernel,
        grid=(n_tiles, D_FF // TF),
        in_specs=[pl.BlockSpec((TM, D_MODEL), lambda i, f: (tile0 + i, 0), pipeline_mode=pl.Buffered(1)),
                  pl.BlockSpec((1, D_MODEL), lambda i, f: (0, 0)),
                  mod(3), mod(4), mod(5),
                  pl.BlockSpec((D_MODEL, TF), lambda i, f: (0, f)),
                  pl.BlockSpec((TF, D_MODEL), lambda i, f: (f, 0))],
        out_specs=pl.BlockSpec((TM, D_MODEL), lambda i, f: (i, 0)),
        out_shape=jax.ShapeDtypeStruct((n_tiles * TM, D_MODEL), F32),
        scratch_shapes=[pltpu.VMEM((TM, D_MODEL), BF16)],
        compiler_params=_params(("arbitrary", "arbitrary"), VMEM_LIMIT),
        name="mlp",
    )(x1, nw, mods3, mods3, mods3, w_up, w_down)


def kernel(x_prompt, x_sample, cache_a_k, cache_a_v, cache_b_k, cache_b_v, c, c_ctx, norm1_w, norm2_w, w_ada, b_ada, w_in, q_norm_a, k_norm_a, q_norm_b, k_norm_b, sink_a, rpb_b, w_br_a, w_br_b, w_out, w_up, w_down):
    assert w_ada.shape[0] == 1, "one trunk layer"
    xp = x_prompt.reshape(M_PROMPT, D_MODEL)
    xs = x_sample.reshape(M_SAMPLE, D_MODEL)

    cvecs = jnp.concatenate([c_ctx[None, :], c, jnp.zeros((8 - 1 - DEC_BATCH, D_MODEL), F32)], axis=0)
    mods = _ada_call(cvecs, w_ada[0], b_ada)
    mods3 = mods.reshape(8, 1, 6 * D_MODEL)

    w = w_in[0]
    rope = _rope_tables()
    h, kva, ka_p, va_p = _norm_kva_call(xp, xs, norm1_w, mods3, w, k_norm_a, rope)
    qa = _proj_call(h, w, "qa", COL_QA, A_Q, TN, q_norm_a, rope)
    qb = _proj_call(h, w, "qb", COL_QB, B_QKV, TN, q_norm_b)
    kb, kb_p = _proj_call(h, w, "kb", COL_KB, B_QKV, TN, k_norm_b)
    vb, vb_p = _proj_call(h, w, "vb", COL_VB, B_QKV, TN)
    gates = _proj_call(h, w, "gate", COL_G, 2 * D_MODEL, TN)

    sink = sink_a[0]
    oa_p, ob_p = _ctx_attn_call(sink, qa, kva, qb, kb, vb)
    oa_s = _win_attn_call(sink, qa, kva,
                          cache_a_k.reshape(DEC_BATCH, PAST_LEN, A_KV), cache_a_v.reshape(DEC_BATCH, PAST_LEN, A_KV))
    ob_s = _na_attn_call(qb, kb, vb,
                         cache_b_k.reshape(DEC_BATCH, PAST_LEN, B_QKV), cache_b_v.reshape(DEC_BATCH, PAST_LEN, B_QKV),
                         rpb_b[0])

    z = _merge_call(oa_p, oa_s, ob_p, ob_s, w_br_a[0], w_br_b[0], gates)
    x1 = _out_call(z, w_out[0], xp, xs, mods3)

    p_tiles = M_PROMPT // TM
    y_p = _mlp_call(x1, norm2_w, mods3, w_up[0], w_down[0], 0, p_tiles)
    y_s = _mlp_call(x1, norm2_w, mods3, w_up[0], w_down[0], p_tiles, M_SAMPLE // TM)

    return (y_p.reshape(BATCH, SEQ, D_MODEL),
            y_s.reshape(DEC_BATCH, DEC_SEQ, D_MODEL),
            ka_p.reshape(BATCH, 1, SEQ, A_KV_HEADS, HEAD_DIM),
            va_p.reshape(BATCH, 1, SEQ, A_KV_HEADS, HEAD_DIM),
            kb_p.reshape(BATCH, 1, SEQ, B_HEADS, HEAD_DIM),
            vb_p.reshape(BATCH, 1, SEQ, B_HEADS, HEAD_DIM))
```

```python
import functools

import numpy as np
import jax
import jax.numpy as jnp
from jax import lax
from jax.experimental import pallas as pl
from jax.experimental.pallas import tpu as pltpu

D_MODEL = 2048
BATCH = 16
SEQ = 256
DEC_BATCH = 2
DEC_SEQ = 1024
PAST_LEN = 256
GRID_W = 64
HEAD_DIM = 128
A_HEADS = 8
A_KV_HEADS = 2
A_GROUP = A_HEADS // A_KV_HEADS
A_WINDOW = 128
BLOCK = 128
B_HEADS = 8
NA_ROWS_MAX = 8
NA_COLS = 16
D_FF = 4 * D_MODEL
ROPE_THETA = 10000.0
EPS = 1e-6
NEG = -1e30
A_Q = A_HEADS * HEAD_DIM
A_KV = A_KV_HEADS * HEAD_DIM
B_QKV = B_HEADS * HEAD_DIM
IN_WIDTH = A_Q + 2 * A_KV + 3 * B_QKV + 2 * D_MODEL
SCALE = HEAD_DIM ** -0.5

M_PROMPT = BATCH * SEQ
M_SAMPLE = DEC_BATCH * DEC_SEQ
M_ALL = M_PROMPT + M_SAMPLE
GRID_ROWS = DEC_SEQ // GRID_W
NA_ROWS = min(NA_ROWS_MAX, GRID_ROWS)

COL_QA = 0
COL_KVA = A_Q
COL_QB = A_Q + 2 * A_KV
COL_KB = COL_QB + B_QKV
COL_VB = COL_KB + B_QKV
COL_G = COL_VB + B_QKV

TM = 1024
TN = 1024
ROW_CHUNK = 256
W_SUB = 512
TM_NORM = 512
TF = 512
MLP_COLS = 512
VMEM_LIMIT = 56 * 1024 * 1024

F32 = jnp.float32
BF16 = jnp.bfloat16


def _mod_row(i, tm):
    p_tiles = M_PROMPT // tm
    return jnp.where(i < p_tiles, 0, 1 + (i - p_tiles) // (DEC_SEQ // tm))


def _dot(a, b):
    return jnp.dot(a, b, preferred_element_type=F32)


def _dot_nt(a, b):
    return lax.dot_general(a, b, (((1,), (1,)), ((), ())), preferred_element_type=F32)


def _params(sem, vmem=None):
    return pltpu.CompilerParams(dimension_semantics=sem, vmem_limit_bytes=vmem)


def _ada_kernel(c_ref, w_ref, b_ref, o_ref):
    cv = c_ref[...]
    s = (cv * jax.nn.sigmoid(cv)).astype(BF16)
    o_ref[...] = _dot(s, w_ref[...].astype(BF16)) + b_ref[...]


def _ada_call(cvecs, w_ada, b_ada):
    tn = 1024
    n = 6 * D_MODEL
    return pl.pallas_call(
        _ada_kernel,
        grid=(n // tn,),
        in_specs=[pl.BlockSpec((8, D_MODEL), lambda j: (0, 0)),
                  pl.BlockSpec((D_MODEL, tn), lambda j: (0, j)),
                  pl.BlockSpec((1, tn), lambda j: (0, j))],
        out_specs=pl.BlockSpec((8, tn), lambda j: (0, j)),
        out_shape=jax.ShapeDtypeStruct((8, n), F32),
        compiler_params=_params(("arbitrary",), VMEM_LIMIT),
        name="ada_mod",
    )(cvecs, w_ada, b_ada)


def _modnorm(x, nw, sc, sh):
    y = x * lax.rsqrt(jnp.mean(x * x, axis=-1, keepdims=True) + EPS)
    return (y * nw) * (1.0 + sc) + sh


def _head_norm(x, nw):
    return x * lax.rsqrt(jnp.mean(x * x, axis=-1, keepdims=True) + EPS) * nw


def _rope(x, cos, sin_signed):
    lane = lax.broadcasted_iota(jnp.int32, x.shape, 1)
    partner = jnp.where((lane % 64) < 32, pltpu.roll(x, 96, 1), pltpu.roll(x, 32, 1))
    return x * cos + partner * sin_signed


def _norm_kva_kernel(xp_ref, xs_ref, nw1_ref, sh_ref, sc_ref, w_ref, nwk_ref, cos_ref, sin_ref,
                     h_ref, kv_ref, kp_ref, vp_ref, wbf):
    i = pl.program_id(0)
    p_tiles = M_PROMPT // TM_NORM

    @pl.when(i == 0)
    def _():
        wbf[...] = w_ref[...].astype(BF16)

    def head_cols(k, base=0):
        return slice(base + k * HEAD_DIM, base + (k + 1) * HEAD_DIM)

    def run(prompt):
        x_ref = xp_ref if prompt else xs_ref
        for r0 in range(0, TM_NORM, ROW_CHUNK):
            rows = slice(r0, r0 + ROW_CHUNK)
            h = _modnorm(x_ref[rows, :], nw1_ref[...], sc_ref[...], sh_ref[...]).astype(BF16)
            h_ref[rows, :] = h
            acc = _dot(h, wbf[...])
            for k in range(A_KV_HEADS):
                y = _head_norm(acc[:, head_cols(k)], nwk_ref[...])
                v = acc[:, head_cols(k, A_KV)]
                if prompt:
                    kp_ref[rows, head_cols(k)] = y
                    vp_ref[rows, head_cols(k)] = v
                else:
                    y = _rope(y, cos_ref[rows, :], sin_ref[rows, :])
                kv_ref[rows, head_cols(k)] = y.astype(BF16)
                kv_ref[rows, head_cols(k, A_KV)] = v.astype(BF16)

    pl.when(i < p_tiles)(functools.partial(run, True))
    pl.when(i >= p_tiles)(functools.partial(run, False))


def _norm_kva_call(xp, xs, nw1, mods3, w_in, nwk, rope):
    tm = TM_NORM
    p_tiles = M_PROMPT // tm
    s_tiles = DEC_SEQ // tm
    assert COL_KVA % (2 * A_KV) == 0
    rope_spec = pl.BlockSpec((tm, HEAD_DIM), lambda i: (jnp.maximum(i - p_tiles, 0) % s_tiles, 0))
    parked = pl.BlockSpec((tm, A_KV), lambda i: (jnp.minimum(i, p_tiles - 1), 0))
    return pl.pallas_call(
        _norm_kva_kernel,
        grid=(M_ALL // tm,),
        in_specs=[pl.BlockSpec((tm, D_MODEL), lambda i: (jnp.minimum(i, p_tiles - 1), 0)),
                  pl.BlockSpec((tm, D_MODEL), lambda i: (jnp.maximum(i - p_tiles, 0), 0)),
                  pl.BlockSpec((1, D_MODEL), lambda i: (0, 0)),
                  pl.BlockSpec((None, 1, D_MODEL), lambda i: (_mod_row(i, tm), 0, 0)),
                  pl.BlockSpec((None, 1, D_MODEL), lambda i: (_mod_row(i, tm), 0, 1)),
                  pl.BlockSpec((D_MODEL, 2 * A_KV), lambda i: (0, COL_KVA // (2 * A_KV)),
                               pipeline_mode=pl.Buffered(1)),
                  pl.BlockSpec((1, HEAD_DIM), lambda i: (0, 0)),
                  rope_spec, rope_spec],
        out_specs=[pl.BlockSpec((tm, D_MODEL), lambda i: (i, 0)),
                   pl.BlockSpec((tm, 2 * A_KV), lambda i: (i, 0)),
                   parked, parked],
        out_shape=[jax.ShapeDtypeStruct((M_ALL, D_MODEL), BF16),
                   jax.ShapeDtypeStruct((M_ALL, 2 * A_KV), BF16),
                   jax.ShapeDtypeStruct((M_PROMPT, A_KV), F32),
                   jax.ShapeDtypeStruct((M_PROMPT, A_KV), F32)],
        scratch_shapes=[pltpu.VMEM((D_MODEL, 2 * A_KV), BF16)],
        compiler_params=_params(("arbitrary",), VMEM_LIMIT),
        name="norm1_proj_kva",
    )(xp, xs, nw1, mods3, mods3, w_in, nwk, *rope)


def _rope_tables():
    n_freq = HEAD_DIM // 4
    pos = np.arange(DEC_SEQ)
    row = (pos // GRID_W).astype(np.float64)
    col = (pos % GRID_W).astype(np.float64)
    inv = ROPE_THETA ** (-np.arange(n_freq, dtype=np.float64) / n_freq)
    ar = row[:, None] * inv
    ac = col[:, None] * inv
    cos = np.concatenate([np.cos(ar), np.cos(ar), np.cos(ac), np.cos(ac)], axis=-1)
    sin = np.concatenate([-np.sin(ar), np.sin(ar), -np.sin(ac), np.sin(ac)], axis=-1)
    return jnp.asarray(cos, F32), jnp.asarray(sin, F32)


def _proj_kernel(*refs, kind, tn):
    nsub = tn // W_SUB
    h_ref, w_refs, refs, wbf = refs[0], refs[1:1 + nsub], refs[1 + nsub:-1], refs[-1]
    i = pl.program_id(1)
    is_prompt = i < M_PROMPT // TM

    @pl.when(i == 0)
    def _():
        for k, w_ref in enumerate(w_refs):
            wbf[:, k * W_SUB:(k + 1) * W_SUB] = w_ref[...].astype(BF16)

    def chunks():
        for r0 in range(0, TM, ROW_CHUNK):
            rows = slice(r0, r0 + ROW_CHUNK)
            acc = _dot(h_ref[rows, :], wbf[...])
            yield rows, acc, [acc[:, a:a + HEAD_DIM] for a in range(0, tn, HEAD_DIM)]

    def head_cols(k, base=0):
        return slice(base + k * HEAD_DIM, base + (k + 1) * HEAD_DIM)

    def per_group(run):
        pl.when(is_prompt)(functools.partial(run, True))
        pl.when(jnp.logical_not(is_prompt))(functools.partial(run, False))

    if kind == "qa":
        nw_ref, cos_ref, sin_ref, q_ref = refs

        def run(prompt):
            for rows, _, heads in chunks():
                for k, x in enumerate(heads):
                    y = _head_norm(x, nw_ref[...])
                    if not prompt:
                        y = _rope(y, cos_ref[rows, :], sin_ref[rows, :])
                    q_ref[rows, head_cols(k)] = y.astype(BF16)
        per_group(run)

    elif kind == "qb":
        nw_ref, q_ref = refs
        for rows, _, heads in chunks():
            for k, x in enumerate(heads):
                q_ref[rows, head_cols(k)] = _head_norm(x, nw_ref[...]).astype(BF16)

    elif kind == "kb":
        nw_ref, k_ref, kp_ref = refs

        def run(prompt):
            for rows, _, heads in chunks():
                for k, x in enumerate(heads):
                    y = _head_norm(x, nw_ref[...])
                    if prompt:
                        kp_ref[rows, head_cols(k)] = y
                    k_ref[rows, head_cols(k)] = y.astype(BF16)
        per_group(run)

    elif kind == "vb":
        v_ref, vp_ref = refs

        def run(prompt):
            for rows, acc, _ in chunks():
                if prompt:
                    vp_ref[rows, :] = acc
                v_ref[rows, :] = acc.astype(BF16)
        per_group(run)

    elif kind == "gate":
        (g_ref,) = refs
        for rows, acc, _ in chunks():
            g_ref[rows, :] = jax.nn.sigmoid(acc)

    else:
        raise ValueError(kind)


def _proj_call(h, w_in, kind, col0, width, tn, norm_w=None, rope=None):
    assert col0 % W_SUB == 0 and tn % W_SUB == 0 and width % tn == 0
    nj = width // tn
    nsub = tn // W_SUB
    c0 = col0 // W_SUB
    p_tiles = M_PROMPT // TM
    s_tiles = DEC_SEQ // TM

    w_mode = dict(pipeline_mode=pl.Buffered(1)) if nj == 1 else {}
    in_specs = [pl.BlockSpec((TM, D_MODEL), lambda j, i: (i, 0))]
    in_specs += [pl.BlockSpec((D_MODEL, W_SUB), lambda j, i, k=k: (0, c0 + nsub * j + k), **w_mode)
                 for k in range(nsub)]
    args = [h] + [w_in] * nsub
    if norm_w is not None:
        in_specs.append(pl.BlockSpec((1, HEAD_DIM), lambda j, i: (0, 0)))
        args.append(norm_w)
    if rope is not None:
        rope_spec = pl.BlockSpec((TM, HEAD_DIM), lambda j, i: (jnp.maximum(i - p_tiles, 0) % s_tiles, 0))
        in_specs += [rope_spec, rope_spec]
        args += list(rope)

    full = pl.BlockSpec((TM, tn), lambda j, i: (i, j))
    if kind in ("qa", "qb"):
        out_specs = full
        out_shape = jax.ShapeDtypeStruct((M_ALL, width), BF16)
    elif kind in ("kb", "vb"):
        cache = pl.BlockSpec((TM, tn), lambda j, i: (jnp.minimum(i, p_tiles - 1), j))
        out_specs = [full, cache]
        out_shape = [jax.ShapeDtypeStruct((M_ALL, width), BF16),
                     jax.ShapeDtypeStruct((M_PROMPT, width), F32)]
    elif kind == "gate":
        out_specs = full
        out_shape = jax.ShapeDtypeStruct((M_ALL, width), F32)
    else:
        raise ValueError(kind)

    return pl.pallas_call(
        functools.partial(_proj_kernel, kind=kind, tn=tn),
        grid=(nj, M_ALL // TM),
        in_specs=in_specs,
        out_specs=out_specs,
        out_shape=out_shape,
        scratch_shapes=[pltpu.VMEM((D_MODEL, tn), BF16)],
        compiler_params=_params(("arbitrary", "arbitrary"), VMEM_LIMIT),
        name="proj_" + kind,
    )(*args)


LOG2E = 1.4426950408889634
QK_LOG2 = SCALE * LOG2E


def _softmax_pv(parts, sink):
    m = functools.reduce(jnp.maximum, [jnp.max(t, axis=-1, keepdims=True) for t, _ in parts])
    if sink is not None:
        m = jnp.maximum(m, sink)
    acc = None
    for t, v in parts:
        p = jnp.exp2(t - m).astype(BF16)
        y = _dot(p, jnp.concatenate([v, jnp.ones_like(v)], axis=1))
        acc = y if acc is None else acc + y
    o, l = acc[:, :HEAD_DIM], acc[:, HEAD_DIM:]
    if sink is not None:
        l = l + jnp.exp2(sink - m)
    return o / l


def _sink_column(sink_ref, hk, rows_per_head):
    n = A_GROUP * rows_per_head
    g = lax.broadcasted_iota(jnp.int32, (n, 1), 0) // rows_per_head
    col = jnp.full((n, 1), sink_ref[hk * A_GROUP], F32)
    for k in range(1, A_GROUP):
        col = jnp.where(g == k, sink_ref[hk * A_GROUP + k], col)
    return col * LOG2E


def _stack_group(q_ref, hk):
    return jnp.concatenate(
        [q_ref[:, (hk * A_GROUP + g) * HEAD_DIM:(hk * A_GROUP + g + 1) * HEAD_DIM] for g in range(A_GROUP)], axis=0)


def _unstack_group(o_ref, hk, o, rows):
    for g in range(A_GROUP):
        c = (hk * A_GROUP + g) * HEAD_DIM
        o_ref[:, c:c + HEAD_DIM] = o[g * rows:(g + 1) * rows].astype(BF16)


def _ctx_attn_kernel(sink_ref, qa_ref, kva_ref, qb_ref, kb_ref, vb_ref, oa_ref, ob_ref):
    for hk in range(A_KV_HEADS):
        k = kva_ref[:, hk * HEAD_DIM:(hk + 1) * HEAD_DIM]
        v = kva_ref[:, A_KV + hk * HEAD_DIM:A_KV + (hk + 1) * HEAD_DIM]
        q4 = _stack_group(qa_ref, hk)
        t = _dot_nt(q4, k) * QK_LOG2
        o = _softmax_pv([(t, v)], _sink_column(sink_ref, hk, SEQ))
        _unstack_group(oa_ref, hk, o, SEQ)
    for h in range(B_HEADS):
        sl = slice(h * HEAD_DIM, (h + 1) * HEAD_DIM)
        t = _dot_nt(qb_ref[:, sl], kb_ref[:, sl]) * QK_LOG2
        ob_ref[:, sl] = _softmax_pv([(t, vb_ref[:, sl])], None).astype(BF16)


def _ctx_attn_call(sink, qa, kva, qb, kb, vb):
    row = lambda w: pl.BlockSpec((SEQ, w), lambda b: (b, 0))
    return pl.pallas_call(
        _ctx_attn_kernel,
        grid=(BATCH,),
        in_specs=[pl.BlockSpec(memory_space=pltpu.SMEM),
                  row(A_Q), row(2 * A_KV), row(B_QKV), row(B_QKV), row(B_QKV)],
        out_specs=[row(A_Q), row(B_QKV)],
        out_shape=[jax.ShapeDtypeStruct((M_PROMPT, A_Q), BF16),
                   jax.ShapeDtypeStruct((M_PROMPT, B_QKV), BF16)],
        compiler_params=_params(("arbitrary",), VMEM_LIMIT),
        name="attn_ctx",
    )(sink, qa, kva, qb, kb, vb)


BAND = 3 * BLOCK


def _cache_to_bf16(n_heads, ck_ref, cv_ref, ckb, cvb):
    for h in range(n_heads):
        sl = slice(h * HEAD_DIM, (h + 1) * HEAD_DIM)
        ckb[:, sl] = ck_ref[:, h, :].astype(BF16)
        cvb[:, sl] = cv_ref[:, h, :].astype(BF16)


def _win_attn_kernel(sink_ref, q_ref, kv_ref, ck_ref, cv_ref, o_ref, ckb, cvb):
    n = pl.program_id(1)
    pl.when(n == 0)(functools.partial(_cache_to_bf16, A_KV_HEADS, ck_ref, cv_ref, ckb, cvb))
    start = pl.multiple_of(jnp.clip((n - 1) * BLOCK, 0, DEC_SEQ - BAND), BLOCK)
    qpos = n * BLOCK + lax.broadcasted_iota(jnp.int32, (BLOCK, BAND), 0)
    kpos = start + lax.broadcasted_iota(jnp.int32, (BLOCK, BAND), 1)
    valid = jnp.abs(qpos - kpos) <= A_WINDOW
    valid = jnp.concatenate([valid.astype(jnp.int32)] * A_GROUP, axis=0) > 0
    for hk in range(A_KV_HEADS):
        sl = slice(hk * HEAD_DIM, (hk + 1) * HEAD_DIM)
        slv = slice(A_KV + hk * HEAD_DIM, A_KV + (hk + 1) * HEAD_DIM)
        k_loc = kv_ref[pl.ds(start, BAND), sl]
        v_loc = kv_ref[pl.ds(start, BAND), slv]
        q4 = _stack_group(q_ref, hk)
        t_loc = jnp.where(valid, _dot_nt(q4, k_loc) * QK_LOG2, NEG)
        t_ctx = _dot_nt(q4, ckb[:, sl]) * QK_LOG2
        o = _softmax_pv([(t_loc, v_loc), (t_ctx, cvb[:, sl])], _sink_column(sink_ref, hk, BLOCK))
        _unstack_group(o_ref, hk, o, BLOCK)


def _win_attn_call(sink, qa, kva, cache_k, cache_v):
    nb = DEC_SEQ // BLOCK
    q0 = M_PROMPT // BLOCK
    b0 = M_PROMPT // DEC_SEQ
    return pl.pallas_call(
        _win_attn_kernel,
        grid=(DEC_BATCH, nb),
        in_specs=[pl.BlockSpec(memory_space=pltpu.SMEM),
                  pl.BlockSpec((BLOCK, A_Q), lambda b, n: (q0 + b * nb + n, 0)),
                  pl.BlockSpec((DEC_SEQ, 2 * A_KV), lambda b, n: (b0 + b, 0)),
                  pl.BlockSpec((None, None, PAST_LEN, A_KV_HEADS, HEAD_DIM), lambda b, n: (b, 0, 0, 0, 0)),
                  pl.BlockSpec((None, None, PAST_LEN, A_KV_HEADS, HEAD_DIM), lambda b, n: (b, 0, 0, 0, 0))],
        out_specs=pl.BlockSpec((BLOCK, A_Q), lambda b, n: (b * nb + n, 0)),
        out_shape=jax.ShapeDtypeStruct((M_SAMPLE, A_Q), BF16),
        scratch_shapes=[pltpu.VMEM((PAST_LEN, A_KV), BF16), pltpu.VMEM((PAST_LEN, A_KV), BF16)],
        compiler_params=_params(("arbitrary", "arbitrary"), VMEM_LIMIT),
        name="attn_window",
    )(sink, qa, kva, cache_k, cache_v)


NA_QROWS = 4
NA_WIN_ROWS = 12
NA_Q = NA_QROWS * GRID_W
NA_KEYS = NA_WIN_ROWS * GRID_W
NA_DR = 2 * NA_ROWS_MAX - 1
NA_PAIRS = NA_DR + 1
RPB_W = 2 * NA_COLS - 1
RPB_PAD = GRID_W - NA_COLS


def _na_row_start(r):
    return jnp.clip(r - NA_ROWS // 2, 0, GRID_ROWS - NA_ROWS)


def _na_window_row0(blk):
    return jnp.clip(blk * NA_QROWS - NA_ROWS // 2, 0, GRID_ROWS - NA_WIN_ROWS)


def _check_na_windows():
    for blk in range(GRID_ROWS // NA_QROWS):
        w0 = int(np.clip(blk * NA_QROWS - NA_ROWS // 2, 0, GRID_ROWS - NA_WIN_ROWS))
        for r in range(blk * NA_QROWS, (blk + 1) * NA_QROWS):
            r0 = int(np.clip(r - NA_ROWS // 2, 0, GRID_ROWS - NA_ROWS))
            assert w0 <= r0 and r0 + NA_ROWS <= w0 + NA_WIN_ROWS, (blk, r)


_check_na_windows()


def _na_attn_kernel(q_ref, k_ref, v_ref, ck_ref, cv_ref, rpb_ref, o_ref, tab_ref, ckb, cvb):
    b, blk = pl.program_id(0), pl.program_id(1)
    lane = lax.broadcasted_iota(jnp.int32, (GRID_W, 2 * GRID_W), 1)
    pl.when(blk == 0)(functools.partial(_cache_to_bf16, B_HEADS, ck_ref, cv_ref, ckb, cvb))

    @pl.when((b == 0) & (blk == 0))
    def _():
        for h in range(B_HEADS):
            for d in range(NA_PAIRS):
                lo = jnp.broadcast_to(rpb_ref[h, d:d + 1, :], (GRID_W, 2 * GRID_W))
                hi = jnp.broadcast_to(rpb_ref[h, d + 1:d + 2, :], (GRID_W, 2 * GRID_W))
                lo = pltpu.roll(lo, GRID_W + 1, 1, stride=1, stride_axis=0)
                hi = pltpu.roll(hi, 1, 1, stride=1, stride_axis=0)
                tab_ref[h, d] = jnp.where(lane < GRID_W, lo, hi) * LOG2E

    row0 = _na_window_row0(blk)
    k0 = pl.multiple_of(row0 * GRID_W, GRID_W)
    qi = lax.broadcasted_iota(jnp.int32, (NA_Q, NA_KEYS), 0)
    ki = lax.broadcasted_iota(jnp.int32, (NA_Q, NA_KEYS), 1)
    qrow, qcol = blk * NA_QROWS + qi // GRID_W, qi % GRID_W
    krow, kcol = row0 + ki // GRID_W, ki % GRID_W
    rstart = _na_row_start(qrow)
    cstart = jnp.clip(qcol - NA_COLS // 2, 0, GRID_W - NA_COLS)
    valid = (krow >= rstart) & (krow < rstart + NA_ROWS) & (kcol >= cstart) & (kcol < cstart + NA_COLS)

    for h in range(B_HEADS):
        sl = slice(h * HEAD_DIM, (h + 1) * HEAD_DIM)
        bias = jnp.concatenate(
            [jnp.concatenate(
                [tab_ref[h, jnp.clip(row0 + 2 * p - blk * NA_QROWS - q + NA_ROWS_MAX, 0, NA_PAIRS - 1)]
                 for p in range(NA_WIN_ROWS // 2)], axis=1)
             for q in range(NA_QROWS)], axis=0)
        k_loc = k_ref[pl.ds(k0, NA_KEYS), sl]
        v_loc = v_ref[pl.ds(k0, NA_KEYS), sl]
        t_loc = jnp.where(valid, _dot_nt(q_ref[:, sl], k_loc) * QK_LOG2 + bias, NEG)
        t_ctx = _dot_nt(q_ref[:, sl], ckb[:, sl]) * QK_LOG2
        o = _softmax_pv([(t_loc, v_loc), (t_ctx, cvb[:, sl])], None)
        o_ref[:, sl] = o.astype(BF16)


def _na_bias_rows(rpb):
    rows = jnp.pad(rpb, ((0, 0), (1, 1), (0, 0)))
    left = jnp.broadcast_to(rows[..., :1], rows.shape[:2] + (RPB_PAD,))
    right = jnp.broadcast_to(rows[..., -1:], rows.shape[:2] + (2 * GRID_W - RPB_PAD - RPB_W,))
    return jnp.concatenate([left, rows, right], axis=-1)


def _na_attn_call(qb, kb, vb, cache_k, cache_v, rpb):
    nblk = GRID_ROWS // NA_QROWS
    q0 = M_PROMPT // NA_Q
    b0 = M_PROMPT // DEC_SEQ
    return pl.pallas_call(
        _na_attn_kernel,
        grid=(DEC_BATCH, nblk),
        in_specs=[pl.BlockSpec((NA_Q, B_QKV), lambda b, r: (q0 + b * nblk + r, 0)),
                  pl.BlockSpec((DEC_SEQ, B_QKV), lambda b, r: (b0 + b, 0)),
                  pl.BlockSpec((DEC_SEQ, B_QKV), lambda b, r: (b0 + b, 0)),
                  pl.BlockSpec((None, None, PAST_LEN, B_HEADS, HEAD_DIM), lambda b, r: (b, 0, 0, 0, 0)),
                  pl.BlockSpec((None, None, PAST_LEN, B_HEADS, HEAD_DIM), lambda b, r: (b, 0, 0, 0, 0)),
                  pl.BlockSpec((B_HEADS, NA_DR + 2, 2 * GRID_W), lambda b, r: (0, 0, 0))],
        out_specs=pl.BlockSpec((NA_Q, B_QKV), lambda b, r: (b * nblk + r, 0)),
        out_shape=jax.ShapeDtypeStruct((M_SAMPLE, B_QKV), BF16),
        scratch_shapes=[pltpu.VMEM((B_HEADS, NA_PAIRS, GRID_W, 2 * GRID_W), F32),
                        pltpu.VMEM((PAST_LEN, B_QKV), BF16), pltpu.VMEM((PAST_LEN, B_QKV), BF16)],
        compiler_params=_params(("arbitrary", "arbitrary"), VMEM_LIMIT),
        name="attn_neighbourhood",
    )(qb, kb, vb, cache_k, cache_v, _na_bias_rows(rpb))


def _merge_kernel(oap, oas, obp, obs, wa_ref, wb_ref, ga_ref, gb_ref, z_ref, wa_bf, wb_bf):
    i = pl.program_id(1)
    p_tiles = M_PROMPT // TM

    @pl.when(i == 0)
    def _():
        wa_bf[...] = wa_ref[...].astype(BF16)
        wb_bf[...] = wb_ref[...].astype(BF16)

    def run(oa_ref, ob_ref):
        for r0 in range(0, TM, ROW_CHUNK):
            rows = slice(r0, r0 + ROW_CHUNK)
            ya = _dot(oa_ref[rows, :], wa_bf[...])
            yb = _dot(ob_ref[rows, :], wb_bf[...])
            z_ref[rows, :] = (ga_ref[rows, :] * ya + gb_ref[rows, :] * yb).astype(BF16)

    pl.when(i < p_tiles)(functools.partial(run, oap, obp))
    pl.when(i >= p_tiles)(functools.partial(run, oas, obs))


def _merge_call(oa_p, oa_s, ob_p, ob_s, w_br_a, w_br_b, gates):
    p_tiles = M_PROMPT // TM
    nj = D_MODEL // TN
    pspec = pl.BlockSpec((TM, A_Q), lambda j, i: (jnp.minimum(i, p_tiles - 1), 0))
    sspec = pl.BlockSpec((TM, A_Q), lambda j, i: (jnp.maximum(i - p_tiles, 0), 0))
    once = dict(pipeline_mode=pl.Buffered(1))
    return pl.pallas_call(
        _merge_kernel,
        grid=(nj, M_ALL // TM),
        in_specs=[pspec, sspec, pspec, sspec,
                  pl.BlockSpec((A_Q, TN), lambda j, i: (0, j), **once),
                  pl.BlockSpec((B_QKV, TN), lambda j, i: (0, j), **once),
                  pl.BlockSpec((TM, TN), lambda j, i: (i, j)),
                  pl.BlockSpec((TM, TN), lambda j, i: (i, nj + j))],
        out_specs=pl.BlockSpec((TM, TN), lambda j, i: (i, j)),
        out_shape=jax.ShapeDtypeStruct((M_ALL, D_MODEL), BF16),
        scratch_shapes=[pltpu.VMEM((A_Q, TN), BF16), pltpu.VMEM((B_QKV, TN), BF16)],
        compiler_params=_params(("arbitrary", "arbitrary"), VMEM_LIMIT),
        name="merge_branches",
    )(oa_p, oa_s, ob_p, ob_s, w_br_a, w_br_b, gates, gates)


def _out_kernel(z_ref, w_ref, xp_ref, xs_ref, g_ref, o_ref, wbf):
    i = pl.program_id(1)
    p_tiles = M_PROMPT // TM

    @pl.when(i == 0)
    def _():
        wbf[...] = w_ref[...].astype(BF16)

    def run(x_ref):
        for r0 in range(0, TM, ROW_CHUNK):
            rows = slice(r0, r0 + ROW_CHUNK)
            o_ref[rows, :] = x_ref[rows, :] + g_ref[...] * _dot(z_ref[rows, :], wbf[...])

    pl.when(i < p_tiles)(functools.partial(run, xp_ref))
    pl.when(i >= p_tiles)(functools.partial(run, xs_ref))


def _out_call(z, w_out, xp, xs, mods3):
    p_tiles = M_PROMPT // TM
    nj = D_MODEL // TN
    return pl.pallas_call(
        _out_kernel,
        grid=(nj, M_ALL // TM),
        in_specs=[pl.BlockSpec((TM, D_MODEL), lambda j, i: (i, 0)),
                  pl.BlockSpec((D_MODEL, TN), lambda j, i: (0, j), pipeline_mode=pl.Buffered(1)),
                  pl.BlockSpec((TM, TN), lambda j, i: (jnp.minimum(i, p_tiles - 1), j)),
                  pl.BlockSpec((TM, TN), lambda j, i: (jnp.maximum(i - p_tiles, 0), j)),
                  pl.BlockSpec((None, 1, TN), lambda j, i: (_mod_row(i, TM), 0, 2 * nj + j))],
        out_specs=pl.BlockSpec((TM, TN), lambda j, i: (i, j)),
        out_shape=jax.ShapeDtypeStruct((M_ALL, D_MODEL), F32),
        scratch_shapes=[pltpu.VMEM((D_MODEL, TN), BF16)],
        compiler_params=_params(("arbitrary", "arbitrary"), VMEM_LIMIT),
        name="out_proj_residual",
    )(z, w_out, xp, xs, mods3)


def _mlp_kernel(x_ref, nw_ref, sh_ref, sc_ref, g_ref, wu_ref, wd_ref, o_ref, h_ref):
    f = pl.program_id(1)

    @pl.when(f == 0)
    def _():
        wu = wu_ref[...].astype(BF16)
        wd = wd_ref[...].astype(BF16)
        for r0 in range(0, TM, ROW_CHUNK):
            rows = slice(r0, r0 + ROW_CHUNK)
            h = _modnorm(x_ref[rows, :], nw_ref[...], sc_ref[...], sh_ref[...]).astype(BF16)
            h_ref[rows, :] = h
            u = jnp.square(jnp.maximum(_dot(h, wu), 0.0)).astype(BF16)
            o_ref[rows, :] = _dot(u, wd)

    @pl.when(f > 0)
    def _():
        u = _dot(h_ref[...], wu_ref[...].astype(BF16))
        u = jnp.square(jnp.maximum(u, 0.0)).astype(BF16)
        for c0 in range(0, D_MODEL, MLP_COLS):
            cols = slice(c0, c0 + MLP_COLS)
            o_ref[:, cols] += _dot(u, wd_ref[:, cols].astype(BF16))

    @pl.when(f == pl.num_programs(1) - 1)
    def _():
        o_ref[...] = x_ref[...] + g_ref[...] * o_ref[...]


def _mlp_call(x1, nw, mods3, w_up, w_down, tile0, n_tiles):
    mod = lambda k: pl.BlockSpec((None, 1, D_MODEL), lambda i, f: (_mod_row(tile0 + i, TM), 0, k))
    return pl.pallas_call(
        _mlp_kernel,
        grid=(n_tiles, D_FF // TF),
        in_specs=[pl.BlockSpec((TM, D_MODEL), lambda i, f: (tile0 + i, 0), pipeline_mode=pl.Buffered(1)),
                  pl.BlockSpec((1, D_MODEL), lambda i, f: (0, 0)),
                  mod(3), mod(4), mod(5),
                  pl.BlockSpec((D_MODEL, TF), lambda i, f: (0, f)),
                  pl.BlockSpec((TF, D_MODEL), lambda i, f: (f, 0))],
        out_specs=pl.BlockSpec((TM, D_MODEL), lambda i, f: (i, 0)),
        out_shape=jax.ShapeDtypeStruct((n_tiles * TM, D_MODEL), F32),
        scratch_shapes=[pltpu.VMEM((TM, D_MODEL), BF16)],
        compiler_params=_params(("arbitrary", "arbitrary"), VMEM_LIMIT),
        name="mlp",
    )(x1, nw, mods3, mods3, mods3, w_up, w_down)


def kernel(x_prompt, x_sample, cache_a_k, cache_a_v, cache_b_k, cache_b_v, c, c_ctx, norm1_w, norm2_w, w_ada, b_ada, w_in, q_norm_a, k_norm_a, q_norm_b, k_norm_b, sink_a, rpb_b, w_br_a, w_br_b, w_out, w_up, w_down):
    assert w_ada.shape[0] == 1, "one trunk layer"
    xp = x_prompt.reshape(M_PROMPT, D_MODEL)
    xs = x_sample.reshape(M_SAMPLE, D_MODEL)

    cvecs = jnp.concatenate([c_ctx[None, :], c, jnp.zeros((8 - 1 - DEC_BATCH, D_MODEL), F32)], axis=0)
    mods = _ada_call(cvecs, w_ada[0], b_ada)
    mods3 = mods.reshape(8, 1, 6 * D_MODEL)

    w = w_in[0]
    rope = _rope_tables()
    h, kva, ka_p, va_p = _norm_kva_call(xp, xs, norm1_w, mods3, w, k_norm_a, rope)
    qa = _proj_call(h, w, "qa", COL_QA, A_Q, TN, q_norm_a, rope)
    qb = _proj_call(h, w, "qb", COL_QB, B_QKV, TN, q_norm_b)
    kb, kb_p = _proj_call(h, w, "kb", COL_KB, B_QKV, TN, k_norm_b)
    vb, vb_p = _proj_call(h, w, "vb", COL_VB, B_QKV, TN)
    gates = _proj_call(h, w, "gate", COL_G, 2 * D_MODEL, TN)

    sink = sink_a[0]
    oa_p, ob_p = _ctx_attn_call(sink, qa, kva, qb, kb, vb)
    oa_s = _win_attn_call(sink, qa, kva, cache_a_k, cache_a_v)
    ob_s = _na_attn_call(qb, kb, vb, cache_b_k, cache_b_v, rpb_b[0])

    z = _merge_call(oa_p, oa_s, ob_p, ob_s, w_br_a[0], w_br_b[0], gates)
    x1 = _out_call(z, w_out[0], xp, xs, mods3)

    p_tiles = M_PROMPT // TM
    y_p = _mlp_call(x1, norm2_w, mods3, w_up[0], w_down[0], 0, p_tiles)
    y_s = _mlp_call(x1, norm2_w, mods3, w_up[0], w_down[0], p_tiles, M_SAMPLE // TM)

    return (y_p.reshape(BATCH, SEQ, D_MODEL),
            y_s.reshape(DEC_BATCH, DEC_SEQ, D_MODEL),
            ka_p.reshape(BATCH, 1, SEQ, A_KV_HEADS, HEAD_DIM),
            va_p.reshape(BATCH, 1, SEQ, A_KV_HEADS, HEAD_DIM),
            kb_p.reshape(BATCH, 1, SEQ, B_HEADS, HEAD_DIM),
            vb_p.reshape(BATCH, 1, SEQ, B_HEADS, HEAD_DIM))
```

```python
import functools

import numpy as np
import jax
import jax.numpy as jnp
from jax import lax
from jax.experimental import pallas as pl
from jax.experimental.pallas import tpu as pltpu

D_MODEL = 2048
BATCH = 16
SEQ = 256
DEC_BATCH = 2
DEC_SEQ = 1024
PAST_LEN = 256
GRID_W = 64
HEAD_DIM = 128
A_HEADS = 8
A_KV_HEADS = 2
A_GROUP = A_HEADS // A_KV_HEADS
A_WINDOW = 128
BLOCK = 128
B_HEADS = 8
NA_ROWS_MAX = 8
NA_COLS = 16
D_FF = 4 * D_MODEL
ROPE_THETA = 10000.0
EPS = 1e-6
NEG = -1e30
A_Q = A_HEADS * HEAD_DIM
A_KV = A_KV_HEADS * HEAD_DIM
B_QKV = B_HEADS * HEAD_DIM
IN_WIDTH = A_Q + 2 * A_KV + 3 * B_QKV + 2 * D_MODEL
SCALE = HEAD_DIM ** -0.5

M_PROMPT = BATCH * SEQ
M_SAMPLE = DEC_BATCH * DEC_SEQ
M_ALL = M_PROMPT + M_SAMPLE
GRID_ROWS = DEC_SEQ // GRID_W
NA_ROWS = min(NA_ROWS_MAX, GRID_ROWS)

COL_QA = 0
COL_KVA = A_Q
COL_QB = A_Q + 2 * A_KV
COL_KB = COL_QB + B_QKV
COL_VB = COL_KB + B_QKV
COL_G = COL_VB + B_QKV

TM = 1024
TN = 1024
ROW_CHUNK = 256
W_SUB = 512
TM_NORM = 512
TF = 512
MLP_COLS = 512
VMEM_LIMIT = 56 * 1024 * 1024

F32 = jnp.float32
BF16 = jnp.bfloat16


def _mod_row(i, tm):
    p_tiles = M_PROMPT // tm
    return jnp.where(i < p_tiles, 0, 1 + (i - p_tiles) // (DEC_SEQ // tm))


def _dot(a, b):
    return jnp.dot(a, b, preferred_element_type=F32)


def _dot_nt(a, b):
    return lax.dot_general(a, b, (((1,), (1,)), ((), ())), preferred_element_type=F32)


def _params(sem, vmem=None):
    return pltpu.CompilerParams(dimension_semantics=sem, vmem_limit_bytes=vmem)


def _ada_kernel(c_ref, w_ref, b_ref, o_ref):
    cv = c_ref[...]
    s = (cv * jax.nn.sigmoid(cv)).astype(BF16)
    o_ref[...] = _dot(s, w_ref[...].astype(BF16)) + b_ref[...]


def _ada_call(cvecs, w_ada, b_ada):
    tn = 1024
    n = 6 * D_MODEL
    return pl.pallas_call(
        _ada_kernel,
        grid=(n // tn,),
        in_specs=[pl.BlockSpec((8, D_MODEL), lambda j: (0, 0)),
                  pl.BlockSpec((D_MODEL, tn), lambda j: (0, j)),
                  pl.BlockSpec((1, tn), lambda j: (0, j))],
        out_specs=pl.BlockSpec((8, tn), lambda j: (0, j)),
        out_shape=jax.ShapeDtypeStruct((8, n), F32),
        compiler_params=_params(("arbitrary",), VMEM_LIMIT),
        name="ada_mod",
    )(cvecs, w_ada, b_ada)


def _modnorm(x, nw, sc, sh):
    y = x * lax.rsqrt(jnp.mean(x * x, axis=-1, keepdims=True) + EPS)
    return y * (nw * (1.0 + sc)) + sh


def _head_norm(x, nw):
    return x * lax.rsqrt(jnp.mean(x * x, axis=-1, keepdims=True) + EPS) * nw


def _rope(x, cos, sin_signed):
    lane = lax.broadcasted_iota(jnp.int32, x.shape, 1)
    partner = jnp.where((lane % 64) < 32, pltpu.roll(x, 96, 1), pltpu.roll(x, 32, 1))
    return x * cos + partner * sin_signed


def _norm_kva_kernel(xp_ref, xs_ref, nw1_ref, sh_ref, sc_ref, w_ref, nwk_ref, cos_ref, sin_ref,
                     h_ref, kv_ref, kp_ref, vp_ref, wbf):
    i = pl.program_id(0)
    p_tiles = M_PROMPT // TM_NORM

    @pl.when(i == 0)
    def _():
        wbf[...] = w_ref[...].astype(BF16)

    def head_cols(k, base=0):
        return slice(base + k * HEAD_DIM, base + (k + 1) * HEAD_DIM)

    def run(prompt):
        x_ref = xp_ref if prompt else xs_ref
        for r0 in range(0, TM_NORM, ROW_CHUNK):
            rows = slice(r0, r0 + ROW_CHUNK)
            h = _modnorm(x_ref[rows, :], nw1_ref[...], sc_ref[...], sh_ref[...]).astype(BF16)
            h_ref[rows, :] = h
            acc = _dot(h, wbf[...])
            for k in range(A_KV_HEADS):
                y = _head_norm(acc[:, head_cols(k)], nwk_ref[...])
                v = acc[:, head_cols(k, A_KV)]
                if prompt:
                    kp_ref[rows, head_cols(k)] = y
                    vp_ref[rows, head_cols(k)] = v
                else:
                    y = _rope(y, cos_ref[rows, :], sin_ref[rows, :])
                kv_ref[rows, head_cols(k)] = y.astype(BF16)
                kv_ref[rows, head_cols(k, A_KV)] = v.astype(BF16)

    pl.when(i < p_tiles)(functools.partial(run, True))
    pl.when(i >= p_tiles)(functools.partial(run, False))


def _norm_kva_call(xp, xs, nw1, mods3, w_in, nwk, rope):
    tm = TM_NORM
    p_tiles = M_PROMPT // tm
    s_tiles = DEC_SEQ // tm
    assert COL_KVA % (2 * A_KV) == 0
    rope_spec = pl.BlockSpec((tm, HEAD_DIM), lambda i: (jnp.maximum(i - p_tiles, 0) % s_tiles, 0))
    parked = pl.BlockSpec((tm, A_KV), lambda i: (jnp.minimum(i, p_tiles - 1), 0))
    return pl.pallas_call(
        _norm_kva_kernel,
        grid=(M_ALL // tm,),
        in_specs=[pl.BlockSpec((tm, D_MODEL), lambda i: (jnp.minimum(i, p_tiles - 1), 0)),
                  pl.BlockSpec((tm, D_MODEL), lambda i: (jnp.maximum(i - p_tiles, 0), 0)),
                  pl.BlockSpec((1, D_MODEL), lambda i: (0, 0)),
                  pl.BlockSpec((None, 1, D_MODEL), lambda i: (_mod_row(i, tm), 0, 0)),
                  pl.BlockSpec((None, 1, D_MODEL), lambda i: (_mod_row(i, tm), 0, 1)),
                  pl.BlockSpec((D_MODEL, 2 * A_KV), lambda i: (0, COL_KVA // (2 * A_KV)),
                               pipeline_mode=pl.Buffered(1)),
                  pl.BlockSpec((1, HEAD_DIM), lambda i: (0, 0)),
                  rope_spec, rope_spec],
        out_specs=[pl.BlockSpec((tm, D_MODEL), lambda i: (i, 0)),
                   pl.BlockSpec((tm, 2 * A_KV), lambda i: (i, 0)),
                   parked, parked],
        out_shape=[jax.ShapeDtypeStruct((M_ALL, D_MODEL), BF16),
                   jax.ShapeDtypeStruct((M_ALL, 2 * A_KV), BF16),
                   jax.ShapeDtypeStruct((M_PROMPT, A_KV), F32),
                   jax.ShapeDtypeStruct((M_PROMPT, A_KV), F32)],
        scratch_shapes=[pltpu.VMEM((D_MODEL, 2 * A_KV), BF16)],
        compiler_params=_params(("arbitrary",), VMEM_LIMIT),
        name="norm1_proj_kva",
    )(xp, xs, nw1, mods3, mods3, w_in, nwk, *rope)


def _rope_tables():
    n_freq = HEAD_DIM // 4
    pos = np.arange(DEC_SEQ)
    row = (pos // GRID_W).astype(np.float64)
    col = (pos % GRID_W).astype(np.float64)
    inv = ROPE_THETA ** (-np.arange(n_freq, dtype=np.float64) / n_freq)
    ar = row[:, None] * inv
    ac = col[:, None] * inv
    cos = np.concatenate([np.cos(ar), np.cos(ar), np.cos(ac), np.cos(ac)], axis=-1)
    sin = np.concatenate([-np.sin(ar), np.sin(ar), -np.sin(ac), np.sin(ac)], axis=-1)
    return jnp.asarray(cos, F32), jnp.asarray(sin, F32)


PROJ_KINDS = ("qa", "qb", "kb", "vb") + ("gate",) * (2 * D_MODEL // TN)
PROJ_COLS = (COL_QA, COL_QB, COL_KB, COL_VB) + tuple(range(COL_G, IN_WIDTH, TN))
N_QKV = 4


def _proj_kernel(h_ref, w0_ref, w1_ref, nqa_ref, nqb_ref, nkb_ref, cos_ref, sin_ref,
                 qkv_ref, gate_ref, kvp_ref, wbf):
    j, i = pl.program_id(0), pl.program_id(1)
    is_prompt = i < M_PROMPT // TM

    @pl.when(i == 0)
    def _():
        wbf[:, :W_SUB] = w0_ref[...].astype(BF16)
        wbf[:, W_SUB:] = w1_ref[...].astype(BF16)

    def head_cols(k):
        return slice(k * HEAD_DIM, (k + 1) * HEAD_DIM)

    def run(kind, prompt):
        for r0 in range(0, TM, ROW_CHUNK):
            rows = slice(r0, r0 + ROW_CHUNK)
            acc = _dot(h_ref[rows, :], wbf[...])
            if kind == "gate":
                gate_ref[rows, :] = jax.nn.sigmoid(acc)
            elif kind == "vb":
                if prompt:
                    kvp_ref[rows, :] = acc
                qkv_ref[rows, :] = acc.astype(BF16)
            else:
                nw_ref = {"qa": nqa_ref, "qb": nqb_ref, "kb": nkb_ref}[kind]
                for k in range(TN // HEAD_DIM):
                    y = _head_norm(acc[:, head_cols(k)], nw_ref[...])
                    if kind == "qa" and not prompt:
                        y = _rope(y, cos_ref[rows, :], sin_ref[rows, :])
                    if kind == "kb" and prompt:
                        kvp_ref[rows, head_cols(k)] = y
                    qkv_ref[rows, head_cols(k)] = y.astype(BF16)

    for p, kind in enumerate(PROJ_KINDS[:N_QKV]):
        if kind == "qb":
            pl.when(j == p)(functools.partial(run, kind, None))
        else:
            pl.when((j == p) & is_prompt)(functools.partial(run, kind, True))
            pl.when((j == p) & jnp.logical_not(is_prompt))(functools.partial(run, kind, False))
    pl.when(j >= N_QKV)(functools.partial(run, "gate", None))


def _proj_call(h, w_in, nqa, nqb, nkb, rope):
    assert PROJ_KINDS[:N_QKV] == ("qa", "qb", "kb", "vb") and TN == 2 * W_SUB and A_Q == B_QKV == TN
    n_panels = len(PROJ_KINDS)
    p_tiles = M_PROMPT // TM
    s_tiles = DEC_SEQ // TM
    n_tiles = M_ALL // TM
    kb_panel, vb_panel = PROJ_KINDS.index("kb"), PROJ_KINDS.index("vb")

    def w_sub(j):
        idx = PROJ_COLS[0] // W_SUB
        for p in range(1, n_panels):
            idx = jnp.where(j == p, PROJ_COLS[p] // W_SUB, idx)
        return idx

    def qkv_idx(j, i):
        done = j >= N_QKV
        return (jnp.where(done, n_tiles - 1, i), jnp.minimum(j, N_QKV - 1))

    def gate_idx(j, i):
        return (jnp.where(j < N_QKV, 0, i), jnp.maximum(j - N_QKV, 0))

    def kvp_idx(j, i):
        row = jnp.where(j < kb_panel, 0, jnp.where(j > vb_panel, p_tiles - 1, jnp.minimum(i, p_tiles - 1)))
        return (row, jnp.clip(j - kb_panel, 0, 1))

    norm_spec = pl.BlockSpec((1, HEAD_DIM), lambda j, i: (0, 0))
    rope_spec = pl.BlockSpec((TM, HEAD_DIM), lambda j, i: (jnp.maximum(i - p_tiles, 0) % s_tiles, 0))
    return pl.pallas_call(
        _proj_kernel,
        grid=(n_panels, n_tiles),
        in_specs=[pl.BlockSpec((TM, D_MODEL), lambda j, i: (i, 0)),
                  pl.BlockSpec((D_MODEL, W_SUB), lambda j, i: (0, w_sub(j))),
                  pl.BlockSpec((D_MODEL, W_SUB), lambda j, i: (0, w_sub(j) + 1)),
                  norm_spec, norm_spec, norm_spec, rope_spec, rope_spec],
        out_specs=[pl.BlockSpec((TM, TN), qkv_idx),
                   pl.BlockSpec((TM, TN), gate_idx),
                   pl.BlockSpec((TM, TN), kvp_idx)],
        out_shape=[jax.ShapeDtypeStruct((M_ALL, N_QKV * TN), BF16),
                   jax.ShapeDtypeStruct((M_ALL, 2 * D_MODEL), F32),
                   jax.ShapeDtypeStruct((M_PROMPT, 2 * B_QKV), F32)],
        scratch_shapes=[pltpu.VMEM((D_MODEL, TN), BF16)],
        compiler_params=_params(("arbitrary", "arbitrary"), VMEM_LIMIT),
        name="proj",
    )(h, w_in, w_in, nqa, nqb, nkb, *rope)


LOG2E = 1.4426950408889634
QK_LOG2 = SCALE * LOG2E


def _softmax_pv(parts, sink):
    m = functools.reduce(jnp.maximum, [jnp.max(t, axis=-1, keepdims=True) for t, _ in parts])
    if sink is not None:
        m = jnp.maximum(m, sink)
    acc = None
    for t, v in parts:
        p = jnp.exp2(t - m).astype(BF16)
        y = _dot(p, jnp.concatenate([v, jnp.ones_like(v)], axis=1))
        acc = y if acc is None else acc + y
    o, l = acc[:, :HEAD_DIM], acc[:, HEAD_DIM:]
    if sink is not None:
        l = l + jnp.exp2(sink - m)
    return o / l


def _sink_column(sink_ref, hk, rows_per_head):
    n = A_GROUP * rows_per_head
    g = lax.broadcasted_iota(jnp.int32, (n, 1), 0) // rows_per_head
    col = jnp.full((n, 1), sink_ref[hk * A_GROUP], F32)
    for k in range(1, A_GROUP):
        col = jnp.where(g == k, sink_ref[hk * A_GROUP + k], col)
    return col * LOG2E


def _stack_group(q_ref, hk):
    return jnp.concatenate(
        [q_ref[:, (hk * A_GROUP + g) * HEAD_DIM:(hk * A_GROUP + g + 1) * HEAD_DIM] for g in range(A_GROUP)], axis=0)


def _unstack_group(o_ref, hk, o, rows):
    for g in range(A_GROUP):
        c = (hk * A_GROUP + g) * HEAD_DIM
        o_ref[:, c:c + HEAD_DIM] = o[g * rows:(g + 1) * rows].astype(BF16)


def _ctx_attn_kernel(sink_ref, qa_ref, kva_ref, qb_ref, kb_ref, vb_ref, oa_ref, ob_ref):
    for hk in range(A_KV_HEADS):
        k = kva_ref[:, hk * HEAD_DIM:(hk + 1) * HEAD_DIM]
        v = kva_ref[:, A_KV + hk * HEAD_DIM:A_KV + (hk + 1) * HEAD_DIM]
        q4 = _stack_group(qa_ref, hk)
        t = _dot_nt(q4, k) * QK_LOG2
        o = _softmax_pv([(t, v)], _sink_column(sink_ref, hk, SEQ))
        _unstack_group(oa_ref, hk, o, SEQ)
    for h in range(B_HEADS):
        sl = slice(h * HEAD_DIM, (h + 1) * HEAD_DIM)
        t = _dot_nt(qb_ref[:, sl], kb_ref[:, sl]) * QK_LOG2
        ob_ref[:, sl] = _softmax_pv([(t, vb_ref[:, sl])], None).astype(BF16)


def _ctx_attn_call(sink, qkv, kva):
    row = lambda w: pl.BlockSpec((SEQ, w), lambda b: (b, 0))
    panel = lambda kind: pl.BlockSpec((SEQ, TN), lambda b: (b, PROJ_KINDS.index(kind)))
    return pl.pallas_call(
        _ctx_attn_kernel,
        grid=(BATCH,),
        in_specs=[pl.BlockSpec(memory_space=pltpu.SMEM),
                  panel("qa"), row(2 * A_KV), panel("qb"), panel("kb"), panel("vb")],
        out_specs=[row(A_Q), row(B_QKV)],
        out_shape=[jax.ShapeDtypeStruct((M_PROMPT, A_Q), BF16),
                   jax.ShapeDtypeStruct((M_PROMPT, B_QKV), BF16)],
        compiler_params=_params(("arbitrary",), VMEM_LIMIT),
        name="attn_ctx",
    )(sink, qkv, kva, qkv, qkv, qkv)


BAND = 3 * BLOCK


def _cache_to_bf16(n_heads, ck_ref, cv_ref, ckb, cvb):
    for h in range(n_heads):
        sl = slice(h * HEAD_DIM, (h + 1) * HEAD_DIM)
        ckb[:, sl] = ck_ref[:, h, :].astype(BF16)
        cvb[:, sl] = cv_ref[:, h, :].astype(BF16)


def _win_attn_kernel(sink_ref, q_ref, kv_ref, ck_ref, cv_ref, o_ref, ckb, cvb):
    n = pl.program_id(1)
    pl.when(n == 0)(functools.partial(_cache_to_bf16, A_KV_HEADS, ck_ref, cv_ref, ckb, cvb))
    start = pl.multiple_of(jnp.clip((n - 1) * BLOCK, 0, DEC_SEQ - BAND), BLOCK)
    qpos = n * BLOCK + lax.broadcasted_iota(jnp.int32, (BLOCK, BAND), 0)
    kpos = start + lax.broadcasted_iota(jnp.int32, (BLOCK, BAND), 1)
    valid = jnp.abs(qpos - kpos) <= A_WINDOW
    valid = jnp.concatenate([valid.astype(jnp.int32)] * A_GROUP, axis=0) > 0
    for hk in range(A_KV_HEADS):
        sl = slice(hk * HEAD_DIM, (hk + 1) * HEAD_DIM)
        slv = slice(A_KV + hk * HEAD_DIM, A_KV + (hk + 1) * HEAD_DIM)
        k_loc = kv_ref[pl.ds(start, BAND), sl]
        v_loc = kv_ref[pl.ds(start, BAND), slv]
        q4 = _stack_group(q_ref, hk)
        t_loc = jnp.where(valid, _dot_nt(q4, k_loc) * QK_LOG2, NEG)
        t_ctx = _dot_nt(q4, ckb[:, sl]) * QK_LOG2
        o = _softmax_pv([(t_loc, v_loc), (t_ctx, cvb[:, sl])], _sink_column(sink_ref, hk, BLOCK))
        _unstack_group(o_ref, hk, o, BLOCK)


def _win_attn_call(sink, qkv, kva, cache_k, cache_v):
    nb = DEC_SEQ // BLOCK
    q0 = M_PROMPT // BLOCK
    b0 = M_PROMPT // DEC_SEQ
    return pl.pallas_call(
        _win_attn_kernel,
        grid=(DEC_BATCH, nb),
        in_specs=[pl.BlockSpec(memory_space=pltpu.SMEM),
                  pl.BlockSpec((BLOCK, A_Q), lambda b, n: (q0 + b * nb + n, PROJ_KINDS.index("qa"))),
                  pl.BlockSpec((DEC_SEQ, 2 * A_KV), lambda b, n: (b0 + b, 0)),
                  pl.BlockSpec((None, None, PAST_LEN, A_KV_HEADS, HEAD_DIM), lambda b, n: (b, 0, 0, 0, 0)),
                  pl.BlockSpec((None, None, PAST_LEN, A_KV_HEADS, HEAD_DIM), lambda b, n: (b, 0, 0, 0, 0))],
        out_specs=pl.BlockSpec((BLOCK, A_Q), lambda b, n: (b * nb + n, 0)),
        out_shape=jax.ShapeDtypeStruct((M_SAMPLE, A_Q), BF16),
        scratch_shapes=[pltpu.VMEM((PAST_LEN, A_KV), BF16), pltpu.VMEM((PAST_LEN, A_KV), BF16)],
        compiler_params=_params(("arbitrary", "arbitrary"), VMEM_LIMIT),
        name="attn_window",
    )(sink, qkv, kva, cache_k, cache_v)


NA_QROWS = 4
NA_WIN_ROWS = 12
NA_Q = NA_QROWS * GRID_W
NA_KEYS = NA_WIN_ROWS * GRID_W
NA_DR = 2 * NA_ROWS_MAX - 1
NA_PAIRS = NA_DR + 1
RPB_W = 2 * NA_COLS - 1
RPB_PAD = GRID_W - NA_COLS


def _na_row_start(r):
    return jnp.clip(r - NA_ROWS // 2, 0, GRID_ROWS - NA_ROWS)


def _na_window_row0(blk):
    return jnp.clip(blk * NA_QROWS - NA_ROWS // 2, 0, GRID_ROWS - NA_WIN_ROWS)


def _check_na_windows():
    for blk in range(GRID_ROWS // NA_QROWS):
        w0 = int(np.clip(blk * NA_QROWS - NA_ROWS // 2, 0, GRID_ROWS - NA_WIN_ROWS))
        for r in range(blk * NA_QROWS, (blk + 1) * NA_QROWS):
            r0 = int(np.clip(r - NA_ROWS // 2, 0, GRID_ROWS - NA_ROWS))
            assert w0 <= r0 and r0 + NA_ROWS <= w0 + NA_WIN_ROWS, (blk, r)


_check_na_windows()


def _na_attn_kernel(q_ref, k_ref, v_ref, ck_ref, cv_ref, rpb_ref, o_ref, tab_ref, ckb, cvb):
    b, blk = pl.program_id(0), pl.program_id(1)
    lane = lax.broadcasted_iota(jnp.int32, (GRID_W, 2 * GRID_W), 1)
    pl.when(blk == 0)(functools.partial(_cache_to_bf16, B_HEADS, ck_ref, cv_ref, ckb, cvb))

    @pl.when((b == 0) & (blk == 0))
    def _():
        for h in range(B_HEADS):
            for d in range(NA_PAIRS):
                lo = jnp.broadcast_to(rpb_ref[h, d:d + 1, :], (GRID_W, 2 * GRID_W))
                hi = jnp.broadcast_to(rpb_ref[h, d + 1:d + 2, :], (GRID_W, 2 * GRID_W))
                lo = pltpu.roll(lo, GRID_W + 1, 1, stride=1, stride_axis=0)
                hi = pltpu.roll(hi, 1, 1, stride=1, stride_axis=0)
                tab_ref[h, d] = jnp.where(lane < GRID_W, lo, hi) * LOG2E

    row0 = _na_window_row0(blk)
    k0 = pl.multiple_of(row0 * GRID_W, GRID_W)
    qi = lax.broadcasted_iota(jnp.int32, (NA_Q, NA_KEYS), 0)
    ki = lax.broadcasted_iota(jnp.int32, (NA_Q, NA_KEYS), 1)
    qrow, qcol = blk * NA_QROWS + qi // GRID_W, qi % GRID_W
    krow, kcol = row0 + ki // GRID_W, ki % GRID_W
    rstart = _na_row_start(qrow)
    cstart = jnp.clip(qcol - NA_COLS // 2, 0, GRID_W - NA_COLS)
    valid = (krow >= rstart) & (krow < rstart + NA_ROWS) & (kcol >= cstart) & (kcol < cstart + NA_COLS)

    for h in range(B_HEADS):
        sl = slice(h * HEAD_DIM, (h + 1) * HEAD_DIM)
        bias = jnp.concatenate(
            [jnp.concatenate(
                [tab_ref[h, jnp.clip(row0 + 2 * p - blk * NA_QROWS - q + NA_ROWS_MAX, 0, NA_PAIRS - 1)]
                 for p in range(NA_WIN_ROWS // 2)], axis=1)
             for q in range(NA_QROWS)], axis=0)
        k_loc = k_ref[pl.ds(k0, NA_KEYS), sl]
        v_loc = v_ref[pl.ds(k0, NA_KEYS), sl]
        t_loc = jnp.where(valid, _dot_nt(q_ref[:, sl], k_loc) * QK_LOG2 + bias, NEG)
        t_ctx = _dot_nt(q_ref[:, sl], ckb[:, sl]) * QK_LOG2
        o = _softmax_pv([(t_loc, v_loc), (t_ctx, cvb[:, sl])], None)
        o_ref[:, sl] = o.astype(BF16)


def _na_bias_rows(rpb):
    rows = jnp.pad(rpb, ((0, 0), (1, 1), (0, 0)))
    left = jnp.broadcast_to(rows[..., :1], rows.shape[:2] + (RPB_PAD,))
    right = jnp.broadcast_to(rows[..., -1:], rows.shape[:2] + (2 * GRID_W - RPB_PAD - RPB_W,))
    return jnp.concatenate([left, rows, right], axis=-1)


def _na_attn_call(qkv, cache_k, cache_v, rpb):
    nblk = GRID_ROWS // NA_QROWS
    q0 = M_PROMPT // NA_Q
    b0 = M_PROMPT // DEC_SEQ
    return pl.pallas_call(
        _na_attn_kernel,
        grid=(DEC_BATCH, nblk),
        in_specs=[pl.BlockSpec((NA_Q, B_QKV), lambda b, r: (q0 + b * nblk + r, PROJ_KINDS.index("qb"))),
                  pl.BlockSpec((DEC_SEQ, B_QKV), lambda b, r: (b0 + b, PROJ_KINDS.index("kb"))),
                  pl.BlockSpec((DEC_SEQ, B_QKV), lambda b, r: (b0 + b, PROJ_KINDS.index("vb"))),
                  pl.BlockSpec((None, None, PAST_LEN, B_HEADS, HEAD_DIM), lambda b, r: (b, 0, 0, 0, 0)),
                  pl.BlockSpec((None, None, PAST_LEN, B_HEADS, HEAD_DIM), lambda b, r: (b, 0, 0, 0, 0)),
                  pl.BlockSpec((B_HEADS, NA_DR + 2, 2 * GRID_W), lambda b, r: (0, 0, 0))],
        out_specs=pl.BlockSpec((NA_Q, B_QKV), lambda b, r: (b * nblk + r, 0)),
        out_shape=jax.ShapeDtypeStruct((M_SAMPLE, B_QKV), BF16),
        scratch_shapes=[pltpu.VMEM((B_HEADS, NA_PAIRS, GRID_W, 2 * GRID_W), F32),
                        pltpu.VMEM((PAST_LEN, B_QKV), BF16), pltpu.VMEM((PAST_LEN, B_QKV), BF16)],
        compiler_params=_params(("arbitrary", "arbitrary"), VMEM_LIMIT),
        name="attn_neighbourhood",
    )(qkv, qkv, qkv, cache_k, cache_v, _na_bias_rows(rpb))


def _merge_kernel(oap, oas, obp, obs, wa_ref, wb_ref, ga_ref, gb_ref, z_ref, wa_bf, wb_bf):
    i = pl.program_id(1)
    p_tiles = M_PROMPT // TM

    @pl.when(i == 0)
    def _():
        wa_bf[...] = wa_ref[...].astype(BF16)
        wb_bf[...] = wb_ref[...].astype(BF16)

    def run(oa_ref, ob_ref):
        for r0 in range(0, TM, ROW_CHUNK):
            rows = slice(r0, r0 + ROW_CHUNK)
            ya = _dot(oa_ref[rows, :], wa_bf[...])
            yb = _dot(ob_ref[rows, :], wb_bf[...])
            z_ref[rows, :] = (ga_ref[rows, :] * ya + gb_ref[rows, :] * yb).astype(BF16)

    pl.when(i < p_tiles)(functools.partial(run, oap, obp))
    pl.when(i >= p_tiles)(functools.partial(run, oas, obs))


def _merge_call(oa_p, oa_s, ob_p, ob_s, w_br_a, w_br_b, gates):
    p_tiles = M_PROMPT // TM
    nj = D_MODEL // TN
    pspec = pl.BlockSpec((TM, A_Q), lambda j, i: (jnp.minimum(i, p_tiles - 1), 0))
    sspec = pl.BlockSpec((TM, A_Q), lambda j, i: (jnp.maximum(i - p_tiles, 0), 0))
    once = dict(pipeline_mode=pl.Buffered(1))
    return pl.pallas_call(
        _merge_kernel,
        grid=(nj, M_ALL // TM),
        in_specs=[pspec, sspec, pspec, sspec,
                  pl.BlockSpec((A_Q, TN), lambda j, i: (0, j), **once),
                  pl.BlockSpec((B_QKV, TN), lambda j, i: (0, j), **once),
                  pl.BlockSpec((TM, TN), lambda j, i: (i, j)),
                  pl.BlockSpec((TM, TN), lambda j, i: (i, nj + j))],
        out_specs=pl.BlockSpec((TM, TN), lambda j, i: (i, j)),
        out_shape=jax.ShapeDtypeStruct((M_ALL, D_MODEL), BF16),
        scratch_shapes=[pltpu.VMEM((A_Q, TN), BF16), pltpu.VMEM((B_QKV, TN), BF16)],
        compiler_params=_params(("arbitrary", "arbitrary"), VMEM_LIMIT),
        name="merge_branches",
    )(oa_p, oa_s, ob_p, ob_s, w_br_a, w_br_b, gates, gates)


def _out_kernel(z_ref, w_ref, xp_ref, xs_ref, g_ref, o_ref, wbf):
    i = pl.program_id(1)
    p_tiles = M_PROMPT // TM

    @pl.when(i == 0)
    def _():
        wbf[...] = w_ref[...].astype(BF16)

    def run(x_ref):
        for r0 in range(0, TM, ROW_CHUNK):
            rows = slice(r0, r0 + ROW_CHUNK)
            o_ref[rows, :] = x_ref[rows, :] + g_ref[...] * _dot(z_ref[rows, :], wbf[...])

    pl.when(i < p_tiles)(functools.partial(run, xp_ref))
    pl.when(i >= p_tiles)(functools.partial(run, xs_ref))


def _out_call(z, w_out, xp, xs, mods3):
    p_tiles = M_PROMPT // TM
    nj = D_MODEL // TN
    return pl.pallas_call(
        _out_kernel,
        grid=(nj, M_ALL // TM),
        in_specs=[pl.BlockSpec((TM, D_MODEL), lambda j, i: (i, 0)),
                  pl.BlockSpec((D_MODEL, TN), lambda j, i: (0, j)),
                  pl.BlockSpec((TM, TN), lambda j, i: (jnp.minimum(i, p_tiles - 1), j)),
                  pl.BlockSpec((TM, TN), lambda j, i: (jnp.maximum(i - p_tiles, 0), j)),
                  pl.BlockSpec((None, 1, TN), lambda j, i: (_mod_row(i, TM), 0, 2 * nj + j))],
        out_specs=pl.BlockSpec((TM, TN), lambda j, i: (i, j)),
        out_shape=jax.ShapeDtypeStruct((M_ALL, D_MODEL), F32),
        scratch_shapes=[pltpu.VMEM((D_MODEL, TN), BF16)],
        compiler_params=_params(("arbitrary", "arbitrary"), VMEM_LIMIT),
        name="out_proj_residual",
    )(z, w_out, xp, xs, mods3)


def _mlp_kernel(x_ref, nw_ref, sh_ref, sc_ref, g_ref, wu_ref, wd_ref, o_ref, h_ref):
    f = pl.program_id(1)

    @pl.when(f == 0)
    def _():
        wu = wu_ref[...].astype(BF16)
        wd = wd_ref[...].astype(BF16)
        for r0 in range(0, TM, ROW_CHUNK):
            rows = slice(r0, r0 + ROW_CHUNK)
            h = _modnorm(x_ref[rows, :], nw_ref[...], sc_ref[...], sh_ref[...]).astype(BF16)
            h_ref[rows, :] = h
            u = jnp.square(jnp.maximum(_dot(h, wu), 0.0)).astype(BF16)
            o_ref[rows, :] = _dot(u, wd)

    @pl.when(f > 0)
    def _():
        u = _dot(h_ref[...], wu_ref[...].astype(BF16))
        u = jnp.square(jnp.maximum(u, 0.0)).astype(BF16)
        for c0 in range(0, D_MODEL, MLP_COLS):
            cols = slice(c0, c0 + MLP_COLS)
            o_ref[:, cols] += _dot(u, wd_ref[:, cols].astype(BF16))

    @pl.when(f == pl.num_programs(1) - 1)
    def _():
        o_ref[...] = x_ref[...] + g_ref[...] * o_ref[...]


def _mlp_call(x1, nw, mods3, w_up, w_down, tile0, n_tiles):
    mod = lambda k: pl.BlockSpec((None, 1, D_MODEL), lambda i, f: (_mod_row(tile0 + i, TM), 0, k))
    return pl.pallas_call(
        _mlp_kernel,
        grid=(n_tiles, D_FF // TF),
        in_specs=[pl.BlockSpec((TM, D_MODEL), lambda i, f: (tile0 + i, 0), pipeline_mode=pl.Buffered(1)),
                  pl.BlockSpec((1, D_MODEL), lambda i, f: (0, 0)),
                  mod(3), mod(4), mod(5),
                  pl.BlockSpec((D_MODEL, TF), lambda i, f: (0, f)),
                  pl.BlockSpec((TF, D_MODEL), lambda i, f: (f, 0))],
        out_specs=pl.BlockSpec((TM, D_MODEL), lambda i, f: (i, 0)),
        out_shape=jax.ShapeDtypeStruct((n_tiles * TM, D_MODEL), F32),
        scratch_shapes=[pltpu.VMEM((TM, D_MODEL), BF16)],
        compiler_params=_params(("arbitrary", "arbitrary"), VMEM_LIMIT),
        name="mlp",
    )(x1, nw, mods3, mods3, mods3, w_up, w_down)


def kernel(x_prompt, x_sample, cache_a_k, cache_a_v, cache_b_k, cache_b_v, c, c_ctx, norm1_w, norm2_w, w_ada, b_ada, w_in, q_norm_a, k_norm_a, q_norm_b, k_norm_b, sink_a, rpb_b, w_br_a, w_br_b, w_out, w_up, w_down):
    assert w_ada.shape[0] == 1, "one trunk layer"
    xp = x_prompt.reshape(M_PROMPT, D_MODEL)
    xs = x_sample.reshape(M_SAMPLE, D_MODEL)

    cvecs = jnp.concatenate([c_ctx[None, :], c, jnp.zeros((8 - 1 - DEC_BATCH, D_MODEL), F32)], axis=0)
    mods = _ada_call(cvecs, w_ada[0], b_ada)
    mods3 = mods.reshape(8, 1, 6 * D_MODEL)

    w = w_in[0]
    rope = _rope_tables()
    h, kva, ka_p, va_p = _norm_kva_call(xp, xs, norm1_w, mods3, w, k_norm_a, rope)
    qkv, gates, kvb_p = _proj_call(h, w, q_norm_a, q_norm_b, k_norm_b, rope)

    sink = sink_a[0]
    oa_p, ob_p = _ctx_attn_call(sink, qkv, kva)
    oa_s = _win_attn_call(sink, qkv, kva, cache_a_k, cache_a_v)
    ob_s = _na_attn_call(qkv, cache_b_k, cache_b_v, rpb_b[0])

    z = _merge_call(oa_p, oa_s, ob_p, ob_s, w_br_a[0], w_br_b[0], gates)
    x1 = _out_call(z, w_out[0], xp, xs, mods3)

    p_tiles = M_PROMPT // TM
    y_p = _mlp_call(x1, norm2_w, mods3, w_up[0], w_down[0], 0, p_tiles)
    y_s = _mlp_call(x1, norm2_w, mods3, w_up[0], w_down[0], p_tiles, M_SAMPLE // TM)

    return (y_p.reshape(BATCH, SEQ, D_MODEL),
            y_s.reshape(DEC_BATCH, DEC_SEQ, D_MODEL),
            ka_p.reshape(BATCH, 1, SEQ, A_KV_HEADS, HEAD_DIM),
            va_p.reshape(BATCH, 1, SEQ, A_KV_HEADS, HEAD_DIM),
            kvb_p[:, :B_QKV].reshape(BATCH, 1, SEQ, B_HEADS, HEAD_DIM),
            kvb_p[:, B_QKV:].reshape(BATCH, 1, SEQ, B_HEADS, HEAD_DIM))
```

```python
import functools

import numpy as np
import jax
import jax.numpy as jnp
from jax import lax
from jax.experimental import pallas as pl
from jax.experimental.pallas import tpu as pltpu

D_MODEL = 2048
BATCH = 16
SEQ = 256
DEC_BATCH = 2
DEC_SEQ = 1024
PAST_LEN = 256
GRID_W = 64
HEAD_DIM = 128
A_HEADS = 8
A_KV_HEADS = 2
A_GROUP = A_HEADS // A_KV_HEADS
A_WINDOW = 128
BLOCK = 128
B_HEADS = 8
NA_ROWS_MAX = 8
NA_COLS = 16
D_FF = 4 * D_MODEL
ROPE_THETA = 10000.0
EPS = 1e-6
NEG = -1e30
A_Q = A_HEADS * HEAD_DIM
A_KV = A_KV_HEADS * HEAD_DIM
B_QKV = B_HEADS * HEAD_DIM
IN_WIDTH = A_Q + 2 * A_KV + 3 * B_QKV + 2 * D_MODEL
SCALE = HEAD_DIM ** -0.5

M_PROMPT = BATCH * SEQ
M_SAMPLE = DEC_BATCH * DEC_SEQ
M_ALL = M_PROMPT + M_SAMPLE
GRID_ROWS = DEC_SEQ // GRID_W
NA_ROWS = min(NA_ROWS_MAX, GRID_ROWS)

COL_QA = 0
COL_KVA = A_Q
COL_QB = A_Q + 2 * A_KV
COL_KB = COL_QB + B_QKV
COL_VB = COL_KB + B_QKV
COL_G = COL_VB + B_QKV

TM = 1024
TN = 1024
ROW_CHUNK = 256
W_SUB = 512
TM_NORM = 512
TF = 512
MLP_COLS = 512
VMEM_LIMIT = 56 * 1024 * 1024

F32 = jnp.float32
BF16 = jnp.bfloat16


def _mod_row(i, tm):
    p_tiles = M_PROMPT // tm
    return jnp.where(i < p_tiles, 0, 1 + (i - p_tiles) // (DEC_SEQ // tm))


def _dot(a, b):
    return jnp.dot(a, b, preferred_element_type=F32)


def _dot_nt(a, b):
    return lax.dot_general(a, b, (((1,), (1,)), ((), ())), preferred_element_type=F32)


def _params(sem, vmem=None):
    return pltpu.CompilerParams(dimension_semantics=sem, vmem_limit_bytes=vmem)


def _ada_kernel(c_ref, w_ref, b_ref, o_ref):
    cv = c_ref[...]
    s = (cv * jax.nn.sigmoid(cv)).astype(BF16)
    o_ref[...] = _dot(s, w_ref[...].astype(BF16)) + b_ref[...]


def _ada_call(cvecs, w_ada, b_ada):
    tn = 1024
    n = 6 * D_MODEL
    return pl.pallas_call(
        _ada_kernel,
        grid=(n // tn,),
        in_specs=[pl.BlockSpec((8, D_MODEL), lambda j: (0, 0)),
                  pl.BlockSpec((D_MODEL, tn), lambda j: (0, j)),
                  pl.BlockSpec((1, tn), lambda j: (0, j))],
        out_specs=pl.BlockSpec((8, tn), lambda j: (0, j)),
        out_shape=jax.ShapeDtypeStruct((8, n), F32),
        compiler_params=_params(("arbitrary",), VMEM_LIMIT),
        name="ada_mod",
    )(cvecs, w_ada, b_ada)


def _modnorm(x, nw, sc, sh):
    y = x * lax.rsqrt(jnp.mean(x * x, axis=-1, keepdims=True) + EPS)
    return y * (nw * (1.0 + sc)) + sh


def _head_norm(x, nw):
    return x * lax.rsqrt(jnp.mean(x * x, axis=-1, keepdims=True) + EPS) * nw


def _rope(x, cos, sin_signed):
    lane = lax.broadcasted_iota(jnp.int32, x.shape, 1)
    partner = jnp.where((lane % 64) < 32, pltpu.roll(x, 96, 1), pltpu.roll(x, 32, 1))
    return x * cos + partner * sin_signed


def _norm_kva_kernel(xp_ref, xs_ref, nw1_ref, sh_ref, sc_ref, w_ref, nwk_ref, cos_ref, sin_ref,
                     h_ref, kv_ref, kp_ref, vp_ref, wbf):
    i = pl.program_id(0)
    p_tiles = M_PROMPT // TM_NORM

    @pl.when(i == 0)
    def _():
        wbf[...] = w_ref[...].astype(BF16)

    def head_cols(k, base=0):
        return slice(base + k * HEAD_DIM, base + (k + 1) * HEAD_DIM)

    def run(prompt):
        x_ref = xp_ref if prompt else xs_ref
        for r0 in range(0, TM_NORM, ROW_CHUNK):
            rows = slice(r0, r0 + ROW_CHUNK)
            h = _modnorm(x_ref[rows, :], nw1_ref[...], sc_ref[...], sh_ref[...]).astype(BF16)
            h_ref[rows, :] = h
            acc = _dot(h, wbf[...])
            for k in range(A_KV_HEADS):
                y = _head_norm(acc[:, head_cols(k)], nwk_ref[...])
                v = acc[:, head_cols(k, A_KV)]
                if prompt:
                    kp_ref[rows, head_cols(k)] = y
                    vp_ref[rows, head_cols(k)] = v
                else:
                    y = _rope(y, cos_ref[rows, :], sin_ref[rows, :])
                kv_ref[rows, head_cols(k)] = y.astype(BF16)
                kv_ref[rows, head_cols(k, A_KV)] = v.astype(BF16)

    pl.when(i < p_tiles)(functools.partial(run, True))
    pl.when(i >= p_tiles)(functools.partial(run, False))


def _norm_kva_call(xp, xs, nw1, mods3, w_in, nwk, rope):
    tm = TM_NORM
    p_tiles = M_PROMPT // tm
    s_tiles = DEC_SEQ // tm
    assert COL_KVA % (2 * A_KV) == 0
    rope_spec = pl.BlockSpec((tm, HEAD_DIM), lambda i: (jnp.maximum(i - p_tiles, 0) % s_tiles, 0))
    parked = pl.BlockSpec((tm, A_KV), lambda i: (jnp.minimum(i, p_tiles - 1), 0))
    return pl.pallas_call(
        _norm_kva_kernel,
        grid=(M_ALL // tm,),
        in_specs=[pl.BlockSpec((tm, D_MODEL), lambda i: (jnp.minimum(i, p_tiles - 1), 0)),
                  pl.BlockSpec((tm, D_MODEL), lambda i: (jnp.maximum(i - p_tiles, 0), 0)),
                  pl.BlockSpec((1, D_MODEL), lambda i: (0, 0)),
                  pl.BlockSpec((None, 1, D_MODEL), lambda i: (_mod_row(i, tm), 0, 0)),
                  pl.BlockSpec((None, 1, D_MODEL), lambda i: (_mod_row(i, tm), 0, 1)),
                  pl.BlockSpec((D_MODEL, 2 * A_KV), lambda i: (0, COL_KVA // (2 * A_KV)),
                               pipeline_mode=pl.Buffered(1)),
                  pl.BlockSpec((1, HEAD_DIM), lambda i: (0, 0)),
                  rope_spec, rope_spec],
        out_specs=[pl.BlockSpec((tm, D_MODEL), lambda i: (i, 0)),
                   pl.BlockSpec((tm, 2 * A_KV), lambda i: (i, 0)),
                   parked, parked],
        out_shape=[jax.ShapeDtypeStruct((M_ALL, D_MODEL), BF16),
                   jax.ShapeDtypeStruct((M_ALL, 2 * A_KV), BF16),
                   jax.ShapeDtypeStruct((M_PROMPT, A_KV), F32),
                   jax.ShapeDtypeStruct((M_PROMPT, A_KV), F32)],
        scratch_shapes=[pltpu.VMEM((D_MODEL, 2 * A_KV), BF16)],
        compiler_params=_params(("arbitrary",), VMEM_LIMIT),
        name="norm1_proj_kva",
    )(xp, xs, nw1, mods3, mods3, w_in, nwk, *rope)


def _rope_tables():
    n_freq = HEAD_DIM // 4
    pos = np.arange(DEC_SEQ)
    row = (pos // GRID_W).astype(np.float64)
    col = (pos % GRID_W).astype(np.float64)
    inv = ROPE_THETA ** (-np.arange(n_freq, dtype=np.float64) / n_freq)
    ar = row[:, None] * inv
    ac = col[:, None] * inv
    cos = np.concatenate([np.cos(ar), np.cos(ar), np.cos(ac), np.cos(ac)], axis=-1)
    sin = np.concatenate([-np.sin(ar), np.sin(ar), -np.sin(ac), np.sin(ac)], axis=-1)
    return jnp.asarray(cos, F32), jnp.asarray(sin, F32)


PROJ_KINDS = ("qa", "qb", "kb", "vb") + ("gate",) * (2 * D_MODEL // TN)
PROJ_COLS = (COL_QA, COL_QB, COL_KB, COL_VB) + tuple(range(COL_G, IN_WIDTH, TN))
N_QKV = 4


def _proj_kernel(h_ref, w0_ref, w1_ref, nqa_ref, nqb_ref, nkb_ref, cos_ref, sin_ref,
                 qkv_ref, gate_ref, kvp_ref, wbf):
    j, i = pl.program_id(0), pl.program_id(1)
    is_prompt = i < M_PROMPT // TM

    @pl.when(i == 0)
    def _():
        wbf[:, :W_SUB] = w0_ref[...].astype(BF16)
        wbf[:, W_SUB:] = w1_ref[...].astype(BF16)

    def head_cols(k):
        return slice(k * HEAD_DIM, (k + 1) * HEAD_DIM)

    def run(kind, prompt):
        for r0 in range(0, TM, ROW_CHUNK):
            rows = slice(r0, r0 + ROW_CHUNK)
            acc = _dot(h_ref[rows, :], wbf[...])
            if kind == "gate":
                gate_ref[rows, :] = jax.nn.sigmoid(acc)
            elif kind == "vb":
                if prompt:
                    kvp_ref[rows, :] = acc
                qkv_ref[rows, :] = acc.astype(BF16)
            else:
                nw_ref = {"qa": nqa_ref, "qb": nqb_ref, "kb": nkb_ref}[kind]
                for k in range(TN // HEAD_DIM):
                    y = _head_norm(acc[:, head_cols(k)], nw_ref[...])
                    if kind == "qa" and not prompt:
                        y = _rope(y, cos_ref[rows, :], sin_ref[rows, :])
                    if kind == "kb" and prompt:
                        kvp_ref[rows, head_cols(k)] = y
                    qkv_ref[rows, head_cols(k)] = y.astype(BF16)

    for p, kind in enumerate(PROJ_KINDS[:N_QKV]):
        if kind == "qb":
            pl.when(j == p)(functools.partial(run, kind, None))
        else:
            pl.when((j == p) & is_prompt)(functools.partial(run, kind, True))
            pl.when((j == p) & jnp.logical_not(is_prompt))(functools.partial(run, kind, False))
    pl.when(j >= N_QKV)(functools.partial(run, "gate", None))


def _proj_call(h, w_in, nqa, nqb, nkb, rope):
    assert PROJ_KINDS[:N_QKV] == ("qa", "qb", "kb", "vb") and TN == 2 * W_SUB and A_Q == B_QKV == TN
    n_panels = len(PROJ_KINDS)
    p_tiles = M_PROMPT // TM
    s_tiles = DEC_SEQ // TM
    n_tiles = M_ALL // TM
    kb_panel, vb_panel = PROJ_KINDS.index("kb"), PROJ_KINDS.index("vb")

    def w_sub(j):
        idx = PROJ_COLS[0] // W_SUB
        for p in range(1, n_panels):
            idx = jnp.where(j == p, PROJ_COLS[p] // W_SUB, idx)
        return idx

    def qkv_idx(j, i):
        done = j >= N_QKV
        return (jnp.minimum(j, N_QKV - 1), jnp.where(done, n_tiles - 1, i), 0)

    def gate_idx(j, i):
        return (jnp.where(j < N_QKV, 0, i), jnp.maximum(j - N_QKV, 0))

    def kvp_idx(j, i):
        row = jnp.where(j < kb_panel, 0, jnp.where(j > vb_panel, p_tiles - 1, jnp.minimum(i, p_tiles - 1)))
        return (jnp.clip(j - kb_panel, 0, 1), row, 0)

    norm_spec = pl.BlockSpec((1, HEAD_DIM), lambda j, i: (0, 0))
    rope_spec = pl.BlockSpec((TM, HEAD_DIM), lambda j, i: (jnp.maximum(i - p_tiles, 0) % s_tiles, 0))
    return pl.pallas_call(
        _proj_kernel,
        grid=(n_panels, n_tiles),
        in_specs=[pl.BlockSpec((TM, D_MODEL), lambda j, i: (i, 0)),
                  pl.BlockSpec((D_MODEL, W_SUB), lambda j, i: (0, w_sub(j))),
                  pl.BlockSpec((D_MODEL, W_SUB), lambda j, i: (0, w_sub(j) + 1)),
                  norm_spec, norm_spec, norm_spec, rope_spec, rope_spec],
        out_specs=[pl.BlockSpec((None, TM, TN), qkv_idx),
                   pl.BlockSpec((TM, TN), gate_idx),
                   pl.BlockSpec((None, TM, TN), kvp_idx)],
        out_shape=[jax.ShapeDtypeStruct((N_QKV, M_ALL, TN), BF16),
                   jax.ShapeDtypeStruct((M_ALL, 2 * D_MODEL), F32),
                   jax.ShapeDtypeStruct((2, M_PROMPT, B_QKV), F32)],
        scratch_shapes=[pltpu.VMEM((D_MODEL, TN), BF16)],
        compiler_params=_params(("arbitrary", "arbitrary"), VMEM_LIMIT),
        name="proj",
    )(h, w_in, w_in, nqa, nqb, nkb, *rope)


LOG2E = 1.4426950408889634
QK_LOG2 = SCALE * LOG2E


def _softmax_pv(parts, sink):
    m = functools.reduce(jnp.maximum, [jnp.max(t, axis=-1, keepdims=True) for t, _ in parts])
    if sink is not None:
        m = jnp.maximum(m, sink)
    acc = None
    for t, v in parts:
        p = jnp.exp2(t - m).astype(BF16)
        y = _dot(p, jnp.concatenate([v, jnp.ones_like(v)], axis=1))
        acc = y if acc is None else acc + y
    o, l = acc[:, :HEAD_DIM], acc[:, HEAD_DIM:]
    if sink is not None:
        l = l + jnp.exp2(sink - m)
    return o / l


def _sink_column(sink_ref, hk, rows_per_head):
    n = A_GROUP * rows_per_head
    g = lax.broadcasted_iota(jnp.int32, (n, 1), 0) // rows_per_head
    col = jnp.full((n, 1), sink_ref[hk * A_GROUP], F32)
    for k in range(1, A_GROUP):
        col = jnp.where(g == k, sink_ref[hk * A_GROUP + k], col)
    return col * LOG2E


def _stack_group(q_ref, hk):
    return jnp.concatenate(
        [q_ref[:, (hk * A_GROUP + g) * HEAD_DIM:(hk * A_GROUP + g + 1) * HEAD_DIM] for g in range(A_GROUP)], axis=0)


def _unstack_group(o_ref, hk, o, rows):
    for g in range(A_GROUP):
        c = (hk * A_GROUP + g) * HEAD_DIM
        o_ref[:, c:c + HEAD_DIM] = o[g * rows:(g + 1) * rows].astype(BF16)


def _ctx_attn_kernel(sink_ref, qa_ref, kva_ref, qb_ref, kb_ref, vb_ref, oa_ref, ob_ref):
    for hk in range(A_KV_HEADS):
        k = kva_ref[:, hk * HEAD_DIM:(hk + 1) * HEAD_DIM]
        v = kva_ref[:, A_KV + hk * HEAD_DIM:A_KV + (hk + 1) * HEAD_DIM]
        q4 = _stack_group(qa_ref, hk)
        t = _dot_nt(q4, k) * QK_LOG2
        o = _softmax_pv([(t, v)], _sink_column(sink_ref, hk, SEQ))
        _unstack_group(oa_ref, hk, o, SEQ)
    for h in range(B_HEADS):
        sl = slice(h * HEAD_DIM, (h + 1) * HEAD_DIM)
        t = _dot_nt(qb_ref[:, sl], kb_ref[:, sl]) * QK_LOG2
        ob_ref[:, sl] = _softmax_pv([(t, vb_ref[:, sl])], None).astype(BF16)


def _ctx_attn_call(sink, qkv, kva):
    row = lambda w: pl.BlockSpec((SEQ, w), lambda b: (b, 0))
    panel = lambda kind: pl.BlockSpec((None, SEQ, TN), lambda b: (PROJ_KINDS.index(kind), b, 0))
    return pl.pallas_call(
        _ctx_attn_kernel,
        grid=(BATCH,),
        in_specs=[pl.BlockSpec(memory_space=pltpu.SMEM),
                  panel("qa"), row(2 * A_KV), panel("qb"), panel("kb"), panel("vb")],
        out_specs=[row(A_Q), row(B_QKV)],
        out_shape=[jax.ShapeDtypeStruct((M_PROMPT, A_Q), BF16),
                   jax.ShapeDtypeStruct((M_PROMPT, B_QKV), BF16)],
        compiler_params=_params(("arbitrary",), VMEM_LIMIT),
        name="attn_ctx",
    )(sink, qkv, kva, qkv, qkv, qkv)


BAND = 3 * BLOCK


def _cache_to_bf16(n_heads, ck_ref, cv_ref, ckb, cvb):
    for h in range(n_heads):
        sl = slice(h * HEAD_DIM, (h + 1) * HEAD_DIM)
        ckb[:, sl] = ck_ref[:, h, :].astype(BF16)
        cvb[:, sl] = cv_ref[:, h, :].astype(BF16)


def _win_attn_kernel(sink_ref, q_ref, kv_ref, ck_ref, cv_ref, o_ref, ckb, cvb):
    n = pl.program_id(1)
    pl.when(n == 0)(functools.partial(_cache_to_bf16, A_KV_HEADS, ck_ref, cv_ref, ckb, cvb))
    start = pl.multiple_of(jnp.clip((n - 1) * BLOCK, 0, DEC_SEQ - BAND), BLOCK)
    qpos = n * BLOCK + lax.broadcasted_iota(jnp.int32, (BLOCK, BAND), 0)
    kpos = start + lax.broadcasted_iota(jnp.int32, (BLOCK, BAND), 1)
    valid = jnp.abs(qpos - kpos) <= A_WINDOW
    valid = jnp.concatenate([valid.astype(jnp.int32)] * A_GROUP, axis=0) > 0
    for hk in range(A_KV_HEADS):
        sl = slice(hk * HEAD_DIM, (hk + 1) * HEAD_DIM)
        slv = slice(A_KV + hk * HEAD_DIM, A_KV + (hk + 1) * HEAD_DIM)
        k_loc = kv_ref[pl.ds(start, BAND), sl]
        v_loc = kv_ref[pl.ds(start, BAND), slv]
        q4 = _stack_group(q_ref, hk)
        t_loc = jnp.where(valid, _dot_nt(q4, k_loc) * QK_LOG2, NEG)
        t_ctx = _dot_nt(q4, ckb[:, sl]) * QK_LOG2
        o = _softmax_pv([(t_loc, v_loc), (t_ctx, cvb[:, sl])], _sink_column(sink_ref, hk, BLOCK))
        _unstack_group(o_ref, hk, o, BLOCK)


def _win_attn_call(sink, qkv, kva, cache_k, cache_v):
    nb = DEC_SEQ // BLOCK
    q0 = M_PROMPT // BLOCK
    b0 = M_PROMPT // DEC_SEQ
    return pl.pallas_call(
        _win_attn_kernel,
        grid=(DEC_BATCH, nb),
        in_specs=[pl.BlockSpec(memory_space=pltpu.SMEM),
                  pl.BlockSpec((None, BLOCK, A_Q), lambda b, n: (PROJ_KINDS.index("qa"), q0 + b * nb + n, 0)),
                  pl.BlockSpec((DEC_SEQ, 2 * A_KV), lambda b, n: (b0 + b, 0)),
                  pl.BlockSpec((None, None, PAST_LEN, A_KV_HEADS, HEAD_DIM), lambda b, n: (b, 0, 0, 0, 0)),
                  pl.BlockSpec((None, None, PAST_LEN, A_KV_HEADS, HEAD_DIM), lambda b, n: (b, 0, 0, 0, 0))],
        out_specs=pl.BlockSpec((BLOCK, A_Q), lambda b, n: (b * nb + n, 0)),
        out_shape=jax.ShapeDtypeStruct((M_SAMPLE, A_Q), BF16),
        scratch_shapes=[pltpu.VMEM((PAST_LEN, A_KV), BF16), pltpu.VMEM((PAST_LEN, A_KV), BF16)],
        compiler_params=_params(("arbitrary", "arbitrary"), VMEM_LIMIT),
        name="attn_window",
    )(sink, qkv, kva, cache_k, cache_v)


NA_QROWS = 4
NA_WIN_ROWS = 12
NA_Q = NA_QROWS * GRID_W
NA_KEYS = NA_WIN_ROWS * GRID_W
NA_DR = 2 * NA_ROWS_MAX - 1
NA_PAIRS = NA_DR + 1
RPB_W = 2 * NA_COLS - 1
RPB_PAD = GRID_W - NA_COLS


def _na_row_start(r):
    return jnp.clip(r - NA_ROWS // 2, 0, GRID_ROWS - NA_ROWS)


def _na_window_row0(blk):
    return jnp.clip(blk * NA_QROWS - NA_ROWS // 2, 0, GRID_ROWS - NA_WIN_ROWS)


def _check_na_windows():
    for blk in range(GRID_ROWS // NA_QROWS):
        w0 = int(np.clip(blk * NA_QROWS - NA_ROWS // 2, 0, GRID_ROWS - NA_WIN_ROWS))
        for r in range(blk * NA_QROWS, (blk + 1) * NA_QROWS):
            r0 = int(np.clip(r - NA_ROWS // 2, 0, GRID_ROWS - NA_ROWS))
            assert w0 <= r0 and r0 + NA_ROWS <= w0 + NA_WIN_ROWS, (blk, r)


_check_na_windows()


def _na_attn_kernel(q_ref, k_ref, v_ref, ck_ref, cv_ref, rpb_ref, o_ref, tab_ref, ckb, cvb):
    b, blk = pl.program_id(0), pl.program_id(1)
    lane = lax.broadcasted_iota(jnp.int32, (GRID_W, 2 * GRID_W), 1)
    pl.when(blk == 0)(functools.partial(_cache_to_bf16, B_HEADS, ck_ref, cv_ref, ckb, cvb))

    @pl.when((b == 0) & (blk == 0))
    def _():
        for h in range(B_HEADS):
            for d in range(NA_PAIRS):
                lo = jnp.broadcast_to(rpb_ref[h, d:d + 1, :], (GRID_W, 2 * GRID_W))
                hi = jnp.broadcast_to(rpb_ref[h, d + 1:d + 2, :], (GRID_W, 2 * GRID_W))
                lo = pltpu.roll(lo, GRID_W + 1, 1, stride=1, stride_axis=0)
                hi = pltpu.roll(hi, 1, 1, stride=1, stride_axis=0)
                tab_ref[h, d] = jnp.where(lane < GRID_W, lo, hi) * LOG2E

    row0 = _na_window_row0(blk)
    k0 = pl.multiple_of(row0 * GRID_W, GRID_W)
    qi = lax.broadcasted_iota(jnp.int32, (NA_Q, NA_KEYS), 0)
    ki = lax.broadcasted_iota(jnp.int32, (NA_Q, NA_KEYS), 1)
    qrow, qcol = blk * NA_QROWS + qi // GRID_W, qi % GRID_W
    krow, kcol = row0 + ki // GRID_W, ki % GRID_W
    rstart = _na_row_start(qrow)
    cstart = jnp.clip(qcol - NA_COLS // 2, 0, GRID_W - NA_COLS)
    valid = (krow >= rstart) & (krow < rstart + NA_ROWS) & (kcol >= cstart) & (kcol < cstart + NA_COLS)

    for h in range(B_HEADS):
        sl = slice(h * HEAD_DIM, (h + 1) * HEAD_DIM)
        bias = jnp.concatenate(
            [jnp.concatenate(
                [tab_ref[h, jnp.clip(row0 + 2 * p - blk * NA_QROWS - q + NA_ROWS_MAX, 0, NA_PAIRS - 1)]
                 for p in range(NA_WIN_ROWS // 2)], axis=1)
             for q in range(NA_QROWS)], axis=0)
        k_loc = k_ref[pl.ds(k0, NA_KEYS), sl]
        v_loc = v_ref[pl.ds(k0, NA_KEYS), sl]
        t_loc = jnp.where(valid, _dot_nt(q_ref[:, sl], k_loc) * QK_LOG2 + bias, NEG)
        t_ctx = _dot_nt(q_ref[:, sl], ckb[:, sl]) * QK_LOG2
        o = _softmax_pv([(t_loc, v_loc), (t_ctx, cvb[:, sl])], None)
        o_ref[:, sl] = o.astype(BF16)


def _na_bias_rows(rpb):
    rows = jnp.pad(rpb, ((0, 0), (1, 1), (0, 0)))
    left = jnp.broadcast_to(rows[..., :1], rows.shape[:2] + (RPB_PAD,))
    right = jnp.broadcast_to(rows[..., -1:], rows.shape[:2] + (2 * GRID_W - RPB_PAD - RPB_W,))
    return jnp.concatenate([left, rows, right], axis=-1)


def _na_attn_call(qkv, cache_k, cache_v, rpb):
    nblk = GRID_ROWS // NA_QROWS
    q0 = M_PROMPT // NA_Q
    b0 = M_PROMPT // DEC_SEQ
    return pl.pallas_call(
        _na_attn_kernel,
        grid=(DEC_BATCH, nblk),
        in_specs=[pl.BlockSpec((None, NA_Q, B_QKV), lambda b, r: (PROJ_KINDS.index("qb"), q0 + b * nblk + r, 0)),
                  pl.BlockSpec((None, DEC_SEQ, B_QKV), lambda b, r: (PROJ_KINDS.index("kb"), b0 + b, 0)),
                  pl.BlockSpec((None, DEC_SEQ, B_QKV), lambda b, r: (PROJ_KINDS.index("vb"), b0 + b, 0)),
                  pl.BlockSpec((None, None, PAST_LEN, B_HEADS, HEAD_DIM), lambda b, r: (b, 0, 0, 0, 0)),
                  pl.BlockSpec((None, None, PAST_LEN, B_HEADS, HEAD_DIM), lambda b, r: (b, 0, 0, 0, 0)),
                  pl.BlockSpec((B_HEADS, NA_DR + 2, 2 * GRID_W), lambda b, r: (0, 0, 0))],
        out_specs=pl.BlockSpec((NA_Q, B_QKV), lambda b, r: (b * nblk + r, 0)),
        out_shape=jax.ShapeDtypeStruct((M_SAMPLE, B_QKV), BF16),
        scratch_shapes=[pltpu.VMEM((B_HEADS, NA_PAIRS, GRID_W, 2 * GRID_W), F32),
                        pltpu.VMEM((PAST_LEN, B_QKV), BF16), pltpu.VMEM((PAST_LEN, B_QKV), BF16)],
        compiler_params=_params(("arbitrary", "arbitrary"), VMEM_LIMIT),
        name="attn_neighbourhood",
    )(qkv, qkv, qkv, cache_k, cache_v, _na_bias_rows(rpb))


def _merge_kernel(oap, oas, obp, obs, wa_ref, wb_ref, ga_ref, gb_ref, z_ref, wa_bf, wb_bf):
    i = pl.program_id(1)
    p_tiles = M_PROMPT // TM

    @pl.when(i == 0)
    def _():
        wa_bf[...] = wa_ref[...].astype(BF16)
        wb_bf[...] = wb_ref[...].astype(BF16)

    def run(oa_ref, ob_ref):
        for r0 in range(0, TM, ROW_CHUNK):
            rows = slice(r0, r0 + ROW_CHUNK)
            ya = _dot(oa_ref[rows, :], wa_bf[...])
            yb = _dot(ob_ref[rows, :], wb_bf[...])
            z_ref[rows, :] = (ga_ref[rows, :] * ya + gb_ref[rows, :] * yb).astype(BF16)

    pl.when(i < p_tiles)(functools.partial(run, oap, obp))
    pl.when(i >= p_tiles)(functools.partial(run, oas, obs))


def _merge_call(oa_p, oa_s, ob_p, ob_s, w_br_a, w_br_b, gates):
    p_tiles = M_PROMPT // TM
    nj = D_MODEL // TN
    pspec = pl.BlockSpec((TM, A_Q), lambda j, i: (jnp.minimum(i, p_tiles - 1), 0))
    sspec = pl.BlockSpec((TM, A_Q), lambda j, i: (jnp.maximum(i - p_tiles, 0), 0))
    once = dict(pipeline_mode=pl.Buffered(1))
    return pl.pallas_call(
        _merge_kernel,
        grid=(nj, M_ALL // TM),
        in_specs=[pspec, sspec, pspec, sspec,
                  pl.BlockSpec((A_Q, TN), lambda j, i: (0, j), **once),
                  pl.BlockSpec((B_QKV, TN), lambda j, i: (0, j), **once),
                  pl.BlockSpec((TM, TN), lambda j, i: (i, j)),
                  pl.BlockSpec((TM, TN), lambda j, i: (i, nj + j))],
        out_specs=pl.BlockSpec((TM, TN), lambda j, i: (i, j)),
        out_shape=jax.ShapeDtypeStruct((M_ALL, D_MODEL), BF16),
        scratch_shapes=[pltpu.VMEM((A_Q, TN), BF16), pltpu.VMEM((B_QKV, TN), BF16)],
        compiler_params=_params(("arbitrary", "arbitrary"), VMEM_LIMIT),
        name="merge_branches",
    )(oa_p, oa_s, ob_p, ob_s, w_br_a, w_br_b, gates, gates)


def _out_kernel(z_ref, w_ref, xp_ref, xs_ref, g_ref, o_ref, wbf):
    i = pl.program_id(1)
    p_tiles = M_PROMPT // TM

    @pl.when(i == 0)
    def _():
        wbf[...] = w_ref[...].astype(BF16)

    def run(x_ref):
        for r0 in range(0, TM, ROW_CHUNK):
            rows = slice(r0, r0 + ROW_CHUNK)
            o_ref[rows, :] = x_ref[rows, :] + g_ref[...] * _dot(z_ref[rows, :], wbf[...])

    pl.when(i < p_tiles)(functools.partial(run, xp_ref))
    pl.when(i >= p_tiles)(functools.partial(run, xs_ref))


def _out_call(z, w_out, xp, xs, mods3):
    p_tiles = M_PROMPT // TM
    nj = D_MODEL // TN
    return pl.pallas_call(
        _out_kernel,
        grid=(nj, M_ALL // TM),
        in_specs=[pl.BlockSpec((TM, D_MODEL), lambda j, i: (i, 0)),
                  pl.BlockSpec((D_MODEL, TN), lambda j, i: (0, j)),
                  pl.BlockSpec((TM, TN), lambda j, i: (jnp.minimum(i, p_tiles - 1), j)),
                  pl.BlockSpec((TM, TN), lambda j, i: (jnp.maximum(i - p_tiles, 0), j)),
                  pl.BlockSpec((None, 1, TN), lambda j, i: (_mod_row(i, TM), 0, 2 * nj + j))],
        out_specs=pl.BlockSpec((TM, TN), lambda j, i: (i, j)),
        out_shape=jax.ShapeDtypeStruct((M_ALL, D_MODEL), F32),
        scratch_shapes=[pltpu.VMEM((D_MODEL, TN), BF16)],
        compiler_params=_params(("arbitrary", "arbitrary"), VMEM_LIMIT),
        name="out_proj_residual",
    )(z, w_out, xp, xs, mods3)


def _mlp_kernel(x_ref, nw_ref, sh_ref, sc_ref, g_ref, wu_ref, wd_ref, o_ref, h_ref):
    f = pl.program_id(1)

    @pl.when(f == 0)
    def _():
        wu = wu_ref[...].astype(BF16)
        wd = wd_ref[...].astype(BF16)
        for r0 in range(0, TM, ROW_CHUNK):
            rows = slice(r0, r0 + ROW_CHUNK)
            h = _modnorm(x_ref[rows, :], nw_ref[...], sc_ref[...], sh_ref[...]).astype(BF16)
            h_ref[rows, :] = h
            u = jnp.square(jnp.maximum(_dot(h, wu), 0.0)).astype(BF16)
            o_ref[rows, :] = _dot(u, wd)

    @pl.when(f > 0)
    def _():
        u = _dot(h_ref[...], wu_ref[...].astype(BF16))
        u = jnp.square(jnp.maximum(u, 0.0)).astype(BF16)
        for c0 in range(0, D_MODEL, MLP_COLS):
            cols = slice(c0, c0 + MLP_COLS)
            o_ref[:, cols] += _dot(u, wd_ref[:, cols].astype(BF16))

    @pl.when(f == pl.num_programs(1) - 1)
    def _():
        o_ref[...] = x_ref[...] + g_ref[...] * o_ref[...]


def _mlp_call(x1, nw, mods3, w_up, w_down, tile0, n_tiles):
    mod = lambda k: pl.BlockSpec((None, 1, D_MODEL), lambda i, f: (_mod_row(tile0 + i, TM), 0, k))
    return pl.pallas_call(
        _mlp_kernel,
        grid=(n_tiles, D_FF // TF),
        in_specs=[pl.BlockSpec((TM, D_MODEL), lambda i, f: (tile0 + i, 0), pipeline_mode=pl.Buffered(1)),
                  pl.BlockSpec((1, D_MODEL), lambda i, f: (0, 0)),
                  mod(3), mod(4), mod(5),
                  pl.BlockSpec((D_MODEL, TF), lambda i, f: (0, f)),
                  pl.BlockSpec((TF, D_MODEL), lambda i, f: (f, 0))],
        out_specs=pl.BlockSpec((TM, D_MODEL), lambda i, f: (i, 0)),
        out_shape=jax.ShapeDtypeStruct((n_tiles * TM, D_MODEL), F32),
        scratch_shapes=[pltpu.VMEM((TM, D_MODEL), BF16)],
        compiler_params=_params(("arbitrary", "arbitrary"), VMEM_LIMIT),
        name="mlp",
    )(x1, nw, mods3, mods3, mods3, w_up, w_down)


def kernel(x_prompt, x_sample, cache_a_k, cache_a_v, cache_b_k, cache_b_v, c, c_ctx, norm1_w, norm2_w, w_ada, b_ada, w_in, q_norm_a, k_norm_a, q_norm_b, k_norm_b, sink_a, rpb_b, w_br_a, w_br_b, w_out, w_up, w_down):
    assert w_ada.shape[0] == 1, "one trunk layer"
    xp = x_prompt.reshape(M_PROMPT, D_MODEL)
    xs = x_sample.reshape(M_SAMPLE, D_MODEL)

    cvecs = jnp.concatenate([c_ctx[None, :], c, jnp.zeros((8 - 1 - DEC_BATCH, D_MODEL), F32)], axis=0)
    mods = _ada_call(cvecs, w_ada[0], b_ada)
    mods3 = mods.reshape(8, 1, 6 * D_MODEL)

    w = w_in[0]
    rope = _rope_tables()
    h, kva, ka_p, va_p = _norm_kva_call(xp, xs, norm1_w, mods3, w, k_norm_a, rope)
    qkv, gates, kvb_p = _proj_call(h, w, q_norm_a, q_norm_b, k_norm_b, rope)

    sink = sink_a[0]
    oa_p, ob_p = _ctx_attn_call(sink, qkv, kva)
    oa_s = _win_attn_call(sink, qkv, kva, cache_a_k, cache_a_v)
    ob_s = _na_attn_call(qkv, cache_b_k, cache_b_v, rpb_b[0])

    z = _merge_call(oa_p, oa_s, ob_p, ob_s, w_br_a[0], w_br_b[0], gates)
    x1 = _out_call(z, w_out[0], xp, xs, mods3)

    p_tiles = M_PROMPT // TM
    y_p = _mlp_call(x1, norm2_w, mods3, w_up[0], w_down[0], 0, p_tiles)
    y_s = _mlp_call(x1, norm2_w, mods3, w_up[0], w_down[0], p_tiles, M_SAMPLE // TM)

    return (y_p.reshape(BATCH, SEQ, D_MODEL),
            y_s.reshape(DEC_BATCH, DEC_SEQ, D_MODEL),
            ka_p.reshape(BATCH, 1, SEQ, A_KV_HEADS, HEAD_DIM),
            va_p.reshape(BATCH, 1, SEQ, A_KV_HEADS, HEAD_DIM),
            kvb_p[0].reshape(BATCH, 1, SEQ, B_HEADS, HEAD_DIM),
            kvb_p[1].reshape(BATCH, 1, SEQ, B_HEADS, HEAD_DIM))
```

```python
import functools

import numpy as np
import jax
import jax.numpy as jnp
from jax import lax
from jax.experimental import pallas as pl
from jax.experimental.pallas import tpu as pltpu

D_MODEL = 2048
BATCH = 16
SEQ = 256
DEC_BATCH = 2
DEC_SEQ = 1024
PAST_LEN = 256
GRID_W = 64
HEAD_DIM = 128
A_HEADS = 8
A_KV_HEADS = 2
A_GROUP = A_HEADS // A_KV_HEADS
A_WINDOW = 128
BLOCK = 128
B_HEADS = 8
NA_ROWS_MAX = 8
NA_COLS = 16
D_FF = 4 * D_MODEL
ROPE_THETA = 10000.0
EPS = 1e-6
NEG = -1e30
A_Q = A_HEADS * HEAD_DIM
A_KV = A_KV_HEADS * HEAD_DIM
B_QKV = B_HEADS * HEAD_DIM
IN_WIDTH = A_Q + 2 * A_KV + 3 * B_QKV + 2 * D_MODEL
SCALE = HEAD_DIM ** -0.5

M_PROMPT = BATCH * SEQ
M_SAMPLE = DEC_BATCH * DEC_SEQ
M_ALL = M_PROMPT + M_SAMPLE
GRID_ROWS = DEC_SEQ // GRID_W
NA_ROWS = min(NA_ROWS_MAX, GRID_ROWS)

COL_QA = 0
COL_KVA = A_Q
COL_QB = A_Q + 2 * A_KV
COL_KB = COL_QB + B_QKV
COL_VB = COL_KB + B_QKV
COL_G = COL_VB + B_QKV

TM = 1024
TN = 1024
ROW_CHUNK = 256
W_SUB = 512
TM_NORM = 512
TF = 512
MLP_COLS = 512
VMEM_LIMIT = 60 * 1024 * 1024

F32 = jnp.float32
BF16 = jnp.bfloat16


def _mod_row(i, tm):
    p_tiles = M_PROMPT // tm
    return jnp.where(i < p_tiles, 0, 1 + (i - p_tiles) // (DEC_SEQ // tm))


def _dot(a, b):
    return jnp.dot(a, b, preferred_element_type=F32)


def _dot_nt(a, b):
    return lax.dot_general(a, b, (((1,), (1,)), ((), ())), preferred_element_type=F32)


def _params(sem, vmem=None):
    return pltpu.CompilerParams(dimension_semantics=sem, vmem_limit_bytes=vmem)


def _ada_kernel(c_ref, w_ref, b_ref, o_ref):
    cv = c_ref[...]
    s = (cv * jax.nn.sigmoid(cv)).astype(BF16)
    o_ref[...] = _dot(s, w_ref[...].astype(BF16)) + b_ref[...]


def _ada_call(cvecs, w_ada, b_ada):
    tn = 1024
    n = 6 * D_MODEL
    return pl.pallas_call(
        _ada_kernel,
        grid=(n // tn,),
        in_specs=[pl.BlockSpec((8, D_MODEL), lambda j: (0, 0)),
                  pl.BlockSpec((D_MODEL, tn), lambda j: (0, j)),
                  pl.BlockSpec((1, tn), lambda j: (0, j))],
        out_specs=pl.BlockSpec((8, tn), lambda j: (0, j)),
        out_shape=jax.ShapeDtypeStruct((8, n), F32),
        compiler_params=_params(("arbitrary",), VMEM_LIMIT),
        name="ada_mod",
    )(cvecs, w_ada, b_ada)


def _modnorm(x, nw, sc, sh):
    y = x * lax.rsqrt(jnp.mean(x * x, axis=-1, keepdims=True) + EPS)
    return y * (nw * (1.0 + sc)) + sh


def _head_norm(x, nw):
    return x * lax.rsqrt(jnp.mean(x * x, axis=-1, keepdims=True) + EPS) * nw


def _rope(x, cos, sin_signed):
    lane = lax.broadcasted_iota(jnp.int32, x.shape, 1)
    partner = jnp.where((lane % 64) < 32, pltpu.roll(x, 96, 1), pltpu.roll(x, 32, 1))
    return x * cos + partner * sin_signed


def _norm_kva_kernel(xp_ref, xs_ref, nw1_ref, sh_ref, sc_ref, w_ref, nwk_ref, cos_ref, sin_ref,
                     h_ref, kv_ref, kp_ref, vp_ref, wbf):
    i = pl.program_id(0)
    p_tiles = M_PROMPT // TM_NORM

    @pl.when(i == 0)
    def _():
        wbf[...] = w_ref[...].astype(BF16)

    def head_cols(k, base=0):
        return slice(base + k * HEAD_DIM, base + (k + 1) * HEAD_DIM)

    def run(prompt):
        x_ref = xp_ref if prompt else xs_ref
        for r0 in range(0, TM_NORM, ROW_CHUNK):
            rows = slice(r0, r0 + ROW_CHUNK)
            h = _modnorm(x_ref[rows, :], nw1_ref[...], sc_ref[...], sh_ref[...]).astype(BF16)
            h_ref[rows, :] = h
            acc = _dot(h, wbf[...])
            for k in range(A_KV_HEADS):
                y = _head_norm(acc[:, head_cols(k)], nwk_ref[...])
                v = acc[:, head_cols(k, A_KV)]
                if prompt:
                    kp_ref[rows, head_cols(k)] = y
                    vp_ref[rows, head_cols(k)] = v
                else:
                    y = _rope(y, cos_ref[rows, :], sin_ref[rows, :])
                kv_ref[rows, head_cols(k)] = y.astype(BF16)
                kv_ref[rows, head_cols(k, A_KV)] = v.astype(BF16)

    pl.when(i < p_tiles)(functools.partial(run, True))
    pl.when(i >= p_tiles)(functools.partial(run, False))


def _norm_kva_call(xp, xs, nw1, mods3, w_in, nwk, rope):
    tm = TM_NORM
    p_tiles = M_PROMPT // tm
    s_tiles = DEC_SEQ // tm
    assert COL_KVA % (2 * A_KV) == 0
    rope_spec = pl.BlockSpec((tm, HEAD_DIM), lambda i: (jnp.maximum(i - p_tiles, 0) % s_tiles, 0))
    parked = pl.BlockSpec((tm, A_KV), lambda i: (jnp.minimum(i, p_tiles - 1), 0))
    return pl.pallas_call(
        _norm_kva_kernel,
        grid=(M_ALL // tm,),
        in_specs=[pl.BlockSpec((tm, D_MODEL), lambda i: (jnp.minimum(i, p_tiles - 1), 0)),
                  pl.BlockSpec((tm, D_MODEL), lambda i: (jnp.maximum(i - p_tiles, 0), 0)),
                  pl.BlockSpec((1, D_MODEL), lambda i: (0, 0)),
                  pl.BlockSpec((None, 1, D_MODEL), lambda i: (_mod_row(i, tm), 0, 0)),
                  pl.BlockSpec((None, 1, D_MODEL), lambda i: (_mod_row(i, tm), 0, 1)),
                  pl.BlockSpec((D_MODEL, 2 * A_KV), lambda i: (0, COL_KVA // (2 * A_KV)),
                               pipeline_mode=pl.Buffered(1)),
                  pl.BlockSpec((1, HEAD_DIM), lambda i: (0, 0)),
                  rope_spec, rope_spec],
        out_specs=[pl.BlockSpec((tm, D_MODEL), lambda i: (i, 0)),
                   pl.BlockSpec((tm, 2 * A_KV), lambda i: (i, 0)),
                   parked, parked],
        out_shape=[jax.ShapeDtypeStruct((M_ALL, D_MODEL), BF16),
                   jax.ShapeDtypeStruct((M_ALL, 2 * A_KV), BF16),
                   jax.ShapeDtypeStruct((M_PROMPT, A_KV), F32),
                   jax.ShapeDtypeStruct((M_PROMPT, A_KV), F32)],
        scratch_shapes=[pltpu.VMEM((D_MODEL, 2 * A_KV), BF16)],
        compiler_params=_params(("arbitrary",), VMEM_LIMIT),
        name="norm1_proj_kva",
    )(xp, xs, nw1, mods3, mods3, w_in, nwk, *rope)


def _rope_tables():
    n_freq = HEAD_DIM // 4
    pos = np.arange(DEC_SEQ)
    row = (pos // GRID_W).astype(np.float64)
    col = (pos % GRID_W).astype(np.float64)
    inv = ROPE_THETA ** (-np.arange(n_freq, dtype=np.float64) / n_freq)
    ar = row[:, None] * inv
    ac = col[:, None] * inv
    cos = np.concatenate([np.cos(ar), np.cos(ar), np.cos(ac), np.cos(ac)], axis=-1)
    sin = np.concatenate([-np.sin(ar), np.sin(ar), -np.sin(ac), np.sin(ac)], axis=-1)
    return jnp.asarray(cos, F32), jnp.asarray(sin, F32)


PROJ_KINDS = ("qa", "qb", "kb", "vb") + ("gate",) * (2 * D_MODEL // TN)
PROJ_COLS = (COL_QA, COL_QB, COL_KB, COL_VB) + tuple(range(COL_G, IN_WIDTH, TN))
N_QKV = 4


def _proj_kernel(h_ref, w0_ref, w1_ref, nqa_ref, nqb_ref, nkb_ref, cos_ref, sin_ref,
                 qkv_ref, gate_ref, kbp_ref, vbp_ref, wbf):
    j, i = pl.program_id(0), pl.program_id(1)
    is_prompt = i < M_PROMPT // TM

    @pl.when(i == 0)
    def _():
        wbf[:, :W_SUB] = w0_ref[...].astype(BF16)
        wbf[:, W_SUB:] = w1_ref[...].astype(BF16)

    def head_cols(k):
        return slice(k * HEAD_DIM, (k + 1) * HEAD_DIM)

    def run(kind, prompt):
        for r0 in range(0, TM, ROW_CHUNK):
            rows = slice(r0, r0 + ROW_CHUNK)
            acc = _dot(h_ref[rows, :], wbf[...])
            if kind == "gate":
                gate_ref[rows, :] = jax.nn.sigmoid(acc)
            elif kind == "vb":
                if prompt:
                    vbp_ref[rows, :] = acc
                qkv_ref[rows, :] = acc.astype(BF16)
            else:
                nw_ref = {"qa": nqa_ref, "qb": nqb_ref, "kb": nkb_ref}[kind]
                for k in range(TN // HEAD_DIM):
                    y = _head_norm(acc[:, head_cols(k)], nw_ref[...])
                    if kind == "qa" and not prompt:
                        y = _rope(y, cos_ref[rows, :], sin_ref[rows, :])
                    if kind == "kb" and prompt:
                        kbp_ref[rows, head_cols(k)] = y
                    qkv_ref[rows, head_cols(k)] = y.astype(BF16)

    for p, kind in enumerate(PROJ_KINDS[:N_QKV]):
        if kind == "qb":
            pl.when(j == p)(functools.partial(run, kind, None))
        else:
            pl.when((j == p) & is_prompt)(functools.partial(run, kind, True))
            pl.when((j == p) & jnp.logical_not(is_prompt))(functools.partial(run, kind, False))
    pl.when(j >= N_QKV)(functools.partial(run, "gate", None))


def _proj_call(h, w_in, nqa, nqb, nkb, rope):
    assert PROJ_KINDS[:N_QKV] == ("qa", "qb", "kb", "vb") and TN == 2 * W_SUB and A_Q == B_QKV == TN
    n_panels = len(PROJ_KINDS)
    p_tiles = M_PROMPT // TM
    s_tiles = DEC_SEQ // TM
    n_tiles = M_ALL // TM
    kb_panel, vb_panel = PROJ_KINDS.index("kb"), PROJ_KINDS.index("vb")

    def w_sub(j):
        idx = PROJ_COLS[0] // W_SUB
        for p in range(1, n_panels):
            idx = jnp.where(j == p, PROJ_COLS[p] // W_SUB, idx)
        return idx

    def qkv_idx(j, i):
        done = j >= N_QKV
        return (jnp.minimum(j, N_QKV - 1), jnp.where(done, n_tiles - 1, i), 0)

    def gate_idx(j, i):
        return (jnp.where(j < N_QKV, 0, i), jnp.maximum(j - N_QKV, 0))

    def cache_idx(panel):
        def idx(j, i):
            return (jnp.where(j < panel, 0, jnp.where(j > panel, p_tiles - 1, jnp.minimum(i, p_tiles - 1))), 0)
        return idx

    norm_spec = pl.BlockSpec((1, HEAD_DIM), lambda j, i: (0, 0))
    rope_spec = pl.BlockSpec((TM, HEAD_DIM), lambda j, i: (jnp.maximum(i - p_tiles, 0) % s_tiles, 0))
    return pl.pallas_call(
        _proj_kernel,
        grid=(n_panels, n_tiles),
        in_specs=[pl.BlockSpec((TM, D_MODEL), lambda j, i: (i, 0)),
                  pl.BlockSpec((D_MODEL, W_SUB), lambda j, i: (0, w_sub(j))),
                  pl.BlockSpec((D_MODEL, W_SUB), lambda j, i: (0, w_sub(j) + 1)),
                  norm_spec, norm_spec, norm_spec, rope_spec, rope_spec],
        out_specs=[pl.BlockSpec((None, TM, TN), qkv_idx),
                   pl.BlockSpec((TM, TN), gate_idx),
                   pl.BlockSpec((TM, TN), cache_idx(kb_panel)),
                   pl.BlockSpec((TM, TN), cache_idx(vb_panel))],
        out_shape=[jax.ShapeDtypeStruct((N_QKV, M_ALL, TN), BF16),
                   jax.ShapeDtypeStruct((M_ALL, 2 * D_MODEL), F32),
                   jax.ShapeDtypeStruct((M_PROMPT, B_QKV), F32),
                   jax.ShapeDtypeStruct((M_PROMPT, B_QKV), F32)],
        scratch_shapes=[pltpu.VMEM((D_MODEL, TN), BF16)],
        compiler_params=_params(("arbitrary", "arbitrary"), VMEM_LIMIT),
        name="proj",
    )(h, w_in, w_in, nqa, nqb, nkb, *rope)


LOG2E = 1.4426950408889634
QK_LOG2 = SCALE * LOG2E


def _softmax_pv(parts, sink):
    m = functools.reduce(jnp.maximum, [jnp.max(t, axis=-1, keepdims=True) for t, _ in parts])
    if sink is not None:
        m = jnp.maximum(m, sink)
    acc = None
    for t, v in parts:
        p = jnp.exp2(t - m).astype(BF16)
        y = _dot(p, jnp.concatenate([v, jnp.ones_like(v)], axis=1))
        acc = y if acc is None else acc + y
    o, l = acc[:, :HEAD_DIM], acc[:, HEAD_DIM:]
    if sink is not None:
        l = l + jnp.exp2(sink - m)
    return o / l


def _sink_column(sink_ref, hk, rows_per_head):
    n = A_GROUP * rows_per_head
    g = lax.broadcasted_iota(jnp.int32, (n, 1), 0) // rows_per_head
    col = jnp.full((n, 1), sink_ref[hk * A_GROUP], F32)
    for k in range(1, A_GROUP):
        col = jnp.where(g == k, sink_ref[hk * A_GROUP + k], col)
    return col * LOG2E


def _stack_group(q_ref, hk):
    return jnp.concatenate(
        [q_ref[:, (hk * A_GROUP + g) * HEAD_DIM:(hk * A_GROUP + g + 1) * HEAD_DIM] for g in range(A_GROUP)], axis=0)


def _unstack_group(o_ref, hk, o, rows):
    for g in range(A_GROUP):
        c = (hk * A_GROUP + g) * HEAD_DIM
        o_ref[:, c:c + HEAD_DIM] = o[g * rows:(g + 1) * rows].astype(BF16)


def _ctx_attn_kernel(sink_ref, qkv_ref, kva_ref, oa_ref, ob_ref):
    qa_ref, qb_ref, kb_ref, vb_ref = (qkv_ref.at[PROJ_KINDS.index(kind)] for kind in ("qa", "qb", "kb", "vb"))
    for hk in range(A_KV_HEADS):
        k = kva_ref[:, hk * HEAD_DIM:(hk + 1) * HEAD_DIM]
        v = kva_ref[:, A_KV + hk * HEAD_DIM:A_KV + (hk + 1) * HEAD_DIM]
        q4 = _stack_group(qa_ref, hk)
        t = _dot_nt(q4, k) * QK_LOG2
        o = _softmax_pv([(t, v)], _sink_column(sink_ref, hk, SEQ))
        _unstack_group(oa_ref, hk, o, SEQ)
    for h in range(B_HEADS):
        sl = slice(h * HEAD_DIM, (h + 1) * HEAD_DIM)
        t = _dot_nt(qb_ref[:, sl], kb_ref[:, sl]) * QK_LOG2
        ob_ref[:, sl] = _softmax_pv([(t, vb_ref[:, sl])], None).astype(BF16)


def _ctx_attn_call(sink, qkv, kva):
    row = lambda w: pl.BlockSpec((SEQ, w), lambda b: (b, 0))
    return pl.pallas_call(
        _ctx_attn_kernel,
        grid=(BATCH,),
        in_specs=[pl.BlockSpec(memory_space=pltpu.SMEM),
                  pl.BlockSpec((N_QKV, SEQ, TN), lambda b: (0, b, 0)), row(2 * A_KV)],
        out_specs=[row(A_Q), row(B_QKV)],
        out_shape=[jax.ShapeDtypeStruct((M_PROMPT, A_Q), BF16),
                   jax.ShapeDtypeStruct((M_PROMPT, B_QKV), BF16)],
        compiler_params=_params(("arbitrary",), VMEM_LIMIT),
        name="attn_ctx",
    )(sink, qkv, kva)


BAND = 3 * BLOCK


def _cache_to_bf16(n_heads, ck_ref, cv_ref, ckb, cvb):
    for h in range(n_heads):
        sl = slice(h * HEAD_DIM, (h + 1) * HEAD_DIM)
        ckb[:, sl] = ck_ref[:, h, :].astype(BF16)
        cvb[:, sl] = cv_ref[:, h, :].astype(BF16)


def _win_attn_kernel(sink_ref, q_ref, kv_ref, ck_ref, cv_ref, o_ref, ckb, cvb):
    n = pl.program_id(1)
    pl.when(n == 0)(functools.partial(_cache_to_bf16, A_KV_HEADS, ck_ref, cv_ref, ckb, cvb))
    start = pl.multiple_of(jnp.clip((n - 1) * BLOCK, 0, DEC_SEQ - BAND), BLOCK)
    qpos = n * BLOCK + lax.broadcasted_iota(jnp.int32, (BLOCK, BAND), 0)
    kpos = start + lax.broadcasted_iota(jnp.int32, (BLOCK, BAND), 1)
    valid = jnp.abs(qpos - kpos) <= A_WINDOW
    valid = jnp.concatenate([valid.astype(jnp.int32)] * A_GROUP, axis=0) > 0
    for hk in range(A_KV_HEADS):
        sl = slice(hk * HEAD_DIM, (hk + 1) * HEAD_DIM)
        slv = slice(A_KV + hk * HEAD_DIM, A_KV + (hk + 1) * HEAD_DIM)
        k_loc = kv_ref[pl.ds(start, BAND), sl]
        v_loc = kv_ref[pl.ds(start, BAND), slv]
        q4 = _stack_group(q_ref, hk)
        t_loc = jnp.where(valid, _dot_nt(q4, k_loc) * QK_LOG2, NEG)
        t_ctx = _dot_nt(q4, ckb[:, sl]) * QK_LOG2
        o = _softmax_pv([(t_loc, v_loc), (t_ctx, cvb[:, sl])], _sink_column(sink_ref, hk, BLOCK))
        _unstack_group(o_ref, hk, o, BLOCK)


def _win_attn_call(sink, qkv, kva, cache_k, cache_v):
    nb = DEC_SEQ // BLOCK
    q0 = M_PROMPT // BLOCK
    b0 = M_PROMPT // DEC_SEQ
    return pl.pallas_call(
        _win_attn_kernel,
        grid=(DEC_BATCH, nb),
        in_specs=[pl.BlockSpec(memory_space=pltpu.SMEM),
                  pl.BlockSpec((None, BLOCK, A_Q), lambda b, n: (PROJ_KINDS.index("qa"), q0 + b * nb + n, 0)),
                  pl.BlockSpec((DEC_SEQ, 2 * A_KV), lambda b, n: (b0 + b, 0)),
                  pl.BlockSpec((None, None, PAST_LEN, A_KV_HEADS, HEAD_DIM), lambda b, n: (b, 0, 0, 0, 0)),
                  pl.BlockSpec((None, None, PAST_LEN, A_KV_HEADS, HEAD_DIM), lambda b, n: (b, 0, 0, 0, 0))],
        out_specs=pl.BlockSpec((BLOCK, A_Q), lambda b, n: (b * nb + n, 0)),
        out_shape=jax.ShapeDtypeStruct((M_SAMPLE, A_Q), BF16),
        scratch_shapes=[pltpu.VMEM((PAST_LEN, A_KV), BF16), pltpu.VMEM((PAST_LEN, A_KV), BF16)],
        compiler_params=_params(("arbitrary", "arbitrary"), VMEM_LIMIT),
        name="attn_window",
    )(sink, qkv, kva, cache_k, cache_v)


NA_QROWS = 4
NA_WIN_ROWS = 12
NA_Q = NA_QROWS * GRID_W
NA_KEYS = NA_WIN_ROWS * GRID_W
NA_DR = 2 * NA_ROWS_MAX - 1
NA_PAIRS = NA_DR + 1
RPB_W = 2 * NA_COLS - 1
RPB_PAD = GRID_W - NA_COLS


def _na_row_start(r):
    return jnp.clip(r - NA_ROWS // 2, 0, GRID_ROWS - NA_ROWS)


def _na_window_row0(blk):
    return jnp.clip(blk * NA_QROWS - NA_ROWS // 2, 0, GRID_ROWS - NA_WIN_ROWS)


def _check_na_windows():
    for blk in range(GRID_ROWS // NA_QROWS):
        w0 = int(np.clip(blk * NA_QROWS - NA_ROWS // 2, 0, GRID_ROWS - NA_WIN_ROWS))
        for r in range(blk * NA_QROWS, (blk + 1) * NA_QROWS):
            r0 = int(np.clip(r - NA_ROWS // 2, 0, GRID_ROWS - NA_ROWS))
            assert w0 <= r0 and r0 + NA_ROWS <= w0 + NA_WIN_ROWS, (blk, r)


_check_na_windows()


def _na_attn_kernel(q_ref, k_ref, v_ref, ck_ref, cv_ref, rpb_ref, o_ref, tab_ref, ckb, cvb):
    b, blk = pl.program_id(0), pl.program_id(1)
    lane = lax.broadcasted_iota(jnp.int32, (GRID_W, 2 * GRID_W), 1)
    pl.when(blk == 0)(functools.partial(_cache_to_bf16, B_HEADS, ck_ref, cv_ref, ckb, cvb))

    @pl.when((b == 0) & (blk == 0))
    def _():
        for h in range(B_HEADS):
            for d in range(NA_PAIRS):
                lo = jnp.broadcast_to(rpb_ref[h, d:d + 1, :], (GRID_W, 2 * GRID_W))
                hi = jnp.broadcast_to(rpb_ref[h, d + 1:d + 2, :], (GRID_W, 2 * GRID_W))
                lo = pltpu.roll(lo, GRID_W + 1, 1, stride=1, stride_axis=0)
                hi = pltpu.roll(hi, 1, 1, stride=1, stride_axis=0)
                tab_ref[h, d] = jnp.where(lane < GRID_W, lo, hi) * LOG2E

    row0 = _na_window_row0(blk)
    k0 = pl.multiple_of(row0 * GRID_W, GRID_W)
    qi = lax.broadcasted_iota(jnp.int32, (NA_Q, NA_KEYS), 0)
    ki = lax.broadcasted_iota(jnp.int32, (NA_Q, NA_KEYS), 1)
    qrow, qcol = blk * NA_QROWS + qi // GRID_W, qi % GRID_W
    krow, kcol = row0 + ki // GRID_W, ki % GRID_W
    rstart = _na_row_start(qrow)
    cstart = jnp.clip(qcol - NA_COLS // 2, 0, GRID_W - NA_COLS)
    valid = (krow >= rstart) & (krow < rstart + NA_ROWS) & (kcol >= cstart) & (kcol < cstart + NA_COLS)

    for h in range(B_HEADS):
        sl = slice(h * HEAD_DIM, (h + 1) * HEAD_DIM)
        bias = jnp.concatenate(
            [jnp.concatenate(
                [tab_ref[h, jnp.clip(row0 + 2 * p - blk * NA_QROWS - q + NA_ROWS_MAX, 0, NA_PAIRS - 1)]
                 for p in range(NA_WIN_ROWS // 2)], axis=1)
             for q in range(NA_QROWS)], axis=0)
        k_loc = k_ref[pl.ds(k0, NA_KEYS), sl]
        v_loc = v_ref[pl.ds(k0, NA_KEYS), sl]
        t_loc = jnp.where(valid, _dot_nt(q_ref[:, sl], k_loc) * QK_LOG2 + bias, NEG)
        t_ctx = _dot_nt(q_ref[:, sl], ckb[:, sl]) * QK_LOG2
        o = _softmax_pv([(t_loc, v_loc), (t_ctx, cvb[:, sl])], None)
        o_ref[:, sl] = o.astype(BF16)


def _na_bias_rows(rpb):
    rows = jnp.pad(rpb, ((0, 0), (1, 1), (0, 0)))
    left = jnp.broadcast_to(rows[..., :1], rows.shape[:2] + (RPB_PAD,))
    right = jnp.broadcast_to(rows[..., -1:], rows.shape[:2] + (2 * GRID_W - RPB_PAD - RPB_W,))
    return jnp.concatenate([left, rows, right], axis=-1)


def _na_attn_call(qkv, cache_k, cache_v, rpb):
    nblk = GRID_ROWS // NA_QROWS
    q0 = M_PROMPT // NA_Q
    b0 = M_PROMPT // DEC_SEQ
    return pl.pallas_call(
        _na_attn_kernel,
        grid=(DEC_BATCH, nblk),
        in_specs=[pl.BlockSpec((None, NA_Q, B_QKV), lambda b, r: (PROJ_KINDS.index("qb"), q0 + b * nblk + r, 0)),
                  pl.BlockSpec((None, DEC_SEQ, B_QKV), lambda b, r: (PROJ_KINDS.index("kb"), b0 + b, 0)),
                  pl.BlockSpec((None, DEC_SEQ, B_QKV), lambda b, r: (PROJ_KINDS.index("vb"), b0 + b, 0)),
                  pl.BlockSpec((None, None, PAST_LEN, B_HEADS, HEAD_DIM), lambda b, r: (b, 0, 0, 0, 0)),
                  pl.BlockSpec((None, None, PAST_LEN, B_HEADS, HEAD_DIM), lambda b, r: (b, 0, 0, 0, 0)),
                  pl.BlockSpec((B_HEADS, NA_DR + 2, 2 * GRID_W), lambda b, r: (0, 0, 0))],
        out_specs=pl.BlockSpec((NA_Q, B_QKV), lambda b, r: (b * nblk + r, 0)),
        out_shape=jax.ShapeDtypeStruct((M_SAMPLE, B_QKV), BF16),
        scratch_shapes=[pltpu.VMEM((B_HEADS, NA_PAIRS, GRID_W, 2 * GRID_W), F32),
                        pltpu.VMEM((PAST_LEN, B_QKV), BF16), pltpu.VMEM((PAST_LEN, B_QKV), BF16)],
        compiler_params=_params(("arbitrary", "arbitrary"), VMEM_LIMIT),
        name="attn_neighbourhood",
    )(qkv, qkv, qkv, cache_k, cache_v, _na_bias_rows(rpb))


def _merge_kernel(oap, oas, obp, obs, wa_ref, wb_ref, ga_ref, gb_ref, z_ref, wa_bf, wb_bf):
    i = pl.program_id(1)
    p_tiles = M_PROMPT // TM

    @pl.when(i == 0)
    def _():
        wa_bf[...] = wa_ref[...].astype(BF16)
        wb_bf[...] = wb_ref[...].astype(BF16)

    def run(oa_ref, ob_ref):
        for r0 in range(0, TM, ROW_CHUNK):
            rows = slice(r0, r0 + ROW_CHUNK)
            ya = _dot(oa_ref[rows, :], wa_bf[...])
            yb = _dot(ob_ref[rows, :], wb_bf[...])
            z_ref[rows, :] = (ga_ref[rows, :] * ya + gb_ref[rows, :] * yb).astype(BF16)

    pl.when(i < p_tiles)(functools.partial(run, oap, obp))
    pl.when(i >= p_tiles)(functools.partial(run, oas, obs))


def _merge_call(oa_p, oa_s, ob_p, ob_s, w_br_a, w_br_b, gates):
    p_tiles = M_PROMPT // TM
    nj = D_MODEL // TN
    pspec = pl.BlockSpec((TM, A_Q), lambda j, i: (jnp.minimum(i, p_tiles - 1), 0))
    sspec = pl.BlockSpec((TM, A_Q), lambda j, i: (jnp.maximum(i - p_tiles, 0), 0))
    once = dict(pipeline_mode=pl.Buffered(1))
    return pl.pallas_call(
        _merge_kernel,
        grid=(nj, M_ALL // TM),
        in_specs=[pspec, sspec, pspec, sspec,
                  pl.BlockSpec((A_Q, TN), lambda j, i: (0, j), **once),
                  pl.BlockSpec((B_QKV, TN), lambda j, i: (0, j), **once),
                  pl.BlockSpec((TM, TN), lambda j, i: (i, j)),
                  pl.BlockSpec((TM, TN), lambda j, i: (i, nj + j))],
        out_specs=pl.BlockSpec((TM, TN), lambda j, i: (i, j)),
        out_shape=jax.ShapeDtypeStruct((M_ALL, D_MODEL), BF16),
        scratch_shapes=[pltpu.VMEM((A_Q, TN), BF16), pltpu.VMEM((B_QKV, TN), BF16)],
        compiler_params=_params(("arbitrary", "arbitrary"), VMEM_LIMIT),
        name="merge_branches",
    )(oa_p, oa_s, ob_p, ob_s, w_br_a, w_br_b, gates, gates)


def _out_kernel(z_ref, w_ref, xp_ref, xs_ref, g_ref, o_ref, wbf):
    i = pl.program_id(1)
    p_tiles = M_PROMPT // TM

    @pl.when(i == 0)
    def _():
        wbf[...] = w_ref[...].astype(BF16)

    def run(x_ref):
        for r0 in range(0, TM, ROW_CHUNK):
            rows = slice(r0, r0 + ROW_CHUNK)
            o_ref[rows, :] = x_ref[rows, :] + g_ref[...] * _dot(z_ref[rows, :], wbf[...])

    pl.when(i < p_tiles)(functools.partial(run, xp_ref))
    pl.when(i >= p_tiles)(functools.partial(run, xs_ref))


def _out_call(z, w_out, xp, xs, mods3):
    p_tiles = M_PROMPT // TM
    nj = D_MODEL // TN
    return pl.pallas_call(
        _out_kernel,
        grid=(nj, M_ALL // TM),
        in_specs=[pl.BlockSpec((TM, D_MODEL), lambda j, i: (i, 0)),
                  pl.BlockSpec((D_MODEL, TN), lambda j, i: (0, j)),
                  pl.BlockSpec((TM, TN), lambda j, i: (jnp.minimum(i, p_tiles - 1), j)),
                  pl.BlockSpec((TM, TN), lambda j, i: (jnp.maximum(i - p_tiles, 0), j)),
                  pl.BlockSpec((None, 1, TN), lambda j, i: (_mod_row(i, TM), 0, 2 * nj + j))],
        out_specs=pl.BlockSpec((TM, TN), lambda j, i: (i, j)),
        out_shape=jax.ShapeDtypeStruct((M_ALL, D_MODEL), F32),
        scratch_shapes=[pltpu.VMEM((D_MODEL, TN), BF16)],
        compiler_params=_params(("arbitrary", "arbitrary"), VMEM_LIMIT),
        name="out_proj_residual",
    )(z, w_out, xp, xs, mods3)


def _mlp_kernel(x_ref, nw_ref, sh_ref, sc_ref, g_ref, wu_ref, wd_ref, o_ref, h_ref):
    f = pl.program_id(1)

    @pl.when(f == 0)
    def _():
        wu = wu_ref[...].astype(BF16)
        wd = wd_ref[...].astype(BF16)
        for r0 in range(0, TM, ROW_CHUNK):
            rows = slice(r0, r0 + ROW_CHUNK)
            h = _modnorm(x_ref[rows, :], nw_ref[...], sc_ref[...], sh_ref[...]).astype(BF16)
            h_ref[rows, :] = h
            u = jnp.square(jnp.maximum(_dot(h, wu), 0.0)).astype(BF16)
            o_ref[rows, :] = _dot(u, wd)

    @pl.when(f > 0)
    def _():
        u = _dot(h_ref[...], wu_ref[...].astype(BF16))
        u = jnp.square(jnp.maximum(u, 0.0)).astype(BF16)
        for c0 in range(0, D_MODEL, MLP_COLS):
            cols = slice(c0, c0 + MLP_COLS)
            o_ref[:, cols] += _dot(u, wd_ref[:, cols].astype(BF16))

    @pl.when(f == pl.num_programs(1) - 1)
    def _():
        o_ref[...] = x_ref[...] + g_ref[...] * o_ref[...]


def _mlp_call(x1, nw, mods3, w_up, w_down, tile0, n_tiles):
    mod = lambda k: pl.BlockSpec((None, 1, D_MODEL), lambda i, f: (_mod_row(tile0 + i, TM), 0, k))
    return pl.pallas_call(
        _mlp_kernel,
        grid=(n_tiles, D_FF // TF),
        in_specs=[pl.BlockSpec((TM, D_MODEL), lambda i, f: (tile0 + i, 0), pipeline_mode=pl.Buffered(1)),
                  pl.BlockSpec((1, D_MODEL), lambda i, f: (0, 0)),
                  mod(3), mod(4), mod(5),
                  pl.BlockSpec((D_MODEL, TF), lambda i, f: (0, f)),
                  pl.BlockSpec((TF, D_MODEL), lambda i, f: (f, 0))],
        out_specs=pl.BlockSpec((TM, D_MODEL), lambda i, f: (i, 0)),
        out_shape=jax.ShapeDtypeStruct((n_tiles * TM, D_MODEL), F32),
        scratch_shapes=[pltpu.VMEM((TM, D_MODEL), BF16)],
        compiler_params=_params(("arbitrary", "arbitrary"), VMEM_LIMIT),
        name="mlp",
    )(x1, nw, mods3, mods3, mods3, w_up, w_down)


def kernel(x_prompt, x_sample, cache_a_k, cache_a_v, cache_b_k, cache_b_v, c, c_ctx, norm1_w, norm2_w, w_ada, b_ada, w_in, q_norm_a, k_norm_a, q_norm_b, k_norm_b, sink_a, rpb_b, w_br_a, w_br_b, w_out, w_up, w_down):
    assert w_ada.shape[0] == 1, "one trunk layer"
    xp = x_prompt.reshape(M_PROMPT, D_MODEL)
    xs = x_sample.reshape(M_SAMPLE, D_MODEL)

    cvecs = jnp.concatenate([c_ctx[None, :], c, jnp.zeros((8 - 1 - DEC_BATCH, D_MODEL), F32)], axis=0)
    mods = _ada_call(cvecs, w_ada[0], b_ada)
    mods3 = mods.reshape(8, 1, 6 * D_MODEL)

    w = w_in[0]
    rope = _rope_tables()
    h, kva, ka_p, va_p = _norm_kva_call(xp, xs, norm1_w, mods3, w, k_norm_a, rope)
    qkv, gates, kb_p, vb_p = _proj_call(h, w, q_norm_a, q_norm_b, k_norm_b, rope)

    sink = sink_a[0]
    oa_p, ob_p = _ctx_attn_call(sink, qkv, kva)
    oa_s = _win_attn_call(sink, qkv, kva, cache_a_k, cache_a_v)
    ob_s = _na_attn_call(qkv, cache_b_k, cache_b_v, rpb_b[0])

    z = _merge_call(oa_p, oa_s, ob_p, ob_s, w_br_a[0], w_br_b[0], gates)
    x1 = _out_call(z, w_out[0], xp, xs, mods3)

    p_tiles = M_PROMPT // TM
    y_p = _mlp_call(x1, norm2_w, mods3, w_up[0], w_down[0], 0, p_tiles)
    y_s = _mlp_call(x1, norm2_w, mods3, w_up[0], w_down[0], p_tiles, M_SAMPLE // TM)

    return (y_p.reshape(BATCH, SEQ, D_MODEL),
            y_s.reshape(DEC_BATCH, DEC_SEQ, D_MODEL),
            ka_p.reshape(BATCH, 1, SEQ, A_KV_HEADS, HEAD_DIM),
            va_p.reshape(BATCH, 1, SEQ, A_KV_HEADS, HEAD_DIM),
            kb_p.reshape(BATCH, 1, SEQ, B_HEADS, HEAD_DIM),
            vb_p.reshape(BATCH, 1, SEQ, B_HEADS, HEAD_DIM))
```

```python
import functools

import numpy as np
import jax
import jax.numpy as jnp
from jax import lax
from jax.experimental import pallas as pl
from jax.experimental.pallas import tpu as pltpu

D_MODEL = 2048
BATCH = 16
SEQ = 256
DEC_BATCH = 2
DEC_SEQ = 1024
PAST_LEN = 256
GRID_W = 64
HEAD_DIM = 128
A_HEADS = 8
A_KV_HEADS = 2
A_GROUP = A_HEADS // A_KV_HEADS
A_WINDOW = 128
BLOCK = 128
B_HEADS = 8
NA_ROWS_MAX = 8
NA_COLS = 16
D_FF = 4 * D_MODEL
ROPE_THETA = 10000.0
EPS = 1e-6
NEG = -1e30
A_Q = A_HEADS * HEAD_DIM
A_KV = A_KV_HEADS * HEAD_DIM
B_QKV = B_HEADS * HEAD_DIM
IN_WIDTH = A_Q + 2 * A_KV + 3 * B_QKV + 2 * D_MODEL
SCALE = HEAD_DIM ** -0.5

M_PROMPT = BATCH * SEQ
M_SAMPLE = DEC_BATCH * DEC_SEQ
M_ALL = M_PROMPT + M_SAMPLE
GRID_ROWS = DEC_SEQ // GRID_W
NA_ROWS = min(NA_ROWS_MAX, GRID_ROWS)

COL_QA = 0
COL_KVA = A_Q
COL_QB = A_Q + 2 * A_KV
COL_KB = COL_QB + B_QKV
COL_VB = COL_KB + B_QKV
COL_G = COL_VB + B_QKV

TM = 1024
TN = 1024
ROW_CHUNK = 256
W_SUB = 512
TM_NORM = 512
TF = 512
MLP_COLS = 512
VMEM_LIMIT = 60 * 1024 * 1024

F32 = jnp.float32
BF16 = jnp.bfloat16


def _mod_row(i, tm):
    p_tiles = M_PROMPT // tm
    return jnp.where(i < p_tiles, 0, 1 + (i - p_tiles) // (DEC_SEQ // tm))


def _dot(a, b):
    return jnp.dot(a, b, preferred_element_type=F32)


def _dot_nt(a, b):
    return lax.dot_general(a, b, (((1,), (1,)), ((), ())), preferred_element_type=F32)


def _params(sem, vmem=None):
    return pltpu.CompilerParams(dimension_semantics=sem, vmem_limit_bytes=vmem)


def _ada_kernel(c_ref, w_ref, b_ref, o_ref):
    cv = c_ref[...]
    s = (cv * jax.nn.sigmoid(cv)).astype(BF16)
    o_ref[:, 0, :] = _dot(s, w_ref[...].astype(BF16)) + b_ref[...]


def _ada_call(cvecs, w_ada, b_ada):
    tn = 1024
    n = 6 * D_MODEL
    return pl.pallas_call(
        _ada_kernel,
        grid=(n // tn,),
        in_specs=[pl.BlockSpec((8, D_MODEL), lambda j: (0, 0)),
                  pl.BlockSpec((D_MODEL, tn), lambda j: (0, j)),
                  pl.BlockSpec((1, tn), lambda j: (0, j))],
        out_specs=pl.BlockSpec((8, 1, tn), lambda j: (0, 0, j)),
        out_shape=jax.ShapeDtypeStruct((8, 1, n), F32),
        compiler_params=_params(("arbitrary",), VMEM_LIMIT),
        name="ada_mod",
    )(cvecs, w_ada, b_ada)


def _modnorm(x, nw, sc, sh):
    y = x * lax.rsqrt(jnp.mean(x * x, axis=-1, keepdims=True) + EPS)
    return y * (nw * (1.0 + sc)) + sh


def _head_norm(x, nw):
    return x * lax.rsqrt(jnp.mean(x * x, axis=-1, keepdims=True) + EPS) * nw


def _rope(x, cos, sin_signed):
    lane = lax.broadcasted_iota(jnp.int32, x.shape, 1)
    partner = jnp.where((lane % 64) < 32, pltpu.roll(x, 96, 1), pltpu.roll(x, 32, 1))
    return x * cos + partner * sin_signed


def _norm_kva_kernel(xp_ref, xs_ref, nw1_ref, sh_ref, sc_ref, w_ref, nwk_ref, cos_ref, sin_ref,
                     h_ref, kv_ref, kp_ref, vp_ref, wbf):
    i = pl.program_id(0)
    p_tiles = M_PROMPT // TM_NORM

    @pl.when(i == 0)
    def _():
        wbf[...] = w_ref[...].astype(BF16)

    def head_cols(k, base=0):
        return slice(base + k * HEAD_DIM, base + (k + 1) * HEAD_DIM)

    def run(prompt):
        x_ref = xp_ref if prompt else xs_ref
        for r0 in range(0, TM_NORM, ROW_CHUNK):
            rows = slice(r0, r0 + ROW_CHUNK)
            h = _modnorm(x_ref[rows, :], nw1_ref[...], sc_ref[...], sh_ref[...]).astype(BF16)
            h_ref[rows, :] = h
            acc = _dot(h, wbf[...])
            for k in range(A_KV_HEADS):
                y = _head_norm(acc[:, head_cols(k)], nwk_ref[...])
                v = acc[:, head_cols(k, A_KV)]
                if prompt:
                    kp_ref[rows, k, :] = y
                    vp_ref[rows, k, :] = v
                else:
                    y = _rope(y, cos_ref[rows, :], sin_ref[rows, :])
                kv_ref[rows, head_cols(k)] = y.astype(BF16)
                kv_ref[rows, head_cols(k, A_KV)] = v.astype(BF16)

    pl.when(i < p_tiles)(functools.partial(run, True))
    pl.when(i >= p_tiles)(functools.partial(run, False))


def _norm_kva_call(xp, xs, nw1, mods3, w_in, nwk, rope):
    tm = TM_NORM
    p_tiles = M_PROMPT // tm
    s_tiles = DEC_SEQ // tm
    assert COL_KVA % (2 * A_KV) == 0
    rope_spec = pl.BlockSpec((tm, HEAD_DIM), lambda i: (jnp.maximum(i - p_tiles, 0) % s_tiles, 0))
    parked = pl.BlockSpec((tm, A_KV_HEADS, HEAD_DIM), lambda i: (jnp.minimum(i, p_tiles - 1), 0, 0))
    return pl.pallas_call(
        _norm_kva_kernel,
        grid=(M_ALL // tm,),
        in_specs=[pl.BlockSpec((tm, D_MODEL), lambda i: (jnp.minimum(i, p_tiles - 1), 0)),
                  pl.BlockSpec((tm, D_MODEL), lambda i: (jnp.maximum(i - p_tiles, 0), 0)),
                  pl.BlockSpec((1, D_MODEL), lambda i: (0, 0)),
                  pl.BlockSpec((None, 1, D_MODEL), lambda i: (_mod_row(i, tm), 0, 0)),
                  pl.BlockSpec((None, 1, D_MODEL), lambda i: (_mod_row(i, tm), 0, 1)),
                  pl.BlockSpec((D_MODEL, 2 * A_KV), lambda i: (0, COL_KVA // (2 * A_KV)),
                               pipeline_mode=pl.Buffered(1)),
                  pl.BlockSpec((1, HEAD_DIM), lambda i: (0, 0)),
                  rope_spec, rope_spec],
        out_specs=[pl.BlockSpec((tm, D_MODEL), lambda i: (i, 0)),
                   pl.BlockSpec((tm, 2 * A_KV), lambda i: (i, 0)),
                   parked, parked],
        out_shape=[jax.ShapeDtypeStruct((M_ALL, D_MODEL), BF16),
                   jax.ShapeDtypeStruct((M_ALL, 2 * A_KV), BF16),
                   jax.ShapeDtypeStruct((M_PROMPT, A_KV_HEADS, HEAD_DIM), F32),
                   jax.ShapeDtypeStruct((M_PROMPT, A_KV_HEADS, HEAD_DIM), F32)],
        scratch_shapes=[pltpu.VMEM((D_MODEL, 2 * A_KV), BF16)],
        compiler_params=_params(("arbitrary",), VMEM_LIMIT),
        name="norm1_proj_kva",
    )(xp, xs, nw1, mods3, mods3, w_in, nwk, *rope)


def _rope_tables():
    n_freq = HEAD_DIM // 4
    pos = np.arange(DEC_SEQ)
    row = (pos // GRID_W).astype(np.float64)
    col = (pos % GRID_W).astype(np.float64)
    inv = ROPE_THETA ** (-np.arange(n_freq, dtype=np.float64) / n_freq)
    ar = row[:, None] * inv
    ac = col[:, None] * inv
    cos = np.concatenate([np.cos(ar), np.cos(ar), np.cos(ac), np.cos(ac)], axis=-1)
    sin = np.concatenate([-np.sin(ar), np.sin(ar), -np.sin(ac), np.sin(ac)], axis=-1)
    return jnp.asarray(cos, F32), jnp.asarray(sin, F32)


PROJ_KINDS = ("qa", "qb", "kb", "vb") + ("gate",) * (2 * D_MODEL // TN)
PROJ_COLS = (COL_QA, COL_QB, COL_KB, COL_VB) + tuple(range(COL_G, IN_WIDTH, TN))
N_QKV = 4


def _proj_kernel(h_ref, w0_ref, w1_ref, nqa_ref, nqb_ref, nkb_ref, cos_ref, sin_ref,
                 qkv_ref, gate_ref, kbp_ref, vbp_ref, wbf):
    j, i = pl.program_id(0), pl.program_id(1)
    is_prompt = i < M_PROMPT // TM

    @pl.when(i == 0)
    def _():
        wbf[:, :W_SUB] = w0_ref[...].astype(BF16)
        wbf[:, W_SUB:] = w1_ref[...].astype(BF16)

    def head_cols(k):
        return slice(k * HEAD_DIM, (k + 1) * HEAD_DIM)

    def run(kind, prompt):
        for r0 in range(0, TM, ROW_CHUNK):
            rows = slice(r0, r0 + ROW_CHUNK)
            acc = _dot(h_ref[rows, :], wbf[...])
            if kind == "gate":
                gate_ref[rows, :] = jax.nn.sigmoid(acc)
            elif kind == "vb":
                if prompt:
                    vbp_ref[rows, :] = acc
                qkv_ref[rows, :] = acc.astype(BF16)
            else:
                nw_ref = {"qa": nqa_ref, "qb": nqb_ref, "kb": nkb_ref}[kind]
                for k in range(TN // HEAD_DIM):
                    y = _head_norm(acc[:, head_cols(k)], nw_ref[...])
                    if kind == "qa" and not prompt:
                        y = _rope(y, cos_ref[rows, :], sin_ref[rows, :])
                    if kind == "kb" and prompt:
                        kbp_ref[rows, head_cols(k)] = y
                    qkv_ref[rows, head_cols(k)] = y.astype(BF16)

    for p, kind in enumerate(PROJ_KINDS[:N_QKV]):
        if kind == "qb":
            pl.when(j == p)(functools.partial(run, kind, None))
        else:
            pl.when((j == p) & is_prompt)(functools.partial(run, kind, True))
            pl.when((j == p) & jnp.logical_not(is_prompt))(functools.partial(run, kind, False))
    pl.when(j >= N_QKV)(functools.partial(run, "gate", None))


def _proj_call(h, w_in, nqa, nqb, nkb, rope):
    assert PROJ_KINDS[:N_QKV] == ("qa", "qb", "kb", "vb") and TN == 2 * W_SUB and A_Q == B_QKV == TN
    n_panels = len(PROJ_KINDS)
    p_tiles = M_PROMPT // TM
    s_tiles = DEC_SEQ // TM
    n_tiles = M_ALL // TM
    kb_panel, vb_panel = PROJ_KINDS.index("kb"), PROJ_KINDS.index("vb")

    def w_sub(j):
        idx = PROJ_COLS[0] // W_SUB
        for p in range(1, n_panels):
            idx = jnp.where(j == p, PROJ_COLS[p] // W_SUB, idx)
        return idx

    def qkv_idx(j, i):
        done = j >= N_QKV
        return (jnp.minimum(j, N_QKV - 1), jnp.where(done, n_tiles - 1, i), 0)

    def gate_idx(j, i):
        return (jnp.where(j < N_QKV, 0, i), jnp.maximum(j - N_QKV, 0))

    def cache_idx(panel):
        def idx(j, i):
            return (jnp.where(j < panel, 0, jnp.where(j > panel, p_tiles - 1, jnp.minimum(i, p_tiles - 1))), 0)
        return idx

    norm_spec = pl.BlockSpec((1, HEAD_DIM), lambda j, i: (0, 0))
    rope_spec = pl.BlockSpec((TM, HEAD_DIM), lambda j, i: (jnp.maximum(i - p_tiles, 0) % s_tiles, 0))
    return pl.pallas_call(
        _proj_kernel,
        grid=(n_panels, n_tiles),
        in_specs=[pl.BlockSpec((TM, D_MODEL), lambda j, i: (i, 0)),
                  pl.BlockSpec((D_MODEL, W_SUB), lambda j, i: (0, w_sub(j))),
                  pl.BlockSpec((D_MODEL, W_SUB), lambda j, i: (0, w_sub(j) + 1)),
                  norm_spec, norm_spec, norm_spec, rope_spec, rope_spec],
        out_specs=[pl.BlockSpec((None, TM, TN), qkv_idx),
                   pl.BlockSpec((TM, TN), gate_idx),
                   pl.BlockSpec((TM, TN), cache_idx(kb_panel)),
                   pl.BlockSpec((TM, TN), cache_idx(vb_panel))],
        out_shape=[jax.ShapeDtypeStruct((N_QKV, M_ALL, TN), BF16),
                   jax.ShapeDtypeStruct((M_ALL, 2 * D_MODEL), F32),
                   jax.ShapeDtypeStruct((M_PROMPT, B_QKV), F32),
                   jax.ShapeDtypeStruct((M_PROMPT, B_QKV), F32)],
        scratch_shapes=[pltpu.VMEM((D_MODEL, TN), BF16)],
        compiler_params=_params(("arbitrary", "arbitrary"), VMEM_LIMIT),
        name="proj",
    )(h, w_in, w_in, nqa, nqb, nkb, *rope)


LOG2E = 1.4426950408889634
QK_LOG2 = SCALE * LOG2E


def _softmax_pv(parts, sink):
    m = functools.reduce(jnp.maximum, [jnp.max(t, axis=-1, keepdims=True) for t, _ in parts])
    if sink is not None:
        m = jnp.maximum(m, sink)
    acc = None
    for t, v in parts:
        p = jnp.exp2(t - m).astype(BF16)
        y = _dot(p, jnp.concatenate([v, jnp.ones_like(v)], axis=1))
        acc = y if acc is None else acc + y
    o, l = acc[:, :HEAD_DIM], acc[:, HEAD_DIM:]
    if sink is not None:
        l = l + jnp.exp2(sink - m)
    return o / l


def _sink_column(sink_ref, hk, rows_per_head):
    n = A_GROUP * rows_per_head
    g = lax.broadcasted_iota(jnp.int32, (n, 1), 0) // rows_per_head
    col = jnp.full((n, 1), sink_ref[hk * A_GROUP], F32)
    for k in range(1, A_GROUP):
        col = jnp.where(g == k, sink_ref[hk * A_GROUP + k], col)
    return col * LOG2E


def _stack_group(q_ref, hk):
    return jnp.concatenate(
        [q_ref[:, (hk * A_GROUP + g) * HEAD_DIM:(hk * A_GROUP + g + 1) * HEAD_DIM] for g in range(A_GROUP)], axis=0)


def _unstack_group(o_ref, hk, o, rows):
    for g in range(A_GROUP):
        c = (hk * A_GROUP + g) * HEAD_DIM
        o_ref[:, c:c + HEAD_DIM] = o[g * rows:(g + 1) * rows].astype(BF16)


def _ctx_attn_kernel(sink_ref, qkv_ref, kva_ref, oa_ref, ob_ref):
    qa_ref, qb_ref, kb_ref, vb_ref = (qkv_ref.at[PROJ_KINDS.index(kind)] for kind in ("qa", "qb", "kb", "vb"))
    for hk in range(A_KV_HEADS):
        k = kva_ref[:, hk * HEAD_DIM:(hk + 1) * HEAD_DIM]
        v = kva_ref[:, A_KV + hk * HEAD_DIM:A_KV + (hk + 1) * HEAD_DIM]
        q4 = _stack_group(qa_ref, hk)
        t = _dot_nt(q4, k) * QK_LOG2
        o = _softmax_pv([(t, v)], _sink_column(sink_ref, hk, SEQ))
        _unstack_group(oa_ref, hk, o, SEQ)
    for h in range(B_HEADS):
        sl = slice(h * HEAD_DIM, (h + 1) * HEAD_DIM)
        t = _dot_nt(qb_ref[:, sl], kb_ref[:, sl]) * QK_LOG2
        ob_ref[:, sl] = _softmax_pv([(t, vb_ref[:, sl])], None).astype(BF16)


def _ctx_attn_call(sink, qkv, kva):
    row = lambda w: pl.BlockSpec((SEQ, w), lambda b: (b, 0))
    return pl.pallas_call(
        _ctx_attn_kernel,
        grid=(BATCH,),
        in_specs=[pl.BlockSpec(memory_space=pltpu.SMEM),
                  pl.BlockSpec((N_QKV, SEQ, TN), lambda b: (0, b, 0)), row(2 * A_KV)],
        out_specs=[row(A_Q), row(B_QKV)],
        out_shape=[jax.ShapeDtypeStruct((M_PROMPT, A_Q), BF16),
                   jax.ShapeDtypeStruct((M_PROMPT, B_QKV), BF16)],
        compiler_params=_params(("arbitrary",), VMEM_LIMIT),
        name="attn_ctx",
    )(sink, qkv, kva)


BAND = 3 * BLOCK


def _cache_to_bf16(n_heads, ck_ref, cv_ref, ckb, cvb):
    for h in range(n_heads):
        sl = slice(h * HEAD_DIM, (h + 1) * HEAD_DIM)
        ckb[:, sl] = ck_ref[:, h, :].astype(BF16)
        cvb[:, sl] = cv_ref[:, h, :].astype(BF16)


def _win_attn_kernel(sink_ref, q_ref, kv_ref, ck_ref, cv_ref, o_ref, ckb, cvb):
    n = pl.program_id(1)
    pl.when(n == 0)(functools.partial(_cache_to_bf16, A_KV_HEADS, ck_ref, cv_ref, ckb, cvb))
    start = pl.multiple_of(jnp.clip((n - 1) * BLOCK, 0, DEC_SEQ - BAND), BLOCK)
    qpos = n * BLOCK + lax.broadcasted_iota(jnp.int32, (BLOCK, BAND), 0)
    kpos = start + lax.broadcasted_iota(jnp.int32, (BLOCK, BAND), 1)
    valid = jnp.abs(qpos - kpos) <= A_WINDOW
    valid = jnp.concatenate([valid.astype(jnp.int32)] * A_GROUP, axis=0) > 0
    for hk in range(A_KV_HEADS):
        sl = slice(hk * HEAD_DIM, (hk + 1) * HEAD_DIM)
        slv = slice(A_KV + hk * HEAD_DIM, A_KV + (hk + 1) * HEAD_DIM)
        k_loc = kv_ref[pl.ds(start, BAND), sl]
        v_loc = kv_ref[pl.ds(start, BAND), slv]
        q4 = _stack_group(q_ref, hk)
        t_loc = jnp.where(valid, _dot_nt(q4, k_loc) * QK_LOG2, NEG)
        t_ctx = _dot_nt(q4, ckb[:, sl]) * QK_LOG2
        o = _softmax_pv([(t_loc, v_loc), (t_ctx, cvb[:, sl])], _sink_column(sink_ref, hk, BLOCK))
        _unstack_group(o_ref, hk, o, BLOCK)


def _win_attn_call(sink, qkv, kva, cache_k, cache_v):
    nb = DEC_SEQ // BLOCK
    q0 = M_PROMPT // BLOCK
    b0 = M_PROMPT // DEC_SEQ
    return pl.pallas_call(
        _win_attn_kernel,
        grid=(DEC_BATCH, nb),
        in_specs=[pl.BlockSpec(memory_space=pltpu.SMEM),
                  pl.BlockSpec((None, BLOCK, A_Q), lambda b, n: (PROJ_KINDS.index("qa"), q0 + b * nb + n, 0)),
                  pl.BlockSpec((DEC_SEQ, 2 * A_KV), lambda b, n: (b0 + b, 0)),
                  pl.BlockSpec((None, None, PAST_LEN, A_KV_HEADS, HEAD_DIM), lambda b, n: (b, 0, 0, 0, 0)),
                  pl.BlockSpec((None, None, PAST_LEN, A_KV_HEADS, HEAD_DIM), lambda b, n: (b, 0, 0, 0, 0))],
        out_specs=pl.BlockSpec((BLOCK, A_Q), lambda b, n: (b * nb + n, 0)),
        out_shape=jax.ShapeDtypeStruct((M_SAMPLE, A_Q), BF16),
        scratch_shapes=[pltpu.VMEM((PAST_LEN, A_KV), BF16), pltpu.VMEM((PAST_LEN, A_KV), BF16)],
        compiler_params=_params(("arbitrary", "arbitrary"), VMEM_LIMIT),
        name="attn_window",
    )(sink, qkv, kva, cache_k, cache_v)


NA_QROWS = 4
NA_WIN_ROWS = 12
NA_Q = NA_QROWS * GRID_W
NA_KEYS = NA_WIN_ROWS * GRID_W
NA_DR = 2 * NA_ROWS_MAX - 1
NA_PAIRS = NA_DR + 1
RPB_W = 2 * NA_COLS - 1
RPB_PAD = GRID_W - NA_COLS


def _na_row_start(r):
    return jnp.clip(r - NA_ROWS // 2, 0, GRID_ROWS - NA_ROWS)


def _na_window_row0(blk):
    return jnp.clip(blk * NA_QROWS - NA_ROWS // 2, 0, GRID_ROWS - NA_WIN_ROWS)


def _check_na_windows():
    for blk in range(GRID_ROWS // NA_QROWS):
        w0 = int(np.clip(blk * NA_QROWS - NA_ROWS // 2, 0, GRID_ROWS - NA_WIN_ROWS))
        for r in range(blk * NA_QROWS, (blk + 1) * NA_QROWS):
            r0 = int(np.clip(r - NA_ROWS // 2, 0, GRID_ROWS - NA_ROWS))
            assert w0 <= r0 and r0 + NA_ROWS <= w0 + NA_WIN_ROWS, (blk, r)


_check_na_windows()


def _na_attn_kernel(q_ref, k_ref, v_ref, ck_ref, cv_ref, rpb_ref, o_ref, tab_ref, ckb, cvb):
    b, blk = pl.program_id(0), pl.program_id(1)
    lane = lax.broadcasted_iota(jnp.int32, (GRID_W, 2 * GRID_W), 1)
    pl.when(blk == 0)(functools.partial(_cache_to_bf16, B_HEADS, ck_ref, cv_ref, ckb, cvb))

    @pl.when((b == 0) & (blk == 0))
    def _():
        for h in range(B_HEADS):
            for d in range(NA_PAIRS):
                lo = jnp.broadcast_to(rpb_ref[h, d:d + 1, :], (GRID_W, 2 * GRID_W))
                hi = jnp.broadcast_to(rpb_ref[h, d + 1:d + 2, :], (GRID_W, 2 * GRID_W))
                lo = pltpu.roll(lo, GRID_W + 1, 1, stride=1, stride_axis=0)
                hi = pltpu.roll(hi, 1, 1, stride=1, stride_axis=0)
                tab_ref[h, d] = jnp.where(lane < GRID_W, lo, hi) * LOG2E

    row0 = _na_window_row0(blk)
    k0 = pl.multiple_of(row0 * GRID_W, GRID_W)
    qi = lax.broadcasted_iota(jnp.int32, (NA_Q, NA_KEYS), 0)
    ki = lax.broadcasted_iota(jnp.int32, (NA_Q, NA_KEYS), 1)
    qrow, qcol = blk * NA_QROWS + qi // GRID_W, qi % GRID_W
    krow, kcol = row0 + ki // GRID_W, ki % GRID_W
    rstart = _na_row_start(qrow)
    cstart = jnp.clip(qcol - NA_COLS // 2, 0, GRID_W - NA_COLS)
    valid = (krow >= rstart) & (krow < rstart + NA_ROWS) & (kcol >= cstart) & (kcol < cstart + NA_COLS)

    for h in range(B_HEADS):
        sl = slice(h * HEAD_DIM, (h + 1) * HEAD_DIM)
        bias = jnp.concatenate(
            [jnp.concatenate(
                [tab_ref[h, jnp.clip(row0 + 2 * p - blk * NA_QROWS - q + NA_ROWS_MAX, 0, NA_PAIRS - 1)]
                 for p in range(NA_WIN_ROWS // 2)], axis=1)
             for q in range(NA_QROWS)], axis=0)
        k_loc = k_ref[pl.ds(k0, NA_KEYS), sl]
        v_loc = v_ref[pl.ds(k0, NA_KEYS), sl]
        t_loc = jnp.where(valid, _dot_nt(q_ref[:, sl], k_loc) * QK_LOG2 + bias, NEG)
        t_ctx = _dot_nt(q_ref[:, sl], ckb[:, sl]) * QK_LOG2
        o = _softmax_pv([(t_loc, v_loc), (t_ctx, cvb[:, sl])], None)
        o_ref[:, sl] = o.astype(BF16)


def _na_bias_rows(rpb):
    rows = jnp.pad(rpb, ((0, 0), (1, 1), (0, 0)))
    left = jnp.broadcast_to(rows[..., :1], rows.shape[:2] + (RPB_PAD,))
    right = jnp.broadcast_to(rows[..., -1:], rows.shape[:2] + (2 * GRID_W - RPB_PAD - RPB_W,))
    return jnp.concatenate([left, rows, right], axis=-1)


def _na_attn_call(qkv, cache_k, cache_v, rpb):
    nblk = GRID_ROWS // NA_QROWS
    q0 = M_PROMPT // NA_Q
    b0 = M_PROMPT // DEC_SEQ
    return pl.pallas_call(
        _na_attn_kernel,
        grid=(DEC_BATCH, nblk),
        in_specs=[pl.BlockSpec((None, NA_Q, B_QKV), lambda b, r: (PROJ_KINDS.index("qb"), q0 + b * nblk + r, 0)),
                  pl.BlockSpec((None, DEC_SEQ, B_QKV), lambda b, r: (PROJ_KINDS.index("kb"), b0 + b, 0)),
                  pl.BlockSpec((None, DEC_SEQ, B_QKV), lambda b, r: (PROJ_KINDS.index("vb"), b0 + b, 0)),
                  pl.BlockSpec((None, None, PAST_LEN, B_HEADS, HEAD_DIM), lambda b, r: (b, 0, 0, 0, 0)),
                  pl.BlockSpec((None, None, PAST_LEN, B_HEADS, HEAD_DIM), lambda b, r: (b, 0, 0, 0, 0)),
                  pl.BlockSpec((B_HEADS, NA_DR + 2, 2 * GRID_W), lambda b, r: (0, 0, 0))],
        out_specs=pl.BlockSpec((NA_Q, B_QKV), lambda b, r: (b * nblk + r, 0)),
        out_shape=jax.ShapeDtypeStruct((M_SAMPLE, B_QKV), BF16),
        scratch_shapes=[pltpu.VMEM((B_HEADS, NA_PAIRS, GRID_W, 2 * GRID_W), F32),
                        pltpu.VMEM((PAST_LEN, B_QKV), BF16), pltpu.VMEM((PAST_LEN, B_QKV), BF16)],
        compiler_params=_params(("arbitrary", "arbitrary"), VMEM_LIMIT),
        name="attn_neighbourhood",
    )(qkv, qkv, qkv, cache_k, cache_v, _na_bias_rows(rpb))


def _merge_kernel(oap, oas, obp, obs, wa_ref, wb_ref, ga_ref, gb_ref, z_ref, wa_bf, wb_bf):
    i = pl.program_id(1)
    p_tiles = M_PROMPT // TM

    @pl.when(i == 0)
    def _():
        wa_bf[...] = wa_ref[...].astype(BF16)
        wb_bf[...] = wb_ref[...].astype(BF16)

    def run(oa_ref, ob_ref):
        for r0 in range(0, TM, ROW_CHUNK):
            rows = slice(r0, r0 + ROW_CHUNK)
            ya = _dot(oa_ref[rows, :], wa_bf[...])
            yb = _dot(ob_ref[rows, :], wb_bf[...])
            z_ref[rows, :] = (ga_ref[rows, :] * ya + gb_ref[rows, :] * yb).astype(BF16)

    pl.when(i < p_tiles)(functools.partial(run, oap, obp))
    pl.when(i >= p_tiles)(functools.partial(run, oas, obs))


def _merge_call(oa_p, oa_s, ob_p, ob_s, w_br_a, w_br_b, gates):
    p_tiles = M_PROMPT // TM
    nj = D_MODEL // TN
    pspec = pl.BlockSpec((TM, A_Q), lambda j, i: (jnp.minimum(i, p_tiles - 1), 0))
    sspec = pl.BlockSpec((TM, A_Q), lambda j, i: (jnp.maximum(i - p_tiles, 0), 0))
    once = dict(pipeline_mode=pl.Buffered(1))
    return pl.pallas_call(
        _merge_kernel,
        grid=(nj, M_ALL // TM),
        in_specs=[pspec, sspec, pspec, sspec,
                  pl.BlockSpec((A_Q, TN), lambda j, i: (0, j), **once),
                  pl.BlockSpec((B_QKV, TN), lambda j, i: (0, j), **once),
                  pl.BlockSpec((TM, TN), lambda j, i: (i, j)),
                  pl.BlockSpec((TM, TN), lambda j, i: (i, nj + j))],
        out_specs=pl.BlockSpec((TM, TN), lambda j, i: (i, j)),
        out_shape=jax.ShapeDtypeStruct((M_ALL, D_MODEL), BF16),
        scratch_shapes=[pltpu.VMEM((A_Q, TN), BF16), pltpu.VMEM((B_QKV, TN), BF16)],
        compiler_params=_params(("arbitrary", "arbitrary"), VMEM_LIMIT),
        name="merge_branches",
    )(oa_p, oa_s, ob_p, ob_s, w_br_a, w_br_b, gates, gates)


def _out_kernel(z_ref, w_ref, xp_ref, xs_ref, g_ref, o_ref, wbf):
    i = pl.program_id(1)
    p_tiles = M_PROMPT // TM

    @pl.when(i == 0)
    def _():
        wbf[...] = w_ref[...].astype(BF16)

    def run(x_ref):
        for r0 in range(0, TM, ROW_CHUNK):
            rows = slice(r0, r0 + ROW_CHUNK)
            o_ref[rows, :] = x_ref[rows, :] + g_ref[...] * _dot(z_ref[rows, :], wbf[...])

    pl.when(i < p_tiles)(functools.partial(run, xp_ref))
    pl.when(i >= p_tiles)(functools.partial(run, xs_ref))


def _out_call(z, w_out, xp, xs, mods3):
    p_tiles = M_PROMPT // TM
    nj = D_MODEL // TN
    return pl.pallas_call(
        _out_kernel,
        grid=(nj, M_ALL // TM),
        in_specs=[pl.BlockSpec((TM, D_MODEL), lambda j, i: (i, 0)),
                  pl.BlockSpec((D_MODEL, TN), lambda j, i: (0, j)),
                  pl.BlockSpec((TM, TN), lambda j, i: (jnp.minimum(i, p_tiles - 1), j)),
                  pl.BlockSpec((TM, TN), lambda j, i: (jnp.maximum(i - p_tiles, 0), j)),
                  pl.BlockSpec((None, 1, TN), lambda j, i: (_mod_row(i, TM), 0, 2 * nj + j))],
        out_specs=pl.BlockSpec((TM, TN), lambda j, i: (i, j)),
        out_shape=jax.ShapeDtypeStruct((M_ALL, D_MODEL), F32),
        scratch_shapes=[pltpu.VMEM((D_MODEL, TN), BF16)],
        compiler_params=_params(("arbitrary", "arbitrary"), VMEM_LIMIT),
        name="out_proj_residual",
    )(z, w_out, xp, xs, mods3)


def _mlp_kernel(x_hbm, nw_ref, sh_ref, sc_ref, g_ref, wu_ref, wd_ref, o_ref, h_ref, x_buf, x_sem, *, tile0):
    i, f = pl.program_id(0), pl.program_id(1)

    def x_copy(tile):
        return pltpu.make_async_copy(x_hbm.at[pl.ds((tile0 + tile) * TM, TM), :], x_buf, x_sem)

    @pl.when((i == 0) & (f == 0))
    def _():
        x_copy(0).start()

    @pl.when(f == 0)
    def _():
        x_copy(i).wait()
        wu = wu_ref[...].astype(BF16)
        wd = wd_ref[...].astype(BF16)
        for r0 in range(0, TM, ROW_CHUNK):
            rows = slice(r0, r0 + ROW_CHUNK)
            x = x_buf[rows, :]
            h = _modnorm(x, nw_ref[...], sc_ref[...], sh_ref[...]).astype(BF16)
            h_ref[rows, :] = h
            u = jnp.square(jnp.maximum(_dot(h, wu), 0.0)).astype(BF16)
            o_ref[rows, :] = x + g_ref[...] * _dot(u, wd)

    @pl.when((f == 1) & (i + 1 < pl.num_programs(0)))
    def _():
        x_copy(i + 1).start()

    @pl.when(f > 0)
    def _():
        u = _dot(h_ref[...], wu_ref[...].astype(BF16))
        u = jnp.square(jnp.maximum(u, 0.0)).astype(BF16)
        for c0 in range(0, D_MODEL, MLP_COLS):
            cols = slice(c0, c0 + MLP_COLS)
            o_ref[:, cols] += g_ref[:, cols] * _dot(u, wd_ref[:, cols].astype(BF16))


def _mlp_call(x1, nw, mods3, w_up, w_down, tile0, n_tiles):
    mod = lambda k: pl.BlockSpec((None, 1, D_MODEL), lambda i, f: (_mod_row(tile0 + i, TM), 0, k))
    return pl.pallas_call(
        functools.partial(_mlp_kernel, tile0=tile0),
        grid=(n_tiles, D_FF // TF),
        in_specs=[pl.BlockSpec(memory_space=pl.ANY),
                  pl.BlockSpec((1, D_MODEL), lambda i, f: (0, 0)),
                  mod(3), mod(4), mod(5),
                  pl.BlockSpec((D_MODEL, TF), lambda i, f: (0, f)),
                  pl.BlockSpec((TF, D_MODEL), lambda i, f: (f, 0))],
        out_specs=pl.BlockSpec((TM, D_MODEL), lambda i, f: (i, 0)),
        out_shape=jax.ShapeDtypeStruct((n_tiles * TM, D_MODEL), F32),
        scratch_shapes=[pltpu.VMEM((TM, D_MODEL), BF16),
                        pltpu.VMEM((TM, D_MODEL), F32),
                        pltpu.SemaphoreType.DMA(())],
        compiler_params=_params(("arbitrary", "arbitrary"), VMEM_LIMIT),
        name="mlp",
    )(x1, nw, mods3, mods3, mods3, w_up, w_down)


def kernel(x_prompt, x_sample, cache_a_k, cache_a_v, cache_b_k, cache_b_v, c, c_ctx, norm1_w, norm2_w, w_ada, b_ada, w_in, q_norm_a, k_norm_a, q_norm_b, k_norm_b, sink_a, rpb_b, w_br_a, w_br_b, w_out, w_up, w_down):
    assert w_ada.shape[0] == 1, "one trunk layer"
    xp = x_prompt.reshape(M_PROMPT, D_MODEL)
    xs = x_sample.reshape(M_SAMPLE, D_MODEL)

    cvecs = jnp.concatenate([c_ctx[None, :], c, jnp.zeros((8 - 1 - DEC_BATCH, D_MODEL), F32)], axis=0)
    mods3 = _ada_call(cvecs, w_ada[0], b_ada)

    w = w_in[0]
    rope = _rope_tables()
    h, kva, ka_p, va_p = _norm_kva_call(xp, xs, norm1_w, mods3, w, k_norm_a, rope)
    qkv, gates, kb_p, vb_p = _proj_call(h, w, q_norm_a, q_norm_b, k_norm_b, rope)

    sink = sink_a[0]
    oa_p, ob_p = _ctx_attn_call(sink, qkv, kva)
    oa_s = _win_attn_call(sink, qkv, kva, cache_a_k, cache_a_v)
    ob_s = _na_attn_call(qkv, cache_b_k, cache_b_v, rpb_b[0])

    z = _merge_call(oa_p, oa_s, ob_p, ob_s, w_br_a[0], w_br_b[0], gates)
    x1 = _out_call(z, w_out[0], xp, xs, mods3)

    p_tiles = M_PROMPT // TM
    y_p = _mlp_call(x1, norm2_w, mods3, w_up[0], w_down[0], 0, p_tiles)
    y_s = _mlp_call(x1, norm2_w, mods3, w_up[0], w_down[0], p_tiles, M_SAMPLE // TM)

    return (y_p.reshape(BATCH, SEQ, D_MODEL),
            y_s.reshape(DEC_BATCH, DEC_SEQ, D_MODEL),
            ka_p.reshape(BATCH, 1, SEQ, A_KV_HEADS, HEAD_DIM),
            va_p.reshape(BATCH, 1, SEQ, A_KV_HEADS, HEAD_DIM),
            kb_p.reshape(BATCH, 1, SEQ, B_HEADS, HEAD_DIM),
            vb_p.reshape(BATCH, 1, SEQ, B_HEADS, HEAD_DIM))
```

```python
import functools

import numpy as np
import jax
import jax.numpy as jnp
from jax import lax
from jax.experimental import pallas as pl
from jax.experimental.pallas import tpu as pltpu

D_MODEL = 2048
BATCH = 16
SEQ = 256
DEC_BATCH = 2
DEC_SEQ = 1024
PAST_LEN = 256
GRID_W = 64
HEAD_DIM = 128
A_HEADS = 8
A_KV_HEADS = 2
A_GROUP = A_HEADS // A_KV_HEADS
A_WINDOW = 128
BLOCK = 128
B_HEADS = 8
NA_ROWS_MAX = 8
NA_COLS = 16
D_FF = 4 * D_MODEL
ROPE_THETA = 10000.0
EPS = 1e-6
NEG = -1e30
A_Q = A_HEADS * HEAD_DIM
A_KV = A_KV_HEADS * HEAD_DIM
B_QKV = B_HEADS * HEAD_DIM
IN_WIDTH = A_Q + 2 * A_KV + 3 * B_QKV + 2 * D_MODEL
SCALE = HEAD_DIM ** -0.5

M_PROMPT = BATCH * SEQ
M_SAMPLE = DEC_BATCH * DEC_SEQ
M_ALL = M_PROMPT + M_SAMPLE
GRID_ROWS = DEC_SEQ // GRID_W
NA_ROWS = min(NA_ROWS_MAX, GRID_ROWS)

COL_QA = 0
COL_KVA = A_Q
COL_QB = A_Q + 2 * A_KV
COL_KB = COL_QB + B_QKV
COL_VB = COL_KB + B_QKV
COL_G = COL_VB + B_QKV

TM = 1024
TN = 1024
ROW_CHUNK = 256
W_SUB = 512
TM_NORM = 512
TF = 512
MLP_COLS = 512
VMEM_LIMIT = 60 * 1024 * 1024

F32 = jnp.float32
BF16 = jnp.bfloat16


def _mod_row(i, tm):
    p_tiles = M_PROMPT // tm
    return jnp.where(i < p_tiles, 0, 1 + (i - p_tiles) // (DEC_SEQ // tm))


def _dot(a, b):
    return jnp.dot(a, b, preferred_element_type=F32)


def _dot_nt(a, b):
    return lax.dot_general(a, b, (((1,), (1,)), ((), ())), preferred_element_type=F32)


def _params(sem, vmem=None):
    return pltpu.CompilerParams(dimension_semantics=sem, vmem_limit_bytes=vmem)


def _ada_kernel(c_ref, w_ref, b_ref, o_ref):
    cv = c_ref[...]
    s = (cv * jax.nn.sigmoid(cv)).astype(BF16)
    o_ref[:, 0, :] = _dot(s, w_ref[...].astype(BF16)) + b_ref[...]


def _ada_call(cvecs, w_ada, b_ada):
    tn = 1024
    n = 6 * D_MODEL
    return pl.pallas_call(
        _ada_kernel,
        grid=(n // tn,),
        in_specs=[pl.BlockSpec((8, D_MODEL), lambda j: (0, 0)),
                  pl.BlockSpec((D_MODEL, tn), lambda j: (0, j)),
                  pl.BlockSpec((1, tn), lambda j: (0, j))],
        out_specs=pl.BlockSpec((8, 1, tn), lambda j: (0, 0, j)),
        out_shape=jax.ShapeDtypeStruct((8, 1, n), F32),
        compiler_params=_params(("arbitrary",), VMEM_LIMIT),
        name="ada_mod",
    )(cvecs, w_ada, b_ada)


def _modnorm(x, nw, sc, sh):
    y = x * lax.rsqrt(jnp.mean(x * x, axis=-1, keepdims=True) + EPS)
    return y * (nw * (1.0 + sc)) + sh


def _head_norm(x, nw):
    return x * lax.rsqrt(jnp.mean(x * x, axis=-1, keepdims=True) + EPS) * nw


def _rope(x, cos, sin_signed):
    lane = lax.broadcasted_iota(jnp.int32, x.shape, 1)
    partner = jnp.where((lane % 64) < 32, pltpu.roll(x, 96, 1), pltpu.roll(x, 32, 1))
    return x * cos + partner * sin_signed


def _norm_kva_kernel(xp_ref, xs_ref, nw1_ref, sh_ref, sc_ref, w_ref, nwk_ref, cos_ref, sin_ref,
                     h_ref, kv_ref, kp_ref, vp_ref, wbf):
    i = pl.program_id(0)
    p_tiles = M_PROMPT // TM_NORM

    @pl.when(i == 0)
    def _():
        wbf[...] = w_ref[...].astype(BF16)

    def head_cols(k, base=0):
        return slice(base + k * HEAD_DIM, base + (k + 1) * HEAD_DIM)

    def run(prompt):
        x_ref = xp_ref if prompt else xs_ref
        for r0 in range(0, TM_NORM, ROW_CHUNK):
            rows = slice(r0, r0 + ROW_CHUNK)
            h = _modnorm(x_ref[rows, :], nw1_ref[...], sc_ref[...], sh_ref[...]).astype(BF16)
            h_ref[rows, :] = h
            acc = _dot(h, wbf[...])
            for k in range(A_KV_HEADS):
                y = _head_norm(acc[:, head_cols(k)], nwk_ref[...])
                v = acc[:, head_cols(k, A_KV)]
                if prompt:
                    kp_ref[rows, k, :] = y
                    vp_ref[rows, k, :] = v
                else:
                    y = _rope(y, cos_ref[rows, :], sin_ref[rows, :])
                kv_ref[rows, head_cols(k)] = y.astype(BF16)
                kv_ref[rows, head_cols(k, A_KV)] = v.astype(BF16)

    pl.when(i < p_tiles)(functools.partial(run, True))
    pl.when(i >= p_tiles)(functools.partial(run, False))


def _norm_kva_call(xp, xs, nw1, mods3, w_in, nwk, rope):
    tm = TM_NORM
    p_tiles = M_PROMPT // tm
    s_tiles = DEC_SEQ // tm
    assert COL_KVA % (2 * A_KV) == 0
    rope_spec = pl.BlockSpec((tm, HEAD_DIM), lambda i: (jnp.maximum(i - p_tiles, 0) % s_tiles, 0))
    parked = pl.BlockSpec((tm, A_KV_HEADS, HEAD_DIM), lambda i: (jnp.minimum(i, p_tiles - 1), 0, 0))
    return pl.pallas_call(
        _norm_kva_kernel,
        grid=(M_ALL // tm,),
        in_specs=[pl.BlockSpec((tm, D_MODEL), lambda i: (jnp.minimum(i, p_tiles - 1), 0)),
                  pl.BlockSpec((tm, D_MODEL), lambda i: (jnp.maximum(i - p_tiles, 0), 0)),
                  pl.BlockSpec((1, D_MODEL), lambda i: (0, 0)),
                  pl.BlockSpec((None, 1, D_MODEL), lambda i: (_mod_row(i, tm), 0, 0)),
                  pl.BlockSpec((None, 1, D_MODEL), lambda i: (_mod_row(i, tm), 0, 1)),
                  pl.BlockSpec((D_MODEL, 2 * A_KV), lambda i: (0, COL_KVA // (2 * A_KV)),
                               pipeline_mode=pl.Buffered(1)),
                  pl.BlockSpec((1, HEAD_DIM), lambda i: (0, 0)),
                  rope_spec, rope_spec],
        out_specs=[pl.BlockSpec((tm, D_MODEL), lambda i: (i, 0)),
                   pl.BlockSpec((tm, 2 * A_KV), lambda i: (i, 0)),
                   parked, parked],
        out_shape=[jax.ShapeDtypeStruct((M_ALL, D_MODEL), BF16),
                   jax.ShapeDtypeStruct((M_ALL, 2 * A_KV), BF16),
                   jax.ShapeDtypeStruct((M_PROMPT, A_KV_HEADS, HEAD_DIM), F32),
                   jax.ShapeDtypeStruct((M_PROMPT, A_KV_HEADS, HEAD_DIM), F32)],
        scratch_shapes=[pltpu.VMEM((D_MODEL, 2 * A_KV), BF16)],
        compiler_params=_params(("arbitrary",), VMEM_LIMIT),
        name="norm1_proj_kva",
    )(xp, xs, nw1, mods3, mods3, w_in, nwk, *rope)


def _rope_tables():
    n_freq = HEAD_DIM // 4
    pos = np.arange(DEC_SEQ)
    row = (pos // GRID_W).astype(np.float64)
    col = (pos % GRID_W).astype(np.float64)
    inv = ROPE_THETA ** (-np.arange(n_freq, dtype=np.float64) / n_freq)
    ar = row[:, None] * inv
    ac = col[:, None] * inv
    cos = np.concatenate([np.cos(ar), np.cos(ar), np.cos(ac), np.cos(ac)], axis=-1)
    sin = np.concatenate([-np.sin(ar), np.sin(ar), -np.sin(ac), np.sin(ac)], axis=-1)
    return jnp.asarray(cos, F32), jnp.asarray(sin, F32)


PROJ_KINDS = ("qa", "qb", "kb", "vb") + ("gate",) * (2 * D_MODEL // TN)
PROJ_COLS = (COL_QA, COL_QB, COL_KB, COL_VB) + tuple(range(COL_G, IN_WIDTH, TN))
N_QKV = 4


def _proj_kernel(h_ref, w0_ref, w1_ref, nqa_ref, nqb_ref, nkb_ref, cos_ref, sin_ref,
                 qkv_ref, gate_ref, kbp_ref, vbp_ref, wbf):
    j, i = pl.program_id(0), pl.program_id(1)
    is_prompt = i < M_PROMPT // TM

    @pl.when(i == 0)
    def _():
        wbf[:, :W_SUB] = w0_ref[...].astype(BF16)
        wbf[:, W_SUB:] = w1_ref[...].astype(BF16)

    def head_cols(k):
        return slice(k * HEAD_DIM, (k + 1) * HEAD_DIM)

    def run(kind, prompt):
        for r0 in range(0, TM, ROW_CHUNK):
            rows = slice(r0, r0 + ROW_CHUNK)
            acc = _dot(h_ref[rows, :], wbf[...])
            if kind == "gate":
                gate_ref[rows, :] = jax.nn.sigmoid(acc)
            elif kind == "vb":
                if prompt:
                    vbp_ref[rows, :] = acc
                qkv_ref[rows, :] = acc.astype(BF16)
            else:
                nw_ref = {"qa": nqa_ref, "qb": nqb_ref, "kb": nkb_ref}[kind]
                for k in range(TN // HEAD_DIM):
                    y = _head_norm(acc[:, head_cols(k)], nw_ref[...])
                    if kind == "qa" and not prompt:
                        y = _rope(y, cos_ref[rows, :], sin_ref[rows, :])
                    if kind == "kb" and prompt:
                        kbp_ref[rows, head_cols(k)] = y
                    qkv_ref[rows, head_cols(k)] = y.astype(BF16)

    for p, kind in enumerate(PROJ_KINDS[:N_QKV]):
        if kind == "qb":
            pl.when(j == p)(functools.partial(run, kind, None))
        else:
            pl.when((j == p) & is_prompt)(functools.partial(run, kind, True))
            pl.when((j == p) & jnp.logical_not(is_prompt))(functools.partial(run, kind, False))
    pl.when(j >= N_QKV)(functools.partial(run, "gate", None))


def _proj_call(h, w_in, nqa, nqb, nkb, rope):
    assert PROJ_KINDS[:N_QKV] == ("qa", "qb", "kb", "vb") and TN == 2 * W_SUB and A_Q == B_QKV == TN
    n_panels = len(PROJ_KINDS)
    p_tiles = M_PROMPT // TM
    s_tiles = DEC_SEQ // TM
    n_tiles = M_ALL // TM
    kb_panel, vb_panel = PROJ_KINDS.index("kb"), PROJ_KINDS.index("vb")

    def w_sub(j):
        idx = PROJ_COLS[0] // W_SUB
        for p in range(1, n_panels):
            idx = jnp.where(j == p, PROJ_COLS[p] // W_SUB, idx)
        return idx

    def qkv_idx(j, i):
        done = j >= N_QKV
        return (jnp.minimum(j, N_QKV - 1), jnp.where(done, n_tiles - 1, i), 0)

    def gate_idx(j, i):
        return (jnp.maximum(j - N_QKV, 0), jnp.where(j < N_QKV, 0, i), 0)

    def cache_idx(panel):
        def idx(j, i):
            return (jnp.where(j < panel, 0, jnp.where(j > panel, p_tiles - 1, jnp.minimum(i, p_tiles - 1))), 0)
        return idx

    norm_spec = pl.BlockSpec((1, HEAD_DIM), lambda j, i: (0, 0))
    rope_spec = pl.BlockSpec((TM, HEAD_DIM), lambda j, i: (jnp.maximum(i - p_tiles, 0) % s_tiles, 0))
    return pl.pallas_call(
        _proj_kernel,
        grid=(n_panels, n_tiles),
        in_specs=[pl.BlockSpec((TM, D_MODEL), lambda j, i: (i, 0)),
                  pl.BlockSpec((D_MODEL, W_SUB), lambda j, i: (0, w_sub(j))),
                  pl.BlockSpec((D_MODEL, W_SUB), lambda j, i: (0, w_sub(j) + 1)),
                  norm_spec, norm_spec, norm_spec, rope_spec, rope_spec],
        out_specs=[pl.BlockSpec((None, TM, TN), qkv_idx),
                   pl.BlockSpec((None, TM, TN), gate_idx),
                   pl.BlockSpec((TM, TN), cache_idx(kb_panel)),
                   pl.BlockSpec((TM, TN), cache_idx(vb_panel))],
        out_shape=[jax.ShapeDtypeStruct((N_QKV, M_ALL, TN), BF16),
                   jax.ShapeDtypeStruct((2 * D_MODEL // TN, M_ALL, TN), F32),
                   jax.ShapeDtypeStruct((M_PROMPT, B_QKV), F32),
                   jax.ShapeDtypeStruct((M_PROMPT, B_QKV), F32)],
        scratch_shapes=[pltpu.VMEM((D_MODEL, TN), BF16)],
        compiler_params=_params(("arbitrary", "arbitrary"), VMEM_LIMIT),
        name="proj",
    )(h, w_in, w_in, nqa, nqb, nkb, *rope)


LOG2E = 1.4426950408889634
QK_LOG2 = SCALE * LOG2E


def _softmax_pv(parts, sink):
    m = functools.reduce(jnp.maximum, [jnp.max(t, axis=-1, keepdims=True) for t, _ in parts])
    if sink is not None:
        m = jnp.maximum(m, sink)
    acc = None
    for t, v in parts:
        p = jnp.exp2(t - m).astype(BF16)
        y = _dot(p, jnp.concatenate([v, jnp.ones_like(v)], axis=1))
        acc = y if acc is None else acc + y
    o, l = acc[:, :HEAD_DIM], acc[:, HEAD_DIM:]
    if sink is not None:
        l = l + jnp.exp2(sink - m)
    return o / l


def _sink_column(sink_ref, hk, rows_per_head):
    n = A_GROUP * rows_per_head
    g = lax.broadcasted_iota(jnp.int32, (n, 1), 0) // rows_per_head
    col = jnp.full((n, 1), sink_ref[hk * A_GROUP], F32)
    for k in range(1, A_GROUP):
        col = jnp.where(g == k, sink_ref[hk * A_GROUP + k], col)
    return col * LOG2E


def _stack_group(q_ref, hk):
    return jnp.concatenate(
        [q_ref[:, (hk * A_GROUP + g) * HEAD_DIM:(hk * A_GROUP + g + 1) * HEAD_DIM] for g in range(A_GROUP)], axis=0)


def _unstack_group(o_ref, hk, o, rows):
    for g in range(A_GROUP):
        c = (hk * A_GROUP + g) * HEAD_DIM
        o_ref[:, c:c + HEAD_DIM] = o[g * rows:(g + 1) * rows].astype(BF16)


def _ctx_attn_kernel(sink_ref, qkv_ref, kva_ref, oa_ref, ob_ref):
    qa_ref, qb_ref, kb_ref, vb_ref = (qkv_ref.at[PROJ_KINDS.index(kind)] for kind in ("qa", "qb", "kb", "vb"))
    for hk in range(A_KV_HEADS):
        k = kva_ref[:, hk * HEAD_DIM:(hk + 1) * HEAD_DIM]
        v = kva_ref[:, A_KV + hk * HEAD_DIM:A_KV + (hk + 1) * HEAD_DIM]
        q4 = _stack_group(qa_ref, hk)
        t = _dot_nt(q4, k) * QK_LOG2
        o = _softmax_pv([(t, v)], _sink_column(sink_ref, hk, SEQ))
        _unstack_group(oa_ref, hk, o, SEQ)
    for h in range(B_HEADS):
        sl = slice(h * HEAD_DIM, (h + 1) * HEAD_DIM)
        t = _dot_nt(qb_ref[:, sl], kb_ref[:, sl]) * QK_LOG2
        ob_ref[:, sl] = _softmax_pv([(t, vb_ref[:, sl])], None).astype(BF16)


def _ctx_attn_call(sink, qkv, kva):
    row = lambda w: pl.BlockSpec((SEQ, w), lambda b: (b, 0))
    return pl.pallas_call(
        _ctx_attn_kernel,
        grid=(BATCH,),
        in_specs=[pl.BlockSpec(memory_space=pltpu.SMEM),
                  pl.BlockSpec((N_QKV, SEQ, TN), lambda b: (0, b, 0)), row(2 * A_KV)],
        out_specs=[row(A_Q), row(B_QKV)],
        out_shape=[jax.ShapeDtypeStruct((M_PROMPT, A_Q), BF16),
                   jax.ShapeDtypeStruct((M_PROMPT, B_QKV), BF16)],
        compiler_params=_params(("arbitrary",), VMEM_LIMIT),
        name="attn_ctx",
    )(sink, qkv, kva)


BAND = 3 * BLOCK


def _cache_to_bf16(n_heads, ck_ref, cv_ref, ckb, cvb):
    for h in range(n_heads):
        sl = slice(h * HEAD_DIM, (h + 1) * HEAD_DIM)
        ckb[:, sl] = ck_ref[:, h, :].astype(BF16)
        cvb[:, sl] = cv_ref[:, h, :].astype(BF16)


def _win_attn_kernel(sink_ref, q_ref, kv_ref, ck_ref, cv_ref, o_ref, ckb, cvb):
    n = pl.program_id(1)
    pl.when(n == 0)(functools.partial(_cache_to_bf16, A_KV_HEADS, ck_ref, cv_ref, ckb, cvb))
    start = pl.multiple_of(jnp.clip((n - 1) * BLOCK, 0, DEC_SEQ - BAND), BLOCK)
    qpos = n * BLOCK + lax.broadcasted_iota(jnp.int32, (BLOCK, BAND), 0)
    kpos = start + lax.broadcasted_iota(jnp.int32, (BLOCK, BAND), 1)
    valid = jnp.abs(qpos - kpos) <= A_WINDOW
    valid = jnp.concatenate([valid.astype(jnp.int32)] * A_GROUP, axis=0) > 0
    for hk in range(A_KV_HEADS):
        sl = slice(hk * HEAD_DIM, (hk + 1) * HEAD_DIM)
        slv = slice(A_KV + hk * HEAD_DIM, A_KV + (hk + 1) * HEAD_DIM)
        k_loc = kv_ref[pl.ds(start, BAND), sl]
        v_loc = kv_ref[pl.ds(start, BAND), slv]
        q4 = _stack_group(q_ref, hk)
        t_loc = jnp.where(valid, _dot_nt(q4, k_loc) * QK_LOG2, NEG)
        t_ctx = _dot_nt(q4, ckb[:, sl]) * QK_LOG2
        o = _softmax_pv([(t_loc, v_loc), (t_ctx, cvb[:, sl])], _sink_column(sink_ref, hk, BLOCK))
        _unstack_group(o_ref, hk, o, BLOCK)


def _win_attn_call(sink, qkv, kva, cache_k, cache_v):
    nb = DEC_SEQ // BLOCK
    q0 = M_PROMPT // BLOCK
    b0 = M_PROMPT // DEC_SEQ
    return pl.pallas_call(
        _win_attn_kernel,
        grid=(DEC_BATCH, nb),
        in_specs=[pl.BlockSpec(memory_space=pltpu.SMEM),
                  pl.BlockSpec((None, BLOCK, A_Q), lambda b, n: (PROJ_KINDS.index("qa"), q0 + b * nb + n, 0)),
                  pl.BlockSpec((DEC_SEQ, 2 * A_KV), lambda b, n: (b0 + b, 0)),
                  pl.BlockSpec((None, None, PAST_LEN, A_KV_HEADS, HEAD_DIM), lambda b, n: (b, 0, 0, 0, 0)),
                  pl.BlockSpec((None, None, PAST_LEN, A_KV_HEADS, HEAD_DIM), lambda b, n: (b, 0, 0, 0, 0))],
        out_specs=pl.BlockSpec((BLOCK, A_Q), lambda b, n: (b * nb + n, 0)),
        out_shape=jax.ShapeDtypeStruct((M_SAMPLE, A_Q), BF16),
        scratch_shapes=[pltpu.VMEM((PAST_LEN, A_KV), BF16), pltpu.VMEM((PAST_LEN, A_KV), BF16)],
        compiler_params=_params(("arbitrary", "arbitrary"), VMEM_LIMIT),
        name="attn_window",
    )(sink, qkv, kva, cache_k, cache_v)


NA_QROWS = 4
NA_WIN_ROWS = 12
NA_Q = NA_QROWS * GRID_W
NA_KEYS = NA_WIN_ROWS * GRID_W
NA_DR = 2 * NA_ROWS_MAX - 1
NA_PAIRS = NA_DR + 1
RPB_W = 2 * NA_COLS - 1
RPB_PAD = GRID_W - NA_COLS


def _na_row_start(r):
    return jnp.clip(r - NA_ROWS // 2, 0, GRID_ROWS - NA_ROWS)


def _na_window_row0(blk):
    return jnp.clip(blk * NA_QROWS - NA_ROWS // 2, 0, GRID_ROWS - NA_WIN_ROWS)


def _check_na_windows():
    for blk in range(GRID_ROWS // NA_QROWS):
        w0 = int(np.clip(blk * NA_QROWS - NA_ROWS // 2, 0, GRID_ROWS - NA_WIN_ROWS))
        for r in range(blk * NA_QROWS, (blk + 1) * NA_QROWS):
            r0 = int(np.clip(r - NA_ROWS // 2, 0, GRID_ROWS - NA_ROWS))
            assert w0 <= r0 and r0 + NA_ROWS <= w0 + NA_WIN_ROWS, (blk, r)


_check_na_windows()


def _na_attn_kernel(q_ref, k_ref, v_ref, ck_ref, cv_ref, rpb_ref, o_ref, tab_ref, ckb, cvb):
    b, blk = pl.program_id(0), pl.program_id(1)
    lane = lax.broadcasted_iota(jnp.int32, (GRID_W, 2 * GRID_W), 1)
    pl.when(blk == 0)(functools.partial(_cache_to_bf16, B_HEADS, ck_ref, cv_ref, ckb, cvb))

    @pl.when((b == 0) & (blk == 0))
    def _():
        for h in range(B_HEADS):
            for d in range(NA_PAIRS):
                lo = jnp.broadcast_to(rpb_ref[h, d:d + 1, :], (GRID_W, 2 * GRID_W))
                hi = jnp.broadcast_to(rpb_ref[h, d + 1:d + 2, :], (GRID_W, 2 * GRID_W))
                lo = pltpu.roll(lo, GRID_W + 1, 1, stride=1, stride_axis=0)
                hi = pltpu.roll(hi, 1, 1, stride=1, stride_axis=0)
                tab_ref[h, d] = jnp.where(lane < GRID_W, lo, hi) * LOG2E

    row0 = _na_window_row0(blk)
    k0 = pl.multiple_of(row0 * GRID_W, GRID_W)
    qi = lax.broadcasted_iota(jnp.int32, (NA_Q, NA_KEYS), 0)
    ki = lax.broadcasted_iota(jnp.int32, (NA_Q, NA_KEYS), 1)
    qrow, qcol = blk * NA_QROWS + qi // GRID_W, qi % GRID_W
    krow, kcol = row0 + ki // GRID_W, ki % GRID_W
    rstart = _na_row_start(qrow)
    cstart = jnp.clip(qcol - NA_COLS // 2, 0, GRID_W - NA_COLS)
    valid = (krow >= rstart) & (krow < rstart + NA_ROWS) & (kcol >= cstart) & (kcol < cstart + NA_COLS)

    for h in range(B_HEADS):
        sl = slice(h * HEAD_DIM, (h + 1) * HEAD_DIM)
        bias = jnp.concatenate(
            [jnp.concatenate(
                [tab_ref[h, jnp.clip(row0 + 2 * p - blk * NA_QROWS - q + NA_ROWS_MAX, 0, NA_PAIRS - 1)]
                 for p in range(NA_WIN_ROWS // 2)], axis=1)
             for q in range(NA_QROWS)], axis=0)
        k_loc = k_ref[pl.ds(k0, NA_KEYS), sl]
        v_loc = v_ref[pl.ds(k0, NA_KEYS), sl]
        t_loc = jnp.where(valid, _dot_nt(q_ref[:, sl], k_loc) * QK_LOG2 + bias, NEG)
        t_ctx = _dot_nt(q_ref[:, sl], ckb[:, sl]) * QK_LOG2
        o = _softmax_pv([(t_loc, v_loc), (t_ctx, cvb[:, sl])], None)
        o_ref[:, sl] = o.astype(BF16)


def _na_bias_rows(rpb):
    rows = jnp.pad(rpb, ((0, 0), (1, 1), (0, 0)))
    left = jnp.broadcast_to(rows[..., :1], rows.shape[:2] + (RPB_PAD,))
    right = jnp.broadcast_to(rows[..., -1:], rows.shape[:2] + (2 * GRID_W - RPB_PAD - RPB_W,))
    return jnp.concatenate([left, rows, right], axis=-1)


def _na_attn_call(qkv, cache_k, cache_v, rpb):
    nblk = GRID_ROWS // NA_QROWS
    q0 = M_PROMPT // NA_Q
    b0 = M_PROMPT // DEC_SEQ
    return pl.pallas_call(
        _na_attn_kernel,
        grid=(DEC_BATCH, nblk),
        in_specs=[pl.BlockSpec((None, NA_Q, B_QKV), lambda b, r: (PROJ_KINDS.index("qb"), q0 + b * nblk + r, 0)),
                  pl.BlockSpec((None, DEC_SEQ, B_QKV), lambda b, r: (PROJ_KINDS.index("kb"), b0 + b, 0)),
                  pl.BlockSpec((None, DEC_SEQ, B_QKV), lambda b, r: (PROJ_KINDS.index("vb"), b0 + b, 0)),
                  pl.BlockSpec((None, None, PAST_LEN, B_HEADS, HEAD_DIM), lambda b, r: (b, 0, 0, 0, 0)),
                  pl.BlockSpec((None, None, PAST_LEN, B_HEADS, HEAD_DIM), lambda b, r: (b, 0, 0, 0, 0)),
                  pl.BlockSpec((B_HEADS, NA_DR + 2, 2 * GRID_W), lambda b, r: (0, 0, 0))],
        out_specs=pl.BlockSpec((NA_Q, B_QKV), lambda b, r: (b * nblk + r, 0)),
        out_shape=jax.ShapeDtypeStruct((M_SAMPLE, B_QKV), BF16),
        scratch_shapes=[pltpu.VMEM((B_HEADS, NA_PAIRS, GRID_W, 2 * GRID_W), F32),
                        pltpu.VMEM((PAST_LEN, B_QKV), BF16), pltpu.VMEM((PAST_LEN, B_QKV), BF16)],
        compiler_params=_params(("arbitrary", "arbitrary"), VMEM_LIMIT),
        name="attn_neighbourhood",
    )(qkv, qkv, qkv, cache_k, cache_v, _na_bias_rows(rpb))


def _merge_kernel(oap, oas, obp, obs, wa_ref, wb_ref, ga_ref, gb_ref, z_ref, wa_bf, wb_bf):
    i = pl.program_id(1)
    p_tiles = M_PROMPT // TM

    @pl.when(i == 0)
    def _():
        wa_bf[...] = wa_ref[...].astype(BF16)
        wb_bf[...] = wb_ref[...].astype(BF16)

    def run(oa_ref, ob_ref):
        for r0 in range(0, TM, ROW_CHUNK):
            rows = slice(r0, r0 + ROW_CHUNK)
            ya = _dot(oa_ref[rows, :], wa_bf[...])
            yb = _dot(ob_ref[rows, :], wb_bf[...])
            z_ref[rows, :] = (ga_ref[rows, :] * ya + gb_ref[rows, :] * yb).astype(BF16)

    pl.when(i < p_tiles)(functools.partial(run, oap, obp))
    pl.when(i >= p_tiles)(functools.partial(run, oas, obs))


def _merge_call(oa_p, oa_s, ob_p, ob_s, w_br_a, w_br_b, gates):
    p_tiles = M_PROMPT // TM
    nj = D_MODEL // TN
    pspec = pl.BlockSpec((TM, A_Q), lambda j, i: (jnp.minimum(i, p_tiles - 1), 0))
    sspec = pl.BlockSpec((TM, A_Q), lambda j, i: (jnp.maximum(i - p_tiles, 0), 0))
    once = dict(pipeline_mode=pl.Buffered(1))
    return pl.pallas_call(
        _merge_kernel,
        grid=(nj, M_ALL // TM),
        in_specs=[pspec, sspec, pspec, sspec,
                  pl.BlockSpec((A_Q, TN), lambda j, i: (0, j), **once),
                  pl.BlockSpec((B_QKV, TN), lambda j, i: (0, j), **once),
                  pl.BlockSpec((None, TM, TN), lambda j, i: (j, i, 0)),
                  pl.BlockSpec((None, TM, TN), lambda j, i: (nj + j, i, 0))],
        out_specs=pl.BlockSpec((TM, TN), lambda j, i: (i, j)),
        out_shape=jax.ShapeDtypeStruct((M_ALL, D_MODEL), BF16),
        scratch_shapes=[pltpu.VMEM((A_Q, TN), BF16), pltpu.VMEM((B_QKV, TN), BF16)],
        compiler_params=_params(("arbitrary", "arbitrary"), VMEM_LIMIT),
        name="merge_branches",
    )(oa_p, oa_s, ob_p, ob_s, w_br_a, w_br_b, gates, gates)


def _out_kernel(z_ref, w_ref, xp_ref, xs_ref, g_ref, o_ref, wbf):
    i = pl.program_id(1)
    p_tiles = M_PROMPT // TM

    @pl.when(i == 0)
    def _():
        wbf[...] = w_ref[...].astype(BF16)

    def run(x_ref):
        for r0 in range(0, TM, ROW_CHUNK):
            rows = slice(r0, r0 + ROW_CHUNK)
            o_ref[rows, :] = x_ref[rows, :] + g_ref[...] * _dot(z_ref[rows, :], wbf[...])

    pl.when(i < p_tiles)(functools.partial(run, xp_ref))
    pl.when(i >= p_tiles)(functools.partial(run, xs_ref))


def _out_call(z, w_out, xp, xs, mods3):
    p_tiles = M_PROMPT // TM
    nj = D_MODEL // TN
    return pl.pallas_call(
        _out_kernel,
        grid=(nj, M_ALL // TM),
        in_specs=[pl.BlockSpec((TM, D_MODEL), lambda j, i: (i, 0)),
                  pl.BlockSpec((D_MODEL, TN), lambda j, i: (0, j)),
                  pl.BlockSpec((TM, TN), lambda j, i: (jnp.minimum(i, p_tiles - 1), j)),
                  pl.BlockSpec((TM, TN), lambda j, i: (jnp.maximum(i - p_tiles, 0), j)),
                  pl.BlockSpec((None, 1, TN), lambda j, i: (_mod_row(i, TM), 0, 2 * nj + j))],
        out_specs=pl.BlockSpec((TM, TN), lambda j, i: (i, j)),
        out_shape=jax.ShapeDtypeStruct((M_ALL, D_MODEL), F32),
        scratch_shapes=[pltpu.VMEM((D_MODEL, TN), BF16)],
        compiler_params=_params(("arbitrary", "arbitrary"), VMEM_LIMIT),
        name="out_proj_residual",
    )(z, w_out, xp, xs, mods3)


def _mlp_kernel(x_hbm, nw_ref, sh_ref, sc_ref, g_ref, wu_ref, wd_ref, o_ref, h_ref, x_buf, x_sem, *, tile0):
    i, f = pl.program_id(0), pl.program_id(1)

    def x_copy(tile):
        return pltpu.make_async_copy(x_hbm.at[pl.ds((tile0 + tile) * TM, TM), :], x_buf, x_sem)

    @pl.when((i == 0) & (f == 0))
    def _():
        x_copy(0).start()

    @pl.when(f == 0)
    def _():
        x_copy(i).wait()
        wu = wu_ref[...].astype(BF16)
        wd = wd_ref[...].astype(BF16)
        for r0 in range(0, TM, ROW_CHUNK):
            rows = slice(r0, r0 + ROW_CHUNK)
            x = x_buf[rows, :]
            h = _modnorm(x, nw_ref[...], sc_ref[...], sh_ref[...]).astype(BF16)
            h_ref[rows, :] = h
            u = jnp.square(jnp.maximum(_dot(h, wu), 0.0)).astype(BF16)
            o_ref[rows, :] = x + g_ref[...] * _dot(u, wd)

    @pl.when((f == 1) & (i + 1 < pl.num_programs(0)))
    def _():
        x_copy(i + 1).start()

    @pl.when(f > 0)
    def _():
        u = _dot(h_ref[...], wu_ref[...].astype(BF16))
        u = jnp.square(jnp.maximum(u, 0.0)).astype(BF16)
        for c0 in range(0, D_MODEL, MLP_COLS):
            cols = slice(c0, c0 + MLP_COLS)
            o_ref[:, cols] += g_ref[:, cols] * _dot(u, wd_ref[:, cols].astype(BF16))


def _mlp_call(x1, nw, mods3, w_up, w_down, tile0, n_tiles):
    mod = lambda k: pl.BlockSpec((None, 1, D_MODEL), lambda i, f: (_mod_row(tile0 + i, TM), 0, k))
    return pl.pallas_call(
        functools.partial(_mlp_kernel, tile0=tile0),
        grid=(n_tiles, D_FF // TF),
        in_specs=[pl.BlockSpec(memory_space=pl.ANY),
                  pl.BlockSpec((1, D_MODEL), lambda i, f: (0, 0)),
                  mod(3), mod(4), mod(5),
                  pl.BlockSpec((D_MODEL, TF), lambda i, f: (0, f)),
                  pl.BlockSpec((TF, D_MODEL), lambda i, f: (f, 0))],
        out_specs=pl.BlockSpec((TM, D_MODEL), lambda i, f: (i, 0)),
        out_shape=jax.ShapeDtypeStruct((n_tiles * TM, D_MODEL), F32),
        scratch_shapes=[pltpu.VMEM((TM, D_MODEL), BF16),
                        pltpu.VMEM((TM, D_MODEL), F32),
                        pltpu.SemaphoreType.DMA(())],
        compiler_params=_params(("arbitrary", "arbitrary"), VMEM_LIMIT),
        name="mlp",
    )(x1, nw, mods3, mods3, mods3, w_up, w_down)


def kernel(x_prompt, x_sample, cache_a_k, cache_a_v, cache_b_k, cache_b_v, c, c_ctx, norm1_w, norm2_w, w_ada, b_ada, w_in, q_norm_a, k_norm_a, q_norm_b, k_norm_b, sink_a, rpb_b, w_br_a, w_br_b, w_out, w_up, w_down):
    assert w_ada.shape[0] == 1, "one trunk layer"
    xp = x_prompt.reshape(M_PROMPT, D_MODEL)
    xs = x_sample.reshape(M_SAMPLE, D_MODEL)

    cvecs = jnp.concatenate([c_ctx[None, :], c, jnp.zeros((8 - 1 - DEC_BATCH, D_MODEL), F32)], axis=0)
    mods3 = _ada_call(cvecs, w_ada[0], b_ada)

    w = w_in[0]
    rope = _rope_tables()
    h, kva, ka_p, va_p = _norm_kva_call(xp, xs, norm1_w, mods3, w, k_norm_a, rope)
    qkv, gates, kb_p, vb_p = _proj_call(h, w, q_norm_a, q_norm_b, k_norm_b, rope)

    sink = sink_a[0]
    oa_p, ob_p = _ctx_attn_call(sink, qkv, kva)
    oa_s = _win_attn_call(sink, qkv, kva, cache_a_k, cache_a_v)
    ob_s = _na_attn_call(qkv, cache_b_k, cache_b_v, rpb_b[0])

    z = _merge_call(oa_p, oa_s, ob_p, ob_s, w_br_a[0], w_br_b[0], gates)
    x1 = _out_call(z, w_out[0], xp, xs, mods3)

    p_tiles = M_PROMPT // TM
    y_p = _mlp_call(x1, norm2_w, mods3, w_up[0], w_down[0], 0, p_tiles)
    y_s = _mlp_call(x1, norm2_w, mods3, w_up[0], w_down[0], p_tiles, M_SAMPLE // TM)

    return (y_p.reshape(BATCH, SEQ, D_MODEL),
            y_s.reshape(DEC_BATCH, DEC_SEQ, D_MODEL),
            ka_p.reshape(BATCH, 1, SEQ, A_KV_HEADS, HEAD_DIM),
            va_p.reshape(BATCH, 1, SEQ, A_KV_HEADS, HEAD_DIM),
            kb_p.reshape(BATCH, 1, SEQ, B_HEADS, HEAD_DIM),
            vb_p.reshape(BATCH, 1, SEQ, B_HEADS, HEAD_DIM))
```

```python
import functools

import numpy as np
import jax
import jax.numpy as jnp
from jax import lax
from jax.experimental import pallas as pl
from jax.experimental.pallas import tpu as pltpu

D_MODEL = 2048
BATCH = 16
SEQ = 256
DEC_BATCH = 2
DEC_SEQ = 1024
PAST_LEN = 256
GRID_W = 64
HEAD_DIM = 128
A_HEADS = 8
A_KV_HEADS = 2
A_GROUP = A_HEADS // A_KV_HEADS
A_WINDOW = 128
BLOCK = 128
B_HEADS = 8
NA_ROWS_MAX = 8
NA_COLS = 16
D_FF = 4 * D_MODEL
ROPE_THETA = 10000.0
EPS = 1e-6
NEG = -1e30
A_Q = A_HEADS * HEAD_DIM
A_KV = A_KV_HEADS * HEAD_DIM
B_QKV = B_HEADS * HEAD_DIM
IN_WIDTH = A_Q + 2 * A_KV + 3 * B_QKV + 2 * D_MODEL
SCALE = HEAD_DIM ** -0.5

M_PROMPT = BATCH * SEQ
M_SAMPLE = DEC_BATCH * DEC_SEQ
M_ALL = M_PROMPT + M_SAMPLE
GRID_ROWS = DEC_SEQ // GRID_W
NA_ROWS = min(NA_ROWS_MAX, GRID_ROWS)

COL_QA = 0
COL_KVA = A_Q
COL_QB = A_Q + 2 * A_KV
COL_KB = COL_QB + B_QKV
COL_VB = COL_KB + B_QKV
COL_G = COL_VB + B_QKV

TM = 1024
TN = 1024
ROW_CHUNK = 256
W_SUB = 512
TM_NORM = 512
TF = 512
MLP_COLS = 512
VMEM_LIMIT = 60 * 1024 * 1024

F32 = jnp.float32
BF16 = jnp.bfloat16


def _mod_row(i, tm):
    p_tiles = M_PROMPT // tm
    return jnp.where(i < p_tiles, 0, 1 + (i - p_tiles) // (DEC_SEQ // tm))


def _dot(a, b):
    return jnp.dot(a, b, preferred_element_type=F32)


def _dot_nt(a, b):
    return lax.dot_general(a, b, (((1,), (1,)), ((), ())), preferred_element_type=F32)


def _params(sem, vmem=None):
    return pltpu.CompilerParams(dimension_semantics=sem, vmem_limit_bytes=vmem)


def _ada_kernel(c_ref, w_ref, b_ref, o_ref):
    cv = c_ref[...]
    s = (cv * jax.nn.sigmoid(cv)).astype(BF16)
    o_ref[:, 0, :] = _dot(s, w_ref[...].astype(BF16)) + b_ref[...]


def _ada_call(cvecs, w_ada, b_ada):
    tn = 1024
    n = 6 * D_MODEL
    return pl.pallas_call(
        _ada_kernel,
        grid=(n // tn,),
        in_specs=[pl.BlockSpec((8, D_MODEL), lambda j: (0, 0)),
                  pl.BlockSpec((D_MODEL, tn), lambda j: (0, j)),
                  pl.BlockSpec((1, tn), lambda j: (0, j))],
        out_specs=pl.BlockSpec((8, 1, tn), lambda j: (0, 0, j)),
        out_shape=jax.ShapeDtypeStruct((8, 1, n), F32),
        compiler_params=_params(("arbitrary",), VMEM_LIMIT),
        name="ada_mod",
    )(cvecs, w_ada, b_ada)


def _modnorm(x, nw, sc, sh):
    y = x * lax.rsqrt(jnp.mean(x * x, axis=-1, keepdims=True) + EPS)
    return y * (nw * (1.0 + sc)) + sh


def _head_norm(x, nw):
    return x * lax.rsqrt(jnp.mean(x * x, axis=-1, keepdims=True) + EPS) * nw


def _rope(x, cos, sin_signed):
    lane = lax.broadcasted_iota(jnp.int32, x.shape, 1)
    partner = jnp.where((lane % 64) < 32, pltpu.roll(x, 96, 1), pltpu.roll(x, 32, 1))
    return x * cos + partner * sin_signed


def _norm_kva_kernel(xp_ref, xs_ref, nw1_ref, sh_ref, sc_ref, w_ref, nwk_ref, cos_ref, sin_ref,
                     h_ref, kv_ref, kp_ref, vp_ref, wbf):
    i = pl.program_id(0)
    p_tiles = M_PROMPT // TM_NORM

    @pl.when(i == 0)
    def _():
        wbf[...] = w_ref[...].astype(BF16)

    def head_cols(k, base=0):
        return slice(base + k * HEAD_DIM, base + (k + 1) * HEAD_DIM)

    def run(prompt):
        x_ref = xp_ref if prompt else xs_ref
        for r0 in range(0, TM_NORM, ROW_CHUNK):
            rows = slice(r0, r0 + ROW_CHUNK)
            h = _modnorm(x_ref[rows, :], nw1_ref[...], sc_ref[...], sh_ref[...]).astype(BF16)
            h_ref[rows, :] = h
            acc = _dot(h, wbf[...])
            for k in range(A_KV_HEADS):
                y = _head_norm(acc[:, head_cols(k)], nwk_ref[...])
                v = acc[:, head_cols(k, A_KV)]
                if prompt:
                    kp_ref[rows, head_cols(k)] = y
                    vp_ref[rows, head_cols(k)] = v
                else:
                    y = _rope(y, cos_ref[rows, :], sin_ref[rows, :])
                kv_ref[rows, head_cols(k)] = y.astype(BF16)
                kv_ref[rows, head_cols(k, A_KV)] = v.astype(BF16)

    pl.when(i < p_tiles)(functools.partial(run, True))
    pl.when(i >= p_tiles)(functools.partial(run, False))


def _norm_kva_call(xp, xs, nw1, mods3, w_in, nwk, rope):
    tm = TM_NORM
    p_tiles = M_PROMPT // tm
    s_tiles = DEC_SEQ // tm
    assert COL_KVA % (2 * A_KV) == 0
    rope_spec = pl.BlockSpec((tm, HEAD_DIM), lambda i: (jnp.maximum(i - p_tiles, 0) % s_tiles, 0))
    parked = pl.BlockSpec((tm, A_KV), lambda i: (jnp.minimum(i, p_tiles - 1), 0))
    return pl.pallas_call(
        _norm_kva_kernel,
        grid=(M_ALL // tm,),
        in_specs=[pl.BlockSpec((tm, D_MODEL), lambda i: (jnp.minimum(i, p_tiles - 1), 0)),
                  pl.BlockSpec((tm, D_MODEL), lambda i: (jnp.maximum(i - p_tiles, 0), 0)),
                  pl.BlockSpec((1, D_MODEL), lambda i: (0, 0)),
                  pl.BlockSpec((None, 1, D_MODEL), lambda i: (_mod_row(i, tm), 0, 0)),
                  pl.BlockSpec((None, 1, D_MODEL), lambda i: (_mod_row(i, tm), 0, 1)),
                  pl.BlockSpec((D_MODEL, 2 * A_KV), lambda i: (0, COL_KVA // (2 * A_KV)),
                               pipeline_mode=pl.Buffered(1)),
                  pl.BlockSpec((1, HEAD_DIM), lambda i: (0, 0)),
                  rope_spec, rope_spec],
        out_specs=[pl.BlockSpec((tm, D_MODEL), lambda i: (i, 0)),
                   pl.BlockSpec((tm, 2 * A_KV), lambda i: (i, 0)),
                   parked, parked],
        out_shape=[jax.ShapeDtypeStruct((M_ALL, D_MODEL), BF16),
                   jax.ShapeDtypeStruct((M_ALL, 2 * A_KV), BF16),
                   jax.ShapeDtypeStruct((M_PROMPT, A_KV), F32),
                   jax.ShapeDtypeStruct((M_PROMPT, A_KV), F32)],
        scratch_shapes=[pltpu.VMEM((D_MODEL, 2 * A_KV), BF16)],
        compiler_params=_params(("arbitrary",), VMEM_LIMIT),
        name="norm1_proj_kva",
    )(xp, xs, nw1, mods3, mods3, w_in, nwk, *rope)


def _rope_tables():
    n_freq = HEAD_DIM // 4
    pos = np.arange(DEC_SEQ)
    row = (pos // GRID_W).astype(np.float64)
    col = (pos % GRID_W).astype(np.float64)
    inv = ROPE_THETA ** (-np.arange(n_freq, dtype=np.float64) / n_freq)
    ar = row[:, None] * inv
    ac = col[:, None] * inv
    cos = np.concatenate([np.cos(ar), np.cos(ar), np.cos(ac), np.cos(ac)], axis=-1)
    sin = np.concatenate([-np.sin(ar), np.sin(ar), -np.sin(ac), np.sin(ac)], axis=-1)
    return jnp.asarray(cos, F32), jnp.asarray(sin, F32)


PROJ_KINDS = ("qa", "qb", "kb", "vb") + ("gate",) * (2 * D_MODEL // TN)
PROJ_COLS = (COL_QA, COL_QB, COL_KB, COL_VB) + tuple(range(COL_G, IN_WIDTH, TN))
N_QKV = 4


def _proj_kernel(h_ref, w0_ref, w1_ref, nqa_ref, nqb_ref, nkb_ref, cos_ref, sin_ref,
                 qkv_ref, gate_ref, kbp_ref, vbp_ref, wbf):
    j, i = pl.program_id(0), pl.program_id(1)
    is_prompt = i < M_PROMPT // TM

    @pl.when(i == 0)
    def _():
        wbf[:, :W_SUB] = w0_ref[...].astype(BF16)
        wbf[:, W_SUB:] = w1_ref[...].astype(BF16)

    def head_cols(k):
        return slice(k * HEAD_DIM, (k + 1) * HEAD_DIM)

    def run(kind, prompt):
        for r0 in range(0, TM, ROW_CHUNK):
            rows = slice(r0, r0 + ROW_CHUNK)
            acc = _dot(h_ref[rows, :], wbf[...])
            if kind == "gate":
                gate_ref[rows, :] = jax.nn.sigmoid(acc)
            elif kind == "vb":
                if prompt:
                    vbp_ref[rows, :] = acc
                qkv_ref[rows, :] = acc.astype(BF16)
            else:
                nw_ref = {"qa": nqa_ref, "qb": nqb_ref, "kb": nkb_ref}[kind]
                for k in range(TN // HEAD_DIM):
                    y = _head_norm(acc[:, head_cols(k)], nw_ref[...])
                    if kind == "qa" and not prompt:
                        y = _rope(y, cos_ref[rows, :], sin_ref[rows, :])
                    if kind == "kb" and prompt:
                        kbp_ref[rows, head_cols(k)] = y
                    qkv_ref[rows, head_cols(k)] = y.astype(BF16)

    for p, kind in enumerate(PROJ_KINDS[:N_QKV]):
        if kind == "qb":
            pl.when(j == p)(functools.partial(run, kind, None))
        else:
            pl.when((j == p) & is_prompt)(functools.partial(run, kind, True))
            pl.when((j == p) & jnp.logical_not(is_prompt))(functools.partial(run, kind, False))
    pl.when(j >= N_QKV)(functools.partial(run, "gate", None))


def _proj_call(h, w_in, nqa, nqb, nkb, rope):
    assert PROJ_KINDS[:N_QKV] == ("qa", "qb", "kb", "vb") and TN == 2 * W_SUB and A_Q == B_QKV == TN
    n_panels = len(PROJ_KINDS)
    p_tiles = M_PROMPT // TM
    s_tiles = DEC_SEQ // TM
    n_tiles = M_ALL // TM
    kb_panel, vb_panel = PROJ_KINDS.index("kb"), PROJ_KINDS.index("vb")

    def w_sub(j):
        idx = PROJ_COLS[0] // W_SUB
        for p in range(1, n_panels):
            idx = jnp.where(j == p, PROJ_COLS[p] // W_SUB, idx)
        return idx

    def qkv_idx(j, i):
        done = j >= N_QKV
        return (jnp.minimum(j, N_QKV - 1), jnp.where(done, n_tiles - 1, i), 0)

    def gate_idx(j, i):
        return (jnp.maximum(j - N_QKV, 0), jnp.where(j < N_QKV, 0, i), 0)

    def cache_idx(panel):
        def idx(j, i):
            return (jnp.where(j < panel, 0, jnp.where(j > panel, p_tiles - 1, jnp.minimum(i, p_tiles - 1))), 0)
        return idx

    norm_spec = pl.BlockSpec((1, HEAD_DIM), lambda j, i: (0, 0))
    rope_spec = pl.BlockSpec((TM, HEAD_DIM), lambda j, i: (jnp.maximum(i - p_tiles, 0) % s_tiles, 0))
    return pl.pallas_call(
        _proj_kernel,
        grid=(n_panels, n_tiles),
        in_specs=[pl.BlockSpec((TM, D_MODEL), lambda j, i: (i, 0)),
                  pl.BlockSpec((D_MODEL, W_SUB), lambda j, i: (0, w_sub(j))),
                  pl.BlockSpec((D_MODEL, W_SUB), lambda j, i: (0, w_sub(j) + 1)),
                  norm_spec, norm_spec, norm_spec, rope_spec, rope_spec],
        out_specs=[pl.BlockSpec((None, TM, TN), qkv_idx),
                   pl.BlockSpec((None, TM, TN), gate_idx),
                   pl.BlockSpec((TM, TN), cache_idx(kb_panel)),
                   pl.BlockSpec((TM, TN), cache_idx(vb_panel))],
        out_shape=[jax.ShapeDtypeStruct((N_QKV, M_ALL, TN), BF16),
                   jax.ShapeDtypeStruct((2 * D_MODEL // TN, M_ALL, TN), F32),
                   jax.ShapeDtypeStruct((M_PROMPT, B_QKV), F32),
                   jax.ShapeDtypeStruct((M_PROMPT, B_QKV), F32)],
        scratch_shapes=[pltpu.VMEM((D_MODEL, TN), BF16)],
        compiler_params=_params(("arbitrary", "arbitrary"), VMEM_LIMIT),
        name="proj",
    )(h, w_in, w_in, nqa, nqb, nkb, *rope)


LOG2E = 1.4426950408889634
QK_LOG2 = SCALE * LOG2E


def _softmax_pv(parts, sink):
    m = functools.reduce(jnp.maximum, [jnp.max(t, axis=-1, keepdims=True) for t, _ in parts])
    if sink is not None:
        m = jnp.maximum(m, sink)
    acc = None
    for t, v in parts:
        p = jnp.exp2(t - m).astype(BF16)
        y = _dot(p, jnp.concatenate([v, jnp.ones_like(v)], axis=1))
        acc = y if acc is None else acc + y
    o, l = acc[:, :HEAD_DIM], acc[:, HEAD_DIM:]
    if sink is not None:
        l = l + jnp.exp2(sink - m)
    return o / l


def _sink_column(sink_ref, hk, rows_per_head):
    n = A_GROUP * rows_per_head
    g = lax.broadcasted_iota(jnp.int32, (n, 1), 0) // rows_per_head
    col = jnp.full((n, 1), sink_ref[hk * A_GROUP], F32)
    for k in range(1, A_GROUP):
        col = jnp.where(g == k, sink_ref[hk * A_GROUP + k], col)
    return col * LOG2E


def _stack_group(q_ref, hk):
    return jnp.concatenate(
        [q_ref[:, (hk * A_GROUP + g) * HEAD_DIM:(hk * A_GROUP + g + 1) * HEAD_DIM] for g in range(A_GROUP)], axis=0)


def _unstack_group(o_ref, hk, o, rows):
    for g in range(A_GROUP):
        c = (hk * A_GROUP + g) * HEAD_DIM
        o_ref[:, c:c + HEAD_DIM] = o[g * rows:(g + 1) * rows].astype(BF16)


def _ctx_attn_kernel(sink_ref, qkv_ref, kva_ref, oa_ref, ob_ref):
    qa_ref, qb_ref, kb_ref, vb_ref = (qkv_ref.at[PROJ_KINDS.index(kind)] for kind in ("qa", "qb", "kb", "vb"))
    for hk in range(A_KV_HEADS):
        k = kva_ref[:, hk * HEAD_DIM:(hk + 1) * HEAD_DIM]
        v = kva_ref[:, A_KV + hk * HEAD_DIM:A_KV + (hk + 1) * HEAD_DIM]
        q4 = _stack_group(qa_ref, hk)
        t = _dot_nt(q4, k) * QK_LOG2
        o = _softmax_pv([(t, v)], _sink_column(sink_ref, hk, SEQ))
        _unstack_group(oa_ref, hk, o, SEQ)
    for h in range(B_HEADS):
        sl = slice(h * HEAD_DIM, (h + 1) * HEAD_DIM)
        t = _dot_nt(qb_ref[:, sl], kb_ref[:, sl]) * QK_LOG2
        ob_ref[:, sl] = _softmax_pv([(t, vb_ref[:, sl])], None).astype(BF16)


def _ctx_attn_call(sink, qkv, kva):
    row = lambda w: pl.BlockSpec((SEQ, w), lambda b: (b, 0))
    return pl.pallas_call(
        _ctx_attn_kernel,
        grid=(BATCH,),
        in_specs=[pl.BlockSpec(memory_space=pltpu.SMEM),
                  pl.BlockSpec((N_QKV, SEQ, TN), lambda b: (0, b, 0)), row(2 * A_KV)],
        out_specs=[row(A_Q), row(B_QKV)],
        out_shape=[jax.ShapeDtypeStruct((M_PROMPT, A_Q), BF16),
                   jax.ShapeDtypeStruct((M_PROMPT, B_QKV), BF16)],
        compiler_params=_params(("arbitrary",), VMEM_LIMIT),
        name="attn_ctx",
    )(sink, qkv, kva)


BAND = 3 * BLOCK


def _cache_to_bf16(n_heads, ck_ref, cv_ref, ckb, cvb):
    for h in range(n_heads):
        sl = slice(h * HEAD_DIM, (h + 1) * HEAD_DIM)
        ckb[:, sl] = ck_ref[:, h, :].astype(BF16)
        cvb[:, sl] = cv_ref[:, h, :].astype(BF16)


def _win_attn_kernel(sink_ref, q_ref, kv_ref, ck_ref, cv_ref, o_ref, ckb, cvb):
    n = pl.program_id(1)
    pl.when(n == 0)(functools.partial(_cache_to_bf16, A_KV_HEADS, ck_ref, cv_ref, ckb, cvb))
    start = pl.multiple_of(jnp.clip((n - 1) * BLOCK, 0, DEC_SEQ - BAND), BLOCK)
    qpos = n * BLOCK + lax.broadcasted_iota(jnp.int32, (BLOCK, BAND), 0)
    kpos = start + lax.broadcasted_iota(jnp.int32, (BLOCK, BAND), 1)
    valid = jnp.abs(qpos - kpos) <= A_WINDOW
    valid = jnp.concatenate([valid.astype(jnp.int32)] * A_GROUP, axis=0) > 0
    for hk in range(A_KV_HEADS):
        sl = slice(hk * HEAD_DIM, (hk + 1) * HEAD_DIM)
        slv = slice(A_KV + hk * HEAD_DIM, A_KV + (hk + 1) * HEAD_DIM)
        k_loc = kv_ref[pl.ds(start, BAND), sl]
        v_loc = kv_ref[pl.ds(start, BAND), slv]
        q4 = _stack_group(q_ref, hk)
        t_loc = jnp.where(valid, _dot_nt(q4, k_loc) * QK_LOG2, NEG)
        t_ctx = _dot_nt(q4, ckb[:, sl]) * QK_LOG2
        o = _softmax_pv([(t_loc, v_loc), (t_ctx, cvb[:, sl])], _sink_column(sink_ref, hk, BLOCK))
        _unstack_group(o_ref, hk, o, BLOCK)


def _win_attn_call(sink, qkv, kva, cache_k, cache_v):
    nb = DEC_SEQ // BLOCK
    q0 = M_PROMPT // BLOCK
    b0 = M_PROMPT // DEC_SEQ
    return pl.pallas_call(
        _win_attn_kernel,
        grid=(DEC_BATCH, nb),
        in_specs=[pl.BlockSpec(memory_space=pltpu.SMEM),
                  pl.BlockSpec((None, BLOCK, A_Q), lambda b, n: (PROJ_KINDS.index("qa"), q0 + b * nb + n, 0)),
                  pl.BlockSpec((DEC_SEQ, 2 * A_KV), lambda b, n: (b0 + b, 0)),
                  pl.BlockSpec((None, None, PAST_LEN, A_KV_HEADS, HEAD_DIM), lambda b, n: (b, 0, 0, 0, 0)),
                  pl.BlockSpec((None, None, PAST_LEN, A_KV_HEADS, HEAD_DIM), lambda b, n: (b, 0, 0, 0, 0))],
        out_specs=pl.BlockSpec((BLOCK, A_Q), lambda b, n: (b * nb + n, 0)),
        out_shape=jax.ShapeDtypeStruct((M_SAMPLE, A_Q), BF16),
        scratch_shapes=[pltpu.VMEM((PAST_LEN, A_KV), BF16), pltpu.VMEM((PAST_LEN, A_KV), BF16)],
        compiler_params=_params(("arbitrary", "arbitrary"), VMEM_LIMIT),
        name="attn_window",
    )(sink, qkv, kva, cache_k, cache_v)


NA_QROWS = 4
NA_WIN_ROWS = 12
NA_Q = NA_QROWS * GRID_W
NA_KEYS = NA_WIN_ROWS * GRID_W
NA_DR = 2 * NA_ROWS_MAX - 1
NA_PAIRS = NA_DR + 1
RPB_W = 2 * NA_COLS - 1
RPB_PAD = GRID_W - NA_COLS


def _na_row_start(r):
    return jnp.clip(r - NA_ROWS // 2, 0, GRID_ROWS - NA_ROWS)


def _na_window_row0(blk):
    return jnp.clip(blk * NA_QROWS - NA_ROWS // 2, 0, GRID_ROWS - NA_WIN_ROWS)


def _check_na_windows():
    for blk in range(GRID_ROWS // NA_QROWS):
        w0 = int(np.clip(blk * NA_QROWS - NA_ROWS // 2, 0, GRID_ROWS - NA_WIN_ROWS))
        for r in range(blk * NA_QROWS, (blk + 1) * NA_QROWS):
            r0 = int(np.clip(r - NA_ROWS // 2, 0, GRID_ROWS - NA_ROWS))
            assert w0 <= r0 and r0 + NA_ROWS <= w0 + NA_WIN_ROWS, (blk, r)


_check_na_windows()


def _na_attn_kernel(q_ref, k_ref, v_ref, ck_ref, cv_ref, rpb_ref, o_ref, tab_ref, ckb, cvb):
    b, blk = pl.program_id(0), pl.program_id(1)
    lane = lax.broadcasted_iota(jnp.int32, (GRID_W, 2 * GRID_W), 1)
    pl.when(blk == 0)(functools.partial(_cache_to_bf16, B_HEADS, ck_ref, cv_ref, ckb, cvb))

    @pl.when((b == 0) & (blk == 0))
    def _():
        for h in range(B_HEADS):
            for d in range(NA_PAIRS):
                lo = jnp.broadcast_to(rpb_ref[h, d:d + 1, :], (GRID_W, 2 * GRID_W))
                hi = jnp.broadcast_to(rpb_ref[h, d + 1:d + 2, :], (GRID_W, 2 * GRID_W))
                lo = pltpu.roll(lo, GRID_W + 1, 1, stride=1, stride_axis=0)
                hi = pltpu.roll(hi, 1, 1, stride=1, stride_axis=0)
                tab_ref[h, d] = jnp.where(lane < GRID_W, lo, hi) * LOG2E

    row0 = _na_window_row0(blk)
    k0 = pl.multiple_of(row0 * GRID_W, GRID_W)
    qi = lax.broadcasted_iota(jnp.int32, (NA_Q, NA_KEYS), 0)
    ki = lax.broadcasted_iota(jnp.int32, (NA_Q, NA_KEYS), 1)
    qrow, qcol = blk * NA_QROWS + qi // GRID_W, qi % GRID_W
    krow, kcol = row0 + ki // GRID_W, ki % GRID_W
    rstart = _na_row_start(qrow)
    cstart = jnp.clip(qcol - NA_COLS // 2, 0, GRID_W - NA_COLS)
    valid = (krow >= rstart) & (krow < rstart + NA_ROWS) & (kcol >= cstart) & (kcol < cstart + NA_COLS)

    for h in range(B_HEADS):
        sl = slice(h * HEAD_DIM, (h + 1) * HEAD_DIM)
        bias = jnp.concatenate(
            [jnp.concatenate(
                [tab_ref[h, jnp.clip(row0 + 2 * p - blk * NA_QROWS - q + NA_ROWS_MAX, 0, NA_PAIRS - 1)]
                 for p in range(NA_WIN_ROWS // 2)], axis=1)
             for q in range(NA_QROWS)], axis=0)
        k_loc = k_ref[pl.ds(k0, NA_KEYS), sl]
        v_loc = v_ref[pl.ds(k0, NA_KEYS), sl]
        t_loc = jnp.where(valid, _dot_nt(q_ref[:, sl], k_loc) * QK_LOG2 + bias, NEG)
        t_ctx = _dot_nt(q_ref[:, sl], ckb[:, sl]) * QK_LOG2
        o = _softmax_pv([(t_loc, v_loc), (t_ctx, cvb[:, sl])], None)
        o_ref[:, sl] = o.astype(BF16)


def _na_bias_rows(rpb):
    rows = jnp.pad(rpb, ((0, 0), (1, 1), (0, 0)))
    left = jnp.broadcast_to(rows[..., :1], rows.shape[:2] + (RPB_PAD,))
    right = jnp.broadcast_to(rows[..., -1:], rows.shape[:2] + (2 * GRID_W - RPB_PAD - RPB_W,))
    return jnp.concatenate([left, rows, right], axis=-1)


def _na_attn_call(qkv, cache_k, cache_v, rpb):
    nblk = GRID_ROWS // NA_QROWS
    q0 = M_PROMPT // NA_Q
    b0 = M_PROMPT // DEC_SEQ
    return pl.pallas_call(
        _na_attn_kernel,
        grid=(DEC_BATCH, nblk),
        in_specs=[pl.BlockSpec((None, NA_Q, B_QKV), lambda b, r: (PROJ_KINDS.index("qb"), q0 + b * nblk + r, 0)),
                  pl.BlockSpec((None, DEC_SEQ, B_QKV), lambda b, r: (PROJ_KINDS.index("kb"), b0 + b, 0)),
                  pl.BlockSpec((None, DEC_SEQ, B_QKV), lambda b, r: (PROJ_KINDS.index("vb"), b0 + b, 0)),
                  pl.BlockSpec((None, None, PAST_LEN, B_HEADS, HEAD_DIM), lambda b, r: (b, 0, 0, 0, 0)),
                  pl.BlockSpec((None, None, PAST_LEN, B_HEADS, HEAD_DIM), lambda b, r: (b, 0, 0, 0, 0)),
                  pl.BlockSpec((B_HEADS, NA_DR + 2, 2 * GRID_W), lambda b, r: (0, 0, 0))],
        out_specs=pl.BlockSpec((NA_Q, B_QKV), lambda b, r: (b * nblk + r, 0)),
        out_shape=jax.ShapeDtypeStruct((M_SAMPLE, B_QKV), BF16),
        scratch_shapes=[pltpu.VMEM((B_HEADS, NA_PAIRS, GRID_W, 2 * GRID_W), F32),
                        pltpu.VMEM((PAST_LEN, B_QKV), BF16), pltpu.VMEM((PAST_LEN, B_QKV), BF16)],
        compiler_params=_params(("arbitrary", "arbitrary"), VMEM_LIMIT),
        name="attn_neighbourhood",
    )(qkv, qkv, qkv, cache_k, cache_v, _na_bias_rows(rpb))


def _merge_kernel(oap, oas, obp, obs, wa_ref, wb_ref, ga_ref, gb_ref, z_ref, wa_bf, wb_bf):
    i = pl.program_id(1)
    p_tiles = M_PROMPT // TM

    @pl.when(i == 0)
    def _():
        wa_bf[...] = wa_ref[...].astype(BF16)
        wb_bf[...] = wb_ref[...].astype(BF16)

    def run(oa_ref, ob_ref):
        for r0 in range(0, TM, ROW_CHUNK):
            rows = slice(r0, r0 + ROW_CHUNK)
            ya = _dot(oa_ref[rows, :], wa_bf[...])
            yb = _dot(ob_ref[rows, :], wb_bf[...])
            z_ref[rows, :] = (ga_ref[rows, :] * ya + gb_ref[rows, :] * yb).astype(BF16)

    pl.when(i < p_tiles)(functools.partial(run, oap, obp))
    pl.when(i >= p_tiles)(functools.partial(run, oas, obs))


def _merge_call(oa_p, oa_s, ob_p, ob_s, w_br_a, w_br_b, gates):
    p_tiles = M_PROMPT // TM
    nj = D_MODEL // TN
    pspec = pl.BlockSpec((TM, A_Q), lambda j, i: (jnp.minimum(i, p_tiles - 1), 0))
    sspec = pl.BlockSpec((TM, A_Q), lambda j, i: (jnp.maximum(i - p_tiles, 0), 0))
    once = dict(pipeline_mode=pl.Buffered(1))
    return pl.pallas_call(
        _merge_kernel,
        grid=(nj, M_ALL // TM),
        in_specs=[pspec, sspec, pspec, sspec,
                  pl.BlockSpec((A_Q, TN), lambda j, i: (0, j), **once),
                  pl.BlockSpec((B_QKV, TN), lambda j, i: (0, j), **once),
                  pl.BlockSpec((None, TM, TN), lambda j, i: (j, i, 0)),
                  pl.BlockSpec((None, TM, TN), lambda j, i: (nj + j, i, 0))],
        out_specs=pl.BlockSpec((TM, TN), lambda j, i: (i, j)),
        out_shape=jax.ShapeDtypeStruct((M_ALL, D_MODEL), BF16),
        scratch_shapes=[pltpu.VMEM((A_Q, TN), BF16), pltpu.VMEM((B_QKV, TN), BF16)],
        compiler_params=_params(("arbitrary", "arbitrary"), VMEM_LIMIT),
        name="merge_branches",
    )(oa_p, oa_s, ob_p, ob_s, w_br_a, w_br_b, gates, gates)


def _out_kernel(z_ref, w_ref, xp_ref, xs_ref, g_ref, o_ref, wbf):
    i = pl.program_id(1)
    p_tiles = M_PROMPT // TM

    @pl.when(i == 0)
    def _():
        wbf[...] = w_ref[...].astype(BF16)

    def run(x_ref):
        for r0 in range(0, TM, ROW_CHUNK):
            rows = slice(r0, r0 + ROW_CHUNK)
            o_ref[rows, :] = x_ref[rows, :] + g_ref[...] * _dot(z_ref[rows, :], wbf[...])

    pl.when(i < p_tiles)(functools.partial(run, xp_ref))
    pl.when(i >= p_tiles)(functools.partial(run, xs_ref))


def _out_call(z, w_out, xp, xs, mods3):
    p_tiles = M_PROMPT // TM
    nj = D_MODEL // TN
    return pl.pallas_call(
        _out_kernel,
        grid=(nj, M_ALL // TM),
        in_specs=[pl.BlockSpec((TM, D_MODEL), lambda j, i: (i, 0)),
                  pl.BlockSpec((D_MODEL, TN), lambda j, i: (0, j)),
                  pl.BlockSpec((TM, TN), lambda j, i: (jnp.minimum(i, p_tiles - 1), j)),
                  pl.BlockSpec((TM, TN), lambda j, i: (jnp.maximum(i - p_tiles, 0), j)),
                  pl.BlockSpec((None, 1, TN), lambda j, i: (_mod_row(i, TM), 0, 2 * nj + j))],
        out_specs=pl.BlockSpec((TM, TN), lambda j, i: (i, j)),
        out_shape=jax.ShapeDtypeStruct((M_ALL, D_MODEL), F32),
        scratch_shapes=[pltpu.VMEM((D_MODEL, TN), BF16)],
        compiler_params=_params(("arbitrary", "arbitrary"), VMEM_LIMIT),
        name="out_proj_residual",
    )(z, w_out, xp, xs, mods3)


def _mlp_kernel(x_hbm, nw_ref, sh_ref, sc_ref, g_ref, wu_ref, wd_ref, o_ref, h_ref, x_buf, x_sem, *, tile0):
    i, f = pl.program_id(0), pl.program_id(1)

    def x_copy(tile):
        return pltpu.make_async_copy(x_hbm.at[pl.ds((tile0 + tile) * TM, TM), :], x_buf, x_sem)

    @pl.when((i == 0) & (f == 0))
    def _():
        x_copy(0).start()

    @pl.when(f == 0)
    def _():
        x_copy(i).wait()
        wu = wu_ref[...].astype(BF16)
        wd = wd_ref[...].astype(BF16)
        for r0 in range(0, TM, ROW_CHUNK):
            rows = slice(r0, r0 + ROW_CHUNK)
            x = x_buf[rows, :]
            h = _modnorm(x, nw_ref[...], sc_ref[...], sh_ref[...]).astype(BF16)
            h_ref[rows, :] = h
            u = jnp.square(jnp.maximum(_dot(h, wu), 0.0)).astype(BF16)
            o_ref[rows, :] = x + g_ref[...] * _dot(u, wd)

    @pl.when((f == 1) & (i + 1 < pl.num_programs(0)))
    def _():
        x_copy(i + 1).start()

    @pl.when(f > 0)
    def _():
        u = _dot(h_ref[...], wu_ref[...].astype(BF16))
        u = jnp.square(jnp.maximum(u, 0.0)).astype(BF16)
        for c0 in range(0, D_MODEL, MLP_COLS):
            cols = slice(c0, c0 + MLP_COLS)
            o_ref[:, cols] += g_ref[:, cols] * _dot(u, wd_ref[:, cols].astype(BF16))


def _mlp_call(x1, nw, mods3, w_up, w_down, tile0, n_tiles):
    mod = lambda k: pl.BlockSpec((None, 1, D_MODEL), lambda i, f: (_mod_row(tile0 + i, TM), 0, k))
    return pl.pallas_call(
        functools.partial(_mlp_kernel, tile0=tile0),
        grid=(n_tiles, D_FF // TF),
        in_specs=[pl.BlockSpec(memory_space=pl.ANY),
                  pl.BlockSpec((1, D_MODEL), lambda i, f: (0, 0)),
                  mod(3), mod(4), mod(5),
                  pl.BlockSpec((D_MODEL, TF), lambda i, f: (0, f)),
                  pl.BlockSpec((TF, D_MODEL), lambda i, f: (f, 0))],
        out_specs=pl.BlockSpec((TM, D_MODEL), lambda i, f: (i, 0)),
        out_shape=jax.ShapeDtypeStruct((n_tiles * TM, D_MODEL), F32),
        scratch_shapes=[pltpu.VMEM((TM, D_MODEL), BF16),
                        pltpu.VMEM((TM, D_MODEL), F32),
                        pltpu.SemaphoreType.DMA(())],
        compiler_params=_params(("arbitrary", "arbitrary"), VMEM_LIMIT),
        name="mlp",
    )(x1, nw, mods3, mods3, mods3, w_up, w_down)


def kernel(x_prompt, x_sample, cache_a_k, cache_a_v, cache_b_k, cache_b_v, c, c_ctx, norm1_w, norm2_w, w_ada, b_ada, w_in, q_norm_a, k_norm_a, q_norm_b, k_norm_b, sink_a, rpb_b, w_br_a, w_br_b, w_out, w_up, w_down):
    assert w_ada.shape[0] == 1, "one trunk layer"
    xp = x_prompt.reshape(M_PROMPT, D_MODEL)
    xs = x_sample.reshape(M_SAMPLE, D_MODEL)

    cvecs = jnp.concatenate([c_ctx[None, :], c, jnp.zeros((8 - 1 - DEC_BATCH, D_MODEL), F32)], axis=0)
    mods3 = _ada_call(cvecs, w_ada[0], b_ada)

    w = w_in[0]
    rope = _rope_tables()
    h, kva, ka_p, va_p = _norm_kva_call(xp, xs, norm1_w, mods3, w, k_norm_a, rope)
    qkv, gates, kb_p, vb_p = _proj_call(h, w, q_norm_a, q_norm_b, k_norm_b, rope)

    sink = sink_a[0]
    oa_p, ob_p = _ctx_attn_call(sink, qkv, kva)
    oa_s = _win_attn_call(sink, qkv, kva, cache_a_k, cache_a_v)
    ob_s = _na_attn_call(qkv, cache_b_k, cache_b_v, rpb_b[0])

    z = _merge_call(oa_p, oa_s, ob_p, ob_s, w_br_a[0], w_br_b[0], gates)
    x1 = _out_call(z, w_out[0], xp, xs, mods3)

    p_tiles = M_PROMPT // TM
    y_p = _mlp_call(x1, norm2_w, mods3, w_up[0], w_down[0], 0, p_tiles)
    y_s = _mlp_call(x1, norm2_w, mods3, w_up[0], w_down[0], p_tiles, M_SAMPLE // TM)

    return (y_p.reshape(BATCH, SEQ, D_MODEL),
            y_s.reshape(DEC_BATCH, DEC_SEQ, D_MODEL),
            ka_p.reshape(BATCH, 1, SEQ, A_KV_HEADS, HEAD_DIM),
            va_p.reshape(BATCH, 1, SEQ, A_KV_HEADS, HEAD_DIM),
            kb_p.reshape(BATCH, 1, SEQ, B_HEADS, HEAD_DIM),
            vb_p.reshape(BATCH, 1, SEQ, B_HEADS, HEAD_DIM))
```

```python
import functools

import numpy as np
import jax
import jax.numpy as jnp
from jax import lax
from jax.experimental import pallas as pl
from jax.experimental.pallas import tpu as pltpu

D_MODEL = 2048
BATCH = 16
SEQ = 256
DEC_BATCH = 2
DEC_SEQ = 1024
PAST_LEN = 256
GRID_W = 64
HEAD_DIM = 128
A_HEADS = 8
A_KV_HEADS = 2
A_GROUP = A_HEADS // A_KV_HEADS
A_WINDOW = 128
BLOCK = 128
B_HEADS = 8
NA_ROWS_MAX = 8
NA_COLS = 16
D_FF = 4 * D_MODEL
ROPE_THETA = 10000.0
EPS = 1e-6
NEG = -1e30
A_Q = A_HEADS * HEAD_DIM
A_KV = A_KV_HEADS * HEAD_DIM
B_QKV = B_HEADS * HEAD_DIM
IN_WIDTH = A_Q + 2 * A_KV + 3 * B_QKV + 2 * D_MODEL
SCALE = HEAD_DIM ** -0.5

M_PROMPT = BATCH * SEQ
M_SAMPLE = DEC_BATCH * DEC_SEQ
M_ALL = M_PROMPT + M_SAMPLE
GRID_ROWS = DEC_SEQ // GRID_W
NA_ROWS = min(NA_ROWS_MAX, GRID_ROWS)

COL_QA = 0
COL_KVA = A_Q
COL_QB = A_Q + 2 * A_KV
COL_KB = COL_QB + B_QKV
COL_VB = COL_KB + B_QKV
COL_G = COL_VB + B_QKV

TM = 1024
TN = 1024
ROW_CHUNK = 256
W_SUB = 512
TM_NORM = 512
TF = 512
MLP_COLS = 512
VMEM_LIMIT = 60 * 1024 * 1024

F32 = jnp.float32
BF16 = jnp.bfloat16


def _mod_row(i, tm):
    p_tiles = M_PROMPT // tm
    return jnp.where(i < p_tiles, 0, 1 + (i - p_tiles) // (DEC_SEQ // tm))


def _dot(a, b):
    return jnp.dot(a, b, preferred_element_type=F32)


def _dot_nt(a, b):
    return lax.dot_general(a, b, (((1,), (1,)), ((), ())), preferred_element_type=F32)


def _params(sem, vmem=None):
    return pltpu.CompilerParams(dimension_semantics=sem, vmem_limit_bytes=vmem)


def _ada_kernel(c_ref, w_ref, b_ref, o_ref):
    cv = c_ref[...]
    s = (cv * jax.nn.sigmoid(cv)).astype(BF16)
    o_ref[...] = _dot(s, w_ref[...].astype(BF16)) + b_ref[...]


def _ada_call(cvecs, w_ada, b_ada):
    tn = 1024
    n = 6 * D_MODEL
    return pl.pallas_call(
        _ada_kernel,
        grid=(n // tn,),
        in_specs=[pl.BlockSpec((8, D_MODEL), lambda j: (0, 0)),
                  pl.BlockSpec((D_MODEL, tn), lambda j: (0, j)),
                  pl.BlockSpec((1, tn), lambda j: (0, j))],
        out_specs=pl.BlockSpec((8, tn), lambda j: (0, j)),
        out_shape=jax.ShapeDtypeStruct((8, n), F32),
        compiler_params=_params(("arbitrary",), VMEM_LIMIT),
        name="ada_mod",
    )(cvecs, w_ada, b_ada)


def _modnorm(x, nw, sc, sh):
    y = x * lax.rsqrt(jnp.mean(x * x, axis=-1, keepdims=True) + EPS)
    return y * (nw * (1.0 + sc)) + sh


def _head_norm(x, nw):
    return x * lax.rsqrt(jnp.mean(x * x, axis=-1, keepdims=True) + EPS) * nw


def _rope(x, cos, sin_signed):
    lane = lax.broadcasted_iota(jnp.int32, x.shape, 1)
    partner = jnp.where((lane % 64) < 32, pltpu.roll(x, 96, 1), pltpu.roll(x, 32, 1))
    return x * cos + partner * sin_signed


def _norm_kva_kernel(xp_ref, xs_ref, nw1_ref, sh_ref, sc_ref, w_ref, nwk_ref, cos_ref, sin_ref,
                     h_ref, kv_ref, kp_ref, vp_ref, wbf):
    i = pl.program_id(0)
    p_tiles = M_PROMPT // TM_NORM

    @pl.when(i == 0)
    def _():
        wbf[...] = w_ref[...].astype(BF16)

    def head_cols(k, base=0):
        return slice(base + k * HEAD_DIM, base + (k + 1) * HEAD_DIM)

    def run(prompt):
        x_ref = xp_ref if prompt else xs_ref
        for r0 in range(0, TM_NORM, ROW_CHUNK):
            rows = slice(r0, r0 + ROW_CHUNK)
            h = _modnorm(x_ref[rows, :], nw1_ref[...], sc_ref[...], sh_ref[...]).astype(BF16)
            h_ref[rows, :] = h
            acc = _dot(h, wbf[...])
            for k in range(A_KV_HEADS):
                y = _head_norm(acc[:, head_cols(k)], nwk_ref[...])
                v = acc[:, head_cols(k, A_KV)]
                if prompt:
                    kp_ref[rows, k, :] = y
                    vp_ref[rows, k, :] = v
                else:
                    y = _rope(y, cos_ref[rows, :], sin_ref[rows, :])
                kv_ref[rows, head_cols(k)] = y.astype(BF16)
                kv_ref[rows, head_cols(k, A_KV)] = v.astype(BF16)

    pl.when(i < p_tiles)(functools.partial(run, True))
    pl.when(i >= p_tiles)(functools.partial(run, False))


def _norm_kva_call(xp, xs, nw1, mods3, w_in, nwk, rope):
    tm = TM_NORM
    p_tiles = M_PROMPT // tm
    s_tiles = DEC_SEQ // tm
    assert COL_KVA % (2 * A_KV) == 0
    rope_spec = pl.BlockSpec((tm, HEAD_DIM), lambda i: (jnp.maximum(i - p_tiles, 0) % s_tiles, 0))
    parked = pl.BlockSpec((tm, A_KV_HEADS, HEAD_DIM), lambda i: (jnp.minimum(i, p_tiles - 1), 0, 0))
    return pl.pallas_call(
        _norm_kva_kernel,
        grid=(M_ALL // tm,),
        in_specs=[pl.BlockSpec((tm, D_MODEL), lambda i: (jnp.minimum(i, p_tiles - 1), 0)),
                  pl.BlockSpec((tm, D_MODEL), lambda i: (jnp.maximum(i - p_tiles, 0), 0)),
                  pl.BlockSpec((1, D_MODEL), lambda i: (0, 0)),
                  pl.BlockSpec((None, 1, D_MODEL), lambda i: (_mod_row(i, tm), 0, 0)),
                  pl.BlockSpec((None, 1, D_MODEL), lambda i: (_mod_row(i, tm), 0, 1)),
                  pl.BlockSpec((D_MODEL, 2 * A_KV), lambda i: (0, COL_KVA // (2 * A_KV)),
                               pipeline_mode=pl.Buffered(1)),
                  pl.BlockSpec((1, HEAD_DIM), lambda i: (0, 0)),
                  rope_spec, rope_spec],
        out_specs=[pl.BlockSpec((tm, D_MODEL), lambda i: (i, 0)),
                   pl.BlockSpec((tm, 2 * A_KV), lambda i: (i, 0)),
                   parked, parked],
        out_shape=[jax.ShapeDtypeStruct((M_ALL, D_MODEL), BF16),
                   jax.ShapeDtypeStruct((M_ALL, 2 * A_KV), BF16),
                   jax.ShapeDtypeStruct((M_PROMPT, A_KV_HEADS, HEAD_DIM), F32),
                   jax.ShapeDtypeStruct((M_PROMPT, A_KV_HEADS, HEAD_DIM), F32)],
        scratch_shapes=[pltpu.VMEM((D_MODEL, 2 * A_KV), BF16)],
        compiler_params=_params(("arbitrary",), VMEM_LIMIT),
        name="norm1_proj_kva",
    )(xp, xs, nw1, mods3, mods3, w_in, nwk, *rope)


def _rope_tables():
    n_freq = HEAD_DIM // 4
    pos = np.arange(DEC_SEQ)
    row = (pos // GRID_W).astype(np.float64)
    col = (pos % GRID_W).astype(np.float64)
    inv = ROPE_THETA ** (-np.arange(n_freq, dtype=np.float64) / n_freq)
    ar = row[:, None] * inv
    ac = col[:, None] * inv
    cos = np.concatenate([np.cos(ar), np.cos(ar), np.cos(ac), np.cos(ac)], axis=-1)
    sin = np.concatenate([-np.sin(ar), np.sin(ar), -np.sin(ac), np.sin(ac)], axis=-1)
    return jnp.asarray(cos, F32), jnp.asarray(sin, F32)


PROJ_KINDS = ("qa", "qb", "kb", "vb") + ("gate",) * (2 * D_MODEL // TN)
PROJ_COLS = (COL_QA, COL_QB, COL_KB, COL_VB) + tuple(range(COL_G, IN_WIDTH, TN))
N_QKV = 4


def _proj_kernel(h_ref, w0_ref, w1_ref, nqa_ref, nqb_ref, nkb_ref, cos_ref, sin_ref,
                 qkv_ref, gate_ref, kbp_ref, vbp_ref, wbf):
    j, i = pl.program_id(0), pl.program_id(1)
    is_prompt = i < M_PROMPT // TM

    @pl.when(i == 0)
    def _():
        wbf[:, :W_SUB] = w0_ref[...].astype(BF16)
        wbf[:, W_SUB:] = w1_ref[...].astype(BF16)

    def head_cols(k):
        return slice(k * HEAD_DIM, (k + 1) * HEAD_DIM)

    def run(kind, prompt):
        for r0 in range(0, TM, ROW_CHUNK):
            rows = slice(r0, r0 + ROW_CHUNK)
            acc = _dot(h_ref[rows, :], wbf[...])
            if kind == "gate":
                gate_ref[rows, :] = jax.nn.sigmoid(acc)
            elif kind == "vb":
                if prompt:
                    vbp_ref[rows, :] = acc
                qkv_ref[rows, :] = acc.astype(BF16)
            else:
                nw_ref = {"qa": nqa_ref, "qb": nqb_ref, "kb": nkb_ref}[kind]
                for k in range(TN // HEAD_DIM):
                    y = _head_norm(acc[:, head_cols(k)], nw_ref[...])
                    if kind == "qa" and not prompt:
                        y = _rope(y, cos_ref[rows, :], sin_ref[rows, :])
                    if kind == "kb" and prompt:
                        kbp_ref[rows, head_cols(k)] = y
                    qkv_ref[rows, head_cols(k)] = y.astype(BF16)

    for p, kind in enumerate(PROJ_KINDS[:N_QKV]):
        if kind == "qb":
            pl.when(j == p)(functools.partial(run, kind, None))
        else:
            pl.when((j == p) & is_prompt)(functools.partial(run, kind, True))
            pl.when((j == p) & jnp.logical_not(is_prompt))(functools.partial(run, kind, False))
    pl.when(j >= N_QKV)(functools.partial(run, "gate", None))


def _proj_call(h, w_in, nqa, nqb, nkb, rope):
    assert PROJ_KINDS[:N_QKV] == ("qa", "qb", "kb", "vb") and TN == 2 * W_SUB and A_Q == B_QKV == TN
    n_panels = len(PROJ_KINDS)
    p_tiles = M_PROMPT // TM
    s_tiles = DEC_SEQ // TM
    n_tiles = M_ALL // TM
    kb_panel, vb_panel = PROJ_KINDS.index("kb"), PROJ_KINDS.index("vb")

    def w_sub(j):
        idx = PROJ_COLS[0] // W_SUB
        for p in range(1, n_panels):
            idx = jnp.where(j == p, PROJ_COLS[p] // W_SUB, idx)
        return idx

    def qkv_idx(j, i):
        done = j >= N_QKV
        return (jnp.minimum(j, N_QKV - 1), jnp.where(done, n_tiles - 1, i), 0)

    def gate_idx(j, i):
        return (jnp.maximum(j - N_QKV, 0), jnp.where(j < N_QKV, 0, i), 0)

    def cache_idx(panel):
        def idx(j, i):
            return (jnp.where(j < panel, 0, jnp.where(j > panel, p_tiles - 1, jnp.minimum(i, p_tiles - 1))), 0)
        return idx

    norm_spec = pl.BlockSpec((1, HEAD_DIM), lambda j, i: (0, 0))
    rope_spec = pl.BlockSpec((TM, HEAD_DIM), lambda j, i: (jnp.maximum(i - p_tiles, 0) % s_tiles, 0))
    return pl.pallas_call(
        _proj_kernel,
        grid=(n_panels, n_tiles),
        in_specs=[pl.BlockSpec((TM, D_MODEL), lambda j, i: (i, 0)),
                  pl.BlockSpec((D_MODEL, W_SUB), lambda j, i: (0, w_sub(j))),
                  pl.BlockSpec((D_MODEL, W_SUB), lambda j, i: (0, w_sub(j) + 1)),
                  norm_spec, norm_spec, norm_spec, rope_spec, rope_spec],
        out_specs=[pl.BlockSpec((None, TM, TN), qkv_idx),
                   pl.BlockSpec((None, TM, TN), gate_idx),
                   pl.BlockSpec((TM, TN), cache_idx(kb_panel)),
                   pl.BlockSpec((TM, TN), cache_idx(vb_panel))],
        out_shape=[jax.ShapeDtypeStruct((N_QKV, M_ALL, TN), BF16),
                   jax.ShapeDtypeStruct((2 * D_MODEL // TN, M_ALL, TN), F32),
                   jax.ShapeDtypeStruct((M_PROMPT, B_QKV), F32),
                   jax.ShapeDtypeStruct((M_PROMPT, B_QKV), F32)],
        scratch_shapes=[pltpu.VMEM((D_MODEL, TN), BF16)],
        compiler_params=_params(("arbitrary", "arbitrary"), VMEM_LIMIT),
        name="proj",
    )(h, w_in, w_in, nqa, nqb, nkb, *rope)


LOG2E = 1.4426950408889634
QK_LOG2 = SCALE * LOG2E


def _softmax_pv(parts, sink):
    m = functools.reduce(jnp.maximum, [jnp.max(t, axis=-1, keepdims=True) for t, _ in parts])
    if sink is not None:
        m = jnp.maximum(m, sink)
    acc = None
    for t, v in parts:
        p = jnp.exp2(t - m).astype(BF16)
        y = _dot(p, jnp.concatenate([v, jnp.ones_like(v)], axis=1))
        acc = y if acc is None else acc + y
    o, l = acc[:, :HEAD_DIM], acc[:, HEAD_DIM:]
    if sink is not None:
        l = l + jnp.exp2(sink - m)
    return o / l


def _sink_column(sink_ref, hk, rows_per_head):
    n = A_GROUP * rows_per_head
    g = lax.broadcasted_iota(jnp.int32, (n, 1), 0) // rows_per_head
    col = jnp.full((n, 1), sink_ref[hk * A_GROUP], F32)
    for k in range(1, A_GROUP):
        col = jnp.where(g == k, sink_ref[hk * A_GROUP + k], col)
    return col * LOG2E


def _stack_group(q_ref, hk):
    return jnp.concatenate(
        [q_ref[:, (hk * A_GROUP + g) * HEAD_DIM:(hk * A_GROUP + g + 1) * HEAD_DIM] for g in range(A_GROUP)], axis=0)


def _unstack_group(o_ref, hk, o, rows):
    for g in range(A_GROUP):
        c = (hk * A_GROUP + g) * HEAD_DIM
        o_ref[:, c:c + HEAD_DIM] = o[g * rows:(g + 1) * rows].astype(BF16)


def _ctx_attn_kernel(sink_ref, qkv_ref, kva_ref, oa_ref, ob_ref):
    qa_ref, qb_ref, kb_ref, vb_ref = (qkv_ref.at[PROJ_KINDS.index(kind)] for kind in ("qa", "qb", "kb", "vb"))
    for hk in range(A_KV_HEADS):
        k = kva_ref[:, hk * HEAD_DIM:(hk + 1) * HEAD_DIM]
        v = kva_ref[:, A_KV + hk * HEAD_DIM:A_KV + (hk + 1) * HEAD_DIM]
        q4 = _stack_group(qa_ref, hk)
        t = _dot_nt(q4, k) * QK_LOG2
        o = _softmax_pv([(t, v)], _sink_column(sink_ref, hk, SEQ))
        _unstack_group(oa_ref, hk, o, SEQ)
    for h in range(B_HEADS):
        sl = slice(h * HEAD_DIM, (h + 1) * HEAD_DIM)
        t = _dot_nt(qb_ref[:, sl], kb_ref[:, sl]) * QK_LOG2
        ob_ref[:, sl] = _softmax_pv([(t, vb_ref[:, sl])], None).astype(BF16)


def _ctx_attn_call(sink, qkv, kva):
    row = lambda w: pl.BlockSpec((SEQ, w), lambda b: (b, 0))
    return pl.pallas_call(
        _ctx_attn_kernel,
        grid=(BATCH,),
        in_specs=[pl.BlockSpec(memory_space=pltpu.SMEM),
                  pl.BlockSpec((N_QKV, SEQ, TN), lambda b: (0, b, 0)), row(2 * A_KV)],
        out_specs=[row(A_Q), row(B_QKV)],
        out_shape=[jax.ShapeDtypeStruct((M_PROMPT, A_Q), BF16),
                   jax.ShapeDtypeStruct((M_PROMPT, B_QKV), BF16)],
        compiler_params=_params(("arbitrary",), VMEM_LIMIT),
        name="attn_ctx",
    )(sink, qkv, kva)


BAND = 3 * BLOCK


def _cache_to_bf16(n_heads, ck_ref, cv_ref, ckb, cvb):
    for h in range(n_heads):
        sl = slice(h * HEAD_DIM, (h + 1) * HEAD_DIM)
        ckb[:, sl] = ck_ref[:, h, :].astype(BF16)
        cvb[:, sl] = cv_ref[:, h, :].astype(BF16)


def _win_attn_kernel(sink_ref, q_ref, kv_ref, ck_ref, cv_ref, o_ref, ckb, cvb):
    n = pl.program_id(1)
    pl.when(n == 0)(functools.partial(_cache_to_bf16, A_KV_HEADS, ck_ref, cv_ref, ckb, cvb))
    start = pl.multiple_of(jnp.clip((n - 1) * BLOCK, 0, DEC_SEQ - BAND), BLOCK)
    qpos = n * BLOCK + lax.broadcasted_iota(jnp.int32, (BLOCK, BAND), 0)
    kpos = start + lax.broadcasted_iota(jnp.int32, (BLOCK, BAND), 1)
    valid = jnp.abs(qpos - kpos) <= A_WINDOW
    valid = jnp.concatenate([valid.astype(jnp.int32)] * A_GROUP, axis=0) > 0
    for hk in range(A_KV_HEADS):
        sl = slice(hk * HEAD_DIM, (hk + 1) * HEAD_DIM)
        slv = slice(A_KV + hk * HEAD_DIM, A_KV + (hk + 1) * HEAD_DIM)
        k_loc = kv_ref[pl.ds(start, BAND), sl]
        v_loc = kv_ref[pl.ds(start, BAND), slv]
        q4 = _stack_group(q_ref, hk)
        t_loc = jnp.where(valid, _dot_nt(q4, k_loc) * QK_LOG2, NEG)
        t_ctx = _dot_nt(q4, ckb[:, sl]) * QK_LOG2
        o = _softmax_pv([(t_loc, v_loc), (t_ctx, cvb[:, sl])], _sink_column(sink_ref, hk, BLOCK))
        _unstack_group(o_ref, hk, o, BLOCK)


def _win_attn_call(sink, qkv, kva, cache_k, cache_v):
    nb = DEC_SEQ // BLOCK
    q0 = M_PROMPT // BLOCK
    b0 = M_PROMPT // DEC_SEQ
    return pl.pallas_call(
        _win_attn_kernel,
        grid=(DEC_BATCH, nb),
        in_specs=[pl.BlockSpec(memory_space=pltpu.SMEM),
                  pl.BlockSpec((None, BLOCK, A_Q), lambda b, n: (PROJ_KINDS.index("qa"), q0 + b * nb + n, 0)),
                  pl.BlockSpec((DEC_SEQ, 2 * A_KV), lambda b, n: (b0 + b, 0)),
                  pl.BlockSpec((None, None, PAST_LEN, A_KV_HEADS, HEAD_DIM), lambda b, n: (b, 0, 0, 0, 0)),
                  pl.BlockSpec((None, None, PAST_LEN, A_KV_HEADS, HEAD_DIM), lambda b, n: (b, 0, 0, 0, 0))],
        out_specs=pl.BlockSpec((BLOCK, A_Q), lambda b, n: (b * nb + n, 0)),
        out_shape=jax.ShapeDtypeStruct((M_SAMPLE, A_Q), BF16),
        scratch_shapes=[pltpu.VMEM((PAST_LEN, A_KV), BF16), pltpu.VMEM((PAST_LEN, A_KV), BF16)],
        compiler_params=_params(("arbitrary", "arbitrary"), VMEM_LIMIT),
        name="attn_window",
    )(sink, qkv, kva, cache_k, cache_v)


NA_QROWS = 4
NA_WIN_ROWS = 12
NA_Q = NA_QROWS * GRID_W
NA_KEYS = NA_WIN_ROWS * GRID_W
NA_DR = 2 * NA_ROWS_MAX - 1
NA_PAIRS = NA_DR + 1
RPB_W = 2 * NA_COLS - 1
RPB_PAD = GRID_W - NA_COLS


def _na_row_start(r):
    return jnp.clip(r - NA_ROWS // 2, 0, GRID_ROWS - NA_ROWS)


def _na_window_row0(blk):
    return jnp.clip(blk * NA_QROWS - NA_ROWS // 2, 0, GRID_ROWS - NA_WIN_ROWS)


def _check_na_windows():
    for blk in range(GRID_ROWS // NA_QROWS):
        w0 = int(np.clip(blk * NA_QROWS - NA_ROWS // 2, 0, GRID_ROWS - NA_WIN_ROWS))
        for r in range(blk * NA_QROWS, (blk + 1) * NA_QROWS):
            r0 = int(np.clip(r - NA_ROWS // 2, 0, GRID_ROWS - NA_ROWS))
            assert w0 <= r0 and r0 + NA_ROWS <= w0 + NA_WIN_ROWS, (blk, r)


_check_na_windows()


def _na_attn_kernel(q_ref, k_ref, v_ref, ck_ref, cv_ref, rpb_ref, o_ref, tab_ref, ckb, cvb):
    b, blk = pl.program_id(0), pl.program_id(1)
    lane = lax.broadcasted_iota(jnp.int32, (GRID_W, 2 * GRID_W), 1)
    pl.when(blk == 0)(functools.partial(_cache_to_bf16, B_HEADS, ck_ref, cv_ref, ckb, cvb))

    @pl.when((b == 0) & (blk == 0))
    def _():
        for h in range(B_HEADS):
            for d in range(NA_PAIRS):
                lo = jnp.broadcast_to(rpb_ref[h, d:d + 1, :], (GRID_W, 2 * GRID_W))
                hi = jnp.broadcast_to(rpb_ref[h, d + 1:d + 2, :], (GRID_W, 2 * GRID_W))
                lo = pltpu.roll(lo, GRID_W + 1, 1, stride=1, stride_axis=0)
                hi = pltpu.roll(hi, 1, 1, stride=1, stride_axis=0)
                tab_ref[h, d] = jnp.where(lane < GRID_W, lo, hi) * LOG2E

    row0 = _na_window_row0(blk)
    k0 = pl.multiple_of(row0 * GRID_W, GRID_W)
    qi = lax.broadcasted_iota(jnp.int32, (NA_Q, NA_KEYS), 0)
    ki = lax.broadcasted_iota(jnp.int32, (NA_Q, NA_KEYS), 1)
    qrow, qcol = blk * NA_QROWS + qi // GRID_W, qi % GRID_W
    krow, kcol = row0 + ki // GRID_W, ki % GRID_W
    rstart = _na_row_start(qrow)
    cstart = jnp.clip(qcol - NA_COLS // 2, 0, GRID_W - NA_COLS)
    valid = (krow >= rstart) & (krow < rstart + NA_ROWS) & (kcol >= cstart) & (kcol < cstart + NA_COLS)

    for h in range(B_HEADS):
        sl = slice(h * HEAD_DIM, (h + 1) * HEAD_DIM)
        bias = jnp.concatenate(
            [jnp.concatenate(
                [tab_ref[h, jnp.clip(row0 + 2 * p - blk * NA_QROWS - q + NA_ROWS_MAX, 0, NA_PAIRS - 1)]
                 for p in range(NA_WIN_ROWS // 2)], axis=1)
             for q in range(NA_QROWS)], axis=0)
        k_loc = k_ref[pl.ds(k0, NA_KEYS), sl]
        v_loc = v_ref[pl.ds(k0, NA_KEYS), sl]
        t_loc = jnp.where(valid, _dot_nt(q_ref[:, sl], k_loc) * QK_LOG2 + bias, NEG)
        t_ctx = _dot_nt(q_ref[:, sl], ckb[:, sl]) * QK_LOG2
        o = _softmax_pv([(t_loc, v_loc), (t_ctx, cvb[:, sl])], None)
        o_ref[:, sl] = o.astype(BF16)


def _na_bias_rows(rpb):
    rows = jnp.pad(rpb, ((0, 0), (1, 1), (0, 0)))
    left = jnp.broadcast_to(rows[..., :1], rows.shape[:2] + (RPB_PAD,))
    right = jnp.broadcast_to(rows[..., -1:], rows.shape[:2] + (2 * GRID_W - RPB_PAD - RPB_W,))
    return jnp.concatenate([left, rows, right], axis=-1)


def _na_attn_call(qkv, cache_k, cache_v, rpb):
    nblk = GRID_ROWS // NA_QROWS
    q0 = M_PROMPT // NA_Q
    b0 = M_PROMPT // DEC_SEQ
    return pl.pallas_call(
        _na_attn_kernel,
        grid=(DEC_BATCH, nblk),
        in_specs=[pl.BlockSpec((None, NA_Q, B_QKV), lambda b, r: (PROJ_KINDS.index("qb"), q0 + b * nblk + r, 0)),
                  pl.BlockSpec((None, DEC_SEQ, B_QKV), lambda b, r: (PROJ_KINDS.index("kb"), b0 + b, 0)),
                  pl.BlockSpec((None, DEC_SEQ, B_QKV), lambda b, r: (PROJ_KINDS.index("vb"), b0 + b, 0)),
                  pl.BlockSpec((None, None, PAST_LEN, B_HEADS, HEAD_DIM), lambda b, r: (b, 0, 0, 0, 0)),
                  pl.BlockSpec((None, None, PAST_LEN, B_HEADS, HEAD_DIM), lambda b, r: (b, 0, 0, 0, 0)),
                  pl.BlockSpec((B_HEADS, NA_DR + 2, 2 * GRID_W), lambda b, r: (0, 0, 0))],
        out_specs=pl.BlockSpec((NA_Q, B_QKV), lambda b, r: (b * nblk + r, 0)),
        out_shape=jax.ShapeDtypeStruct((M_SAMPLE, B_QKV), BF16),
        scratch_shapes=[pltpu.VMEM((B_HEADS, NA_PAIRS, GRID_W, 2 * GRID_W), F32),
                        pltpu.VMEM((PAST_LEN, B_QKV), BF16), pltpu.VMEM((PAST_LEN, B_QKV), BF16)],
        compiler_params=_params(("arbitrary", "arbitrary"), VMEM_LIMIT),
        name="attn_neighbourhood",
    )(qkv, qkv, qkv, cache_k, cache_v, _na_bias_rows(rpb))


def _merge_kernel(oap, oas, obp, obs, wa_ref, wb_ref, ga_ref, gb_ref, z_ref, wa_bf, wb_bf):
    i = pl.program_id(1)
    p_tiles = M_PROMPT // TM

    @pl.when(i == 0)
    def _():
        wa_bf[...] = wa_ref[...].astype(BF16)
        wb_bf[...] = wb_ref[...].astype(BF16)

    def run(oa_ref, ob_ref):
        for r0 in range(0, TM, ROW_CHUNK):
            rows = slice(r0, r0 + ROW_CHUNK)
            ya = _dot(oa_ref[rows, :], wa_bf[...])
            yb = _dot(ob_ref[rows, :], wb_bf[...])
            z_ref[rows, :] = (ga_ref[rows, :] * ya + gb_ref[rows, :] * yb).astype(BF16)

    pl.when(i < p_tiles)(functools.partial(run, oap, obp))
    pl.when(i >= p_tiles)(functools.partial(run, oas, obs))


def _merge_call(oa_p, oa_s, ob_p, ob_s, w_br_a, w_br_b, gates):
    p_tiles = M_PROMPT // TM
    nj = D_MODEL // TN
    pspec = pl.BlockSpec((TM, A_Q), lambda j, i: (jnp.minimum(i, p_tiles - 1), 0))
    sspec = pl.BlockSpec((TM, A_Q), lambda j, i: (jnp.maximum(i - p_tiles, 0), 0))
    once = dict(pipeline_mode=pl.Buffered(1))
    return pl.pallas_call(
        _merge_kernel,
        grid=(nj, M_ALL // TM),
        in_specs=[pspec, sspec, pspec, sspec,
                  pl.BlockSpec((A_Q, TN), lambda j, i: (0, j), **once),
                  pl.BlockSpec((B_QKV, TN), lambda j, i: (0, j), **once),
                  pl.BlockSpec((None, TM, TN), lambda j, i: (j, i, 0)),
                  pl.BlockSpec((None, TM, TN), lambda j, i: (nj + j, i, 0))],
        out_specs=pl.BlockSpec((TM, TN), lambda j, i: (i, j)),
        out_shape=jax.ShapeDtypeStruct((M_ALL, D_MODEL), BF16),
        scratch_shapes=[pltpu.VMEM((A_Q, TN), BF16), pltpu.VMEM((B_QKV, TN), BF16)],
        compiler_params=_params(("arbitrary", "arbitrary"), VMEM_LIMIT),
        name="merge_branches",
    )(oa_p, oa_s, ob_p, ob_s, w_br_a, w_br_b, gates, gates)


def _out_kernel(z_ref, w_ref, xp_ref, xs_ref, g_ref, o_ref, wbf):
    i = pl.program_id(1)
    p_tiles = M_PROMPT // TM

    @pl.when(i == 0)
    def _():
        wbf[...] = w_ref[...].astype(BF16)

    def run(x_ref):
        for r0 in range(0, TM, ROW_CHUNK):
            rows = slice(r0, r0 + ROW_CHUNK)
            o_ref[rows, :] = x_ref[rows, :] + g_ref[...] * _dot(z_ref[rows, :], wbf[...])

    pl.when(i < p_tiles)(functools.partial(run, xp_ref))
    pl.when(i >= p_tiles)(functools.partial(run, xs_ref))


def _out_call(z, w_out, xp, xs, mods3):
    p_tiles = M_PROMPT // TM
    nj = D_MODEL // TN
    return pl.pallas_call(
        _out_kernel,
        grid=(nj, M_ALL // TM),
        in_specs=[pl.BlockSpec((TM, D_MODEL), lambda j, i: (i, 0)),
                  pl.BlockSpec((D_MODEL, TN), lambda j, i: (0, j)),
                  pl.BlockSpec((TM, TN), lambda j, i: (jnp.minimum(i, p_tiles - 1), j)),
                  pl.BlockSpec((TM, TN), lambda j, i: (jnp.maximum(i - p_tiles, 0), j)),
                  pl.BlockSpec((None, 1, TN), lambda j, i: (_mod_row(i, TM), 0, 2 * nj + j))],
        out_specs=pl.BlockSpec((TM, TN), lambda j, i: (i, j)),
        out_shape=jax.ShapeDtypeStruct((M_ALL, D_MODEL), F32),
        scratch_shapes=[pltpu.VMEM((D_MODEL, TN), BF16)],
        compiler_params=_params(("arbitrary", "arbitrary"), VMEM_LIMIT),
        name="out_proj_residual",
    )(z, w_out, xp, xs, mods3)


def _mlp_kernel(x_hbm, nw_ref, sh_ref, sc_ref, g_ref, wu_ref, wd_ref, o_ref, h_ref, x_buf, x_sem, *, tile0):
    i, f = pl.program_id(0), pl.program_id(1)

    def x_copy(tile):
        return pltpu.make_async_copy(x_hbm.at[pl.ds((tile0 + tile) * TM, TM), :], x_buf, x_sem)

    @pl.when((i == 0) & (f == 0))
    def _():
        x_copy(0).start()

    @pl.when(f == 0)
    def _():
        x_copy(i).wait()
        wu = wu_ref[...].astype(BF16)
        wd = wd_ref[...].astype(BF16)
        for r0 in range(0, TM, ROW_CHUNK):
            rows = slice(r0, r0 + ROW_CHUNK)
            x = x_buf[rows, :]
            h = _modnorm(x, nw_ref[...], sc_ref[...], sh_ref[...]).astype(BF16)
            h_ref[rows, :] = h
            u = jnp.square(jnp.maximum(_dot(h, wu), 0.0)).astype(BF16)
            o_ref[rows, :] = x + g_ref[...] * _dot(u, wd)

    @pl.when((f == 1) & (i + 1 < pl.num_programs(0)))
    def _():
        x_copy(i + 1).start()

    @pl.when(f > 0)
    def _():
        u = _dot(h_ref[...], wu_ref[...].astype(BF16))
        u = jnp.square(jnp.maximum(u, 0.0)).astype(BF16)
        for c0 in range(0, D_MODEL, MLP_COLS):
            cols = slice(c0, c0 + MLP_COLS)
            o_ref[:, cols] += g_ref[:, cols] * _dot(u, wd_ref[:, cols].astype(BF16))


def _mlp_call(x1, nw, mods3, w_up, w_down, tile0, n_tiles):
    mod = lambda k: pl.BlockSpec((None, 1, D_MODEL), lambda i, f: (_mod_row(tile0 + i, TM), 0, k))
    return pl.pallas_call(
        functools.partial(_mlp_kernel, tile0=tile0),
        grid=(n_tiles, D_FF // TF),
        in_specs=[pl.BlockSpec(memory_space=pl.ANY),
                  pl.BlockSpec((1, D_MODEL), lambda i, f: (0, 0)),
                  mod(3), mod(4), mod(5),
                  pl.BlockSpec((D_MODEL, TF), lambda i, f: (0, f)),
                  pl.BlockSpec((TF, D_MODEL), lambda i, f: (f, 0))],
        out_specs=pl.BlockSpec((TM, D_MODEL), lambda i, f: (i, 0)),
        out_shape=jax.ShapeDtypeStruct((n_tiles * TM, D_MODEL), F32),
        scratch_shapes=[pltpu.VMEM((TM, D_MODEL), BF16),
                        pltpu.VMEM((TM, D_MODEL), F32),
                        pltpu.SemaphoreType.DMA(())],
        compiler_params=_params(("arbitrary", "arbitrary"), VMEM_LIMIT),
        name="mlp",
    )(x1, nw, mods3, mods3, mods3, w_up, w_down)


def kernel(x_prompt, x_sample, cache_a_k, cache_a_v, cache_b_k, cache_b_v, c, c_ctx, norm1_w, norm2_w, w_ada, b_ada, w_in, q_norm_a, k_norm_a, q_norm_b, k_norm_b, sink_a, rpb_b, w_br_a, w_br_b, w_out, w_up, w_down):
    assert w_ada.shape[0] == 1, "one trunk layer"
    xp = x_prompt.reshape(M_PROMPT, D_MODEL)
    xs = x_sample.reshape(M_SAMPLE, D_MODEL)

    cvecs = jnp.concatenate([c_ctx[None, :], c, jnp.zeros((8 - 1 - DEC_BATCH, D_MODEL), F32)], axis=0)
    mods3 = _ada_call(cvecs, w_ada[0], b_ada).reshape(8, 1, 6 * D_MODEL)

    w = w_in[0]
    rope = _rope_tables()
    h, kva, ka_p, va_p = _norm_kva_call(xp, xs, norm1_w, mods3, w, k_norm_a, rope)
    qkv, gates, kb_p, vb_p = _proj_call(h, w, q_norm_a, q_norm_b, k_norm_b, rope)

    sink = sink_a[0]
    oa_p, ob_p = _ctx_attn_call(sink, qkv, kva)
    oa_s = _win_attn_call(sink, qkv, kva, cache_a_k, cache_a_v)
    ob_s = _na_attn_call(qkv, cache_b_k, cache_b_v, rpb_b[0])

    z = _merge_call(oa_p, oa_s, ob_p, ob_s, w_br_a[0], w_br_b[0], gates)
    x1 = _out_call(z, w_out[0], xp, xs, mods3)

    p_tiles = M_PROMPT // TM
    y_p = _mlp_call(x1, norm2_w, mods3, w_up[0], w_down[0], 0, p_tiles)
    y_s = _mlp_call(x1, norm2_w, mods3, w_up[0], w_down[0], p_tiles, M_SAMPLE // TM)

    return (y_p.reshape(BATCH, SEQ, D_MODEL),
            y_s.reshape(DEC_BATCH, DEC_SEQ, D_MODEL),
            ka_p.reshape(BATCH, 1, SEQ, A_KV_HEADS, HEAD_DIM),
            va_p.reshape(BATCH, 1, SEQ, A_KV_HEADS, HEAD_DIM),
            kb_p.reshape(BATCH, 1, SEQ, B_HEADS, HEAD_DIM),
            vb_p.reshape(BATCH, 1, SEQ, B_HEADS, HEAD_DIM))
```

```python
import functools

import numpy as np
import jax
import jax.numpy as jnp
from jax import lax
from jax.experimental import pallas as pl
from jax.experimental.pallas import tpu as pltpu

D_MODEL = 2048
BATCH = 16
SEQ = 256
DEC_BATCH = 2
DEC_SEQ = 1024
PAST_LEN = 256
GRID_W = 64
HEAD_DIM = 128
A_HEADS = 8
A_KV_HEADS = 2
A_GROUP = A_HEADS // A_KV_HEADS
A_WINDOW = 128
BLOCK = 128
B_HEADS = 8
NA_ROWS_MAX = 8
NA_COLS = 16
D_FF = 4 * D_MODEL
ROPE_THETA = 10000.0
EPS = 1e-6
NEG = -1e30
A_Q = A_HEADS * HEAD_DIM
A_KV = A_KV_HEADS * HEAD_DIM
B_QKV = B_HEADS * HEAD_DIM
IN_WIDTH = A_Q + 2 * A_KV + 3 * B_QKV + 2 * D_MODEL
SCALE = HEAD_DIM ** -0.5

M_PROMPT = BATCH * SEQ
M_SAMPLE = DEC_BATCH * DEC_SEQ
M_ALL = M_PROMPT + M_SAMPLE
GRID_ROWS = DEC_SEQ // GRID_W
NA_ROWS = min(NA_ROWS_MAX, GRID_ROWS)

COL_QA = 0
COL_KVA = A_Q
COL_QB = A_Q + 2 * A_KV
COL_KB = COL_QB + B_QKV
COL_VB = COL_KB + B_QKV
COL_G = COL_VB + B_QKV

TM = 1024
TN = 1024
ROW_CHUNK = 256
W_SUB = 512
TM_NORM = 512
TF = 512
MLP_COLS = 512
VMEM_LIMIT = 60 * 1024 * 1024

F32 = jnp.float32
BF16 = jnp.bfloat16


def _mod_row(i, tm):
    p_tiles = M_PROMPT // tm
    return jnp.where(i < p_tiles, 0, 1 + (i - p_tiles) // (DEC_SEQ // tm))


def _dot(a, b):
    return jnp.dot(a, b, preferred_element_type=F32)


def _dot_nt(a, b):
    return lax.dot_general(a, b, (((1,), (1,)), ((), ())), preferred_element_type=F32)


def _params(sem, vmem=None):
    return pltpu.CompilerParams(dimension_semantics=sem, vmem_limit_bytes=vmem)


def _ada_kernel(c_ref, w_ref, b_ref, o_ref):
    cv = c_ref[...]
    s = (cv * jax.nn.sigmoid(cv)).astype(BF16)
    o_ref[...] = _dot(s, w_ref[...].astype(BF16)) + b_ref[...]


def _ada_call(cvecs, w_ada, b_ada):
    tn = 1024
    n = 6 * D_MODEL
    return pl.pallas_call(
        _ada_kernel,
        grid=(n // tn,),
        in_specs=[pl.BlockSpec((8, D_MODEL), lambda j: (0, 0)),
                  pl.BlockSpec((D_MODEL, tn), lambda j: (0, j)),
                  pl.BlockSpec((1, tn), lambda j: (0, j))],
        out_specs=pl.BlockSpec((8, tn), lambda j: (0, j)),
        out_shape=jax.ShapeDtypeStruct((8, n), F32),
        compiler_params=_params(("arbitrary",), VMEM_LIMIT),
        name="ada_mod",
    )(cvecs, w_ada, b_ada)


def _modnorm(x, nw, sc, sh):
    y = x * lax.rsqrt(jnp.mean(x * x, axis=-1, keepdims=True) + EPS)
    return y * (nw * (1.0 + sc)) + sh


def _head_norm(x, nw):
    return x * lax.rsqrt(jnp.mean(x * x, axis=-1, keepdims=True) + EPS) * nw


def _rope(x, cos, sin_signed):
    lane = lax.broadcasted_iota(jnp.int32, x.shape, 1)
    partner = jnp.where((lane % 64) < 32, pltpu.roll(x, 96, 1), pltpu.roll(x, 32, 1))
    return x * cos + partner * sin_signed


def _norm_kva_kernel(xp_ref, xs_ref, nw1_ref, sh_ref, sc_ref, w_ref, nwk_ref, cos_ref, sin_ref,
                     h_ref, kv_ref, kp_ref, vp_ref, wbf):
    i = pl.program_id(0)
    p_tiles = M_PROMPT // TM_NORM

    @pl.when(i == 0)
    def _():
        wbf[...] = w_ref[...].astype(BF16)

    def head_cols(k, base=0):
        return slice(base + k * HEAD_DIM, base + (k + 1) * HEAD_DIM)

    def run(prompt):
        x_ref = xp_ref if prompt else xs_ref
        for r0 in range(0, TM_NORM, ROW_CHUNK):
            rows = slice(r0, r0 + ROW_CHUNK)
            h = _modnorm(x_ref[rows, :], nw1_ref[...], sc_ref[...], sh_ref[...]).astype(BF16)
            h_ref[rows, :] = h
            acc = _dot(h, wbf[...])
            for k in range(A_KV_HEADS):
                y = _head_norm(acc[:, head_cols(k)], nwk_ref[...])
                v = acc[:, head_cols(k, A_KV)]
                if prompt:
                    kp_ref[rows, k, :] = y
                    vp_ref[rows, k, :] = v
                else:
                    y = _rope(y, cos_ref[rows, :], sin_ref[rows, :])
                kv_ref[rows, head_cols(k)] = y.astype(BF16)
                kv_ref[rows, head_cols(k, A_KV)] = v.astype(BF16)

    pl.when(i < p_tiles)(functools.partial(run, True))
    pl.when(i >= p_tiles)(functools.partial(run, False))


def _norm_kva_call(xp, xs, nw1, mods3, w_in, nwk, rope):
    tm = TM_NORM
    p_tiles = M_PROMPT // tm
    s_tiles = DEC_SEQ // tm
    assert COL_KVA % (2 * A_KV) == 0
    rope_spec = pl.BlockSpec((tm, HEAD_DIM), lambda i: (jnp.maximum(i - p_tiles, 0) % s_tiles, 0))
    parked = pl.BlockSpec((tm, A_KV_HEADS, HEAD_DIM), lambda i: (jnp.minimum(i, p_tiles - 1), 0, 0))
    return pl.pallas_call(
        _norm_kva_kernel,
        grid=(M_ALL // tm,),
        in_specs=[pl.BlockSpec((tm, D_MODEL), lambda i: (jnp.minimum(i, p_tiles - 1), 0)),
                  pl.BlockSpec((tm, D_MODEL), lambda i: (jnp.maximum(i - p_tiles, 0), 0)),
                  pl.BlockSpec((1, D_MODEL), lambda i: (0, 0)),
                  pl.BlockSpec((None, 1, D_MODEL), lambda i: (_mod_row(i, tm), 0, 0)),
                  pl.BlockSpec((None, 1, D_MODEL), lambda i: (_mod_row(i, tm), 0, 1)),
                  pl.BlockSpec((D_MODEL, 2 * A_KV), lambda i: (0, COL_KVA // (2 * A_KV)),
                               pipeline_mode=pl.Buffered(1)),
                  pl.BlockSpec((1, HEAD_DIM), lambda i: (0, 0)),
                  rope_spec, rope_spec],
        out_specs=[pl.BlockSpec((tm, D_MODEL), lambda i: (i, 0)),
                   pl.BlockSpec((tm, 2 * A_KV), lambda i: (i, 0)),
                   parked, parked],
        out_shape=[jax.ShapeDtypeStruct((M_ALL, D_MODEL), BF16),
                   jax.ShapeDtypeStruct((M_ALL, 2 * A_KV), BF16),
                   jax.ShapeDtypeStruct((M_PROMPT, A_KV_HEADS, HEAD_DIM), F32),
                   jax.ShapeDtypeStruct((M_PROMPT, A_KV_HEADS, HEAD_DIM), F32)],
        scratch_shapes=[pltpu.VMEM((D_MODEL, 2 * A_KV), BF16)],
        compiler_params=_params(("arbitrary",), VMEM_LIMIT),
        name="norm1_proj_kva",
    )(xp, xs, nw1, mods3, mods3, w_in, nwk, *rope)


def _rope_tables():
    n_freq = HEAD_DIM // 4
    pos = np.arange(DEC_SEQ)
    row = (pos // GRID_W).astype(np.float64)
    col = (pos % GRID_W).astype(np.float64)
    inv = ROPE_THETA ** (-np.arange(n_freq, dtype=np.float64) / n_freq)
    ar = row[:, None] * inv
    ac = col[:, None] * inv
    cos = np.concatenate([np.cos(ar), np.cos(ar), np.cos(ac), np.cos(ac)], axis=-1)
    sin = np.concatenate([-np.sin(ar), np.sin(ar), -np.sin(ac), np.sin(ac)], axis=-1)
    return jnp.asarray(cos, F32), jnp.asarray(sin, F32)


PROJ_COL = {"qa": COL_QA, "qb": COL_QB, "kb": COL_KB, "vb": COL_VB}
GATE_PANELS = 2 * D_MODEL // TN
PROJ_KV = ("kb", "vb")
PROJ_QG = ("qa", "qb") + ("gate",) * GATE_PANELS


def _proj_kernel(*refs, kinds):
    names = ["h", "w0", "w1", "nqa", "nqb", "nkb", "cos", "sin", "qkv"]
    names += ["gate"] * ("gate" in kinds) + ["kbp"] * ("kb" in kinds) + ["vbp"] * ("vb" in kinds) + ["wbf"]
    r = dict(zip(names, refs, strict=True))
    n_qkv = sum(k != "gate" for k in kinds)
    j, i = pl.program_id(0), pl.program_id(1)
    is_prompt = i < M_PROMPT // TM

    @pl.when(i == 0)
    def _():
        r["wbf"][:, :W_SUB] = r["w0"][...].astype(BF16)
        r["wbf"][:, W_SUB:] = r["w1"][...].astype(BF16)

    def head_cols(k):
        return slice(k * HEAD_DIM, (k + 1) * HEAD_DIM)

    def run(kind, prompt):
        for r0 in range(0, TM, ROW_CHUNK):
            rows = slice(r0, r0 + ROW_CHUNK)
            acc = _dot(r["h"][rows, :], r["wbf"][...])
            if kind == "gate":
                r["gate"][rows, :] = jax.nn.sigmoid(acc)
            elif kind == "vb":
                if prompt:
                    r["vbp"][rows, :] = acc
                r["qkv"][rows, :] = acc.astype(BF16)
            else:
                nw_ref = r[{"qa": "nqa", "qb": "nqb", "kb": "nkb"}[kind]]
                for k in range(TN // HEAD_DIM):
                    y = _head_norm(acc[:, head_cols(k)], nw_ref[...])
                    if kind == "qa" and not prompt:
                        y = _rope(y, r["cos"][rows, :], r["sin"][rows, :])
                    if kind == "kb" and prompt:
                        r["kbp"][rows, head_cols(k)] = y
                    r["qkv"][rows, head_cols(k)] = y.astype(BF16)

    for p, kind in enumerate(kinds[:n_qkv]):
        if kind == "qb":
            pl.when(j == p)(functools.partial(run, kind, None))
        else:
            pl.when((j == p) & is_prompt)(functools.partial(run, kind, True))
            pl.when((j == p) & jnp.logical_not(is_prompt))(functools.partial(run, kind, False))
    if "gate" in kinds:
        pl.when(j >= n_qkv)(functools.partial(run, "gate", None))


def _proj_call(h, w_in, nqa, nqb, nkb, rope, kinds):
    assert TN == 2 * W_SUB and A_Q == B_QKV == TN
    n_panels = len(kinds)
    n_qkv = sum(k != "gate" for k in kinds)
    assert all(k == "gate" for k in kinds[n_qkv:])
    cols = [PROJ_COL[k] for k in kinds[:n_qkv]] + list(range(COL_G, IN_WIDTH, TN))[:n_panels - n_qkv]
    p_tiles = M_PROMPT // TM
    s_tiles = DEC_SEQ // TM
    n_tiles = M_ALL // TM

    def w_sub(j):
        idx = cols[0] // W_SUB
        for p in range(1, n_panels):
            idx = jnp.where(j == p, cols[p] // W_SUB, idx)
        return idx

    def qkv_idx(j, i):
        return (jnp.minimum(j, n_qkv - 1), jnp.where(j >= n_qkv, n_tiles - 1, i), 0)

    def gate_idx(j, i):
        return (jnp.maximum(j - n_qkv, 0), jnp.where(j < n_qkv, 0, i), 0)

    def cache_idx(panel):
        def idx(j, i):
            return (jnp.where(j < panel, 0, jnp.where(j > panel, p_tiles - 1, jnp.minimum(i, p_tiles - 1))), 0)
        return idx

    out_specs = [pl.BlockSpec((None, TM, TN), qkv_idx)]
    out_shape = [jax.ShapeDtypeStruct((n_qkv, M_ALL, TN), BF16)]
    if "gate" in kinds:
        out_specs.append(pl.BlockSpec((None, TM, TN), gate_idx))
        out_shape.append(jax.ShapeDtypeStruct((n_panels - n_qkv, M_ALL, TN), F32))
    for kind in ("kb", "vb"):
        if kind in kinds:
            out_specs.append(pl.BlockSpec((TM, TN), cache_idx(kinds.index(kind))))
            out_shape.append(jax.ShapeDtypeStruct((M_PROMPT, B_QKV), F32))

    norm_spec = pl.BlockSpec((1, HEAD_DIM), lambda j, i: (0, 0))
    rope_spec = pl.BlockSpec((TM, HEAD_DIM), lambda j, i: (jnp.maximum(i - p_tiles, 0) % s_tiles, 0))
    return pl.pallas_call(
        functools.partial(_proj_kernel, kinds=kinds),
        grid=(n_panels, n_tiles),
        in_specs=[pl.BlockSpec((TM, D_MODEL), lambda j, i: (i, 0)),
                  pl.BlockSpec((D_MODEL, W_SUB), lambda j, i: (0, w_sub(j))),
                  pl.BlockSpec((D_MODEL, W_SUB), lambda j, i: (0, w_sub(j) + 1)),
                  norm_spec, norm_spec, norm_spec, rope_spec, rope_spec],
        out_specs=out_specs,
        out_shape=out_shape,
        scratch_shapes=[pltpu.VMEM((D_MODEL, TN), BF16)],
        compiler_params=_params(("arbitrary", "arbitrary"), VMEM_LIMIT),
        name="proj_" + "_".join(dict.fromkeys(kinds)),
    )(h, w_in, w_in, nqa, nqb, nkb, *rope)


LOG2E = 1.4426950408889634
QK_LOG2 = SCALE * LOG2E


def _softmax_pv(parts, sink):
    m = functools.reduce(jnp.maximum, [jnp.max(t, axis=-1, keepdims=True) for t, _ in parts])
    if sink is not None:
        m = jnp.maximum(m, sink)
    acc = None
    for t, v in parts:
        p = jnp.exp2(t - m).astype(BF16)
        y = _dot(p, jnp.concatenate([v, jnp.ones_like(v)], axis=1))
        acc = y if acc is None else acc + y
    o, l = acc[:, :HEAD_DIM], acc[:, HEAD_DIM:]
    if sink is not None:
        l = l + jnp.exp2(sink - m)
    return o / l


def _sink_column(sink_ref, hk, rows_per_head):
    n = A_GROUP * rows_per_head
    g = lax.broadcasted_iota(jnp.int32, (n, 1), 0) // rows_per_head
    col = jnp.full((n, 1), sink_ref[hk * A_GROUP], F32)
    for k in range(1, A_GROUP):
        col = jnp.where(g == k, sink_ref[hk * A_GROUP + k], col)
    return col * LOG2E


def _stack_group(q_ref, hk):
    return jnp.concatenate(
        [q_ref[:, (hk * A_GROUP + g) * HEAD_DIM:(hk * A_GROUP + g + 1) * HEAD_DIM] for g in range(A_GROUP)], axis=0)


def _unstack_group(o_ref, hk, o, rows):
    for g in range(A_GROUP):
        c = (hk * A_GROUP + g) * HEAD_DIM
        o_ref[:, c:c + HEAD_DIM] = o[g * rows:(g + 1) * rows].astype(BF16)


def _ctx_attn_kernel(sink_ref, q_ref, kv_ref, kva_ref, oa_ref, ob_ref):
    qa_ref, qb_ref = q_ref.at[PROJ_QG.index("qa")], q_ref.at[PROJ_QG.index("qb")]
    kb_ref, vb_ref = kv_ref.at[PROJ_KV.index("kb")], kv_ref.at[PROJ_KV.index("vb")]
    for hk in range(A_KV_HEADS):
        k = kva_ref[:, hk * HEAD_DIM:(hk + 1) * HEAD_DIM]
        v = kva_ref[:, A_KV + hk * HEAD_DIM:A_KV + (hk + 1) * HEAD_DIM]
        q4 = _stack_group(qa_ref, hk)
        t = _dot_nt(q4, k) * QK_LOG2
        o = _softmax_pv([(t, v)], _sink_column(sink_ref, hk, SEQ))
        _unstack_group(oa_ref, hk, o, SEQ)
    for h in range(B_HEADS):
        sl = slice(h * HEAD_DIM, (h + 1) * HEAD_DIM)
        t = _dot_nt(qb_ref[:, sl], kb_ref[:, sl]) * QK_LOG2
        ob_ref[:, sl] = _softmax_pv([(t, vb_ref[:, sl])], None).astype(BF16)


def _ctx_attn_call(sink, q, kv, kva):
    row = lambda w: pl.BlockSpec((SEQ, w), lambda b: (b, 0))
    return pl.pallas_call(
        _ctx_attn_kernel,
        grid=(BATCH,),
        in_specs=[pl.BlockSpec(memory_space=pltpu.SMEM),
                  pl.BlockSpec((2, SEQ, TN), lambda b: (0, b, 0)),
                  pl.BlockSpec((2, SEQ, TN), lambda b: (0, b, 0)), row(2 * A_KV)],
        out_specs=[row(A_Q), row(B_QKV)],
        out_shape=[jax.ShapeDtypeStruct((M_PROMPT, A_Q), BF16),
                   jax.ShapeDtypeStruct((M_PROMPT, B_QKV), BF16)],
        compiler_params=_params(("arbitrary",), VMEM_LIMIT),
        name="attn_ctx",
    )(sink, q, kv, kva)


BAND = 3 * BLOCK


def _cache_to_bf16(n_heads, ck_ref, cv_ref, ckb, cvb):
    for h in range(n_heads):
        sl = slice(h * HEAD_DIM, (h + 1) * HEAD_DIM)
        ckb[:, sl] = ck_ref[:, h, :].astype(BF16)
        cvb[:, sl] = cv_ref[:, h, :].astype(BF16)


def _win_attn_kernel(sink_ref, q_ref, kv_ref, ck_ref, cv_ref, o_ref, ckb, cvb):
    n = pl.program_id(1)
    pl.when(n == 0)(functools.partial(_cache_to_bf16, A_KV_HEADS, ck_ref, cv_ref, ckb, cvb))
    start = pl.multiple_of(jnp.clip((n - 1) * BLOCK, 0, DEC_SEQ - BAND), BLOCK)
    qpos = n * BLOCK + lax.broadcasted_iota(jnp.int32, (BLOCK, BAND), 0)
    kpos = start + lax.broadcasted_iota(jnp.int32, (BLOCK, BAND), 1)
    valid = jnp.abs(qpos - kpos) <= A_WINDOW
    valid = jnp.concatenate([valid.astype(jnp.int32)] * A_GROUP, axis=0) > 0
    for hk in range(A_KV_HEADS):
        sl = slice(hk * HEAD_DIM, (hk + 1) * HEAD_DIM)
        slv = slice(A_KV + hk * HEAD_DIM, A_KV + (hk + 1) * HEAD_DIM)
        k_loc = kv_ref[pl.ds(start, BAND), sl]
        v_loc = kv_ref[pl.ds(start, BAND), slv]
        q4 = _stack_group(q_ref, hk)
        t_loc = jnp.where(valid, _dot_nt(q4, k_loc) * QK_LOG2, NEG)
        t_ctx = _dot_nt(q4, ckb[:, sl]) * QK_LOG2
        o = _softmax_pv([(t_loc, v_loc), (t_ctx, cvb[:, sl])], _sink_column(sink_ref, hk, BLOCK))
        _unstack_group(o_ref, hk, o, BLOCK)


def _win_attn_call(sink, q, kva, cache_k, cache_v):
    nb = DEC_SEQ // BLOCK
    q0 = M_PROMPT // BLOCK
    b0 = M_PROMPT // DEC_SEQ
    return pl.pallas_call(
        _win_attn_kernel,
        grid=(DEC_BATCH, nb),
        in_specs=[pl.BlockSpec(memory_space=pltpu.SMEM),
                  pl.BlockSpec((None, BLOCK, A_Q), lambda b, n: (PROJ_QG.index("qa"), q0 + b * nb + n, 0)),
                  pl.BlockSpec((DEC_SEQ, 2 * A_KV), lambda b, n: (b0 + b, 0)),
                  pl.BlockSpec((None, None, PAST_LEN, A_KV_HEADS, HEAD_DIM), lambda b, n: (b, 0, 0, 0, 0)),
                  pl.BlockSpec((None, None, PAST_LEN, A_KV_HEADS, HEAD_DIM), lambda b, n: (b, 0, 0, 0, 0))],
        out_specs=pl.BlockSpec((BLOCK, A_Q), lambda b, n: (b * nb + n, 0)),
        out_shape=jax.ShapeDtypeStruct((M_SAMPLE, A_Q), BF16),
        scratch_shapes=[pltpu.VMEM((PAST_LEN, A_KV), BF16), pltpu.VMEM((PAST_LEN, A_KV), BF16)],
        compiler_params=_params(("arbitrary", "arbitrary"), VMEM_LIMIT),
        name="attn_window",
    )(sink, q, kva, cache_k, cache_v)


NA_QROWS = 4
NA_WIN_ROWS = 12
NA_Q = NA_QROWS * GRID_W
NA_KEYS = NA_WIN_ROWS * GRID_W
NA_DR = 2 * NA_ROWS_MAX - 1
NA_PAIRS = NA_DR + 1
RPB_W = 2 * NA_COLS - 1
RPB_PAD = GRID_W - NA_COLS


def _na_row_start(r):
    return jnp.clip(r - NA_ROWS // 2, 0, GRID_ROWS - NA_ROWS)


def _na_window_row0(blk):
    return jnp.clip(blk * NA_QROWS - NA_ROWS // 2, 0, GRID_ROWS - NA_WIN_ROWS)


def _check_na_windows():
    for blk in range(GRID_ROWS // NA_QROWS):
        w0 = int(np.clip(blk * NA_QROWS - NA_ROWS // 2, 0, GRID_ROWS - NA_WIN_ROWS))
        for r in range(blk * NA_QROWS, (blk + 1) * NA_QROWS):
            r0 = int(np.clip(r - NA_ROWS // 2, 0, GRID_ROWS - NA_ROWS))
            assert w0 <= r0 and r0 + NA_ROWS <= w0 + NA_WIN_ROWS, (blk, r)


_check_na_windows()


def _na_attn_kernel(q_ref, k_ref, v_ref, ck_ref, cv_ref, rpb_ref, o_ref, tab_ref, ckb, cvb):
    b, blk = pl.program_id(0), pl.program_id(1)
    lane = lax.broadcasted_iota(jnp.int32, (GRID_W, 2 * GRID_W), 1)
    pl.when(blk == 0)(functools.partial(_cache_to_bf16, B_HEADS, ck_ref, cv_ref, ckb, cvb))

    @pl.when((b == 0) & (blk == 0))
    def _():
        for h in range(B_HEADS):
            for d in range(NA_PAIRS):
                lo = jnp.broadcast_to(rpb_ref[h, d:d + 1, :], (GRID_W, 2 * GRID_W))
                hi = jnp.broadcast_to(rpb_ref[h, d + 1:d + 2, :], (GRID_W, 2 * GRID_W))
                lo = pltpu.roll(lo, GRID_W + 1, 1, stride=1, stride_axis=0)
                hi = pltpu.roll(hi, 1, 1, stride=1, stride_axis=0)
                tab_ref[h, d] = jnp.where(lane < GRID_W, lo, hi) * LOG2E

    row0 = _na_window_row0(blk)
    k0 = pl.multiple_of(row0 * GRID_W, GRID_W)
    qi = lax.broadcasted_iota(jnp.int32, (NA_Q, NA_KEYS), 0)
    ki = lax.broadcasted_iota(jnp.int32, (NA_Q, NA_KEYS), 1)
    qrow, qcol = blk * NA_QROWS + qi // GRID_W, qi % GRID_W
    krow, kcol = row0 + ki // GRID_W, ki % GRID_W
    rstart = _na_row_start(qrow)
    cstart = jnp.clip(qcol - NA_COLS // 2, 0, GRID_W - NA_COLS)
    valid = (krow >= rstart) & (krow < rstart + NA_ROWS) & (kcol >= cstart) & (kcol < cstart + NA_COLS)

    for h in range(B_HEADS):
        sl = slice(h * HEAD_DIM, (h + 1) * HEAD_DIM)
        bias = jnp.concatenate(
            [jnp.concatenate(
                [tab_ref[h, jnp.clip(row0 + 2 * p - blk * NA_QROWS - q + NA_ROWS_MAX, 0, NA_PAIRS - 1)]
                 for p in range(NA_WIN_ROWS // 2)], axis=1)
             for q in range(NA_QROWS)], axis=0)
        k_loc = k_ref[pl.ds(k0, NA_KEYS), sl]
        v_loc = v_ref[pl.ds(k0, NA_KEYS), sl]
        t_loc = jnp.where(valid, _dot_nt(q_ref[:, sl], k_loc) * QK_LOG2 + bias, NEG)
        t_ctx = _dot_nt(q_ref[:, sl], ckb[:, sl]) * QK_LOG2
        o = _softmax_pv([(t_loc, v_loc), (t_ctx, cvb[:, sl])], None)
        o_ref[:, sl] = o.astype(BF16)


def _na_bias_rows(rpb):
    rows = jnp.pad(rpb, ((0, 0), (1, 1), (0, 0)))
    left = jnp.broadcast_to(rows[..., :1], rows.shape[:2] + (RPB_PAD,))
    right = jnp.broadcast_to(rows[..., -1:], rows.shape[:2] + (2 * GRID_W - RPB_PAD - RPB_W,))
    return jnp.concatenate([left, rows, right], axis=-1)


def _na_attn_call(q, kv, cache_k, cache_v, rpb):
    nblk = GRID_ROWS // NA_QROWS
    q0 = M_PROMPT // NA_Q
    b0 = M_PROMPT // DEC_SEQ
    return pl.pallas_call(
        _na_attn_kernel,
        grid=(DEC_BATCH, nblk),
        in_specs=[pl.BlockSpec((None, NA_Q, B_QKV), lambda b, r: (PROJ_QG.index("qb"), q0 + b * nblk + r, 0)),
                  pl.BlockSpec((None, DEC_SEQ, B_QKV), lambda b, r: (PROJ_KV.index("kb"), b0 + b, 0)),
                  pl.BlockSpec((None, DEC_SEQ, B_QKV), lambda b, r: (PROJ_KV.index("vb"), b0 + b, 0)),
                  pl.BlockSpec((None, None, PAST_LEN, B_HEADS, HEAD_DIM), lambda b, r: (b, 0, 0, 0, 0)),
                  pl.BlockSpec((None, None, PAST_LEN, B_HEADS, HEAD_DIM), lambda b, r: (b, 0, 0, 0, 0)),
                  pl.BlockSpec((B_HEADS, NA_DR + 2, 2 * GRID_W), lambda b, r: (0, 0, 0))],
        out_specs=pl.BlockSpec((NA_Q, B_QKV), lambda b, r: (b * nblk + r, 0)),
        out_shape=jax.ShapeDtypeStruct((M_SAMPLE, B_QKV), BF16),
        scratch_shapes=[pltpu.VMEM((B_HEADS, NA_PAIRS, GRID_W, 2 * GRID_W), F32),
                        pltpu.VMEM((PAST_LEN, B_QKV), BF16), pltpu.VMEM((PAST_LEN, B_QKV), BF16)],
        compiler_params=_params(("arbitrary", "arbitrary"), VMEM_LIMIT),
        name="attn_neighbourhood",
    )(q, kv, kv, cache_k, cache_v, _na_bias_rows(rpb))


def _merge_kernel(oap, oas, obp, obs, wa_ref, wb_ref, ga_ref, gb_ref, z_ref, wa_bf, wb_bf):
    i = pl.program_id(1)
    p_tiles = M_PROMPT // TM

    @pl.when(i == 0)
    def _():
        wa_bf[...] = wa_ref[...].astype(BF16)
        wb_bf[...] = wb_ref[...].astype(BF16)

    def run(oa_ref, ob_ref):
        for r0 in range(0, TM, ROW_CHUNK):
            rows = slice(r0, r0 + ROW_CHUNK)
            ya = _dot(oa_ref[rows, :], wa_bf[...])
            yb = _dot(ob_ref[rows, :], wb_bf[...])
            z_ref[rows, :] = (ga_ref[rows, :] * ya + gb_ref[rows, :] * yb).astype(BF16)

    pl.when(i < p_tiles)(functools.partial(run, oap, obp))
    pl.when(i >= p_tiles)(functools.partial(run, oas, obs))


def _merge_call(oa_p, oa_s, ob_p, ob_s, w_br_a, w_br_b, gates):
    p_tiles = M_PROMPT // TM
    nj = D_MODEL // TN
    pspec = pl.BlockSpec((TM, A_Q), lambda j, i: (jnp.minimum(i, p_tiles - 1), 0))
    sspec = pl.BlockSpec((TM, A_Q), lambda j, i: (jnp.maximum(i - p_tiles, 0), 0))
    once = dict(pipeline_mode=pl.Buffered(1))
    return pl.pallas_call(
        _merge_kernel,
        grid=(nj, M_ALL // TM),
        in_specs=[pspec, sspec, pspec, sspec,
                  pl.BlockSpec((A_Q, TN), lambda j, i: (0, j), **once),
                  pl.BlockSpec((B_QKV, TN), lambda j, i: (0, j), **once),
                  pl.BlockSpec((None, TM, TN), lambda j, i: (j, i, 0)),
                  pl.BlockSpec((None, TM, TN), lambda j, i: (nj + j, i, 0))],
        out_specs=pl.BlockSpec((TM, TN), lambda j, i: (i, j)),
        out_shape=jax.ShapeDtypeStruct((M_ALL, D_MODEL), BF16),
        scratch_shapes=[pltpu.VMEM((A_Q, TN), BF16), pltpu.VMEM((B_QKV, TN), BF16)],
        compiler_params=_params(("arbitrary", "arbitrary"), VMEM_LIMIT),
        name="merge_branches",
    )(oa_p, oa_s, ob_p, ob_s, w_br_a, w_br_b, gates, gates)


def _out_kernel(z_ref, w_ref, xp_ref, xs_ref, g_ref, o_ref, wbf):
    i = pl.program_id(1)
    p_tiles = M_PROMPT // TM

    @pl.when(i == 0)
    def _():
        wbf[...] = w_ref[...].astype(BF16)

    def run(x_ref):
        for r0 in range(0, TM, ROW_CHUNK):
            rows = slice(r0, r0 + ROW_CHUNK)
            o_ref[rows, :] = x_ref[rows, :] + g_ref[...] * _dot(z_ref[rows, :], wbf[...])

    pl.when(i < p_tiles)(functools.partial(run, xp_ref))
    pl.when(i >= p_tiles)(functools.partial(run, xs_ref))


def _out_call(z, w_out, xp, xs, mods3):
    p_tiles = M_PROMPT // TM
    nj = D_MODEL // TN
    return pl.pallas_call(
        _out_kernel,
        grid=(nj, M_ALL // TM),
        in_specs=[pl.BlockSpec((TM, D_MODEL), lambda j, i: (i, 0)),
                  pl.BlockSpec((D_MODEL, TN), lambda j, i: (0, j)),
                  pl.BlockSpec((TM, TN), lambda j, i: (jnp.minimum(i, p_tiles - 1), j)),
                  pl.BlockSpec((TM, TN), lambda j, i: (jnp.maximum(i - p_tiles, 0), j)),
                  pl.BlockSpec((None, 1, TN), lambda j, i: (_mod_row(i, TM), 0, 2 * nj + j))],
        out_specs=pl.BlockSpec((TM, TN), lambda j, i: (i, j)),
        out_shape=jax.ShapeDtypeStruct((M_ALL, D_MODEL), F32),
        scratch_shapes=[pltpu.VMEM((D_MODEL, TN), BF16)],
        compiler_params=_params(("arbitrary", "arbitrary"), VMEM_LIMIT),
        name="out_proj_residual",
    )(z, w_out, xp, xs, mods3)


def _mlp_kernel(x_hbm, nw_ref, sh_ref, sc_ref, g_ref, wu_ref, wd_ref, o_ref, h_ref, x_buf, x_sem, *, tile0):
    i, f = pl.program_id(0), pl.program_id(1)

    def x_copy(tile):
        return pltpu.make_async_copy(x_hbm.at[pl.ds((tile0 + tile) * TM, TM), :], x_buf, x_sem)

    @pl.when((i == 0) & (f == 0))
    def _():
        x_copy(0).start()

    @pl.when(f == 0)
    def _():
        x_copy(i).wait()
        wu = wu_ref[...].astype(BF16)
        wd = wd_ref[...].astype(BF16)
        for r0 in range(0, TM, ROW_CHUNK):
            rows = slice(r0, r0 + ROW_CHUNK)
            x = x_buf[rows, :]
            h = _modnorm(x, nw_ref[...], sc_ref[...], sh_ref[...]).astype(BF16)
            h_ref[rows, :] = h
            u = jnp.square(jnp.maximum(_dot(h, wu), 0.0)).astype(BF16)
            o_ref[rows, :] = x + g_ref[...] * _dot(u, wd)

    @pl.when((f == 1) & (i + 1 < pl.num_programs(0)))
    def _():
        x_copy(i + 1).start()

    @pl.when(f > 0)
    def _():
        u = _dot(h_ref[...], wu_ref[...].astype(BF16))
        u = jnp.square(jnp.maximum(u, 0.0)).astype(BF16)
        for c0 in range(0, D_MODEL, MLP_COLS):
            cols = slice(c0, c0 + MLP_COLS)
            o_ref[:, cols] += g_ref[:, cols] * _dot(u, wd_ref[:, cols].astype(BF16))


def _mlp_call(x1, nw, mods3, w_up, w_down, tile0, n_tiles):
    mod = lambda k: pl.BlockSpec((None, 1, D_MODEL), lambda i, f: (_mod_row(tile0 + i, TM), 0, k))
    return pl.pallas_call(
        functools.partial(_mlp_kernel, tile0=tile0),
        grid=(n_tiles, D_FF // TF),
        in_specs=[pl.BlockSpec(memory_space=pl.ANY),
                  pl.BlockSpec((1, D_MODEL), lambda i, f: (0, 0)),
                  mod(3), mod(4), mod(5),
                  pl.BlockSpec((D_MODEL, TF), lambda i, f: (0, f)),
                  pl.BlockSpec((TF, D_MODEL), lambda i, f: (f, 0))],
        out_specs=pl.BlockSpec((TM, D_MODEL), lambda i, f: (i, 0)),
        out_shape=jax.ShapeDtypeStruct((n_tiles * TM, D_MODEL), F32),
        scratch_shapes=[pltpu.VMEM((TM, D_MODEL), BF16),
                        pltpu.VMEM((TM, D_MODEL), F32),
                        pltpu.SemaphoreType.DMA(())],
        compiler_params=_params(("arbitrary", "arbitrary"), VMEM_LIMIT),
        name="mlp",
    )(x1, nw, mods3, mods3, mods3, w_up, w_down)


def kernel(x_prompt, x_sample, cache_a_k, cache_a_v, cache_b_k, cache_b_v, c, c_ctx, norm1_w, norm2_w, w_ada, b_ada, w_in, q_norm_a, k_norm_a, q_norm_b, k_norm_b, sink_a, rpb_b, w_br_a, w_br_b, w_out, w_up, w_down):
    assert w_ada.shape[0] == 1, "one trunk layer"
    xp = x_prompt.reshape(M_PROMPT, D_MODEL)
    xs = x_sample.reshape(M_SAMPLE, D_MODEL)

    cvecs = jnp.concatenate([c_ctx[None, :], c, jnp.zeros((8 - 1 - DEC_BATCH, D_MODEL), F32)], axis=0)
    mods3 = _ada_call(cvecs, w_ada[0], b_ada).reshape(8, 1, 6 * D_MODEL)

    w = w_in[0]
    rope = _rope_tables()
    h, kva, ka_p, va_p = _norm_kva_call(xp, xs, norm1_w, mods3, w, k_norm_a, rope)
    kv, kb_p, vb_p = _proj_call(h, w, q_norm_a, q_norm_b, k_norm_b, rope, PROJ_KV)
    q, gates = _proj_call(h, w, q_norm_a, q_norm_b, k_norm_b, rope, PROJ_QG)

    sink = sink_a[0]
    oa_p, ob_p = _ctx_attn_call(sink, q, kv, kva)
    oa_s = _win_attn_call(sink, q, kva, cache_a_k, cache_a_v)
    ob_s = _na_attn_call(q, kv, cache_b_k, cache_b_v, rpb_b[0])

    z = _merge_call(oa_p, oa_s, ob_p, ob_s, w_br_a[0], w_br_b[0], gates)
    x1 = _out_call(z, w_out[0], xp, xs, mods3)

    p_tiles = M_PROMPT // TM
    y_p = _mlp_call(x1, norm2_w, mods3, w_up[0], w_down[0], 0, p_tiles)
    y_s = _mlp_call(x1, norm2_w, mods3, w_up[0], w_down[0], p_tiles, M_SAMPLE // TM)

    return (y_p.reshape(BATCH, SEQ, D_MODEL),
            y_s.reshape(DEC_BATCH, DEC_SEQ, D_MODEL),
            ka_p.reshape(BATCH, 1, SEQ, A_KV_HEADS, HEAD_DIM),
            va_p.reshape(BATCH, 1, SEQ, A_KV_HEADS, HEAD_DIM),
            kb_p.reshape(BATCH, 1, SEQ, B_HEADS, HEAD_DIM),
            vb_p.reshape(BATCH, 1, SEQ, B_HEADS, HEAD_DIM))
```

```python
import functools

import numpy as np
import jax
import jax.numpy as jnp
from jax import lax
from jax.experimental import pallas as pl
from jax.experimental.pallas import tpu as pltpu

D_MODEL = 2048
BATCH = 16
SEQ = 256
DEC_BATCH = 2
DEC_SEQ = 1024
PAST_LEN = 256
GRID_W = 64
HEAD_DIM = 128
A_HEADS = 8
A_KV_HEADS = 2
A_GROUP = A_HEADS // A_KV_HEADS
A_WINDOW = 128
BLOCK = 128
B_HEADS = 8
NA_ROWS_MAX = 8
NA_COLS = 16
D_FF = 4 * D_MODEL
ROPE_THETA = 10000.0
EPS = 1e-6
NEG = -1e30
A_Q = A_HEADS * HEAD_DIM
A_KV = A_KV_HEADS * HEAD_DIM
B_QKV = B_HEADS * HEAD_DIM
IN_WIDTH = A_Q + 2 * A_KV + 3 * B_QKV + 2 * D_MODEL
SCALE = HEAD_DIM ** -0.5

M_PROMPT = BATCH * SEQ
M_SAMPLE = DEC_BATCH * DEC_SEQ
M_ALL = M_PROMPT + M_SAMPLE
GRID_ROWS = DEC_SEQ // GRID_W
NA_ROWS = min(NA_ROWS_MAX, GRID_ROWS)

COL_QA = 0
COL_KVA = A_Q
COL_QB = A_Q + 2 * A_KV
COL_KB = COL_QB + B_QKV
COL_VB = COL_KB + B_QKV
COL_G = COL_VB + B_QKV

TM = 1024
TN = 1024
ROW_CHUNK = 256
W_SUB = 512
TM_NORM = 512
TF = 512
MLP_COLS = 512
ADA_HEAD = 2 * D_MODEL
ADA_SIDE_COLS = 256
VMEM_LIMIT = 60 * 1024 * 1024

F32 = jnp.float32
BF16 = jnp.bfloat16


def _mod_row(i, tm):
    p_tiles = M_PROMPT // tm
    return jnp.where(i < p_tiles, 0, 1 + (i - p_tiles) // (DEC_SEQ // tm))


def _dot(a, b):
    return jnp.dot(a, b, preferred_element_type=F32)


def _dot_nt(a, b):
    return lax.dot_general(a, b, (((1,), (1,)), ((), ())), preferred_element_type=F32)


def _params(sem, vmem=None):
    return pltpu.CompilerParams(dimension_semantics=sem, vmem_limit_bytes=vmem)


def _ada_block(c_ref, w_ref, b_ref):
    cv = c_ref[...]
    s = (cv * jax.nn.sigmoid(cv)).astype(BF16)
    return _dot(s, w_ref[...].astype(BF16)) + b_ref[...]


def _ada_kernel(c_ref, w_ref, b_ref, o_ref):
    o_ref[...] = _ada_block(c_ref, w_ref, b_ref)


def _ada_call(cvecs, w_ada, b_ada):
    tn = 1024
    n = ADA_HEAD
    return pl.pallas_call(
        _ada_kernel,
        grid=(n // tn,),
        in_specs=[pl.BlockSpec((8, D_MODEL), lambda j: (0, 0)),
                  pl.BlockSpec((D_MODEL, tn), lambda j: (0, j)),
                  pl.BlockSpec((1, tn), lambda j: (0, j))],
        out_specs=pl.BlockSpec((8, tn), lambda j: (0, j)),
        out_shape=jax.ShapeDtypeStruct((8, n), F32),
        compiler_params=_params(("arbitrary",), VMEM_LIMIT),
        name="ada_mod",
    )(cvecs, w_ada, b_ada)


def _modnorm(x, nw, sc, sh):
    y = x * lax.rsqrt(jnp.mean(x * x, axis=-1, keepdims=True) + EPS)
    return y * (nw * (1.0 + sc)) + sh


def _head_norm(x, nw):
    return x * lax.rsqrt(jnp.mean(x * x, axis=-1, keepdims=True) + EPS) * nw


def _rope(x, cos, sin_signed):
    lane = lax.broadcasted_iota(jnp.int32, x.shape, 1)
    partner = jnp.where((lane % 64) < 32, pltpu.roll(x, 96, 1), pltpu.roll(x, 32, 1))
    return x * cos + partner * sin_signed


def _norm_kva_kernel(xp_ref, xs_ref, nw1_ref, sh_ref, sc_ref, w_ref, nwk_ref, cos_ref, sin_ref,
                     h_ref, kv_ref, kp_ref, vp_ref, wbf):
    i = pl.program_id(0)
    p_tiles = M_PROMPT // TM_NORM

    @pl.when(i == 0)
    def _():
        wbf[...] = w_ref[...].astype(BF16)

    def head_cols(k, base=0):
        return slice(base + k * HEAD_DIM, base + (k + 1) * HEAD_DIM)

    def run(prompt):
        x_ref = xp_ref if prompt else xs_ref
        for r0 in range(0, TM_NORM, ROW_CHUNK):
            rows = slice(r0, r0 + ROW_CHUNK)
            h = _modnorm(x_ref[rows, :], nw1_ref[...], sc_ref[...], sh_ref[...]).astype(BF16)
            h_ref[rows, :] = h
            acc = _dot(h, wbf[...])
            for k in range(A_KV_HEADS):
                y = _head_norm(acc[:, head_cols(k)], nwk_ref[...])
                v = acc[:, head_cols(k, A_KV)]
                if prompt:
                    kp_ref[rows, k, :] = y
                    vp_ref[rows, k, :] = v
                else:
                    y = _rope(y, cos_ref[rows, :], sin_ref[rows, :])
                kv_ref[rows, head_cols(k)] = y.astype(BF16)
                kv_ref[rows, head_cols(k, A_KV)] = v.astype(BF16)

    pl.when(i < p_tiles)(functools.partial(run, True))
    pl.when(i >= p_tiles)(functools.partial(run, False))


def _norm_kva_call(xp, xs, nw1, mods3, w_in, nwk, rope):
    tm = TM_NORM
    p_tiles = M_PROMPT // tm
    s_tiles = DEC_SEQ // tm
    assert COL_KVA % (2 * A_KV) == 0
    rope_spec = pl.BlockSpec((tm, HEAD_DIM), lambda i: (jnp.maximum(i - p_tiles, 0) % s_tiles, 0))
    parked = pl.BlockSpec((tm, A_KV_HEADS, HEAD_DIM), lambda i: (jnp.minimum(i, p_tiles - 1), 0, 0))
    return pl.pallas_call(
        _norm_kva_kernel,
        grid=(M_ALL // tm,),
        in_specs=[pl.BlockSpec((tm, D_MODEL), lambda i: (jnp.minimum(i, p_tiles - 1), 0)),
                  pl.BlockSpec((tm, D_MODEL), lambda i: (jnp.maximum(i - p_tiles, 0), 0)),
                  pl.BlockSpec((1, D_MODEL), lambda i: (0, 0)),
                  pl.BlockSpec((None, 1, D_MODEL), lambda i: (_mod_row(i, tm), 0, 0)),
                  pl.BlockSpec((None, 1, D_MODEL), lambda i: (_mod_row(i, tm), 0, 1)),
                  pl.BlockSpec((D_MODEL, 2 * A_KV), lambda i: (0, COL_KVA // (2 * A_KV)),
                               pipeline_mode=pl.Buffered(1)),
                  pl.BlockSpec((1, HEAD_DIM), lambda i: (0, 0)),
                  rope_spec, rope_spec],
        out_specs=[pl.BlockSpec((tm, D_MODEL), lambda i: (i, 0)),
                   pl.BlockSpec((tm, 2 * A_KV), lambda i: (i, 0)),
                   parked, parked],
        out_shape=[jax.ShapeDtypeStruct((M_ALL, D_MODEL), BF16),
                   jax.ShapeDtypeStruct((M_ALL, 2 * A_KV), BF16),
                   jax.ShapeDtypeStruct((M_PROMPT, A_KV_HEADS, HEAD_DIM), F32),
                   jax.ShapeDtypeStruct((M_PROMPT, A_KV_HEADS, HEAD_DIM), F32)],
        scratch_shapes=[pltpu.VMEM((D_MODEL, 2 * A_KV), BF16)],
        compiler_params=_params(("arbitrary",), VMEM_LIMIT),
        name="norm1_proj_kva",
    )(xp, xs, nw1, mods3, mods3, w_in, nwk, *rope)


def _rope_tables():
    n_freq = HEAD_DIM // 4
    pos = np.arange(DEC_SEQ)
    row = (pos // GRID_W).astype(np.float64)
    col = (pos % GRID_W).astype(np.float64)
    inv = ROPE_THETA ** (-np.arange(n_freq, dtype=np.float64) / n_freq)
    ar = row[:, None] * inv
    ac = col[:, None] * inv
    cos = np.concatenate([np.cos(ar), np.cos(ar), np.cos(ac), np.cos(ac)], axis=-1)
    sin = np.concatenate([-np.sin(ar), np.sin(ar), -np.sin(ac), np.sin(ac)], axis=-1)
    return jnp.asarray(cos, F32), jnp.asarray(sin, F32)


PROJ_COL = {"qa": COL_QA, "qb": COL_QB, "kb": COL_KB, "vb": COL_VB}
GATE_PANELS = 2 * D_MODEL // TN
PROJ_KV = ("kb", "vb")
PROJ_QG = ("qa", "qb") + ("gate",) * GATE_PANELS


def _proj_kernel(*refs, kinds, with_ada):
    names = ["h", "w0", "w1", "nqa", "nqb", "nkb", "cos", "sin"] + ["cv", "wada", "bada"] * with_ada + ["qkv"]
    names += ["gate"] * ("gate" in kinds) + ["kbp"] * ("kb" in kinds) + ["vbp"] * ("vb" in kinds)
    names += ["mods"] * with_ada + ["wbf"]
    r = dict(zip(names, refs, strict=True))
    n_qkv = sum(k != "gate" for k in kinds)
    j, i = pl.program_id(0), pl.program_id(1)
    is_prompt = i < M_PROMPT // TM

    @pl.when(i == 0)
    def _():
        r["wbf"][:, :W_SUB] = r["w0"][...].astype(BF16)
        r["wbf"][:, W_SUB:] = r["w1"][...].astype(BF16)

    def head_cols(k):
        return slice(k * HEAD_DIM, (k + 1) * HEAD_DIM)

    def run(kind, prompt):
        if with_ada:
            r["mods"][:, 0, :] = _ada_block(r["cv"], r["wada"], r["bada"])
        for r0 in range(0, TM, ROW_CHUNK):
            rows = slice(r0, r0 + ROW_CHUNK)
            acc = _dot(r["h"][rows, :], r["wbf"][...])
            if kind == "gate":
                r["gate"][rows, :] = jax.nn.sigmoid(acc)
            elif kind == "vb":
                if prompt:
                    r["vbp"][rows, :] = acc
                r["qkv"][rows, :] = acc.astype(BF16)
            else:
                nw_ref = r[{"qa": "nqa", "qb": "nqb", "kb": "nkb"}[kind]]
                for k in range(TN // HEAD_DIM):
                    y = _head_norm(acc[:, head_cols(k)], nw_ref[...])
                    if kind == "qa" and not prompt:
                        y = _rope(y, r["cos"][rows, :], r["sin"][rows, :])
                    if kind == "kb" and prompt:
                        r["kbp"][rows, head_cols(k)] = y
                    r["qkv"][rows, head_cols(k)] = y.astype(BF16)

    for p, kind in enumerate(kinds[:n_qkv]):
        if kind == "qb":
            pl.when(j == p)(functools.partial(run, kind, None))
        else:
            pl.when((j == p) & is_prompt)(functools.partial(run, kind, True))
            pl.when((j == p) & jnp.logical_not(is_prompt))(functools.partial(run, kind, False))
    if "gate" in kinds:
        pl.when(j >= n_qkv)(functools.partial(run, "gate", None))


def _proj_call(h, w_in, nqa, nqb, nkb, rope, kinds, ada=None):
    assert TN == 2 * W_SUB and A_Q == B_QKV == TN
    n_panels = len(kinds)
    n_qkv = sum(k != "gate" for k in kinds)
    assert all(k == "gate" for k in kinds[n_qkv:])
    cols = [PROJ_COL[k] for k in kinds[:n_qkv]] + list(range(COL_G, IN_WIDTH, TN))[:n_panels - n_qkv]
    p_tiles = M_PROMPT // TM
    s_tiles = DEC_SEQ // TM
    n_tiles = M_ALL // TM

    def w_sub(j):
        idx = cols[0] // W_SUB
        for p in range(1, n_panels):
            idx = jnp.where(j == p, cols[p] // W_SUB, idx)
        return idx

    def qkv_idx(j, i):
        return (jnp.minimum(j, n_qkv - 1), jnp.where(j >= n_qkv, n_tiles - 1, i), 0)

    def gate_idx(j, i):
        return (jnp.maximum(j - n_qkv, 0), jnp.where(j < n_qkv, 0, i), 0)

    def cache_idx(panel):
        def idx(j, i):
            return (jnp.where(j < panel, 0, jnp.where(j > panel, p_tiles - 1, jnp.minimum(i, p_tiles - 1))), 0)
        return idx

    out_specs = [pl.BlockSpec((None, TM, TN), qkv_idx)]
    out_shape = [jax.ShapeDtypeStruct((n_qkv, M_ALL, TN), BF16)]
    if "gate" in kinds:
        out_specs.append(pl.BlockSpec((None, TM, TN), gate_idx))
        out_shape.append(jax.ShapeDtypeStruct((n_panels - n_qkv, M_ALL, TN), F32))
    for kind in ("kb", "vb"):
        if kind in kinds:
            out_specs.append(pl.BlockSpec((TM, TN), cache_idx(kinds.index(kind))))
            out_shape.append(jax.ShapeDtypeStruct((M_PROMPT, B_QKV), F32))

    norm_spec = pl.BlockSpec((1, HEAD_DIM), lambda j, i: (0, 0))
    rope_spec = pl.BlockSpec((TM, HEAD_DIM), lambda j, i: (jnp.maximum(i - p_tiles, 0) % s_tiles, 0))
    in_specs = [pl.BlockSpec((TM, D_MODEL), lambda j, i: (i, 0)),
                pl.BlockSpec((D_MODEL, W_SUB), lambda j, i: (0, w_sub(j))),
                pl.BlockSpec((D_MODEL, W_SUB), lambda j, i: (0, w_sub(j) + 1)),
                norm_spec, norm_spec, norm_spec, rope_spec, rope_spec]
    args = [h, w_in, w_in, nqa, nqb, nkb, *rope]
    if ada is not None:
        n_side = (6 * D_MODEL - ADA_HEAD) // ADA_SIDE_COLS
        assert n_side <= n_panels * n_tiles
        side = lambda j, i: jnp.minimum(j * n_tiles + i, n_side - 1)
        in_specs += [pl.BlockSpec((8, D_MODEL), lambda j, i: (0, 0)),
                     pl.BlockSpec((D_MODEL, ADA_SIDE_COLS), lambda j, i: (0, ADA_HEAD // ADA_SIDE_COLS + side(j, i))),
                     pl.BlockSpec((1, ADA_SIDE_COLS), lambda j, i: (0, ADA_HEAD // ADA_SIDE_COLS + side(j, i)))]
        args += list(ada)
        out_specs.append(pl.BlockSpec((8, 1, ADA_SIDE_COLS), lambda j, i: (0, 0, side(j, i))))
        out_shape.append(jax.ShapeDtypeStruct((8, 1, 6 * D_MODEL - ADA_HEAD), F32))
    return pl.pallas_call(
        functools.partial(_proj_kernel, kinds=kinds, with_ada=ada is not None),
        grid=(n_panels, n_tiles),
        in_specs=in_specs,
        out_specs=out_specs,
        out_shape=out_shape,
        scratch_shapes=[pltpu.VMEM((D_MODEL, TN), BF16)],
        compiler_params=_params(("arbitrary", "arbitrary"), VMEM_LIMIT),
        name="proj_" + "_".join(dict.fromkeys(kinds)),
    )(*args)


LOG2E = 1.4426950408889634
QK_LOG2 = SCALE * LOG2E


def _softmax_pv(parts, sink):
    m = functools.reduce(jnp.maximum, [jnp.max(t, axis=-1, keepdims=True) for t, _ in parts])
    if sink is not None:
        m = jnp.maximum(m, sink)
    acc = None
    for t, v in parts:
        p = jnp.exp2(t - m).astype(BF16)
        y = _dot(p, jnp.concatenate([v, jnp.ones_like(v)], axis=1))
        acc = y if acc is None else acc + y
    o, l = acc[:, :HEAD_DIM], acc[:, HEAD_DIM:]
    if sink is not None:
        l = l + jnp.exp2(sink - m)
    return o / l


def _sink_column(sink_ref, hk, rows_per_head):
    n = A_GROUP * rows_per_head
    g = lax.broadcasted_iota(jnp.int32, (n, 1), 0) // rows_per_head
    col = jnp.full((n, 1), sink_ref[hk * A_GROUP], F32)
    for k in range(1, A_GROUP):
        col = jnp.where(g == k, sink_ref[hk * A_GROUP + k], col)
    return col * LOG2E


def _stack_group(q_ref, hk):
    return jnp.concatenate(
        [q_ref[:, (hk * A_GROUP + g) * HEAD_DIM:(hk * A_GROUP + g + 1) * HEAD_DIM] for g in range(A_GROUP)], axis=0)


def _unstack_group(o_ref, hk, o, rows):
    for g in range(A_GROUP):
        c = (hk * A_GROUP + g) * HEAD_DIM
        o_ref[:, c:c + HEAD_DIM] = o[g * rows:(g + 1) * rows].astype(BF16)


def _ctx_attn_kernel(sink_ref, q_ref, kv_ref, kva_ref, oa_ref, ob_ref):
    qa_ref, qb_ref = q_ref.at[PROJ_QG.index("qa")], q_ref.at[PROJ_QG.index("qb")]
    kb_ref, vb_ref = kv_ref.at[PROJ_KV.index("kb")], kv_ref.at[PROJ_KV.index("vb")]
    for hk in range(A_KV_HEADS):
        k = kva_ref[:, hk * HEAD_DIM:(hk + 1) * HEAD_DIM]
        v = kva_ref[:, A_KV + hk * HEAD_DIM:A_KV + (hk + 1) * HEAD_DIM]
        q4 = _stack_group(qa_ref, hk)
        t = _dot_nt(q4, k) * QK_LOG2
        o = _softmax_pv([(t, v)], _sink_column(sink_ref, hk, SEQ))
        _unstack_group(oa_ref, hk, o, SEQ)
    for h in range(B_HEADS):
        sl = slice(h * HEAD_DIM, (h + 1) * HEAD_DIM)
        t = _dot_nt(qb_ref[:, sl], kb_ref[:, sl]) * QK_LOG2
        ob_ref[:, sl] = _softmax_pv([(t, vb_ref[:, sl])], None).astype(BF16)


def _ctx_attn_call(sink, q, kv, kva):
    row = lambda w: pl.BlockSpec((SEQ, w), lambda b: (b, 0))
    return pl.pallas_call(
        _ctx_attn_kernel,
        grid=(BATCH,),
        in_specs=[pl.BlockSpec(memory_space=pltpu.SMEM),
                  pl.BlockSpec((2, SEQ, TN), lambda b: (0, b, 0)),
                  pl.BlockSpec((2, SEQ, TN), lambda b: (0, b, 0)), row(2 * A_KV)],
        out_specs=[row(A_Q), row(B_QKV)],
        out_shape=[jax.ShapeDtypeStruct((M_PROMPT, A_Q), BF16),
                   jax.ShapeDtypeStruct((M_PROMPT, B_QKV), BF16)],
        compiler_params=_params(("arbitrary",), VMEM_LIMIT),
        name="attn_ctx",
    )(sink, q, kv, kva)


BAND = 3 * BLOCK


def _cache_to_bf16(n_heads, ck_ref, cv_ref, ckb, cvb):
    for h in range(n_heads):
        sl = slice(h * HEAD_DIM, (h + 1) * HEAD_DIM)
        ckb[:, sl] = ck_ref[:, h, :].astype(BF16)
        cvb[:, sl] = cv_ref[:, h, :].astype(BF16)


def _win_attn_kernel(sink_ref, q_ref, kv_ref, ck_ref, cv_ref, o_ref, ckb, cvb):
    n = pl.program_id(1)
    pl.when(n == 0)(functools.partial(_cache_to_bf16, A_KV_HEADS, ck_ref, cv_ref, ckb, cvb))
    start = pl.multiple_of(jnp.clip((n - 1) * BLOCK, 0, DEC_SEQ - BAND), BLOCK)
    qpos = n * BLOCK + lax.broadcasted_iota(jnp.int32, (BLOCK, BAND), 0)
    kpos = start + lax.broadcasted_iota(jnp.int32, (BLOCK, BAND), 1)
    valid = jnp.abs(qpos - kpos) <= A_WINDOW
    valid = jnp.concatenate([valid.astype(jnp.int32)] * A_GROUP, axis=0) > 0
    for hk in range(A_KV_HEADS):
        sl = slice(hk * HEAD_DIM, (hk + 1) * HEAD_DIM)
        slv = slice(A_KV + hk * HEAD_DIM, A_KV + (hk + 1) * HEAD_DIM)
        k_loc = kv_ref[pl.ds(start, BAND), sl]
        v_loc = kv_ref[pl.ds(start, BAND), slv]
        q4 = _stack_group(q_ref, hk)
        t_loc = jnp.where(valid, _dot_nt(q4, k_loc) * QK_LOG2, NEG)
        t_ctx = _dot_nt(q4, ckb[:, sl]) * QK_LOG2
        o = _softmax_pv([(t_loc, v_loc), (t_ctx, cvb[:, sl])], _sink_column(sink_ref, hk, BLOCK))
        _unstack_group(o_ref, hk, o, BLOCK)


def _win_attn_call(sink, q, kva, cache_k, cache_v):
    nb = DEC_SEQ // BLOCK
    q0 = M_PROMPT // BLOCK
    b0 = M_PROMPT // DEC_SEQ
    return pl.pallas_call(
        _win_attn_kernel,
        grid=(DEC_BATCH, nb),
        in_specs=[pl.BlockSpec(memory_space=pltpu.SMEM),
                  pl.BlockSpec((None, BLOCK, A_Q), lambda b, n: (PROJ_QG.index("qa"), q0 + b * nb + n, 0)),
                  pl.BlockSpec((DEC_SEQ, 2 * A_KV), lambda b, n: (b0 + b, 0)),
                  pl.BlockSpec((None, None, PAST_LEN, A_KV_HEADS, HEAD_DIM), lambda b, n: (b, 0, 0, 0, 0)),
                  pl.BlockSpec((None, None, PAST_LEN, A_KV_HEADS, HEAD_DIM), lambda b, n: (b, 0, 0, 0, 0))],
        out_specs=pl.BlockSpec((BLOCK, A_Q), lambda b, n: (b * nb + n, 0)),
        out_shape=jax.ShapeDtypeStruct((M_SAMPLE, A_Q), BF16),
        scratch_shapes=[pltpu.VMEM((PAST_LEN, A_KV), BF16), pltpu.VMEM((PAST_LEN, A_KV), BF16)],
        compiler_params=_params(("arbitrary", "arbitrary"), VMEM_LIMIT),
        name="attn_window",
    )(sink, q, kva, cache_k, cache_v)


NA_QROWS = 4
NA_WIN_ROWS = 12
NA_Q = NA_QROWS * GRID_W
NA_KEYS = NA_WIN_ROWS * GRID_W
NA_DR = 2 * NA_ROWS_MAX - 1
NA_PAIRS = NA_DR + 1
RPB_W = 2 * NA_COLS - 1
RPB_PAD = GRID_W - NA_COLS


def _na_row_start(r):
    return jnp.clip(r - NA_ROWS // 2, 0, GRID_ROWS - NA_ROWS)


def _na_window_row0(blk):
    return jnp.clip(blk * NA_QROWS - NA_ROWS // 2, 0, GRID_ROWS - NA_WIN_ROWS)


def _check_na_windows():
    for blk in range(GRID_ROWS // NA_QROWS):
        w0 = int(np.clip(blk * NA_QROWS - NA_ROWS // 2, 0, GRID_ROWS - NA_WIN_ROWS))
        for r in range(blk * NA_QROWS, (blk + 1) * NA_QROWS):
            r0 = int(np.clip(r - NA_ROWS // 2, 0, GRID_ROWS - NA_ROWS))
            assert w0 <= r0 and r0 + NA_ROWS <= w0 + NA_WIN_ROWS, (blk, r)


_check_na_windows()


def _na_attn_kernel(q_ref, k_ref, v_ref, ck_ref, cv_ref, rpb_ref, o_ref, tab_ref, ckb, cvb):
    b, blk = pl.program_id(0), pl.program_id(1)
    lane = lax.broadcasted_iota(jnp.int32, (GRID_W, 2 * GRID_W), 1)
    pl.when(blk == 0)(functools.partial(_cache_to_bf16, B_HEADS, ck_ref, cv_ref, ckb, cvb))

    @pl.when((b == 0) & (blk == 0))
    def _():
        for h in range(B_HEADS):
            for d in range(NA_PAIRS):
                lo = jnp.broadcast_to(rpb_ref[h, d:d + 1, :], (GRID_W, 2 * GRID_W))
                hi = jnp.broadcast_to(rpb_ref[h, d + 1:d + 2, :], (GRID_W, 2 * GRID_W))
                lo = pltpu.roll(lo, GRID_W + 1, 1, stride=1, stride_axis=0)
                hi = pltpu.roll(hi, 1, 1, stride=1, stride_axis=0)
                tab_ref[h, d] = jnp.where(lane < GRID_W, lo, hi) * LOG2E

    row0 = _na_window_row0(blk)
    k0 = pl.multiple_of(row0 * GRID_W, GRID_W)
    qi = lax.broadcasted_iota(jnp.int32, (NA_Q, NA_KEYS), 0)
    ki = lax.broadcasted_iota(jnp.int32, (NA_Q, NA_KEYS), 1)
    qrow, qcol = blk * NA_QROWS + qi // GRID_W, qi % GRID_W
    krow, kcol = row0 + ki // GRID_W, ki % GRID_W
    rstart = _na_row_start(qrow)
    cstart = jnp.clip(qcol - NA_COLS // 2, 0, GRID_W - NA_COLS)
    valid = (krow >= rstart) & (krow < rstart + NA_ROWS) & (kcol >= cstart) & (kcol < cstart + NA_COLS)

    for h in range(B_HEADS):
        sl = slice(h * HEAD_DIM, (h + 1) * HEAD_DIM)
        bias = jnp.concatenate(
            [jnp.concatenate(
                [tab_ref[h, jnp.clip(row0 + 2 * p - blk * NA_QROWS - q + NA_ROWS_MAX, 0, NA_PAIRS - 1)]
                 for p in range(NA_WIN_ROWS // 2)], axis=1)
             for q in range(NA_QROWS)], axis=0)
        k_loc = k_ref[pl.ds(k0, NA_KEYS), sl]
        v_loc = v_ref[pl.ds(k0, NA_KEYS), sl]
        t_loc = jnp.where(valid, _dot_nt(q_ref[:, sl], k_loc) * QK_LOG2 + bias, NEG)
        t_ctx = _dot_nt(q_ref[:, sl], ckb[:, sl]) * QK_LOG2
        o = _softmax_pv([(t_loc, v_loc), (t_ctx, cvb[:, sl])], None)
        o_ref[:, sl] = o.astype(BF16)


def _na_bias_rows(rpb):
    rows = jnp.pad(rpb, ((0, 0), (1, 1), (0, 0)))
    left = jnp.broadcast_to(rows[..., :1], rows.shape[:2] + (RPB_PAD,))
    right = jnp.broadcast_to(rows[..., -1:], rows.shape[:2] + (2 * GRID_W - RPB_PAD - RPB_W,))
    return jnp.concatenate([left, rows, right], axis=-1)


def _na_attn_call(q, kv, cache_k, cache_v, rpb):
    nblk = GRID_ROWS // NA_QROWS
    q0 = M_PROMPT // NA_Q
    b0 = M_PROMPT // DEC_SEQ
    return pl.pallas_call(
        _na_attn_kernel,
        grid=(DEC_BATCH, nblk),
        in_specs=[pl.BlockSpec((None, NA_Q, B_QKV), lambda b, r: (PROJ_QG.index("qb"), q0 + b * nblk + r, 0)),
                  pl.BlockSpec((None, DEC_SEQ, B_QKV), lambda b, r: (PROJ_KV.index("kb"), b0 + b, 0)),
                  pl.BlockSpec((None, DEC_SEQ, B_QKV), lambda b, r: (PROJ_KV.index("vb"), b0 + b, 0)),
                  pl.BlockSpec((None, None, PAST_LEN, B_HEADS, HEAD_DIM), lambda b, r: (b, 0, 0, 0, 0)),
                  pl.BlockSpec((None, None, PAST_LEN, B_HEADS, HEAD_DIM), lambda b, r: (b, 0, 0, 0, 0)),
                  pl.BlockSpec((B_HEADS, NA_DR + 2, 2 * GRID_W), lambda b, r: (0, 0, 0))],
        out_specs=pl.BlockSpec((NA_Q, B_QKV), lambda b, r: (b * nblk + r, 0)),
        out_shape=jax.ShapeDtypeStruct((M_SAMPLE, B_QKV), BF16),
        scratch_shapes=[pltpu.VMEM((B_HEADS, NA_PAIRS, GRID_W, 2 * GRID_W), F32),
                        pltpu.VMEM((PAST_LEN, B_QKV), BF16), pltpu.VMEM((PAST_LEN, B_QKV), BF16)],
        compiler_params=_params(("arbitrary", "arbitrary"), VMEM_LIMIT),
        name="attn_neighbourhood",
    )(q, kv, kv, cache_k, cache_v, _na_bias_rows(rpb))


def _merge_kernel(oap, oas, obp, obs, wa_ref, wb_ref, ga_ref, gb_ref, z_ref, wa_bf, wb_bf):
    i = pl.program_id(1)
    p_tiles = M_PROMPT // TM

    @pl.when(i == 0)
    def _():
        wa_bf[...] = wa_ref[...].astype(BF16)
        wb_bf[...] = wb_ref[...].astype(BF16)

    def run(oa_ref, ob_ref):
        for r0 in range(0, TM, ROW_CHUNK):
            rows = slice(r0, r0 + ROW_CHUNK)
            ya = _dot(oa_ref[rows, :], wa_bf[...])
            yb = _dot(ob_ref[rows, :], wb_bf[...])
            z_ref[rows, :] = (ga_ref[rows, :] * ya + gb_ref[rows, :] * yb).astype(BF16)

    pl.when(i < p_tiles)(functools.partial(run, oap, obp))
    pl.when(i >= p_tiles)(functools.partial(run, oas, obs))


def _merge_call(oa_p, oa_s, ob_p, ob_s, w_br_a, w_br_b, gates):
    p_tiles = M_PROMPT // TM
    nj = D_MODEL // TN
    pspec = pl.BlockSpec((TM, A_Q), lambda j, i: (jnp.minimum(i, p_tiles - 1), 0))
    sspec = pl.BlockSpec((TM, A_Q), lambda j, i: (jnp.maximum(i - p_tiles, 0), 0))
    once = dict(pipeline_mode=pl.Buffered(1))
    return pl.pallas_call(
        _merge_kernel,
        grid=(nj, M_ALL // TM),
        in_specs=[pspec, sspec, pspec, sspec,
                  pl.BlockSpec((A_Q, TN), lambda j, i: (0, j), **once),
                  pl.BlockSpec((B_QKV, TN), lambda j, i: (0, j), **once),
                  pl.BlockSpec((None, TM, TN), lambda j, i: (j, i, 0)),
                  pl.BlockSpec((None, TM, TN), lambda j, i: (nj + j, i, 0))],
        out_specs=pl.BlockSpec((TM, TN), lambda j, i: (i, j)),
        out_shape=jax.ShapeDtypeStruct((M_ALL, D_MODEL), BF16),
        scratch_shapes=[pltpu.VMEM((A_Q, TN), BF16), pltpu.VMEM((B_QKV, TN), BF16)],
        compiler_params=_params(("arbitrary", "arbitrary"), VMEM_LIMIT),
        name="merge_branches",
    )(oa_p, oa_s, ob_p, ob_s, w_br_a, w_br_b, gates, gates)


def _out_kernel(z_ref, w_ref, xp_ref, xs_ref, g_ref, o_ref, wbf):
    i = pl.program_id(1)
    p_tiles = M_PROMPT // TM

    @pl.when(i == 0)
    def _():
        wbf[...] = w_ref[...].astype(BF16)

    def run(x_ref):
        for r0 in range(0, TM, ROW_CHUNK):
            rows = slice(r0, r0 + ROW_CHUNK)
            o_ref[rows, :] = x_ref[rows, :] + g_ref[...] * _dot(z_ref[rows, :], wbf[...])

    pl.when(i < p_tiles)(functools.partial(run, xp_ref))
    pl.when(i >= p_tiles)(functools.partial(run, xs_ref))


def _out_call(z, w_out, xp, xs, mods3):
    p_tiles = M_PROMPT // TM
    nj = D_MODEL // TN
    return pl.pallas_call(
        _out_kernel,
        grid=(nj, M_ALL // TM),
        in_specs=[pl.BlockSpec((TM, D_MODEL), lambda j, i: (i, 0)),
                  pl.BlockSpec((D_MODEL, TN), lambda j, i: (0, j)),
                  pl.BlockSpec((TM, TN), lambda j, i: (jnp.minimum(i, p_tiles - 1), j)),
                  pl.BlockSpec((TM, TN), lambda j, i: (jnp.maximum(i - p_tiles, 0), j)),
                  pl.BlockSpec((None, 1, TN), lambda j, i: (_mod_row(i, TM), 0, (2 * D_MODEL - ADA_HEAD) // TN + j))],
        out_specs=pl.BlockSpec((TM, TN), lambda j, i: (i, j)),
        out_shape=jax.ShapeDtypeStruct((M_ALL, D_MODEL), F32),
        scratch_shapes=[pltpu.VMEM((D_MODEL, TN), BF16)],
        compiler_params=_params(("arbitrary", "arbitrary"), VMEM_LIMIT),
        name="out_proj_residual",
    )(z, w_out, xp, xs, mods3)


def _mlp_kernel(x_hbm, nw_ref, sh_ref, sc_ref, g_ref, wu_ref, wd_ref, o_ref, h_ref, x_buf, x_sem, *, tile0):
    i, f = pl.program_id(0), pl.program_id(1)

    def x_copy(tile):
        return pltpu.make_async_copy(x_hbm.at[pl.ds((tile0 + tile) * TM, TM), :], x_buf, x_sem)

    @pl.when((i == 0) & (f == 0))
    def _():
        x_copy(0).start()

    @pl.when(f == 0)
    def _():
        x_copy(i).wait()
        wu = wu_ref[...].astype(BF16)
        wd = wd_ref[...].astype(BF16)
        for r0 in range(0, TM, ROW_CHUNK):
            rows = slice(r0, r0 + ROW_CHUNK)
            x = x_buf[rows, :]
            h = _modnorm(x, nw_ref[...], sc_ref[...], sh_ref[...]).astype(BF16)
            h_ref[rows, :] = h
            u = jnp.square(jnp.maximum(_dot(h, wu), 0.0)).astype(BF16)
            o_ref[rows, :] = x + g_ref[...] * _dot(u, wd)

    @pl.when((f == 1) & (i + 1 < pl.num_programs(0)))
    def _():
        x_copy(i + 1).start()

    @pl.when(f > 0)
    def _():
        u = _dot(h_ref[...], wu_ref[...].astype(BF16))
        u = jnp.square(jnp.maximum(u, 0.0)).astype(BF16)
        for c0 in range(0, D_MODEL, MLP_COLS):
            cols = slice(c0, c0 + MLP_COLS)
            o_ref[:, cols] += g_ref[:, cols] * _dot(u, wd_ref[:, cols].astype(BF16))


def _mlp_call(x1, nw, mods3, w_up, w_down, tile0, n_tiles):
    mod = lambda k: pl.BlockSpec((None, 1, D_MODEL),
                                 lambda i, f: (_mod_row(tile0 + i, TM), 0, k - ADA_HEAD // D_MODEL))
    return pl.pallas_call(
        functools.partial(_mlp_kernel, tile0=tile0),
        grid=(n_tiles, D_FF // TF),
        in_specs=[pl.BlockSpec(memory_space=pl.ANY),
                  pl.BlockSpec((1, D_MODEL), lambda i, f: (0, 0)),
                  mod(3), mod(4), mod(5),
                  pl.BlockSpec((D_MODEL, TF), lambda i, f: (0, f)),
                  pl.BlockSpec((TF, D_MODEL), lambda i, f: (f, 0))],
        out_specs=pl.BlockSpec((TM, D_MODEL), lambda i, f: (i, 0)),
        out_shape=jax.ShapeDtypeStruct((n_tiles * TM, D_MODEL), F32),
        scratch_shapes=[pltpu.VMEM((TM, D_MODEL), BF16),
                        pltpu.VMEM((TM, D_MODEL), F32),
                        pltpu.SemaphoreType.DMA(())],
        compiler_params=_params(("arbitrary", "arbitrary"), VMEM_LIMIT),
        name="mlp",
    )(x1, nw, mods3, mods3, mods3, w_up, w_down)


def kernel(x_prompt, x_sample, cache_a_k, cache_a_v, cache_b_k, cache_b_v, c, c_ctx, norm1_w, norm2_w, w_ada, b_ada, w_in, q_norm_a, k_norm_a, q_norm_b, k_norm_b, sink_a, rpb_b, w_br_a, w_br_b, w_out, w_up, w_down):
    assert w_ada.shape[0] == 1, "one trunk layer"
    xp = x_prompt.reshape(M_PROMPT, D_MODEL)
    xs = x_sample.reshape(M_SAMPLE, D_MODEL)

    cvecs = jnp.concatenate([c_ctx[None, :], c, jnp.zeros((8 - 1 - DEC_BATCH, D_MODEL), F32)], axis=0)
    mods_head = _ada_call(cvecs, w_ada[0], b_ada).reshape(8, 1, ADA_HEAD)

    w = w_in[0]
    rope = _rope_tables()
    h, kva, ka_p, va_p = _norm_kva_call(xp, xs, norm1_w, mods_head, w, k_norm_a, rope)
    kv, kb_p, vb_p = _proj_call(h, w, q_norm_a, q_norm_b, k_norm_b, rope, PROJ_KV)
    q, gates, mods_tail = _proj_call(h, w, q_norm_a, q_norm_b, k_norm_b, rope, PROJ_QG,
                                     ada=(cvecs, w_ada[0], b_ada))

    sink = sink_a[0]
    oa_p, ob_p = _ctx_attn_call(sink, q, kv, kva)
    oa_s = _win_attn_call(sink, q, kva, cache_a_k, cache_a_v)
    ob_s = _na_attn_call(q, kv, cache_b_k, cache_b_v, rpb_b[0])

    z = _merge_call(oa_p, oa_s, ob_p, ob_s, w_br_a[0], w_br_b[0], gates)
    x1 = _out_call(z, w_out[0], xp, xs, mods_tail)

    p_tiles = M_PROMPT // TM
    y_p = _mlp_call(x1, norm2_w, mods_tail, w_up[0], w_down[0], 0, p_tiles)
    y_s = _mlp_call(x1, norm2_w, mods_tail, w_up[0], w_down[0], p_tiles, M_SAMPLE // TM)

    return (y_p.reshape(BATCH, SEQ, D_MODEL),
            y_s.reshape(DEC_BATCH, DEC_SEQ, D_MODEL),
            ka_p.reshape(BATCH, 1, SEQ, A_KV_HEADS, HEAD_DIM),
            va_p.reshape(BATCH, 1, SEQ, A_KV_HEADS, HEAD_DIM),
            kb_p.reshape(BATCH, 1, SEQ, B_HEADS, HEAD_DIM),
            vb_p.reshape(BATCH, 1, SEQ, B_HEADS, HEAD_DIM))
```

```python
import functools

import numpy as np
import jax
import jax.numpy as jnp
from jax import lax
from jax.experimental import pallas as pl
from jax.experimental.pallas import tpu as pltpu

D_MODEL = 2048
BATCH = 16
SEQ = 256
DEC_BATCH = 2
DEC_SEQ = 1024
PAST_LEN = 256
GRID_W = 64
HEAD_DIM = 128
A_HEADS = 8
A_KV_HEADS = 2
A_GROUP = A_HEADS // A_KV_HEADS
A_WINDOW = 128
BLOCK = 128
B_HEADS = 8
NA_ROWS_MAX = 8
NA_COLS = 16
D_FF = 4 * D_MODEL
ROPE_THETA = 10000.0
EPS = 1e-6
NEG = -1e30
A_Q = A_HEADS * HEAD_DIM
A_KV = A_KV_HEADS * HEAD_DIM
B_QKV = B_HEADS * HEAD_DIM
IN_WIDTH = A_Q + 2 * A_KV + 3 * B_QKV + 2 * D_MODEL
SCALE = HEAD_DIM ** -0.5

M_PROMPT = BATCH * SEQ
M_SAMPLE = DEC_BATCH * DEC_SEQ
M_ALL = M_PROMPT + M_SAMPLE
GRID_ROWS = DEC_SEQ // GRID_W
NA_ROWS = min(NA_ROWS_MAX, GRID_ROWS)

COL_QA = 0
COL_KVA = A_Q
COL_QB = A_Q + 2 * A_KV
COL_KB = COL_QB + B_QKV
COL_VB = COL_KB + B_QKV
COL_G = COL_VB + B_QKV

TM = 1024
TN = 1024
ROW_CHUNK = 256
W_SUB = 512
TM_NORM = 512
TF = 512
MLP_COLS = 512
ADA_HEAD = 2 * D_MODEL
ADA_SIDE_COLS = 1024
ADA_SIDE = 4 * D_MODEL
VMEM_LIMIT = 60 * 1024 * 1024

F32 = jnp.float32
BF16 = jnp.bfloat16


def _mod_row(i, tm):
    p_tiles = M_PROMPT // tm
    return jnp.where(i < p_tiles, 0, 1 + (i - p_tiles) // (DEC_SEQ // tm))


def _row_chunks(tm):
    edges = list(range(0, tm, ROW_CHUNK)) + [tm - ROW_CHUNK // 2, tm]
    edges = sorted(set(edges))
    return [slice(a, b) for a, b in zip(edges[:-1], edges[1:])]


def _dot(a, b):
    return jnp.dot(a, b, preferred_element_type=F32)


def _dot_nt(a, b):
    return lax.dot_general(a, b, (((1,), (1,)), ((), ())), preferred_element_type=F32)


def _params(sem, vmem=None):
    return pltpu.CompilerParams(dimension_semantics=sem, vmem_limit_bytes=vmem)


def _ada_side(s_ref, w_ref, b_ref, o_ref):
    o_ref[:, 0, :] = _dot(s_ref[...], w_ref[...].astype(BF16)) + b_ref[...]


def _ada_kernel(cctx_ref, c_ref, w_ref, b_ref, o_ref, s_ref):
    row = lax.broadcasted_iota(jnp.int32, (8, D_MODEL), 0)
    cv = jnp.where(row == 0, cctx_ref[...], 0.0)
    for b in range(DEC_BATCH):
        cv = jnp.where(row == 1 + b, c_ref[b:b + 1, :], cv)
    s_ref[...] = (cv * jax.nn.sigmoid(cv)).astype(BF16)
    _ada_side(s_ref, w_ref, b_ref, o_ref)


def _ada_call(c_ctx, c, w_ada, b_ada):
    tn = 1024
    n = ADA_HEAD
    return pl.pallas_call(
        _ada_kernel,
        grid=(n // tn,),
        in_specs=[pl.BlockSpec((1, D_MODEL), lambda j: (0, 0)),
                  pl.BlockSpec((DEC_BATCH, D_MODEL), lambda j: (0, 0)),
                  pl.BlockSpec((D_MODEL, tn), lambda j: (0, j)),
                  pl.BlockSpec((1, tn), lambda j: (0, j))],
        out_specs=[pl.BlockSpec((8, 1, tn), lambda j: (0, 0, j)),
                   pl.BlockSpec((8, D_MODEL), lambda j: (0, 0))],
        out_shape=[jax.ShapeDtypeStruct((8, 1, n), F32),
                   jax.ShapeDtypeStruct((8, D_MODEL), BF16)],
        compiler_params=_params(("arbitrary",), VMEM_LIMIT),
        name="ada_mod",
    )(c_ctx, c, w_ada, b_ada)


def _ada_side_specs(first_col, step):
    blk = lambda *g: first_col // ADA_SIDE_COLS + step(*g)
    in_specs = [pl.BlockSpec((8, D_MODEL), lambda *g: (0, 0)),
                pl.BlockSpec((D_MODEL, ADA_SIDE_COLS), lambda *g: (0, blk(*g))),
                pl.BlockSpec((1, ADA_SIDE_COLS), lambda *g: (0, blk(*g)))]
    out_spec = pl.BlockSpec((8, 1, ADA_SIDE_COLS), lambda *g: (0, 0, step(*g)))
    return in_specs, out_spec, jax.ShapeDtypeStruct((8, 1, ADA_SIDE), F32)


def _modnorm(x, nw, sc, sh):
    y = x * lax.rsqrt(jnp.mean(x * x, axis=-1, keepdims=True) + EPS)
    return y * (nw * (1.0 + sc)) + sh


def _head_norm(x, nw):
    return x * lax.rsqrt(jnp.mean(x * x, axis=-1, keepdims=True) + EPS) * nw


def _rope(x, cos, sin_signed):
    lane = lax.broadcasted_iota(jnp.int32, x.shape, 1)
    partner = jnp.where((lane % 64) < 32, pltpu.roll(x, 96, 1), pltpu.roll(x, 32, 1))
    return x * cos + partner * sin_signed


def _norm_kva_kernel(xp_ref, xs_ref, nw1_ref, sh_ref, sc_ref, w_ref, nwk_ref, cos_ref, sin_ref,
                     h_ref, kv_ref, kp_ref, vp_ref, wbf):
    i = pl.program_id(0)
    p_tiles = M_PROMPT // TM_NORM

    @pl.when(i == 0)
    def _():
        wbf[...] = w_ref[...].astype(BF16)

    def head_cols(k, base=0):
        return slice(base + k * HEAD_DIM, base + (k + 1) * HEAD_DIM)

    def run(prompt):
        x_ref = xp_ref if prompt else xs_ref
        for r0 in range(0, TM_NORM, ROW_CHUNK):
            rows = slice(r0, r0 + ROW_CHUNK)
            h = _modnorm(x_ref[rows, :], nw1_ref[...], sc_ref[...], sh_ref[...]).astype(BF16)
            h_ref[rows, :] = h
            acc = _dot(h, wbf[...])
            for k in range(A_KV_HEADS):
                y = _head_norm(acc[:, head_cols(k)], nwk_ref[...])
                v = acc[:, head_cols(k, A_KV)]
                if prompt:
                    kp_ref[rows, k, :] = y
                    vp_ref[rows, k, :] = v
                else:
                    y = _rope(y, cos_ref[rows, :], sin_ref[rows, :])
                kv_ref[rows, head_cols(k)] = y.astype(BF16)
                kv_ref[rows, head_cols(k, A_KV)] = v.astype(BF16)

    pl.when(i < p_tiles)(functools.partial(run, True))
    pl.when(i >= p_tiles)(functools.partial(run, False))


def _norm_kva_call(xp, xs, nw1, mods3, w_in, nwk, rope):
    tm = TM_NORM
    p_tiles = M_PROMPT // tm
    s_tiles = DEC_SEQ // tm
    assert COL_KVA % (2 * A_KV) == 0
    rope_spec = pl.BlockSpec((tm, HEAD_DIM), lambda i: (jnp.maximum(i - p_tiles, 0) % s_tiles, 0))
    parked = pl.BlockSpec((tm, A_KV_HEADS, HEAD_DIM), lambda i: (jnp.minimum(i, p_tiles - 1), 0, 0))
    return pl.pallas_call(
        _norm_kva_kernel,
        grid=(M_ALL // tm,),
        in_specs=[pl.BlockSpec((tm, D_MODEL), lambda i: (jnp.minimum(i, p_tiles - 1), 0)),
                  pl.BlockSpec((tm, D_MODEL), lambda i: (jnp.maximum(i - p_tiles, 0), 0)),
                  pl.BlockSpec((1, D_MODEL), lambda i: (0, 0)),
                  pl.BlockSpec((None, 1, D_MODEL), lambda i: (_mod_row(i, tm), 0, 0)),
                  pl.BlockSpec((None, 1, D_MODEL), lambda i: (_mod_row(i, tm), 0, 1)),
                  pl.BlockSpec((D_MODEL, 2 * A_KV), lambda i: (0, COL_KVA // (2 * A_KV)),
                               pipeline_mode=pl.Buffered(1)),
                  pl.BlockSpec((1, HEAD_DIM), lambda i: (0, 0)),
                  rope_spec, rope_spec],
        out_specs=[pl.BlockSpec((tm, D_MODEL), lambda i: (i, 0)),
                   pl.BlockSpec((tm, 2 * A_KV), lambda i: (i, 0)),
                   parked, parked],
        out_shape=[jax.ShapeDtypeStruct((M_ALL, D_MODEL), BF16),
                   jax.ShapeDtypeStruct((M_ALL, 2 * A_KV), BF16),
                   jax.ShapeDtypeStruct((M_PROMPT, A_KV_HEADS, HEAD_DIM), F32),
                   jax.ShapeDtypeStruct((M_PROMPT, A_KV_HEADS, HEAD_DIM), F32)],
        scratch_shapes=[pltpu.VMEM((D_MODEL, 2 * A_KV), BF16)],
        compiler_params=_params(("arbitrary",), VMEM_LIMIT),
        name="norm1_proj_kva",
    )(xp, xs, nw1, mods3, mods3, w_in, nwk, *rope)


def _rope_tables():
    n_freq = HEAD_DIM // 4
    pos = np.arange(DEC_SEQ)
    row = (pos // GRID_W).astype(np.float64)
    col = (pos % GRID_W).astype(np.float64)
    inv = ROPE_THETA ** (-np.arange(n_freq, dtype=np.float64) / n_freq)
    ar = row[:, None] * inv
    ac = col[:, None] * inv
    cos = np.concatenate([np.cos(ar), np.cos(ar), np.cos(ac), np.cos(ac)], axis=-1)
    sin = np.concatenate([-np.sin(ar), np.sin(ar), -np.sin(ac), np.sin(ac)], axis=-1)
    return jnp.asarray(cos, F32), jnp.asarray(sin, F32)


PROJ_COL = {"qa": COL_QA, "qb": COL_QB, "kb": COL_KB, "vb": COL_VB}
GATE_PANELS = 2 * D_MODEL // TN
PROJ_KV = ("kb", "vb")
PROJ_QG = ("qa", "qb") + ("gate",) * GATE_PANELS


def _proj_kernel(*refs, kinds):
    names = ["h", "w0", "w1", "nqa", "nqb", "nkb", "cos", "sin", "qkv"]
    names += ["gate"] * ("gate" in kinds) + ["kbp"] * ("kb" in kinds) + ["vbp"] * ("vb" in kinds) + ["wbf"]
    r = dict(zip(names, refs, strict=True))
    n_qkv = sum(k != "gate" for k in kinds)
    j, i = pl.program_id(0), pl.program_id(1)
    is_prompt = i < M_PROMPT // TM

    @pl.when(i == 0)
    def _():
        r["wbf"][:, :W_SUB] = r["w0"][...].astype(BF16)
        r["wbf"][:, W_SUB:] = r["w1"][...].astype(BF16)

    def head_cols(k):
        return slice(k * HEAD_DIM, (k + 1) * HEAD_DIM)

    def run(kind, prompt):
        for rows in _row_chunks(TM):
            acc = _dot(r["h"][rows, :], r["wbf"][...])
            if kind == "gate":
                r["gate"][rows, :] = jax.nn.sigmoid(acc)
            elif kind == "vb":
                if prompt:
                    r["vbp"][rows, :] = acc
                r["qkv"][rows, :] = acc.astype(BF16)
            else:
                nw_ref = r[{"qa": "nqa", "qb": "nqb", "kb": "nkb"}[kind]]
                for k in range(TN // HEAD_DIM):
                    y = _head_norm(acc[:, head_cols(k)], nw_ref[...])
                    if kind == "qa" and not prompt:
                        y = _rope(y, r["cos"][rows, :], r["sin"][rows, :])
                    if kind == "kb" and prompt:
                        r["kbp"][rows, head_cols(k)] = y
                    r["qkv"][rows, head_cols(k)] = y.astype(BF16)

    for p, kind in enumerate(kinds[:n_qkv]):
        if kind == "qb":
            pl.when(j == p)(functools.partial(run, kind, None))
        else:
            pl.when((j == p) & is_prompt)(functools.partial(run, kind, True))
            pl.when((j == p) & jnp.logical_not(is_prompt))(functools.partial(run, kind, False))
    if "gate" in kinds:
        pl.when(j >= n_qkv)(functools.partial(run, "gate", None))


def _proj_call(h, w_in, nqa, nqb, nkb, rope, kinds):
    assert TN == 2 * W_SUB and A_Q == B_QKV == TN
    n_panels = len(kinds)
    n_qkv = sum(k != "gate" for k in kinds)
    assert all(k == "gate" for k in kinds[n_qkv:])
    cols = [PROJ_COL[k] for k in kinds[:n_qkv]] + list(range(COL_G, IN_WIDTH, TN))[:n_panels - n_qkv]
    p_tiles = M_PROMPT // TM
    s_tiles = DEC_SEQ // TM
    n_tiles = M_ALL // TM

    def w_sub(j):
        idx = cols[0] // W_SUB
        for p in range(1, n_panels):
            idx = jnp.where(j == p, cols[p] // W_SUB, idx)
        return idx

    def qkv_idx(j, i):
        return (jnp.minimum(j, n_qkv - 1), jnp.where(j >= n_qkv, n_tiles - 1, i), 0)

    def gate_idx(j, i):
        return (jnp.maximum(j - n_qkv, 0), jnp.where(j < n_qkv, 0, i), 0)

    def cache_idx(panel):
        def idx(j, i):
            return (jnp.where(j < panel, 0, jnp.where(j > panel, p_tiles - 1, jnp.minimum(i, p_tiles - 1))), 0)
        return idx

    out_specs = [pl.BlockSpec((None, TM, TN), qkv_idx)]
    out_shape = [jax.ShapeDtypeStruct((n_qkv, M_ALL, TN), BF16)]
    if "gate" in kinds:
        out_specs.append(pl.BlockSpec((None, TM, TN), gate_idx))
        out_shape.append(jax.ShapeDtypeStruct((n_panels - n_qkv, M_ALL, TN), F32))
    for kind in ("kb", "vb"):
        if kind in kinds:
            out_specs.append(pl.BlockSpec((TM, TN), cache_idx(kinds.index(kind))))
            out_shape.append(jax.ShapeDtypeStruct((M_PROMPT, B_QKV), F32))

    norm_spec = pl.BlockSpec((1, HEAD_DIM), lambda j, i: (0, 0))
    rope_spec = pl.BlockSpec((TM, HEAD_DIM), lambda j, i: (jnp.maximum(i - p_tiles, 0) % s_tiles, 0))
    in_specs = [pl.BlockSpec((TM, D_MODEL), lambda j, i: (i, 0)),
                pl.BlockSpec((D_MODEL, W_SUB), lambda j, i: (0, w_sub(j))),
                pl.BlockSpec((D_MODEL, W_SUB), lambda j, i: (0, w_sub(j) + 1)),
                norm_spec, norm_spec, norm_spec, rope_spec, rope_spec]
    args = [h, w_in, w_in, nqa, nqb, nkb, *rope]
    return pl.pallas_call(
        functools.partial(_proj_kernel, kinds=kinds),
        grid=(n_panels, n_tiles),
        in_specs=in_specs,
        out_specs=out_specs,
        out_shape=out_shape,
        scratch_shapes=[pltpu.VMEM((D_MODEL, TN), BF16)],
        compiler_params=_params(("arbitrary", "arbitrary"), VMEM_LIMIT),
        name="proj_" + "_".join(dict.fromkeys(kinds)),
    )(*args)


LOG2E = 1.4426950408889634
QK_LOG2 = SCALE * LOG2E


def _softmax_pv(parts, sink):
    m = functools.reduce(jnp.maximum, [jnp.max(t, axis=-1, keepdims=True) for t, _ in parts])
    if sink is not None:
        m = jnp.maximum(m, sink)
    acc = None
    for t, v in parts:
        p = jnp.exp2(t - m).astype(BF16)
        y = _dot(p, jnp.concatenate([v, jnp.ones_like(v)], axis=1))
        acc = y if acc is None else acc + y
    o, l = acc[:, :HEAD_DIM], acc[:, HEAD_DIM:]
    if sink is not None:
        l = l + jnp.exp2(sink - m)
    return o / l


def _sink_column(sink_ref, hk, rows_per_head):
    n = A_GROUP * rows_per_head
    g = lax.broadcasted_iota(jnp.int32, (n, 1), 0) // rows_per_head
    col = jnp.full((n, 1), sink_ref[hk * A_GROUP], F32)
    for k in range(1, A_GROUP):
        col = jnp.where(g == k, sink_ref[hk * A_GROUP + k], col)
    return col * LOG2E


def _stack_group(q_ref, hk):
    return jnp.concatenate(
        [q_ref[:, (hk * A_GROUP + g) * HEAD_DIM:(hk * A_GROUP + g + 1) * HEAD_DIM] for g in range(A_GROUP)], axis=0)


def _unstack_group(o_ref, hk, o, rows):
    for g in range(A_GROUP):
        c = (hk * A_GROUP + g) * HEAD_DIM
        o_ref[:, c:c + HEAD_DIM] = o[g * rows:(g + 1) * rows].astype(BF16)


def _ctx_attn_kernel(sink_ref, q_ref, kv_ref, kva_ref, oa_ref, ob_ref):
    qa_ref, qb_ref = q_ref.at[PROJ_QG.index("qa")], q_ref.at[PROJ_QG.index("qb")]
    kb_ref, vb_ref = kv_ref.at[PROJ_KV.index("kb")], kv_ref.at[PROJ_KV.index("vb")]
    for hk in range(A_KV_HEADS):
        k = kva_ref[:, hk * HEAD_DIM:(hk + 1) * HEAD_DIM]
        v = kva_ref[:, A_KV + hk * HEAD_DIM:A_KV + (hk + 1) * HEAD_DIM]
        q4 = _stack_group(qa_ref, hk)
        t = _dot_nt(q4, k) * QK_LOG2
        o = _softmax_pv([(t, v)], _sink_column(sink_ref, hk, SEQ))
        _unstack_group(oa_ref, hk, o, SEQ)
    for h in range(B_HEADS):
        sl = slice(h * HEAD_DIM, (h + 1) * HEAD_DIM)
        t = _dot_nt(qb_ref[:, sl], kb_ref[:, sl]) * QK_LOG2
        ob_ref[:, sl] = _softmax_pv([(t, vb_ref[:, sl])], None).astype(BF16)


def _ctx_attn_call(sink, q, kv, kva):
    row = lambda w: pl.BlockSpec((SEQ, w), lambda b: (b, 0))
    return pl.pallas_call(
        _ctx_attn_kernel,
        grid=(BATCH,),
        in_specs=[pl.BlockSpec(memory_space=pltpu.SMEM),
                  pl.BlockSpec((2, SEQ, TN), lambda b: (0, b, 0)),
                  pl.BlockSpec((2, SEQ, TN), lambda b: (0, b, 0)), row(2 * A_KV)],
        out_specs=[row(A_Q), row(B_QKV)],
        out_shape=[jax.ShapeDtypeStruct((M_PROMPT, A_Q), BF16),
                   jax.ShapeDtypeStruct((M_PROMPT, B_QKV), BF16)],
        compiler_params=_params(("arbitrary",), VMEM_LIMIT),
        name="attn_ctx",
    )(sink, q, kv, kva)


BAND = 3 * BLOCK


def _cache_to_bf16(n_heads, ck_ref, cv_ref, ckb, cvb):
    for h in range(n_heads):
        sl = slice(h * HEAD_DIM, (h + 1) * HEAD_DIM)
        ckb[:, sl] = ck_ref[:, h, :].astype(BF16)
        cvb[:, sl] = cv_ref[:, h, :].astype(BF16)


def _win_attn_kernel(sink_ref, q_ref, kv_ref, ck_ref, cv_ref, o_ref, ckb, cvb):
    n = pl.program_id(1)
    pl.when(n == 0)(functools.partial(_cache_to_bf16, A_KV_HEADS, ck_ref, cv_ref, ckb, cvb))
    start = pl.multiple_of(jnp.clip((n - 1) * BLOCK, 0, DEC_SEQ - BAND), BLOCK)
    qpos = n * BLOCK + lax.broadcasted_iota(jnp.int32, (BLOCK, BAND), 0)
    kpos = start + lax.broadcasted_iota(jnp.int32, (BLOCK, BAND), 1)
    valid = jnp.abs(qpos - kpos) <= A_WINDOW
    valid = jnp.concatenate([valid.astype(jnp.int32)] * A_GROUP, axis=0) > 0
    for hk in range(A_KV_HEADS):
        sl = slice(hk * HEAD_DIM, (hk + 1) * HEAD_DIM)
        slv = slice(A_KV + hk * HEAD_DIM, A_KV + (hk + 1) * HEAD_DIM)
        k_loc = kv_ref[pl.ds(start, BAND), sl]
        v_loc = kv_ref[pl.ds(start, BAND), slv]
        q4 = _stack_group(q_ref, hk)
        t_loc = jnp.where(valid, _dot_nt(q4, k_loc) * QK_LOG2, NEG)
        t_ctx = _dot_nt(q4, ckb[:, sl]) * QK_LOG2
        o = _softmax_pv([(t_loc, v_loc), (t_ctx, cvb[:, sl])], _sink_column(sink_ref, hk, BLOCK))
        _unstack_group(o_ref, hk, o, BLOCK)


def _win_attn_call(sink, q, kva, cache_k, cache_v):
    nb = DEC_SEQ // BLOCK
    q0 = M_PROMPT // BLOCK
    b0 = M_PROMPT // DEC_SEQ
    return pl.pallas_call(
        _win_attn_kernel,
        grid=(DEC_BATCH, nb),
        in_specs=[pl.BlockSpec(memory_space=pltpu.SMEM),
                  pl.BlockSpec((None, BLOCK, A_Q), lambda b, n: (PROJ_QG.index("qa"), q0 + b * nb + n, 0)),
                  pl.BlockSpec((DEC_SEQ, 2 * A_KV), lambda b, n: (b0 + b, 0)),
                  pl.BlockSpec((None, None, PAST_LEN, A_KV_HEADS, HEAD_DIM), lambda b, n: (b, 0, 0, 0, 0)),
                  pl.BlockSpec((None, None, PAST_LEN, A_KV_HEADS, HEAD_DIM), lambda b, n: (b, 0, 0, 0, 0))],
        out_specs=pl.BlockSpec((BLOCK, A_Q), lambda b, n: (b * nb + n, 0)),
        out_shape=jax.ShapeDtypeStruct((M_SAMPLE, A_Q), BF16),
        scratch_shapes=[pltpu.VMEM((PAST_LEN, A_KV), BF16), pltpu.VMEM((PAST_LEN, A_KV), BF16)],
        compiler_params=_params(("arbitrary", "arbitrary"), VMEM_LIMIT),
        name="attn_window",
    )(sink, q, kva, cache_k, cache_v)


NA_QROWS = 4
NA_WIN_ROWS = 12
NA_Q = NA_QROWS * GRID_W
NA_KEYS = NA_WIN_ROWS * GRID_W
NA_DR = 2 * NA_ROWS_MAX - 1
NA_PAIRS = NA_DR + 1
RPB_W = 2 * NA_COLS - 1
RPB_PAD = GRID_W - NA_COLS


def _na_row_start(r):
    return jnp.clip(r - NA_ROWS // 2, 0, GRID_ROWS - NA_ROWS)


def _na_window_row0(blk):
    return jnp.clip(blk * NA_QROWS - NA_ROWS // 2, 0, GRID_ROWS - NA_WIN_ROWS)


def _check_na_windows():
    for blk in range(GRID_ROWS // NA_QROWS):
        w0 = int(np.clip(blk * NA_QROWS - NA_ROWS // 2, 0, GRID_ROWS - NA_WIN_ROWS))
        for r in range(blk * NA_QROWS, (blk + 1) * NA_QROWS):
            r0 = int(np.clip(r - NA_ROWS // 2, 0, GRID_ROWS - NA_ROWS))
            assert w0 <= r0 and r0 + NA_ROWS <= w0 + NA_WIN_ROWS, (blk, r)


_check_na_windows()


def _na_attn_kernel(q_ref, k_ref, v_ref, ck_ref, cv_ref, rpb_ref, s_ref, wada_ref, bada_ref,
                    o_ref, mods_ref, tab_ref, ckb, cvb):
    b, blk = pl.program_id(0), pl.program_id(1)
    _ada_side(s_ref, wada_ref, bada_ref, mods_ref)
    lane = lax.broadcasted_iota(jnp.int32, (GRID_W, 2 * GRID_W), 1)
    pl.when(blk == 0)(functools.partial(_cache_to_bf16, B_HEADS, ck_ref, cv_ref, ckb, cvb))

    @pl.when((b == 0) & (blk == 0))
    def _():
        for h in range(B_HEADS):
            for d in range(NA_PAIRS):
                lo = jnp.broadcast_to(rpb_ref[h, d:d + 1, :], (GRID_W, 2 * GRID_W))
                hi = jnp.broadcast_to(rpb_ref[h, d + 1:d + 2, :], (GRID_W, 2 * GRID_W))
                lo = pltpu.roll(lo, GRID_W + 1, 1, stride=1, stride_axis=0)
                hi = pltpu.roll(hi, 1, 1, stride=1, stride_axis=0)
                tab_ref[h, d] = jnp.where(lane < GRID_W, lo, hi) * LOG2E

    row0 = _na_window_row0(blk)
    k0 = pl.multiple_of(row0 * GRID_W, GRID_W)
    qi = lax.broadcasted_iota(jnp.int32, (NA_Q, NA_KEYS), 0)
    ki = lax.broadcasted_iota(jnp.int32, (NA_Q, NA_KEYS), 1)
    qrow, qcol = blk * NA_QROWS + qi // GRID_W, qi % GRID_W
    krow, kcol = row0 + ki // GRID_W, ki % GRID_W
    rstart = _na_row_start(qrow)
    cstart = jnp.clip(qcol - NA_COLS // 2, 0, GRID_W - NA_COLS)
    valid = (krow >= rstart) & (krow < rstart + NA_ROWS) & (kcol >= cstart) & (kcol < cstart + NA_COLS)

    for h in range(B_HEADS):
        sl = slice(h * HEAD_DIM, (h + 1) * HEAD_DIM)
        bias = jnp.concatenate(
            [jnp.concatenate(
                [tab_ref[h, jnp.clip(row0 + 2 * p - blk * NA_QROWS - q + NA_ROWS_MAX, 0, NA_PAIRS - 1)]
                 for p in range(NA_WIN_ROWS // 2)], axis=1)
             for q in range(NA_QROWS)], axis=0)
        k_loc = k_ref[pl.ds(k0, NA_KEYS), sl]
        v_loc = v_ref[pl.ds(k0, NA_KEYS), sl]
        t_loc = jnp.where(valid, _dot_nt(q_ref[:, sl], k_loc) * QK_LOG2 + bias, NEG)
        t_ctx = _dot_nt(q_ref[:, sl], ckb[:, sl]) * QK_LOG2
        o = _softmax_pv([(t_loc, v_loc), (t_ctx, cvb[:, sl])], None)
        o_ref[:, sl] = o.astype(BF16)


def _na_bias_rows(rpb):
    rows = jnp.pad(rpb, ((0, 0), (1, 1), (0, 0)))
    left = jnp.broadcast_to(rows[..., :1], rows.shape[:2] + (RPB_PAD,))
    right = jnp.broadcast_to(rows[..., -1:], rows.shape[:2] + (2 * GRID_W - RPB_PAD - RPB_W,))
    return jnp.concatenate([left, rows, right], axis=-1)


def _na_attn_call(q, kv, cache_k, cache_v, rpb, ada):
    nblk = GRID_ROWS // NA_QROWS
    assert DEC_BATCH * nblk * ADA_SIDE_COLS == ADA_SIDE and ADA_HEAD + ADA_SIDE == 6 * D_MODEL
    side_in, side_out, side_shape = _ada_side_specs(ADA_HEAD, lambda b, r: b * nblk + r)
    q0 = M_PROMPT // NA_Q
    b0 = M_PROMPT // DEC_SEQ
    return pl.pallas_call(
        _na_attn_kernel,
        grid=(DEC_BATCH, nblk),
        in_specs=[pl.BlockSpec((None, NA_Q, B_QKV), lambda b, r: (PROJ_QG.index("qb"), q0 + b * nblk + r, 0)),
                  pl.BlockSpec((None, DEC_SEQ, B_QKV), lambda b, r: (PROJ_KV.index("kb"), b0 + b, 0)),
                  pl.BlockSpec((None, DEC_SEQ, B_QKV), lambda b, r: (PROJ_KV.index("vb"), b0 + b, 0)),
                  pl.BlockSpec((None, None, PAST_LEN, B_HEADS, HEAD_DIM), lambda b, r: (b, 0, 0, 0, 0)),
                  pl.BlockSpec((None, None, PAST_LEN, B_HEADS, HEAD_DIM), lambda b, r: (b, 0, 0, 0, 0)),
                  pl.BlockSpec((B_HEADS, NA_DR + 2, 2 * GRID_W), lambda b, r: (0, 0, 0))] + side_in,
        out_specs=[pl.BlockSpec((NA_Q, B_QKV), lambda b, r: (b * nblk + r, 0)), side_out],
        out_shape=[jax.ShapeDtypeStruct((M_SAMPLE, B_QKV), BF16), side_shape],
        scratch_shapes=[pltpu.VMEM((B_HEADS, NA_PAIRS, GRID_W, 2 * GRID_W), F32),
                        pltpu.VMEM((PAST_LEN, B_QKV), BF16), pltpu.VMEM((PAST_LEN, B_QKV), BF16)],
        compiler_params=_params(("arbitrary", "arbitrary"), VMEM_LIMIT),
        name="attn_neighbourhood",
    )(q, kv, kv, cache_k, cache_v, _na_bias_rows(rpb), *ada)


def _merge_kernel(oap, oas, obp, obs, wa_ref, wb_ref, ga_ref, gb_ref, z_ref, wa_bf, wb_bf):
    i = pl.program_id(1)
    p_tiles = M_PROMPT // TM

    @pl.when(i == 0)
    def _():
        wa_bf[...] = wa_ref[...].astype(BF16)
        wb_bf[...] = wb_ref[...].astype(BF16)

    def run(oa_ref, ob_ref):
        for rows in _row_chunks(TM):
            ya = _dot(oa_ref[rows, :], wa_bf[...])
            yb = _dot(ob_ref[rows, :], wb_bf[...])
            z_ref[rows, :] = (ga_ref[rows, :] * ya + gb_ref[rows, :] * yb).astype(BF16)

    pl.when(i < p_tiles)(functools.partial(run, oap, obp))
    pl.when(i >= p_tiles)(functools.partial(run, oas, obs))


def _merge_call(oa_p, oa_s, ob_p, ob_s, w_br_a, w_br_b, gates):
    p_tiles = M_PROMPT // TM
    nj = D_MODEL // TN
    pspec = pl.BlockSpec((TM, A_Q), lambda j, i: (jnp.minimum(i, p_tiles - 1), 0))
    sspec = pl.BlockSpec((TM, A_Q), lambda j, i: (jnp.maximum(i - p_tiles, 0), 0))
    once = dict(pipeline_mode=pl.Buffered(1))
    return pl.pallas_call(
        _merge_kernel,
        grid=(nj, M_ALL // TM),
        in_specs=[pspec, sspec, pspec, sspec,
                  pl.BlockSpec((A_Q, TN), lambda j, i: (0, j), **once),
                  pl.BlockSpec((B_QKV, TN), lambda j, i: (0, j), **once),
                  pl.BlockSpec((None, TM, TN), lambda j, i: (j, i, 0)),
                  pl.BlockSpec((None, TM, TN), lambda j, i: (nj + j, i, 0))],
        out_specs=pl.BlockSpec((TM, TN), lambda j, i: (i, j)),
        out_shape=jax.ShapeDtypeStruct((M_ALL, D_MODEL), BF16),
        scratch_shapes=[pltpu.VMEM((A_Q, TN), BF16), pltpu.VMEM((B_QKV, TN), BF16)],
        compiler_params=_params(("arbitrary", "arbitrary"), VMEM_LIMIT),
        name="merge_branches",
    )(oa_p, oa_s, ob_p, ob_s, w_br_a, w_br_b, gates, gates)


def _out_kernel(z_ref, w_ref, xp_ref, xs_ref, g_ref, o_ref, wbf):
    i = pl.program_id(1)
    p_tiles = M_PROMPT // TM

    @pl.when(i == 0)
    def _():
        wbf[...] = w_ref[...].astype(BF16)

    def run(x_ref):
        for rows in _row_chunks(TM):
            o_ref[rows, :] = x_ref[rows, :] + g_ref[...] * _dot(z_ref[rows, :], wbf[...])

    pl.when(i < p_tiles)(functools.partial(run, xp_ref))
    pl.when(i >= p_tiles)(functools.partial(run, xs_ref))


def _out_call(z, w_out, xp, xs, mods_tail):
    p_tiles = M_PROMPT // TM
    nj = D_MODEL // TN
    return pl.pallas_call(
        _out_kernel,
        grid=(nj, M_ALL // TM),
        in_specs=[pl.BlockSpec((TM, D_MODEL), lambda j, i: (i, 0)),
                  pl.BlockSpec((D_MODEL, TN), lambda j, i: (0, j)),
                  pl.BlockSpec((TM, TN), lambda j, i: (jnp.minimum(i, p_tiles - 1), j)),
                  pl.BlockSpec((TM, TN), lambda j, i: (jnp.maximum(i - p_tiles, 0), j)),
                  pl.BlockSpec((None, 1, TN), lambda j, i: (_mod_row(i, TM), 0, j))],
        out_specs=pl.BlockSpec((TM, TN), lambda j, i: (i, j)),
        out_shape=jax.ShapeDtypeStruct((M_ALL, D_MODEL), F32),
        scratch_shapes=[pltpu.VMEM((D_MODEL, TN), BF16)],
        compiler_params=_params(("arbitrary", "arbitrary"), VMEM_LIMIT),
        name="out_proj_residual",
    )(z, w_out, xp, xs, mods_tail)


def _mlp_kernel(x_hbm, nw_ref, sh_ref, sc_ref, g_ref, wu_ref, wd_ref, o_ref, h_ref, x_buf, x_sem, *, tile0):
    i, f = pl.program_id(0), pl.program_id(1)

    def x_copy(tile):
        return pltpu.make_async_copy(x_hbm.at[pl.ds((tile0 + tile) * TM, TM), :], x_buf, x_sem)

    @pl.when((i == 0) & (f == 0))
    def _():
        x_copy(0).start()

    @pl.when(f == 0)
    def _():
        x_copy(i).wait()
        wu = wu_ref[...].astype(BF16)
        wd = wd_ref[...].astype(BF16)
        for r0 in range(0, TM, ROW_CHUNK):
            rows = slice(r0, r0 + ROW_CHUNK)
            x = x_buf[rows, :]
            h = _modnorm(x, nw_ref[...], sc_ref[...], sh_ref[...]).astype(BF16)
            h_ref[rows, :] = h
            u = jnp.square(jnp.maximum(_dot(h, wu), 0.0)).astype(BF16)
            o_ref[rows, :] = x + g_ref[...] * _dot(u, wd)

    @pl.when((f == 1) & (i + 1 < pl.num_programs(0)))
    def _():
        x_copy(i + 1).start()

    @pl.when(f > 0)
    def _():
        u = _dot(h_ref[...], wu_ref[...].astype(BF16))
        u = jnp.square(jnp.maximum(u, 0.0)).astype(BF16)
        for c0 in range(0, D_MODEL, MLP_COLS):
            cols = slice(c0, c0 + MLP_COLS)
            o_ref[:, cols] += g_ref[:, cols] * _dot(u, wd_ref[:, cols].astype(BF16))


def _mlp_call(x1, nw, mods_tail, w_up, w_down, tile0, n_tiles):
    mod = lambda k: pl.BlockSpec((None, 1, D_MODEL), lambda i, f: (_mod_row(tile0 + i, TM), 0, k))
    return pl.pallas_call(
        functools.partial(_mlp_kernel, tile0=tile0),
        grid=(n_tiles, D_FF // TF),
        in_specs=[pl.BlockSpec(memory_space=pl.ANY),
                  pl.BlockSpec((1, D_MODEL), lambda i, f: (0, 0)),
                  mod(1), mod(2), mod(3),
                  pl.BlockSpec((D_MODEL, TF), lambda i, f: (0, f)),
                  pl.BlockSpec((TF, D_MODEL), lambda i, f: (f, 0))],
        out_specs=pl.BlockSpec((TM, D_MODEL), lambda i, f: (i, 0)),
        out_shape=jax.ShapeDtypeStruct((n_tiles * TM, D_MODEL), F32),
        scratch_shapes=[pltpu.VMEM((TM, D_MODEL), BF16),
                        pltpu.VMEM((TM, D_MODEL), F32),
                        pltpu.SemaphoreType.DMA(())],
        compiler_params=_params(("arbitrary", "arbitrary"), VMEM_LIMIT),
        name="mlp",
    )(x1, nw, mods_tail, mods_tail, mods_tail, w_up, w_down)


def kernel(x_prompt, x_sample, cache_a_k, cache_a_v, cache_b_k, cache_b_v, c, c_ctx, norm1_w, norm2_w, w_ada, b_ada, w_in, q_norm_a, k_norm_a, q_norm_b, k_norm_b, sink_a, rpb_b, w_br_a, w_br_b, w_out, w_up, w_down):
    assert w_ada.shape[0] == 1, "one trunk layer"
    xp = x_prompt.reshape(M_PROMPT, D_MODEL)
    xs = x_sample.reshape(M_SAMPLE, D_MODEL)

    mods_head, cond = _ada_call(c_ctx[None, :], c, w_ada[0], b_ada)
    ada = (cond, w_ada[0], b_ada)

    w = w_in[0]
    rope = _rope_tables()
    h, kva, ka_p, va_p = _norm_kva_call(xp, xs, norm1_w, mods_head, w, k_norm_a, rope)
    kv, kb_p, vb_p = _proj_call(h, w, q_norm_a, q_norm_b, k_norm_b, rope, PROJ_KV)
    q, gates = _proj_call(h, w, q_norm_a, q_norm_b, k_norm_b, rope, PROJ_QG)

    sink = sink_a[0]
    oa_p, ob_p = _ctx_attn_call(sink, q, kv, kva)
    oa_s = _win_attn_call(sink, q, kva, cache_a_k, cache_a_v)
    ob_s, mods_tail = _na_attn_call(q, kv, cache_b_k, cache_b_v, rpb_b[0], ada)

    z = _merge_call(oa_p, oa_s, ob_p, ob_s, w_br_a[0], w_br_b[0], gates)
    x1 = _out_call(z, w_out[0], xp, xs, mods_tail)

    p_tiles = M_PROMPT // TM
    y_p = _mlp_call(x1, norm2_w, mods_tail, w_up[0], w_down[0], 0, p_tiles)
    y_s = _mlp_call(x1, norm2_w, mods_tail, w_up[0], w_down[0], p_tiles, M_SAMPLE // TM)

    return (y_p.reshape(BATCH, SEQ, D_MODEL),
            y_s.reshape(DEC_BATCH, DEC_SEQ, D_MODEL),
            ka_p.reshape(BATCH, 1, SEQ, A_KV_HEADS, HEAD_DIM),
            va_p.reshape(BATCH, 1, SEQ, A_KV_HEADS, HEAD_DIM),
            kb_p.reshape(BATCH, 1, SEQ, B_HEADS, HEAD_DIM),
            vb_p.reshape(BATCH, 1, SEQ, B_HEADS, HEAD_DIM))
```

```python
import functools

import numpy as np
import jax
import jax.numpy as jnp
from jax import lax
from jax.experimental import pallas as pl
from jax.experimental.pallas import tpu as pltpu

D_MODEL = 2048
BATCH = 16
SEQ = 256
DEC_BATCH = 2
DEC_SEQ = 1024
PAST_LEN = 256
GRID_W = 64
HEAD_DIM = 128
A_HEADS = 8
A_KV_HEADS = 2
A_GROUP = A_HEADS // A_KV_HEADS
A_WINDOW = 128
BLOCK = 128
B_HEADS = 8
NA_ROWS_MAX = 8
NA_COLS = 16
D_FF = 4 * D_MODEL
ROPE_THETA = 10000.0
EPS = 1e-6
NEG = -1e30
A_Q = A_HEADS * HEAD_DIM
A_KV = A_KV_HEADS * HEAD_DIM
B_QKV = B_HEADS * HEAD_DIM
IN_WIDTH = A_Q + 2 * A_KV + 3 * B_QKV + 2 * D_MODEL
SCALE = HEAD_DIM ** -0.5

M_PROMPT = BATCH * SEQ
M_SAMPLE = DEC_BATCH * DEC_SEQ
M_ALL = M_PROMPT + M_SAMPLE
GRID_ROWS = DEC_SEQ // GRID_W
NA_ROWS = min(NA_ROWS_MAX, GRID_ROWS)

COL_QA = 0
COL_KVA = A_Q
COL_QB = A_Q + 2 * A_KV
COL_KB = COL_QB + B_QKV
COL_VB = COL_KB + B_QKV
COL_G = COL_VB + B_QKV

TM = 1024
TN = 1024
ROW_CHUNK = 256
W_SUB = 512
TM_NORM = 512
TF = 512
MLP_COLS = 512
ADA_HEAD = 2 * D_MODEL
ADA_SIDE_COLS = 1024
ADA_SIDE = 4 * D_MODEL
VMEM_LIMIT = 60 * 1024 * 1024

F32 = jnp.float32
BF16 = jnp.bfloat16


def _mod_row(i, tm):
    p_tiles = M_PROMPT // tm
    return jnp.where(i < p_tiles, 0, 1 + (i - p_tiles) // (DEC_SEQ // tm))


def _row_chunks(tm):
    edges = list(range(0, tm, ROW_CHUNK)) + [tm - ROW_CHUNK // 2, tm]
    edges = sorted(set(edges))
    return [slice(a, b) for a, b in zip(edges[:-1], edges[1:])]


def _dot(a, b):
    return jnp.dot(a, b, preferred_element_type=F32)


def _dot_nt(a, b):
    return lax.dot_general(a, b, (((1,), (1,)), ((), ())), preferred_element_type=F32)


def _params(sem, vmem=None):
    return pltpu.CompilerParams(dimension_semantics=sem, vmem_limit_bytes=vmem)


def _ada_side(s_ref, w_ref, b_ref, o_ref):
    o_ref[:, 0, :] = _dot(s_ref[...], w_ref[...].astype(BF16)) + b_ref[...]


def _ada_kernel(cctx_ref, c_ref, w_ref, b_ref, o_ref, s_ref):
    row = lax.broadcasted_iota(jnp.int32, (8, D_MODEL), 0)
    cv = jnp.where(row == 0, cctx_ref[...], 0.0)
    for b in range(DEC_BATCH):
        cv = jnp.where(row == 1 + b, c_ref[b:b + 1, :], cv)
    s_ref[...] = (cv * jax.nn.sigmoid(cv)).astype(BF16)
    _ada_side(s_ref, w_ref, b_ref, o_ref)


def _ada_call(c_ctx, c, w_ada, b_ada):
    tn = 1024
    n = ADA_HEAD
    return pl.pallas_call(
        _ada_kernel,
        grid=(n // tn,),
        in_specs=[pl.BlockSpec((1, D_MODEL), lambda j: (0, 0)),
                  pl.BlockSpec((DEC_BATCH, D_MODEL), lambda j: (0, 0)),
                  pl.BlockSpec((D_MODEL, tn), lambda j: (0, j)),
                  pl.BlockSpec((1, tn), lambda j: (0, j))],
        out_specs=[pl.BlockSpec((8, 1, tn), lambda j: (0, 0, j)),
                   pl.BlockSpec((8, D_MODEL), lambda j: (0, 0))],
        out_shape=[jax.ShapeDtypeStruct((8, 1, n), F32),
                   jax.ShapeDtypeStruct((8, D_MODEL), BF16)],
        compiler_params=_params(("arbitrary",), VMEM_LIMIT),
        name="ada_mod",
    )(c_ctx, c, w_ada, b_ada)


def _ada_side_specs(first_col, step):
    blk = lambda *g: first_col // ADA_SIDE_COLS + step(*g)
    in_specs = [pl.BlockSpec((8, D_MODEL), lambda *g: (0, 0)),
                pl.BlockSpec((D_MODEL, ADA_SIDE_COLS), lambda *g: (0, blk(*g))),
                pl.BlockSpec((1, ADA_SIDE_COLS), lambda *g: (0, blk(*g)))]
    out_spec = pl.BlockSpec((8, 1, ADA_SIDE_COLS), lambda *g: (0, 0, step(*g)))
    return in_specs, out_spec, jax.ShapeDtypeStruct((8, 1, ADA_SIDE), F32)


def _modnorm(x, nw, sc, sh):
    y = x * lax.rsqrt(jnp.mean(x * x, axis=-1, keepdims=True) + EPS)
    return y * (nw * (1.0 + sc)) + sh


def _head_norm(x, nw):
    return x * lax.rsqrt(jnp.mean(x * x, axis=-1, keepdims=True) + EPS) * nw


def _rope(x, cos, sin_signed):
    lane = lax.broadcasted_iota(jnp.int32, x.shape, 1)
    partner = jnp.where((lane % 64) < 32, pltpu.roll(x, 96, 1), pltpu.roll(x, 32, 1))
    return x * cos + partner * sin_signed


def _norm_kva_kernel(xp_ref, xs_ref, nw1_ref, sh_ref, sc_ref, w_ref, nwk_ref, cos_ref, sin_ref,
                     h_ref, kv_ref, kp_ref, vp_ref, wbf):
    i = pl.program_id(0)
    p_tiles = M_PROMPT // TM_NORM

    @pl.when(i == 0)
    def _():
        wbf[...] = w_ref[...].astype(BF16)

    def head_cols(k, base=0):
        return slice(base + k * HEAD_DIM, base + (k + 1) * HEAD_DIM)

    def run(prompt):
        x_ref = xp_ref if prompt else xs_ref
        for r0 in range(0, TM_NORM, ROW_CHUNK):
            rows = slice(r0, r0 + ROW_CHUNK)
            h = _modnorm(x_ref[rows, :], nw1_ref[...], sc_ref[...], sh_ref[...]).astype(BF16)
            h_ref[rows, :] = h
            acc = _dot(h, wbf[...])
            for k in range(A_KV_HEADS):
                y = _head_norm(acc[:, head_cols(k)], nwk_ref[...])
                v = acc[:, head_cols(k, A_KV)]
                if prompt:
                    kp_ref[rows, k, :] = y
                    vp_ref[rows, k, :] = v
                else:
                    y = _rope(y, cos_ref[rows, :], sin_ref[rows, :])
                kv_ref[rows, head_cols(k)] = y.astype(BF16)
                kv_ref[rows, head_cols(k, A_KV)] = v.astype(BF16)

    pl.when(i < p_tiles)(functools.partial(run, True))
    pl.when(i >= p_tiles)(functools.partial(run, False))


def _norm_kva_call(xp, xs, nw1, mods3, w_in, nwk, rope):
    tm = TM_NORM
    p_tiles = M_PROMPT // tm
    s_tiles = DEC_SEQ // tm
    assert COL_KVA % (2 * A_KV) == 0
    rope_spec = pl.BlockSpec((tm, HEAD_DIM), lambda i: (jnp.maximum(i - p_tiles, 0) % s_tiles, 0))
    parked = pl.BlockSpec((tm, A_KV_HEADS, HEAD_DIM), lambda i: (jnp.minimum(i, p_tiles - 1), 0, 0))
    return pl.pallas_call(
        _norm_kva_kernel,
        grid=(M_ALL // tm,),
        in_specs=[pl.BlockSpec((tm, D_MODEL), lambda i: (jnp.minimum(i, p_tiles - 1), 0)),
                  pl.BlockSpec((tm, D_MODEL), lambda i: (jnp.maximum(i - p_tiles, 0), 0)),
                  pl.BlockSpec((1, D_MODEL), lambda i: (0, 0)),
                  pl.BlockSpec((None, 1, D_MODEL), lambda i: (_mod_row(i, tm), 0, 0)),
                  pl.BlockSpec((None, 1, D_MODEL), lambda i: (_mod_row(i, tm), 0, 1)),
                  pl.BlockSpec((D_MODEL, 2 * A_KV), lambda i: (0, COL_KVA // (2 * A_KV)),
                               pipeline_mode=pl.Buffered(1)),
                  pl.BlockSpec((1, HEAD_DIM), lambda i: (0, 0)),
                  rope_spec, rope_spec],
        out_specs=[pl.BlockSpec((tm, D_MODEL), lambda i: (i, 0)),
                   pl.BlockSpec((tm, 2 * A_KV), lambda i: (i, 0)),
                   parked, parked],
        out_shape=[jax.ShapeDtypeStruct((M_ALL, D_MODEL), BF16),
                   jax.ShapeDtypeStruct((M_ALL, 2 * A_KV), BF16),
                   jax.ShapeDtypeStruct((M_PROMPT, A_KV_HEADS, HEAD_DIM), F32),
                   jax.ShapeDtypeStruct((M_PROMPT, A_KV_HEADS, HEAD_DIM), F32)],
        scratch_shapes=[pltpu.VMEM((D_MODEL, 2 * A_KV), BF16)],
        compiler_params=_params(("arbitrary",), VMEM_LIMIT),
        name="norm1_proj_kva",
    )(xp, xs, nw1, mods3, mods3, w_in, nwk, *rope)


def _rope_tables():
    n_freq = HEAD_DIM // 4
    pos = np.arange(DEC_SEQ)
    row = (pos // GRID_W).astype(np.float64)
    col = (pos % GRID_W).astype(np.float64)
    inv = ROPE_THETA ** (-np.arange(n_freq, dtype=np.float64) / n_freq)
    ar = row[:, None] * inv
    ac = col[:, None] * inv
    cos = np.concatenate([np.cos(ar), np.cos(ar), np.cos(ac), np.cos(ac)], axis=-1)
    sin = np.concatenate([-np.sin(ar), np.sin(ar), -np.sin(ac), np.sin(ac)], axis=-1)
    return jnp.asarray(cos, F32), jnp.asarray(sin, F32)


PROJ_COL = {"qa": COL_QA, "qb": COL_QB, "kb": COL_KB, "vb": COL_VB}
GATE_PANELS = 2 * D_MODEL // TN
PROJ_KV = ("kb", "vb")
PROJ_QG = ("qa", "qb") + ("gate",) * GATE_PANELS


def _proj_kernel(*refs, kinds):
    names = ["h", "w0", "w1", "nqa", "nqb", "nkb", "cos", "sin", "qkv"]
    names += ["gate"] * ("gate" in kinds) + ["kbp"] * ("kb" in kinds) + ["vbp"] * ("vb" in kinds) + ["wbf"]
    r = dict(zip(names, refs, strict=True))
    n_qkv = sum(k != "gate" for k in kinds)
    j, i = pl.program_id(0), pl.program_id(1)
    is_prompt = i < M_PROMPT // TM

    @pl.when(i == 0)
    def _():
        r["wbf"][:, :W_SUB] = r["w0"][...].astype(BF16)
        r["wbf"][:, W_SUB:] = r["w1"][...].astype(BF16)

    def head_cols(k):
        return slice(k * HEAD_DIM, (k + 1) * HEAD_DIM)

    def run(kind, prompt):
        for rows in _row_chunks(TM):
            acc = _dot(r["h"][rows, :], r["wbf"][...])
            if kind == "gate":
                r["gate"][rows, :] = jax.nn.sigmoid(acc)
            elif kind == "vb":
                if prompt:
                    r["vbp"][rows, :] = acc
                r["qkv"][rows, :] = acc.astype(BF16)
            else:
                nw_ref = r[{"qa": "nqa", "qb": "nqb", "kb": "nkb"}[kind]]
                for k in range(TN // HEAD_DIM):
                    y = _head_norm(acc[:, head_cols(k)], nw_ref[...])
                    if kind == "qa" and not prompt:
                        y = _rope(y, r["cos"][rows, :], r["sin"][rows, :])
                    if kind == "kb" and prompt:
                        r["kbp"][rows, head_cols(k)] = y
                    r["qkv"][rows, head_cols(k)] = y.astype(BF16)

    for p, kind in enumerate(kinds[:n_qkv]):
        if kind == "qb":
            pl.when(j == p)(functools.partial(run, kind, None))
        else:
            pl.when((j == p) & is_prompt)(functools.partial(run, kind, True))
            pl.when((j == p) & jnp.logical_not(is_prompt))(functools.partial(run, kind, False))
    if "gate" in kinds:
        pl.when(j >= n_qkv)(functools.partial(run, "gate", None))


def _proj_call(h, w_in, nqa, nqb, nkb, rope, kinds):
    assert TN == 2 * W_SUB and A_Q == B_QKV == TN
    n_panels = len(kinds)
    n_qkv = sum(k != "gate" for k in kinds)
    assert all(k == "gate" for k in kinds[n_qkv:])
    cols = [PROJ_COL[k] for k in kinds[:n_qkv]] + list(range(COL_G, IN_WIDTH, TN))[:n_panels - n_qkv]
    p_tiles = M_PROMPT // TM
    s_tiles = DEC_SEQ // TM
    n_tiles = M_ALL // TM

    def w_sub(j):
        idx = cols[0] // W_SUB
        for p in range(1, n_panels):
            idx = jnp.where(j == p, cols[p] // W_SUB, idx)
        return idx

    def qkv_idx(j, i):
        return (jnp.minimum(j, n_qkv - 1), jnp.where(j >= n_qkv, n_tiles - 1, i), 0)

    def gate_idx(j, i):
        return (jnp.maximum(j - n_qkv, 0), jnp.where(j < n_qkv, 0, i), 0)

    def cache_idx(panel):
        def idx(j, i):
            return (jnp.where(j < panel, 0, jnp.where(j > panel, p_tiles - 1, jnp.minimum(i, p_tiles - 1))), 0)
        return idx

    out_specs = [pl.BlockSpec((None, TM, TN), qkv_idx)]
    out_shape = [jax.ShapeDtypeStruct((n_qkv, M_ALL, TN), BF16)]
    if "gate" in kinds:
        out_specs.append(pl.BlockSpec((None, TM, TN), gate_idx))
        out_shape.append(jax.ShapeDtypeStruct((n_panels - n_qkv, M_ALL, TN), F32))
    for kind in ("kb", "vb"):
        if kind in kinds:
            out_specs.append(pl.BlockSpec((TM, TN), cache_idx(kinds.index(kind))))
            out_shape.append(jax.ShapeDtypeStruct((M_PROMPT, B_QKV), F32))

    norm_spec = pl.BlockSpec((1, HEAD_DIM), lambda j, i: (0, 0))
    rope_spec = pl.BlockSpec((TM, HEAD_DIM), lambda j, i: (jnp.maximum(i - p_tiles, 0) % s_tiles, 0))
    in_specs = [pl.BlockSpec((TM, D_MODEL), lambda j, i: (i, 0)),
                pl.BlockSpec((D_MODEL, W_SUB), lambda j, i: (0, w_sub(j))),
                pl.BlockSpec((D_MODEL, W_SUB), lambda j, i: (0, w_sub(j) + 1)),
                norm_spec, norm_spec, norm_spec, rope_spec, rope_spec]
    args = [h, w_in, w_in, nqa, nqb, nkb, *rope]
    return pl.pallas_call(
        functools.partial(_proj_kernel, kinds=kinds),
        grid=(n_panels, n_tiles),
        in_specs=in_specs,
        out_specs=out_specs,
        out_shape=out_shape,
        scratch_shapes=[pltpu.VMEM((D_MODEL, TN), BF16)],
        compiler_params=_params(("arbitrary", "arbitrary"), VMEM_LIMIT),
        name="proj_" + "_".join(dict.fromkeys(kinds)),
    )(*args)


LOG2E = 1.4426950408889634
QK_LOG2 = SCALE * LOG2E


def _softmax_pv(parts, sink):
    m = functools.reduce(jnp.maximum, [jnp.max(t, axis=-1, keepdims=True) for t, _ in parts])
    if sink is not None:
        m = jnp.maximum(m, sink)
    acc = None
    for t, v in parts:
        p = jnp.exp2(t - m).astype(BF16)
        y = _dot(p, jnp.concatenate([v, jnp.ones_like(v)], axis=1))
        acc = y if acc is None else acc + y
    o, l = acc[:, :HEAD_DIM], acc[:, HEAD_DIM:]
    if sink is not None:
        l = l + jnp.exp2(sink - m)
    return o / l


def _sink_column(sink_ref, hk, rows_per_head):
    n = A_GROUP * rows_per_head
    g = lax.broadcasted_iota(jnp.int32, (n, 1), 0) // rows_per_head
    col = jnp.full((n, 1), sink_ref[hk * A_GROUP], F32)
    for k in range(1, A_GROUP):
        col = jnp.where(g == k, sink_ref[hk * A_GROUP + k], col)
    return col * LOG2E


def _stack_group(q_ref, hk):
    return jnp.concatenate(
        [q_ref[:, (hk * A_GROUP + g) * HEAD_DIM:(hk * A_GROUP + g + 1) * HEAD_DIM] for g in range(A_GROUP)], axis=0)


def _unstack_group(o_ref, hk, o, rows):
    for g in range(A_GROUP):
        c = (hk * A_GROUP + g) * HEAD_DIM
        o_ref[:, c:c + HEAD_DIM] = o[g * rows:(g + 1) * rows].astype(BF16)


CTX_REQS = 4


def _ctx_attn_kernel(sink_ref, q_ref, kv_ref, kva_ref, oa_ref, ob_ref):
    for req in range(CTX_REQS):
        rows = pl.ds(req * SEQ, SEQ)
        qa_ref, qb_ref = q_ref.at[PROJ_QG.index("qa"), rows], q_ref.at[PROJ_QG.index("qb"), rows]
        kb_ref, vb_ref = kv_ref.at[PROJ_KV.index("kb"), rows], kv_ref.at[PROJ_KV.index("vb"), rows]
        ka_ref, oa, ob = kva_ref.at[rows], oa_ref.at[rows], ob_ref.at[rows]
        for hk in range(A_KV_HEADS):
            k = ka_ref[:, hk * HEAD_DIM:(hk + 1) * HEAD_DIM]
            v = ka_ref[:, A_KV + hk * HEAD_DIM:A_KV + (hk + 1) * HEAD_DIM]
            q4 = _stack_group(qa_ref, hk)
            t = _dot_nt(q4, k) * QK_LOG2
            o = _softmax_pv([(t, v)], _sink_column(sink_ref, hk, SEQ))
            _unstack_group(oa, hk, o, SEQ)
        for h in range(B_HEADS):
            sl = slice(h * HEAD_DIM, (h + 1) * HEAD_DIM)
            t = _dot_nt(qb_ref[:, sl], kb_ref[:, sl]) * QK_LOG2
            ob[:, sl] = _softmax_pv([(t, vb_ref[:, sl])], None).astype(BF16)


def _ctx_attn_call(sink, q, kv, kva):
    rows = CTX_REQS * SEQ
    row = lambda w: pl.BlockSpec((rows, w), lambda b: (b, 0))
    return pl.pallas_call(
        _ctx_attn_kernel,
        grid=(BATCH // CTX_REQS,),
        in_specs=[pl.BlockSpec(memory_space=pltpu.SMEM),
                  pl.BlockSpec((2, rows, TN), lambda b: (0, b, 0)),
                  pl.BlockSpec((2, rows, TN), lambda b: (0, b, 0)), row(2 * A_KV)],
        out_specs=[row(A_Q), row(B_QKV)],
        out_shape=[jax.ShapeDtypeStruct((M_PROMPT, A_Q), BF16),
                   jax.ShapeDtypeStruct((M_PROMPT, B_QKV), BF16)],
        compiler_params=_params(("arbitrary",), VMEM_LIMIT),
        name="attn_ctx",
    )(sink, q, kv, kva)


BAND = 3 * BLOCK


def _cache_to_bf16(n_heads, ck_ref, cv_ref, ckb, cvb):
    for h in range(n_heads):
        sl = slice(h * HEAD_DIM, (h + 1) * HEAD_DIM)
        ckb[:, sl] = ck_ref[:, h, :].astype(BF16)
        cvb[:, sl] = cv_ref[:, h, :].astype(BF16)


WIN_BLOCKS = 2


def _win_attn_kernel(sink_ref, q_ref, kv_ref, ck_ref, cv_ref, o_ref, ckb, cvb):
    step = pl.program_id(1)
    pl.when(step == 0)(functools.partial(_cache_to_bf16, A_KV_HEADS, ck_ref, cv_ref, ckb, cvb))
    for blk in range(WIN_BLOCKS):
        n = step * WIN_BLOCKS + blk
        rows = pl.ds(blk * BLOCK, BLOCK)
        start = pl.multiple_of(jnp.clip((n - 1) * BLOCK, 0, DEC_SEQ - BAND), BLOCK)
        qpos = n * BLOCK + lax.broadcasted_iota(jnp.int32, (BLOCK, BAND), 0)
        kpos = start + lax.broadcasted_iota(jnp.int32, (BLOCK, BAND), 1)
        valid = jnp.abs(qpos - kpos) <= A_WINDOW
        valid = jnp.concatenate([valid.astype(jnp.int32)] * A_GROUP, axis=0) > 0
        for hk in range(A_KV_HEADS):
            sl = slice(hk * HEAD_DIM, (hk + 1) * HEAD_DIM)
            slv = slice(A_KV + hk * HEAD_DIM, A_KV + (hk + 1) * HEAD_DIM)
            k_loc = kv_ref[pl.ds(start, BAND), sl]
            v_loc = kv_ref[pl.ds(start, BAND), slv]
            q4 = _stack_group(q_ref.at[rows], hk)
            t_loc = jnp.where(valid, _dot_nt(q4, k_loc) * QK_LOG2, NEG)
            t_ctx = _dot_nt(q4, ckb[:, sl]) * QK_LOG2
            o = _softmax_pv([(t_loc, v_loc), (t_ctx, cvb[:, sl])], _sink_column(sink_ref, hk, BLOCK))
            _unstack_group(o_ref.at[rows], hk, o, BLOCK)


def _win_attn_call(sink, q, kva, cache_k, cache_v):
    rows = WIN_BLOCKS * BLOCK
    nb = DEC_SEQ // rows
    q0 = M_PROMPT // rows
    b0 = M_PROMPT // DEC_SEQ
    return pl.pallas_call(
        _win_attn_kernel,
        grid=(DEC_BATCH, nb),
        in_specs=[pl.BlockSpec(memory_space=pltpu.SMEM),
                  pl.BlockSpec((None, rows, A_Q), lambda b, n: (PROJ_QG.index("qa"), q0 + b * nb + n, 0)),
                  pl.BlockSpec((DEC_SEQ, 2 * A_KV), lambda b, n: (b0 + b, 0)),
                  pl.BlockSpec((None, None, PAST_LEN, A_KV_HEADS, HEAD_DIM), lambda b, n: (b, 0, 0, 0, 0)),
                  pl.BlockSpec((None, None, PAST_LEN, A_KV_HEADS, HEAD_DIM), lambda b, n: (b, 0, 0, 0, 0))],
        out_specs=pl.BlockSpec((rows, A_Q), lambda b, n: (b * nb + n, 0)),
        out_shape=jax.ShapeDtypeStruct((M_SAMPLE, A_Q), BF16),
        scratch_shapes=[pltpu.VMEM((PAST_LEN, A_KV), BF16), pltpu.VMEM((PAST_LEN, A_KV), BF16)],
        compiler_params=_params(("arbitrary", "arbitrary"), VMEM_LIMIT),
        name="attn_window",
    )(sink, q, kva, cache_k, cache_v)


NA_QROWS = 4
NA_WIN_ROWS = 12
NA_Q = NA_QROWS * GRID_W
NA_KEYS = NA_WIN_ROWS * GRID_W
NA_DR = 2 * NA_ROWS_MAX - 1
NA_PAIRS = NA_DR + 1
RPB_W = 2 * NA_COLS - 1
RPB_PAD = GRID_W - NA_COLS


def _na_row_start(r):
    return jnp.clip(r - NA_ROWS // 2, 0, GRID_ROWS - NA_ROWS)


def _na_window_row0(blk):
    return jnp.clip(blk * NA_QROWS - NA_ROWS // 2, 0, GRID_ROWS - NA_WIN_ROWS)


def _check_na_windows():
    for blk in range(GRID_ROWS // NA_QROWS):
        w0 = int(np.clip(blk * NA_QROWS - NA_ROWS // 2, 0, GRID_ROWS - NA_WIN_ROWS))
        for r in range(blk * NA_QROWS, (blk + 1) * NA_QROWS):
            r0 = int(np.clip(r - NA_ROWS // 2, 0, GRID_ROWS - NA_ROWS))
            assert w0 <= r0 and r0 + NA_ROWS <= w0 + NA_WIN_ROWS, (blk, r)


_check_na_windows()


def _na_attn_kernel(q_ref, k_ref, v_ref, ck_ref, cv_ref, rpb_ref, s_ref, wada_ref, bada_ref,
                    o_ref, mods_ref, tab_ref, ckb, cvb):
    b, blk = pl.program_id(0), pl.program_id(1)
    _ada_side(s_ref, wada_ref, bada_ref, mods_ref)
    lane = lax.broadcasted_iota(jnp.int32, (GRID_W, 2 * GRID_W), 1)
    pl.when(blk == 0)(functools.partial(_cache_to_bf16, B_HEADS, ck_ref, cv_ref, ckb, cvb))

    @pl.when((b == 0) & (blk == 0))
    def _():
        for h in range(B_HEADS):
            for d in range(NA_PAIRS):
                lo = jnp.broadcast_to(rpb_ref[h, d:d + 1, :], (GRID_W, 2 * GRID_W))
                hi = jnp.broadcast_to(rpb_ref[h, d + 1:d + 2, :], (GRID_W, 2 * GRID_W))
                lo = pltpu.roll(lo, GRID_W + 1, 1, stride=1, stride_axis=0)
                hi = pltpu.roll(hi, 1, 1, stride=1, stride_axis=0)
                tab_ref[h, d] = jnp.where(lane < GRID_W, lo, hi) * LOG2E

    row0 = _na_window_row0(blk)
    k0 = pl.multiple_of(row0 * GRID_W, GRID_W)
    qi = lax.broadcasted_iota(jnp.int32, (NA_Q, NA_KEYS), 0)
    ki = lax.broadcasted_iota(jnp.int32, (NA_Q, NA_KEYS), 1)
    qrow, qcol = blk * NA_QROWS + qi // GRID_W, qi % GRID_W
    krow, kcol = row0 + ki // GRID_W, ki % GRID_W
    rstart = _na_row_start(qrow)
    cstart = jnp.clip(qcol - NA_COLS // 2, 0, GRID_W - NA_COLS)
    valid = (krow >= rstart) & (krow < rstart + NA_ROWS) & (kcol >= cstart) & (kcol < cstart + NA_COLS)

    for h in range(B_HEADS):
        sl = slice(h * HEAD_DIM, (h + 1) * HEAD_DIM)
        bias = jnp.concatenate(
            [jnp.concatenate(
                [tab_ref[h, jnp.clip(row0 + 2 * p - blk * NA_QROWS - q + NA_ROWS_MAX, 0, NA_PAIRS - 1)]
                 for p in range(NA_WIN_ROWS // 2)], axis=1)
             for q in range(NA_QROWS)], axis=0)
        k_loc = k_ref[pl.ds(k0, NA_KEYS), sl]
        v_loc = v_ref[pl.ds(k0, NA_KEYS), sl]
        t_loc = jnp.where(valid, _dot_nt(q_ref[:, sl], k_loc) * QK_LOG2 + bias, NEG)
        t_ctx = _dot_nt(q_ref[:, sl], ckb[:, sl]) * QK_LOG2
        o = _softmax_pv([(t_loc, v_loc), (t_ctx, cvb[:, sl])], None)
        o_ref[:, sl] = o.astype(BF16)


def _na_bias_rows(rpb):
    rows = jnp.pad(rpb, ((0, 0), (1, 1), (0, 0)))
    left = jnp.broadcast_to(rows[..., :1], rows.shape[:2] + (RPB_PAD,))
    right = jnp.broadcast_to(rows[..., -1:], rows.shape[:2] + (2 * GRID_W - RPB_PAD - RPB_W,))
    return jnp.concatenate([left, rows, right], axis=-1)


def _na_attn_call(q, kv, cache_k, cache_v, rpb, ada):
    nblk = GRID_ROWS // NA_QROWS
    assert DEC_BATCH * nblk * ADA_SIDE_COLS == ADA_SIDE and ADA_HEAD + ADA_SIDE == 6 * D_MODEL
    side_in, side_out, side_shape = _ada_side_specs(ADA_HEAD, lambda b, r: b * nblk + r)
    q0 = M_PROMPT // NA_Q
    b0 = M_PROMPT // DEC_SEQ
    return pl.pallas_call(
        _na_attn_kernel,
        grid=(DEC_BATCH, nblk),
        in_specs=[pl.BlockSpec((None, NA_Q, B_QKV), lambda b, r: (PROJ_QG.index("qb"), q0 + b * nblk + r, 0)),
                  pl.BlockSpec((None, DEC_SEQ, B_QKV), lambda b, r: (PROJ_KV.index("kb"), b0 + b, 0)),
                  pl.BlockSpec((None, DEC_SEQ, B_QKV), lambda b, r: (PROJ_KV.index("vb"), b0 + b, 0)),
                  pl.BlockSpec((None, None, PAST_LEN, B_HEADS, HEAD_DIM), lambda b, r: (b, 0, 0, 0, 0)),
                  pl.BlockSpec((None, None, PAST_LEN, B_HEADS, HEAD_DIM), lambda b, r: (b, 0, 0, 0, 0)),
                  pl.BlockSpec((B_HEADS, NA_DR + 2, 2 * GRID_W), lambda b, r: (0, 0, 0))] + side_in,
        out_specs=[pl.BlockSpec((NA_Q, B_QKV), lambda b, r: (b * nblk + r, 0)), side_out],
        out_shape=[jax.ShapeDtypeStruct((M_SAMPLE, B_QKV), BF16), side_shape],
        scratch_shapes=[pltpu.VMEM((B_HEADS, NA_PAIRS, GRID_W, 2 * GRID_W), F32),
                        pltpu.VMEM((PAST_LEN, B_QKV), BF16), pltpu.VMEM((PAST_LEN, B_QKV), BF16)],
        compiler_params=_params(("arbitrary", "arbitrary"), VMEM_LIMIT),
        name="attn_neighbourhood",
    )(q, kv, kv, cache_k, cache_v, _na_bias_rows(rpb), *ada)


def _merge_kernel(oap, oas, obp, obs, wa_ref, wb_ref, ga_ref, gb_ref, z_ref, wa_bf, wb_bf):
    i = pl.program_id(1)
    p_tiles = M_PROMPT // TM

    @pl.when(i == 0)
    def _():
        wa_bf[...] = wa_ref[...].astype(BF16)
        wb_bf[...] = wb_ref[...].astype(BF16)

    def run(oa_ref, ob_ref):
        for rows in _row_chunks(TM):
            ya = _dot(oa_ref[rows, :], wa_bf[...])
            yb = _dot(ob_ref[rows, :], wb_bf[...])
            z_ref[rows, :] = (ga_ref[rows, :] * ya + gb_ref[rows, :] * yb).astype(BF16)

    pl.when(i < p_tiles)(functools.partial(run, oap, obp))
    pl.when(i >= p_tiles)(functools.partial(run, oas, obs))


def _merge_call(oa_p, oa_s, ob_p, ob_s, w_br_a, w_br_b, gates):
    p_tiles = M_PROMPT // TM
    nj = D_MODEL // TN
    pspec = pl.BlockSpec((TM, A_Q), lambda j, i: (jnp.minimum(i, p_tiles - 1), 0))
    sspec = pl.BlockSpec((TM, A_Q), lambda j, i: (jnp.maximum(i - p_tiles, 0), 0))
    once = dict(pipeline_mode=pl.Buffered(1))
    return pl.pallas_call(
        _merge_kernel,
        grid=(nj, M_ALL // TM),
        in_specs=[pspec, sspec, pspec, sspec,
                  pl.BlockSpec((A_Q, TN), lambda j, i: (0, j), **once),
                  pl.BlockSpec((B_QKV, TN), lambda j, i: (0, j), **once),
                  pl.BlockSpec((None, TM, TN), lambda j, i: (j, i, 0)),
                  pl.BlockSpec((None, TM, TN), lambda j, i: (nj + j, i, 0))],
        out_specs=pl.BlockSpec((TM, TN), lambda j, i: (i, j)),
        out_shape=jax.ShapeDtypeStruct((M_ALL, D_MODEL), BF16),
        scratch_shapes=[pltpu.VMEM((A_Q, TN), BF16), pltpu.VMEM((B_QKV, TN), BF16)],
        compiler_params=_params(("arbitrary", "arbitrary"), VMEM_LIMIT),
        name="merge_branches",
    )(oa_p, oa_s, ob_p, ob_s, w_br_a, w_br_b, gates, gates)


def _out_kernel(z_ref, w_ref, xp_ref, xs_ref, g_ref, o_ref, wbf):
    i = pl.program_id(1)
    p_tiles = M_PROMPT // TM

    @pl.when(i == 0)
    def _():
        wbf[...] = w_ref[...].astype(BF16)

    def run(x_ref):
        for rows in _row_chunks(TM):
            o_ref[rows, :] = x_ref[rows, :] + g_ref[...] * _dot(z_ref[rows, :], wbf[...])

    pl.when(i < p_tiles)(functools.partial(run, xp_ref))
    pl.when(i >= p_tiles)(functools.partial(run, xs_ref))


def _out_call(z, w_out, xp, xs, mods_tail):
    p_tiles = M_PROMPT // TM
    nj = D_MODEL // TN
    return pl.pallas_call(
        _out_kernel,
        grid=(nj, M_ALL // TM),
        in_specs=[pl.BlockSpec((TM, D_MODEL), lambda j, i: (i, 0)),
                  pl.BlockSpec((D_MODEL, TN), lambda j, i: (0, j)),
                  pl.BlockSpec((TM, TN), lambda j, i: (jnp.minimum(i, p_tiles - 1), j)),
                  pl.BlockSpec((TM, TN), lambda j, i: (jnp.maximum(i - p_tiles, 0), j)),
                  pl.BlockSpec((None, 1, TN), lambda j, i: (_mod_row(i, TM), 0, j))],
        out_specs=pl.BlockSpec((TM, TN), lambda j, i: (i, j)),
        out_shape=jax.ShapeDtypeStruct((M_ALL, D_MODEL), F32),
        scratch_shapes=[pltpu.VMEM((D_MODEL, TN), BF16)],
        compiler_params=_params(("arbitrary", "arbitrary"), VMEM_LIMIT),
        name="out_proj_residual",
    )(z, w_out, xp, xs, mods_tail)


def _mlp_kernel(x_hbm, nw_ref, sh_ref, sc_ref, g_ref, wu_ref, wd_ref, o_ref, h_ref, x_buf, x_sem, *, tile0):
    i, f = pl.program_id(0), pl.program_id(1)

    def x_copy(tile):
        return pltpu.make_async_copy(x_hbm.at[pl.ds((tile0 + tile) * TM, TM), :], x_buf, x_sem)

    @pl.when((i == 0) & (f == 0))
    def _():
        x_copy(0).start()

    @pl.when(f == 0)
    def _():
        x_copy(i).wait()
        wu = wu_ref[...].astype(BF16)
        wd = wd_ref[...].astype(BF16)
        for r0 in range(0, TM, ROW_CHUNK):
            rows = slice(r0, r0 + ROW_CHUNK)
            x = x_buf[rows, :]
            h = _modnorm(x, nw_ref[...], sc_ref[...], sh_ref[...]).astype(BF16)
            h_ref[rows, :] = h
            u = jnp.square(jnp.maximum(_dot(h, wu), 0.0)).astype(BF16)
            o_ref[rows, :] = x + g_ref[...] * _dot(u, wd)

    @pl.when((f == 1) & (i + 1 < pl.num_programs(0)))
    def _():
        x_copy(i + 1).start()

    @pl.when(f > 0)
    def _():
        u = _dot(h_ref[...], wu_ref[...].astype(BF16))
        u = jnp.square(jnp.maximum(u, 0.0)).astype(BF16)
        for c0 in range(0, D_MODEL, MLP_COLS):
            cols = slice(c0, c0 + MLP_COLS)
            o_ref[:, cols] += g_ref[:, cols] * _dot(u, wd_ref[:, cols].astype(BF16))


def _mlp_call(x1, nw, mods_tail, w_up, w_down, tile0, n_tiles):
    mod = lambda k: pl.BlockSpec((None, 1, D_MODEL), lambda i, f: (_mod_row(tile0 + i, TM), 0, k))
    return pl.pallas_call(
        functools.partial(_mlp_kernel, tile0=tile0),
        grid=(n_tiles, D_FF // TF),
        in_specs=[pl.BlockSpec(memory_space=pl.ANY),
                  pl.BlockSpec((1, D_MODEL), lambda i, f: (0, 0)),
                  mod(1), mod(2), mod(3),
                  pl.BlockSpec((D_MODEL, TF), lambda i, f: (0, f)),
                  pl.BlockSpec((TF, D_MODEL), lambda i, f: (f, 0))],
        out_specs=pl.BlockSpec((TM, D_MODEL), lambda i, f: (i, 0)),
        out_shape=jax.ShapeDtypeStruct((n_tiles * TM, D_MODEL), F32),
        scratch_shapes=[pltpu.VMEM((TM, D_MODEL), BF16),
                        pltpu.VMEM((TM, D_MODEL), F32),
                        pltpu.SemaphoreType.DMA(())],
        compiler_params=_params(("arbitrary", "arbitrary"), VMEM_LIMIT),
        name="mlp",
    )(x1, nw, mods_tail, mods_tail, mods_tail, w_up, w_down)


def kernel(x_prompt, x_sample, cache_a_k, cache_a_v, cache_b_k, cache_b_v, c, c_ctx, norm1_w, norm2_w, w_ada, b_ada, w_in, q_norm_a, k_norm_a, q_norm_b, k_norm_b, sink_a, rpb_b, w_br_a, w_br_b, w_out, w_up, w_down):
    assert w_ada.shape[0] == 1, "one trunk layer"
    xp = x_prompt.reshape(M_PROMPT, D_MODEL)
    xs = x_sample.reshape(M_SAMPLE, D_MODEL)

    mods_head, cond = _ada_call(c_ctx[None, :], c, w_ada[0], b_ada)
    ada = (cond, w_ada[0], b_ada)

    w = w_in[0]
    rope = _rope_tables()
    h, kva, ka_p, va_p = _norm_kva_call(xp, xs, norm1_w, mods_head, w, k_norm_a, rope)
    kv, kb_p, vb_p = _proj_call(h, w, q_norm_a, q_norm_b, k_norm_b, rope, PROJ_KV)
    q, gates = _proj_call(h, w, q_norm_a, q_norm_b, k_norm_b, rope, PROJ_QG)

    sink = sink_a[0]
    oa_p, ob_p = _ctx_attn_call(sink, q, kv, kva)
    oa_s = _win_attn_call(sink, q, kva, cache_a_k, cache_a_v)
    ob_s, mods_tail = _na_attn_call(q, kv, cache_b_k, cache_b_v, rpb_b[0], ada)

    z = _merge_call(oa_p, oa_s, ob_p, ob_s, w_br_a[0], w_br_b[0], gates)
    x1 = _out_call(z, w_out[0], xp, xs, mods_tail)

    p_tiles = M_PROMPT // TM
    y_p = _mlp_call(x1, norm2_w, mods_tail, w_up[0], w_down[0], 0, p_tiles)
    y_s = _mlp_call(x1, norm2_w, mods_tail, w_up[0], w_down[0], p_tiles, M_SAMPLE // TM)

    return (y_p.reshape(BATCH, SEQ, D_MODEL),
            y_s.reshape(DEC_BATCH, DEC_SEQ, D_MODEL),
            ka_p.reshape(BATCH, 1, SEQ, A_KV_HEADS, HEAD_DIM),
            va_p.reshape(BATCH, 1, SEQ, A_KV_HEADS, HEAD_DIM),
            kb_p.reshape(BATCH, 1, SEQ, B_HEADS, HEAD_DIM),
            vb_p.reshape(BATCH, 1, SEQ, B_HEADS, HEAD_DIM))
```

```python
import functools

import numpy as np
import jax
import jax.numpy as jnp
from jax import lax
from jax.experimental import pallas as pl
from jax.experimental.pallas import tpu as pltpu

D_MODEL = 2048
BATCH = 16
SEQ = 256
DEC_BATCH = 2
DEC_SEQ = 1024
PAST_LEN = 256
GRID_W = 64
HEAD_DIM = 128
A_HEADS = 8
A_KV_HEADS = 2
A_GROUP = A_HEADS // A_KV_HEADS
A_WINDOW = 128
BLOCK = 128
B_HEADS = 8
NA_ROWS_MAX = 8
NA_COLS = 16
D_FF = 4 * D_MODEL
ROPE_THETA = 10000.0
EPS = 1e-6
NEG = -1e30
A_Q = A_HEADS * HEAD_DIM
A_KV = A_KV_HEADS * HEAD_DIM
B_QKV = B_HEADS * HEAD_DIM
IN_WIDTH = A_Q + 2 * A_KV + 3 * B_QKV + 2 * D_MODEL
SCALE = HEAD_DIM ** -0.5

M_PROMPT = BATCH * SEQ
M_SAMPLE = DEC_BATCH * DEC_SEQ
M_ALL = M_PROMPT + M_SAMPLE
GRID_ROWS = DEC_SEQ // GRID_W
NA_ROWS = min(NA_ROWS_MAX, GRID_ROWS)

COL_QA = 0
COL_KVA = A_Q
COL_QB = A_Q + 2 * A_KV
COL_KB = COL_QB + B_QKV
COL_VB = COL_KB + B_QKV
COL_G = COL_VB + B_QKV

TM = 1024
TN = 1024
ROW_CHUNK = 256
W_SUB = 512
TM_NORM = 512
TF = 512
MLP_COLS = 512
ADA_HEAD = 2 * D_MODEL
ADA_SIDE_COLS = 2048
ADA_SIDE = 4 * D_MODEL
VMEM_LIMIT = 60 * 1024 * 1024

F32 = jnp.float32
BF16 = jnp.bfloat16


def _mod_row(i, tm):
    p_tiles = M_PROMPT // tm
    return jnp.where(i < p_tiles, 0, 1 + (i - p_tiles) // (DEC_SEQ // tm))


def _row_chunks(tm):
    edges = list(range(0, tm, ROW_CHUNK)) + [tm - ROW_CHUNK // 2, tm]
    edges = sorted(set(edges))
    return [slice(a, b) for a, b in zip(edges[:-1], edges[1:])]


def _dot(a, b):
    return jnp.dot(a, b, preferred_element_type=F32)


def _dot_nt(a, b):
    return lax.dot_general(a, b, (((1,), (1,)), ((), ())), preferred_element_type=F32)


def _params(sem, vmem=None):
    return pltpu.CompilerParams(dimension_semantics=sem, vmem_limit_bytes=vmem)


def _ada_side(s_ref, w_ref, b_ref, o_ref):
    o_ref[:, 0, :] = _dot(s_ref[...], w_ref[...].astype(BF16)) + b_ref[...]


def _ada_kernel(cctx_ref, c_ref, w_ref, b_ref, o_ref, s_ref):
    row = lax.broadcasted_iota(jnp.int32, (8, D_MODEL), 0)
    cv = jnp.where(row == 0, cctx_ref[...], 0.0)
    for b in range(DEC_BATCH):
        cv = jnp.where(row == 1 + b, c_ref[b:b + 1, :], cv)
    s_ref[...] = (cv * jax.nn.sigmoid(cv)).astype(BF16)
    _ada_side(s_ref, w_ref, b_ref, o_ref)


def _ada_call(c_ctx, c, w_ada, b_ada):
    tn = 1024
    n = ADA_HEAD
    return pl.pallas_call(
        _ada_kernel,
        grid=(n // tn,),
        in_specs=[pl.BlockSpec((1, D_MODEL), lambda j: (0, 0)),
                  pl.BlockSpec((DEC_BATCH, D_MODEL), lambda j: (0, 0)),
                  pl.BlockSpec((D_MODEL, tn), lambda j: (0, j)),
                  pl.BlockSpec((1, tn), lambda j: (0, j))],
        out_specs=[pl.BlockSpec((8, 1, tn), lambda j: (0, 0, j)),
                   pl.BlockSpec((8, D_MODEL), lambda j: (0, 0))],
        out_shape=[jax.ShapeDtypeStruct((8, 1, n), F32),
                   jax.ShapeDtypeStruct((8, D_MODEL), BF16)],
        compiler_params=_params(("arbitrary",), VMEM_LIMIT),
        name="ada_mod",
    )(c_ctx, c, w_ada, b_ada)


def _ada_side_specs(first_col, step):
    blk = lambda *g: first_col // ADA_SIDE_COLS + step(*g)
    in_specs = [pl.BlockSpec((8, D_MODEL), lambda *g: (0, 0)),
                pl.BlockSpec((D_MODEL, ADA_SIDE_COLS), lambda *g: (0, blk(*g))),
                pl.BlockSpec((1, ADA_SIDE_COLS), lambda *g: (0, blk(*g)))]
    out_spec = pl.BlockSpec((8, 1, ADA_SIDE_COLS), lambda *g: (0, 0, step(*g)))
    return in_specs, out_spec, jax.ShapeDtypeStruct((8, 1, ADA_SIDE), F32)


def _modnorm(x, nw, sc, sh):
    y = x * lax.rsqrt(jnp.mean(x * x, axis=-1, keepdims=True) + EPS)
    return y * (nw * (1.0 + sc)) + sh


def _head_norm(x, nw):
    return x * lax.rsqrt(jnp.mean(x * x, axis=-1, keepdims=True) + EPS) * nw


def _rope(x, cos, sin_signed):
    lane = lax.broadcasted_iota(jnp.int32, x.shape, 1)
    partner = jnp.where((lane % 64) < 32, pltpu.roll(x, 96, 1), pltpu.roll(x, 32, 1))
    return x * cos + partner * sin_signed


def _norm_kva_kernel(xp_ref, xs_ref, nw1_ref, sh_ref, sc_ref, w_ref, nwk_ref, cos_ref, sin_ref,
                     h_ref, kv_ref, kp_ref, vp_ref, wbf):
    i = pl.program_id(0)
    p_tiles = M_PROMPT // TM_NORM

    @pl.when(i == 0)
    def _():
        wbf[...] = w_ref[...].astype(BF16)

    def head_cols(k, base=0):
        return slice(base + k * HEAD_DIM, base + (k + 1) * HEAD_DIM)

    def run(prompt):
        x_ref = xp_ref if prompt else xs_ref
        for r0 in range(0, TM_NORM, ROW_CHUNK):
            rows = slice(r0, r0 + ROW_CHUNK)
            h = _modnorm(x_ref[rows, :], nw1_ref[...], sc_ref[...], sh_ref[...]).astype(BF16)
            h_ref[rows, :] = h
            acc = _dot(h, wbf[...])
            for k in range(A_KV_HEADS):
                y = _head_norm(acc[:, head_cols(k)], nwk_ref[...])
                v = acc[:, head_cols(k, A_KV)]
                if prompt:
                    kp_ref[rows, k, :] = y
                    vp_ref[rows, k, :] = v
                else:
                    y = _rope(y, cos_ref[rows, :], sin_ref[rows, :])
                kv_ref[rows, head_cols(k)] = y.astype(BF16)
                kv_ref[rows, head_cols(k, A_KV)] = v.astype(BF16)

    pl.when(i < p_tiles)(functools.partial(run, True))
    pl.when(i >= p_tiles)(functools.partial(run, False))


def _norm_kva_call(xp, xs, nw1, mods3, w_in, nwk, rope):
    tm = TM_NORM
    p_tiles = M_PROMPT // tm
    s_tiles = DEC_SEQ // tm
    assert COL_KVA % (2 * A_KV) == 0
    rope_spec = pl.BlockSpec((tm, HEAD_DIM), lambda i: (jnp.maximum(i - p_tiles, 0) % s_tiles, 0))
    parked = pl.BlockSpec((tm, A_KV_HEADS, HEAD_DIM), lambda i: (jnp.minimum(i, p_tiles - 1), 0, 0))
    return pl.pallas_call(
        _norm_kva_kernel,
        grid=(M_ALL // tm,),
        in_specs=[pl.BlockSpec((tm, D_MODEL), lambda i: (jnp.minimum(i, p_tiles - 1), 0)),
                  pl.BlockSpec((tm, D_MODEL), lambda i: (jnp.maximum(i - p_tiles, 0), 0)),
                  pl.BlockSpec((1, D_MODEL), lambda i: (0, 0)),
                  pl.BlockSpec((None, 1, D_MODEL), lambda i: (_mod_row(i, tm), 0, 0)),
                  pl.BlockSpec((None, 1, D_MODEL), lambda i: (_mod_row(i, tm), 0, 1)),
                  pl.BlockSpec((D_MODEL, 2 * A_KV), lambda i: (0, COL_KVA // (2 * A_KV)),
                               pipeline_mode=pl.Buffered(1)),
                  pl.BlockSpec((1, HEAD_DIM), lambda i: (0, 0)),
                  rope_spec, rope_spec],
        out_specs=[pl.BlockSpec((tm, D_MODEL), lambda i: (i, 0)),
                   pl.BlockSpec((tm, 2 * A_KV), lambda i: (i, 0)),
                   parked, parked],
        out_shape=[jax.ShapeDtypeStruct((M_ALL, D_MODEL), BF16),
                   jax.ShapeDtypeStruct((M_ALL, 2 * A_KV), BF16),
                   jax.ShapeDtypeStruct((M_PROMPT, A_KV_HEADS, HEAD_DIM), F32),
                   jax.ShapeDtypeStruct((M_PROMPT, A_KV_HEADS, HEAD_DIM), F32)],
        scratch_shapes=[pltpu.VMEM((D_MODEL, 2 * A_KV), BF16)],
        compiler_params=_params(("arbitrary",), VMEM_LIMIT),
        name="norm1_proj_kva",
    )(xp, xs, nw1, mods3, mods3, w_in, nwk, *rope)


def _rope_tables():
    n_freq = HEAD_DIM // 4
    pos = np.arange(DEC_SEQ)
    row = (pos // GRID_W).astype(np.float64)
    col = (pos % GRID_W).astype(np.float64)
    inv = ROPE_THETA ** (-np.arange(n_freq, dtype=np.float64) / n_freq)
    ar = row[:, None] * inv
    ac = col[:, None] * inv
    cos = np.concatenate([np.cos(ar), np.cos(ar), np.cos(ac), np.cos(ac)], axis=-1)
    sin = np.concatenate([-np.sin(ar), np.sin(ar), -np.sin(ac), np.sin(ac)], axis=-1)
    return jnp.asarray(cos, F32), jnp.asarray(sin, F32)


PROJ_COL = {"qa": COL_QA, "qb": COL_QB, "kb": COL_KB, "vb": COL_VB}
GATE_PANELS = 2 * D_MODEL // TN
PROJ_KV = ("kb", "vb")
PROJ_QG = ("qa", "qb") + ("gate",) * GATE_PANELS


def _proj_kernel(*refs, kinds):
    names = ["h", "w0", "w1", "nqa", "nqb", "nkb", "cos", "sin", "qkv"]
    names += ["gate"] * ("gate" in kinds) + ["kbp"] * ("kb" in kinds) + ["vbp"] * ("vb" in kinds) + ["wbf"]
    r = dict(zip(names, refs, strict=True))
    n_qkv = sum(k != "gate" for k in kinds)
    j, i = pl.program_id(0), pl.program_id(1)
    is_prompt = i < M_PROMPT // TM

    @pl.when(i == 0)
    def _():
        r["wbf"][:, :W_SUB] = r["w0"][...].astype(BF16)
        r["wbf"][:, W_SUB:] = r["w1"][...].astype(BF16)

    def head_cols(k):
        return slice(k * HEAD_DIM, (k + 1) * HEAD_DIM)

    def run(kind, prompt):
        for rows in _row_chunks(TM):
            acc = _dot(r["h"][rows, :], r["wbf"][...])
            if kind == "gate":
                r["gate"][rows, :] = jax.nn.sigmoid(acc)
            elif kind == "vb":
                if prompt:
                    r["vbp"][rows, :] = acc
                r["qkv"][rows, :] = acc.astype(BF16)
            else:
                nw_ref = r[{"qa": "nqa", "qb": "nqb", "kb": "nkb"}[kind]]
                for k in range(TN // HEAD_DIM):
                    y = _head_norm(acc[:, head_cols(k)], nw_ref[...])
                    if kind == "qa" and not prompt:
                        y = _rope(y, r["cos"][rows, :], r["sin"][rows, :])
                    if kind == "kb" and prompt:
                        r["kbp"][rows, head_cols(k)] = y
                    r["qkv"][rows, head_cols(k)] = y.astype(BF16)

    for p, kind in enumerate(kinds[:n_qkv]):
        if kind == "qb":
            pl.when(j == p)(functools.partial(run, kind, None))
        else:
            pl.when((j == p) & is_prompt)(functools.partial(run, kind, True))
            pl.when((j == p) & jnp.logical_not(is_prompt))(functools.partial(run, kind, False))
    if "gate" in kinds:
        pl.when(j >= n_qkv)(functools.partial(run, "gate", None))


def _proj_call(h, w_in, nqa, nqb, nkb, rope, kinds):
    assert TN == 2 * W_SUB and A_Q == B_QKV == TN
    n_panels = len(kinds)
    n_qkv = sum(k != "gate" for k in kinds)
    assert all(k == "gate" for k in kinds[n_qkv:])
    cols = [PROJ_COL[k] for k in kinds[:n_qkv]] + list(range(COL_G, IN_WIDTH, TN))[:n_panels - n_qkv]
    p_tiles = M_PROMPT // TM
    s_tiles = DEC_SEQ // TM
    n_tiles = M_ALL // TM

    def w_sub(j):
        idx = cols[0] // W_SUB
        for p in range(1, n_panels):
            idx = jnp.where(j == p, cols[p] // W_SUB, idx)
        return idx

    def qkv_idx(j, i):
        return (jnp.minimum(j, n_qkv - 1), jnp.where(j >= n_qkv, n_tiles - 1, i), 0)

    def gate_idx(j, i):
        return (jnp.maximum(j - n_qkv, 0), jnp.where(j < n_qkv, 0, i), 0)

    def cache_idx(panel):
        def idx(j, i):
            return (jnp.where(j < panel, 0, jnp.where(j > panel, p_tiles - 1, jnp.minimum(i, p_tiles - 1))), 0)
        return idx

    out_specs = [pl.BlockSpec((None, TM, TN), qkv_idx)]
    out_shape = [jax.ShapeDtypeStruct((n_qkv, M_ALL, TN), BF16)]
    if "gate" in kinds:
        out_specs.append(pl.BlockSpec((None, TM, TN), gate_idx))
        out_shape.append(jax.ShapeDtypeStruct((n_panels - n_qkv, M_ALL, TN), F32))
    for kind in ("kb", "vb"):
        if kind in kinds:
            out_specs.append(pl.BlockSpec((TM, TN), cache_idx(kinds.index(kind))))
            out_shape.append(jax.ShapeDtypeStruct((M_PROMPT, B_QKV), F32))

    norm_spec = pl.BlockSpec((1, HEAD_DIM), lambda j, i: (0, 0))
    rope_spec = pl.BlockSpec((TM, HEAD_DIM), lambda j, i: (jnp.maximum(i - p_tiles, 0) % s_tiles, 0))
    in_specs = [pl.BlockSpec((TM, D_MODEL), lambda j, i: (i, 0)),
                pl.BlockSpec((D_MODEL, W_SUB), lambda j, i: (0, w_sub(j))),
                pl.BlockSpec((D_MODEL, W_SUB), lambda j, i: (0, w_sub(j) + 1)),
                norm_spec, norm_spec, norm_spec, rope_spec, rope_spec]
    args = [h, w_in, w_in, nqa, nqb, nkb, *rope]
    return pl.pallas_call(
        functools.partial(_proj_kernel, kinds=kinds),
        grid=(n_panels, n_tiles),
        in_specs=in_specs,
        out_specs=out_specs,
        out_shape=out_shape,
        scratch_shapes=[pltpu.VMEM((D_MODEL, TN), BF16)],
        compiler_params=_params(("arbitrary", "arbitrary"), VMEM_LIMIT),
        name="proj_" + "_".join(dict.fromkeys(kinds)),
    )(*args)


LOG2E = 1.4426950408889634
QK_LOG2 = SCALE * LOG2E


def _softmax_pv(parts, sink):
    m = functools.reduce(jnp.maximum, [jnp.max(t, axis=-1, keepdims=True) for t, _ in parts])
    if sink is not None:
        m = jnp.maximum(m, sink)
    acc = None
    for t, v in parts:
        p = jnp.exp2(t - m).astype(BF16)
        y = _dot(p, jnp.concatenate([v, jnp.ones_like(v)], axis=1))
        acc = y if acc is None else acc + y
    o, l = acc[:, :HEAD_DIM], acc[:, HEAD_DIM:]
    if sink is not None:
        l = l + jnp.exp2(sink - m)
    return o / l


def _sink_column(sink_ref, hk, rows_per_head):
    n = A_GROUP * rows_per_head
    g = lax.broadcasted_iota(jnp.int32, (n, 1), 0) // rows_per_head
    col = jnp.full((n, 1), sink_ref[hk * A_GROUP], F32)
    for k in range(1, A_GROUP):
        col = jnp.where(g == k, sink_ref[hk * A_GROUP + k], col)
    return col * LOG2E


def _stack_group(q_ref, hk):
    return jnp.concatenate(
        [q_ref[:, (hk * A_GROUP + g) * HEAD_DIM:(hk * A_GROUP + g + 1) * HEAD_DIM] for g in range(A_GROUP)], axis=0)


def _unstack_group(o_ref, hk, o, rows):
    for g in range(A_GROUP):
        c = (hk * A_GROUP + g) * HEAD_DIM
        o_ref[:, c:c + HEAD_DIM] = o[g * rows:(g + 1) * rows].astype(BF16)


CTX_REQS = 4


def _ctx_attn_kernel(sink_ref, q_ref, kv_ref, kva_ref, oa_ref, ob_ref):
    for req in range(CTX_REQS):
        rows = pl.ds(req * SEQ, SEQ)
        qa_ref, qb_ref = q_ref.at[PROJ_QG.index("qa"), rows], q_ref.at[PROJ_QG.index("qb"), rows]
        kb_ref, vb_ref = kv_ref.at[PROJ_KV.index("kb"), rows], kv_ref.at[PROJ_KV.index("vb"), rows]
        ka_ref, oa, ob = kva_ref.at[rows], oa_ref.at[rows], ob_ref.at[rows]
        for hk in range(A_KV_HEADS):
            k = ka_ref[:, hk * HEAD_DIM:(hk + 1) * HEAD_DIM]
            v = ka_ref[:, A_KV + hk * HEAD_DIM:A_KV + (hk + 1) * HEAD_DIM]
            q4 = _stack_group(qa_ref, hk)
            t = _dot_nt(q4, k) * QK_LOG2
            o = _softmax_pv([(t, v)], _sink_column(sink_ref, hk, SEQ))
            _unstack_group(oa, hk, o, SEQ)
        for h in range(B_HEADS):
            sl = slice(h * HEAD_DIM, (h + 1) * HEAD_DIM)
            t = _dot_nt(qb_ref[:, sl], kb_ref[:, sl]) * QK_LOG2
            ob[:, sl] = _softmax_pv([(t, vb_ref[:, sl])], None).astype(BF16)


def _ctx_attn_call(sink, q, kv, kva):
    rows = CTX_REQS * SEQ
    row = lambda w: pl.BlockSpec((rows, w), lambda b: (b, 0))
    return pl.pallas_call(
        _ctx_attn_kernel,
        grid=(BATCH // CTX_REQS,),
        in_specs=[pl.BlockSpec(memory_space=pltpu.SMEM),
                  pl.BlockSpec((2, rows, TN), lambda b: (0, b, 0)),
                  pl.BlockSpec((2, rows, TN), lambda b: (0, b, 0)), row(2 * A_KV)],
        out_specs=[row(A_Q), row(B_QKV)],
        out_shape=[jax.ShapeDtypeStruct((M_PROMPT, A_Q), BF16),
                   jax.ShapeDtypeStruct((M_PROMPT, B_QKV), BF16)],
        compiler_params=_params(("arbitrary",), VMEM_LIMIT),
        name="attn_ctx",
    )(sink, q, kv, kva)


BAND = 3 * BLOCK


def _cache_to_bf16(n_heads, ck_ref, cv_ref, ckb, cvb):
    for h in range(n_heads):
        sl = slice(h * HEAD_DIM, (h + 1) * HEAD_DIM)
        ckb[:, sl] = ck_ref[:, h, :].astype(BF16)
        cvb[:, sl] = cv_ref[:, h, :].astype(BF16)


WIN_BLOCKS = 4


def _win_attn_kernel(sink_ref, q_ref, kv_ref, ck_ref, cv_ref, o_ref, ckb, cvb):
    step = pl.program_id(1)
    pl.when(step == 0)(functools.partial(_cache_to_bf16, A_KV_HEADS, ck_ref, cv_ref, ckb, cvb))
    for blk in range(WIN_BLOCKS):
        n = step * WIN_BLOCKS + blk
        rows = pl.ds(blk * BLOCK, BLOCK)
        start = pl.multiple_of(jnp.clip((n - 1) * BLOCK, 0, DEC_SEQ - BAND), BLOCK)
        qpos = n * BLOCK + lax.broadcasted_iota(jnp.int32, (BLOCK, BAND), 0)
        kpos = start + lax.broadcasted_iota(jnp.int32, (BLOCK, BAND), 1)
        valid = jnp.abs(qpos - kpos) <= A_WINDOW
        valid = jnp.concatenate([valid.astype(jnp.int32)] * A_GROUP, axis=0) > 0
        for hk in range(A_KV_HEADS):
            sl = slice(hk * HEAD_DIM, (hk + 1) * HEAD_DIM)
            slv = slice(A_KV + hk * HEAD_DIM, A_KV + (hk + 1) * HEAD_DIM)
            k_loc = kv_ref[pl.ds(start, BAND), sl]
            v_loc = kv_ref[pl.ds(start, BAND), slv]
            q4 = _stack_group(q_ref.at[rows], hk)
            t_loc = jnp.where(valid, _dot_nt(q4, k_loc) * QK_LOG2, NEG)
            t_ctx = _dot_nt(q4, ckb[:, sl]) * QK_LOG2
            o = _softmax_pv([(t_loc, v_loc), (t_ctx, cvb[:, sl])], _sink_column(sink_ref, hk, BLOCK))
            _unstack_group(o_ref.at[rows], hk, o, BLOCK)


def _win_attn_call(sink, q, kva, cache_k, cache_v):
    rows = WIN_BLOCKS * BLOCK
    nb = DEC_SEQ // rows
    q0 = M_PROMPT // rows
    b0 = M_PROMPT // DEC_SEQ
    return pl.pallas_call(
        _win_attn_kernel,
        grid=(DEC_BATCH, nb),
        in_specs=[pl.BlockSpec(memory_space=pltpu.SMEM),
                  pl.BlockSpec((None, rows, A_Q), lambda b, n: (PROJ_QG.index("qa"), q0 + b * nb + n, 0)),
                  pl.BlockSpec((DEC_SEQ, 2 * A_KV), lambda b, n: (b0 + b, 0)),
                  pl.BlockSpec((None, None, PAST_LEN, A_KV_HEADS, HEAD_DIM), lambda b, n: (b, 0, 0, 0, 0)),
                  pl.BlockSpec((None, None, PAST_LEN, A_KV_HEADS, HEAD_DIM), lambda b, n: (b, 0, 0, 0, 0))],
        out_specs=pl.BlockSpec((rows, A_Q), lambda b, n: (b * nb + n, 0)),
        out_shape=jax.ShapeDtypeStruct((M_SAMPLE, A_Q), BF16),
        scratch_shapes=[pltpu.VMEM((PAST_LEN, A_KV), BF16), pltpu.VMEM((PAST_LEN, A_KV), BF16)],
        compiler_params=_params(("arbitrary", "arbitrary"), VMEM_LIMIT),
        name="attn_window",
    )(sink, q, kva, cache_k, cache_v)


NA_QROWS = 4
NA_BLOCKS = 2
NA_WIN_ROWS = 12
NA_Q = NA_QROWS * GRID_W
NA_KEYS = NA_WIN_ROWS * GRID_W
NA_DR = 2 * NA_ROWS_MAX - 1
NA_PAIRS = NA_DR + 1
RPB_W = 2 * NA_COLS - 1
RPB_PAD = GRID_W - NA_COLS


def _na_row_start(r):
    return jnp.clip(r - NA_ROWS // 2, 0, GRID_ROWS - NA_ROWS)


def _na_window_row0(blk):
    return jnp.clip(blk * NA_QROWS - NA_ROWS // 2, 0, GRID_ROWS - NA_WIN_ROWS)


def _check_na_windows():
    for blk in range(GRID_ROWS // NA_QROWS):
        w0 = int(np.clip(blk * NA_QROWS - NA_ROWS // 2, 0, GRID_ROWS - NA_WIN_ROWS))
        for r in range(blk * NA_QROWS, (blk + 1) * NA_QROWS):
            r0 = int(np.clip(r - NA_ROWS // 2, 0, GRID_ROWS - NA_ROWS))
            assert w0 <= r0 and r0 + NA_ROWS <= w0 + NA_WIN_ROWS, (blk, r)


_check_na_windows()


def _na_attn_kernel(q_ref, k_ref, v_ref, ck_ref, cv_ref, rpb_ref, s_ref, wada_ref, bada_ref,
                    o_ref, mods_ref, tab_ref, ckb, cvb):
    b, step = pl.program_id(0), pl.program_id(1)
    lane = lax.broadcasted_iota(jnp.int32, (GRID_W, 2 * GRID_W), 1)
    pl.when(step == 0)(functools.partial(_cache_to_bf16, B_HEADS, ck_ref, cv_ref, ckb, cvb))

    @pl.when((b == 0) & (step == 0))
    def _():
        for h in range(B_HEADS):
            for d in range(NA_PAIRS):
                lo = jnp.broadcast_to(rpb_ref[h, d:d + 1, :], (GRID_W, 2 * GRID_W))
                hi = jnp.broadcast_to(rpb_ref[h, d + 1:d + 2, :], (GRID_W, 2 * GRID_W))
                lo = pltpu.roll(lo, GRID_W + 1, 1, stride=1, stride_axis=0)
                hi = pltpu.roll(hi, 1, 1, stride=1, stride_axis=0)
                tab_ref[h, d] = jnp.where(lane < GRID_W, lo, hi) * LOG2E

    for sub in range(NA_BLOCKS):
        blk = step * NA_BLOCKS + sub
        rows = pl.ds(sub * NA_Q, NA_Q)
        row0 = _na_window_row0(blk)
        k0 = pl.multiple_of(row0 * GRID_W, GRID_W)
        qi = lax.broadcasted_iota(jnp.int32, (NA_Q, NA_KEYS), 0)
        ki = lax.broadcasted_iota(jnp.int32, (NA_Q, NA_KEYS), 1)
        qrow, qcol = blk * NA_QROWS + qi // GRID_W, qi % GRID_W
        krow, kcol = row0 + ki // GRID_W, ki % GRID_W
        rstart = _na_row_start(qrow)
        cstart = jnp.clip(qcol - NA_COLS // 2, 0, GRID_W - NA_COLS)
        valid = (krow >= rstart) & (krow < rstart + NA_ROWS) & (kcol >= cstart) & (kcol < cstart + NA_COLS)

        for h in range(B_HEADS):
            sl = slice(h * HEAD_DIM, (h + 1) * HEAD_DIM)
            bias = jnp.concatenate(
                [jnp.concatenate(
                    [tab_ref[h, jnp.clip(row0 + 2 * p - blk * NA_QROWS - q + NA_ROWS_MAX, 0, NA_PAIRS - 1)]
                     for p in range(NA_WIN_ROWS // 2)], axis=1)
                 for q in range(NA_QROWS)], axis=0)
            k_loc = k_ref[pl.ds(k0, NA_KEYS), sl]
            v_loc = v_ref[pl.ds(k0, NA_KEYS), sl]
            t_loc = jnp.where(valid, _dot_nt(q_ref[rows, sl], k_loc) * QK_LOG2 + bias, NEG)
            t_ctx = _dot_nt(q_ref[rows, sl], ckb[:, sl]) * QK_LOG2
            o = _softmax_pv([(t_loc, v_loc), (t_ctx, cvb[:, sl])], None)
            o_ref[rows, sl] = o.astype(BF16)
    _ada_side(s_ref, wada_ref, bada_ref, mods_ref)


def _na_bias_rows(rpb):
    rows = jnp.pad(rpb, ((0, 0), (1, 1), (0, 0)))
    left = jnp.broadcast_to(rows[..., :1], rows.shape[:2] + (RPB_PAD,))
    right = jnp.broadcast_to(rows[..., -1:], rows.shape[:2] + (2 * GRID_W - RPB_PAD - RPB_W,))
    return jnp.concatenate([left, rows, right], axis=-1)


def _na_attn_call(q, kv, cache_k, cache_v, rpb, ada):
    nblk = GRID_ROWS // (NA_QROWS * NA_BLOCKS)
    assert DEC_BATCH * nblk * ADA_SIDE_COLS == ADA_SIDE and ADA_HEAD + ADA_SIDE == 6 * D_MODEL
    side_in, side_out, side_shape = _ada_side_specs(ADA_HEAD, lambda b, r: b * nblk + r)
    q0 = M_PROMPT // (NA_Q * NA_BLOCKS)
    b0 = M_PROMPT // DEC_SEQ
    return pl.pallas_call(
        _na_attn_kernel,
        grid=(DEC_BATCH, nblk),
        in_specs=[pl.BlockSpec((None, NA_Q * NA_BLOCKS, B_QKV), lambda b, r: (PROJ_QG.index("qb"), q0 + b * nblk + r, 0)),
                  pl.BlockSpec((None, DEC_SEQ, B_QKV), lambda b, r: (PROJ_KV.index("kb"), b0 + b, 0)),
                  pl.BlockSpec((None, DEC_SEQ, B_QKV), lambda b, r: (PROJ_KV.index("vb"), b0 + b, 0)),
                  pl.BlockSpec((None, None, PAST_LEN, B_HEADS, HEAD_DIM), lambda b, r: (b, 0, 0, 0, 0)),
                  pl.BlockSpec((None, None, PAST_LEN, B_HEADS, HEAD_DIM), lambda b, r: (b, 0, 0, 0, 0)),
                  pl.BlockSpec((B_HEADS, NA_DR + 2, 2 * GRID_W), lambda b, r: (0, 0, 0))] + side_in,
        out_specs=[pl.BlockSpec((NA_Q * NA_BLOCKS, B_QKV), lambda b, r: (b * nblk + r, 0)), side_out],
        out_shape=[jax.ShapeDtypeStruct((M_SAMPLE, B_QKV), BF16), side_shape],
        scratch_shapes=[pltpu.VMEM((B_HEADS, NA_PAIRS, GRID_W, 2 * GRID_W), F32),
                        pltpu.VMEM((PAST_LEN, B_QKV), BF16), pltpu.VMEM((PAST_LEN, B_QKV), BF16)],
        compiler_params=_params(("arbitrary", "arbitrary"), VMEM_LIMIT),
        name="attn_neighbourhood",
    )(q, kv, kv, cache_k, cache_v, _na_bias_rows(rpb), *ada)


def _merge_kernel(oap, oas, obp, obs, wa_ref, wb_ref, ga_ref, gb_ref, z_ref, wa_bf, wb_bf):
    i = pl.program_id(1)
    p_tiles = M_PROMPT // TM

    @pl.when(i == 0)
    def _():
        wa_bf[...] = wa_ref[...].astype(BF16)
        wb_bf[...] = wb_ref[...].astype(BF16)

    def run(oa_ref, ob_ref):
        for rows in _row_chunks(TM):
            ya = _dot(oa_ref[rows, :], wa_bf[...])
            yb = _dot(ob_ref[rows, :], wb_bf[...])
            z_ref[rows, :] = (ga_ref[rows, :] * ya + gb_ref[rows, :] * yb).astype(BF16)

    pl.when(i < p_tiles)(functools.partial(run, oap, obp))
    pl.when(i >= p_tiles)(functools.partial(run, oas, obs))


def _merge_call(oa_p, oa_s, ob_p, ob_s, w_br_a, w_br_b, gates):
    p_tiles = M_PROMPT // TM
    nj = D_MODEL // TN
    pspec = pl.BlockSpec((TM, A_Q), lambda j, i: (jnp.minimum(i, p_tiles - 1), 0))
    sspec = pl.BlockSpec((TM, A_Q), lambda j, i: (jnp.maximum(i - p_tiles, 0), 0))
    once = dict(pipeline_mode=pl.Buffered(1))
    return pl.pallas_call(
        _merge_kernel,
        grid=(nj, M_ALL // TM),
        in_specs=[pspec, sspec, pspec, sspec,
                  pl.BlockSpec((A_Q, TN), lambda j, i: (0, j), **once),
                  pl.BlockSpec((B_QKV, TN), lambda j, i: (0, j), **once),
                  pl.BlockSpec((None, TM, TN), lambda j, i: (j, i, 0)),
                  pl.BlockSpec((None, TM, TN), lambda j, i: (nj + j, i, 0))],
        out_specs=pl.BlockSpec((TM, TN), lambda j, i: (i, j)),
        out_shape=jax.ShapeDtypeStruct((M_ALL, D_MODEL), BF16),
        scratch_shapes=[pltpu.VMEM((A_Q, TN), BF16), pltpu.VMEM((B_QKV, TN), BF16)],
        compiler_params=_params(("arbitrary", "arbitrary"), VMEM_LIMIT),
        name="merge_branches",
    )(oa_p, oa_s, ob_p, ob_s, w_br_a, w_br_b, gates, gates)


def _out_kernel(z_ref, w_ref, xp_ref, xs_ref, g_ref, o_ref, wbf):
    i = pl.program_id(1)
    p_tiles = M_PROMPT // TM

    @pl.when(i == 0)
    def _():
        wbf[...] = w_ref[...].astype(BF16)

    def run(x_ref):
        for rows in _row_chunks(TM):
            o_ref[rows, :] = x_ref[rows, :] + g_ref[...] * _dot(z_ref[rows, :], wbf[...])

    pl.when(i < p_tiles)(functools.partial(run, xp_ref))
    pl.when(i >= p_tiles)(functools.partial(run, xs_ref))


def _out_call(z, w_out, xp, xs, mods_tail):
    p_tiles = M_PROMPT // TM
    nj = D_MODEL // TN
    return pl.pallas_call(
        _out_kernel,
        grid=(nj, M_ALL // TM),
        in_specs=[pl.BlockSpec((TM, D_MODEL), lambda j, i: (i, 0)),
                  pl.BlockSpec((D_MODEL, TN), lambda j, i: (0, j)),
                  pl.BlockSpec((TM, TN), lambda j, i: (jnp.minimum(i, p_tiles - 1), j)),
                  pl.BlockSpec((TM, TN), lambda j, i: (jnp.maximum(i - p_tiles, 0), j)),
                  pl.BlockSpec((None, 1, TN), lambda j, i: (_mod_row(i, TM), 0, j))],
        out_specs=pl.BlockSpec((TM, TN), lambda j, i: (i, j)),
        out_shape=jax.ShapeDtypeStruct((M_ALL, D_MODEL), F32),
        scratch_shapes=[pltpu.VMEM((D_MODEL, TN), BF16)],
        compiler_params=_params(("arbitrary", "arbitrary"), VMEM_LIMIT),
        name="out_proj_residual",
    )(z, w_out, xp, xs, mods_tail)


def _mlp_kernel(x_hbm, nw_ref, sh_ref, sc_ref, g_ref, wu_ref, wd_ref, o_ref, h_ref, x_buf, x_sem, *, tile0):
    i, f = pl.program_id(0), pl.program_id(1)

    def x_copy(tile):
        return pltpu.make_async_copy(x_hbm.at[pl.ds((tile0 + tile) * TM, TM), :], x_buf, x_sem)

    @pl.when((i == 0) & (f == 0))
    def _():
        x_copy(0).start()

    @pl.when(f == 0)
    def _():
        x_copy(i).wait()
        wu = wu_ref[...].astype(BF16)
        wd = wd_ref[...].astype(BF16)
        for r0 in range(0, TM, ROW_CHUNK):
            rows = slice(r0, r0 + ROW_CHUNK)
            x = x_buf[rows, :]
            h = _modnorm(x, nw_ref[...], sc_ref[...], sh_ref[...]).astype(BF16)
            h_ref[rows, :] = h
            u = jnp.square(jnp.maximum(_dot(h, wu), 0.0)).astype(BF16)
            o_ref[rows, :] = x + g_ref[...] * _dot(u, wd)

    @pl.when((f == 1) & (i + 1 < pl.num_programs(0)))
    def _():
        x_copy(i + 1).start()

    @pl.when(f > 0)
    def _():
        u = _dot(h_ref[...], wu_ref[...].astype(BF16))
        u = jnp.square(jnp.maximum(u, 0.0)).astype(BF16)
        for c0 in range(0, D_MODEL, MLP_COLS):
            cols = slice(c0, c0 + MLP_COLS)
            o_ref[:, cols] += g_ref[:, cols] * _dot(u, wd_ref[:, cols].astype(BF16))


def _mlp_call(x1, nw, mods_tail, w_up, w_down, tile0, n_tiles):
    mod = lambda k: pl.BlockSpec((None, 1, D_MODEL), lambda i, f: (_mod_row(tile0 + i, TM), 0, k))
    return pl.pallas_call(
        functools.partial(_mlp_kernel, tile0=tile0),
        grid=(n_tiles, D_FF // TF),
        in_specs=[pl.BlockSpec(memory_space=pl.ANY),
                  pl.BlockSpec((1, D_MODEL), lambda i, f: (0, 0)),
                  mod(1), mod(2), mod(3),
                  pl.BlockSpec((D_MODEL, TF), lambda i, f: (0, f)),
                  pl.BlockSpec((TF, D_MODEL), lambda i, f: (f, 0))],
        out_specs=pl.BlockSpec((TM, D_MODEL), lambda i, f: (i, 0)),
        out_shape=jax.ShapeDtypeStruct((n_tiles * TM, D_MODEL), F32),
        scratch_shapes=[pltpu.VMEM((TM, D_MODEL), BF16),
                        pltpu.VMEM((TM, D_MODEL), F32),
                        pltpu.SemaphoreType.DMA(())],
        compiler_params=_params(("arbitrary", "arbitrary"), VMEM_LIMIT),
        name="mlp",
    )(x1, nw, mods_tail, mods_tail, mods_tail, w_up, w_down)


def kernel(x_prompt, x_sample, cache_a_k, cache_a_v, cache_b_k, cache_b_v, c, c_ctx, norm1_w, norm2_w, w_ada, b_ada, w_in, q_norm_a, k_norm_a, q_norm_b, k_norm_b, sink_a, rpb_b, w_br_a, w_br_b, w_out, w_up, w_down):
    assert w_ada.shape[0] == 1, "one trunk layer"
    xp = x_prompt.reshape(M_PROMPT, D_MODEL)
    xs = x_sample.reshape(M_SAMPLE, D_MODEL)

    mods_head, cond = _ada_call(c_ctx[None, :], c, w_ada[0], b_ada)
    ada = (cond, w_ada[0], b_ada)

    w = w_in[0]
    rope = _rope_tables()
    h, kva, ka_p, va_p = _norm_kva_call(xp, xs, norm1_w, mods_head, w, k_norm_a, rope)
    kv, kb_p, vb_p = _proj_call(h, w, q_norm_a, q_norm_b, k_norm_b, rope, PROJ_KV)
    q, gates = _proj_call(h, w, q_norm_a, q_norm_b, k_norm_b, rope, PROJ_QG)

    sink = sink_a[0]
    oa_p, ob_p = _ctx_attn_call(sink, q, kv, kva)
    oa_s = _win_attn_call(sink, q, kva, cache_a_k, cache_a_v)
    ob_s, mods_tail = _na_attn_call(q, kv, cache_b_k, cache_b_v, rpb_b[0], ada)

    z = _merge_call(oa_p, oa_s, ob_p, ob_s, w_br_a[0], w_br_b[0], gates)
    x1 = _out_call(z, w_out[0], xp, xs, mods_tail)

    p_tiles = M_PROMPT // TM
    y_p = _mlp_call(x1, norm2_w, mods_tail, w_up[0], w_down[0], 0, p_tiles)
    y_s = _mlp_call(x1, norm2_w, mods_tail, w_up[0], w_down[0], p_tiles, M_SAMPLE // TM)

    return (y_p.reshape(BATCH, SEQ, D_MODEL),
            y_s.reshape(DEC_BATCH, DEC_SEQ, D_MODEL),
            ka_p.reshape(BATCH, 1, SEQ, A_KV_HEADS, HEAD_DIM),
            va_p.reshape(BATCH, 1, SEQ, A_KV_HEADS, HEAD_DIM),
            kb_p.reshape(BATCH, 1, SEQ, B_HEADS, HEAD_DIM),
            vb_p.reshape(BATCH, 1, SEQ, B_HEADS, HEAD_DIM))
```

```python
import functools

import numpy as np
import jax
import jax.numpy as jnp
from jax import lax
from jax.experimental import pallas as pl
from jax.experimental.pallas import tpu as pltpu

D_MODEL = 2048
BATCH = 16
SEQ = 256
DEC_BATCH = 2
DEC_SEQ = 1024
PAST_LEN = 256
GRID_W = 64
HEAD_DIM = 128
A_HEADS = 8
A_KV_HEADS = 2
A_GROUP = A_HEADS // A_KV_HEADS
A_WINDOW = 128
BLOCK = 128
B_HEADS = 8
NA_ROWS_MAX = 8
NA_COLS = 16
D_FF = 4 * D_MODEL
ROPE_THETA = 10000.0
EPS = 1e-6
NEG = -1e30
A_Q = A_HEADS * HEAD_DIM
A_KV = A_KV_HEADS * HEAD_DIM
B_QKV = B_HEADS * HEAD_DIM
IN_WIDTH = A_Q + 2 * A_KV + 3 * B_QKV + 2 * D_MODEL
SCALE = HEAD_DIM ** -0.5

M_PROMPT = BATCH * SEQ
M_SAMPLE = DEC_BATCH * DEC_SEQ
M_ALL = M_PROMPT + M_SAMPLE
GRID_ROWS = DEC_SEQ // GRID_W
NA_ROWS = min(NA_ROWS_MAX, GRID_ROWS)

COL_QA = 0
COL_KVA = A_Q
COL_QB = A_Q + 2 * A_KV
COL_KB = COL_QB + B_QKV
COL_VB = COL_KB + B_QKV
COL_G = COL_VB + B_QKV

TM = 1024
TN = 1024
ROW_CHUNK = 256
W_SUB = 512
TM_NORM = 512
TF = 512
MLP_COLS = 512
ADA_HEAD = 2 * D_MODEL
ADA_SIDE_COLS = 1024
ADA_SIDE = 2 * D_MODEL
VMEM_LIMIT = 60 * 1024 * 1024

F32 = jnp.float32
BF16 = jnp.bfloat16


def _mod_row(i, tm):
    p_tiles = M_PROMPT // tm
    return jnp.where(i < p_tiles, 0, 1 + (i - p_tiles) // (DEC_SEQ // tm))


def _row_chunks(tm):
    edges = list(range(0, tm, ROW_CHUNK)) + [tm - ROW_CHUNK // 2, tm]
    edges = sorted(set(edges))
    return [slice(a, b) for a, b in zip(edges[:-1], edges[1:])]


def _dot(a, b):
    return jnp.dot(a, b, preferred_element_type=F32)


def _dot_nt(a, b):
    return lax.dot_general(a, b, (((1,), (1,)), ((), ())), preferred_element_type=F32)


def _params(sem, vmem=None):
    return pltpu.CompilerParams(dimension_semantics=sem, vmem_limit_bytes=vmem)


def _ada_side(s_ref, w_ref, b_ref, o_ref):
    o_ref[:, 0, :] = _dot(s_ref[...], w_ref[...].astype(BF16)) + b_ref[...]


def _ada_kernel(cctx_ref, c_ref, w_ref, b_ref, o_ref, s_ref):
    row = lax.broadcasted_iota(jnp.int32, (8, D_MODEL), 0)
    cv = jnp.where(row == 0, cctx_ref[...], 0.0)
    for b in range(DEC_BATCH):
        cv = jnp.where(row == 1 + b, c_ref[b:b + 1, :], cv)
    s_ref[...] = (cv * jax.nn.sigmoid(cv)).astype(BF16)
    _ada_side(s_ref, w_ref, b_ref, o_ref)


def _ada_call(c_ctx, c, w_ada, b_ada):
    tn = 1024
    n = ADA_HEAD
    return pl.pallas_call(
        _ada_kernel,
        grid=(n // tn,),
        in_specs=[pl.BlockSpec((1, D_MODEL), lambda j: (0, 0)),
                  pl.BlockSpec((DEC_BATCH, D_MODEL), lambda j: (0, 0)),
                  pl.BlockSpec((D_MODEL, tn), lambda j: (0, j)),
                  pl.BlockSpec((1, tn), lambda j: (0, j))],
        out_specs=[pl.BlockSpec((8, 1, tn), lambda j: (0, 0, j)),
                   pl.BlockSpec((8, D_MODEL), lambda j: (0, 0))],
        out_shape=[jax.ShapeDtypeStruct((8, 1, n), F32),
                   jax.ShapeDtypeStruct((8, D_MODEL), BF16)],
        compiler_params=_params(("arbitrary",), VMEM_LIMIT),
        name="ada_mod",
    )(c_ctx, c, w_ada, b_ada)


def _ada_side_specs(first_col, step):
    blk = lambda *g: first_col // ADA_SIDE_COLS + step(*g)
    in_specs = [pl.BlockSpec((8, D_MODEL), lambda *g: (0, 0)),
                pl.BlockSpec((D_MODEL, ADA_SIDE_COLS), lambda *g: (0, blk(*g))),
                pl.BlockSpec((1, ADA_SIDE_COLS), lambda *g: (0, blk(*g)))]
    out_spec = pl.BlockSpec((8, 1, ADA_SIDE_COLS), lambda *g: (0, 0, step(*g)))
    return in_specs, out_spec, jax.ShapeDtypeStruct((8, 1, ADA_SIDE), F32)


def _modnorm(x, nw, sc, sh):
    y = x * lax.rsqrt(jnp.mean(x * x, axis=-1, keepdims=True) + EPS)
    return y * (nw * (1.0 + sc)) + sh


def _head_norm(x, nw):
    return x * lax.rsqrt(jnp.mean(x * x, axis=-1, keepdims=True) + EPS) * nw


def _rope(x, cos, sin_signed):
    lane = lax.broadcasted_iota(jnp.int32, x.shape, 1)
    partner = jnp.where((lane % 64) < 32, pltpu.roll(x, 96, 1), pltpu.roll(x, 32, 1))
    return x * cos + partner * sin_signed


def _norm_kva_kernel(xp_ref, xs_ref, nw1_ref, sh_ref, sc_ref, w_ref, nwk_ref, cos_ref, sin_ref,
                     h_ref, kv_ref, kp_ref, vp_ref, wbf):
    i = pl.program_id(0)
    p_tiles = M_PROMPT // TM_NORM

    @pl.when(i == 0)
    def _():
        wbf[...] = w_ref[...].astype(BF16)

    def head_cols(k, base=0):
        return slice(base + k * HEAD_DIM, base + (k + 1) * HEAD_DIM)

    def run(prompt):
        x_ref = xp_ref if prompt else xs_ref
        for r0 in range(0, TM_NORM, ROW_CHUNK):
            rows = slice(r0, r0 + ROW_CHUNK)
            h = _modnorm(x_ref[rows, :], nw1_ref[...], sc_ref[...], sh_ref[...]).astype(BF16)
            h_ref[rows, :] = h
            acc = _dot(h, wbf[...])
            for k in range(A_KV_HEADS):
                y = _head_norm(acc[:, head_cols(k)], nwk_ref[...])
                v = acc[:, head_cols(k, A_KV)]
                if prompt:
                    kp_ref[rows, k, :] = y
                    vp_ref[rows, k, :] = v
                else:
                    y = _rope(y, cos_ref[rows, :], sin_ref[rows, :])
                kv_ref[rows, head_cols(k)] = y.astype(BF16)
                kv_ref[rows, head_cols(k, A_KV)] = v.astype(BF16)

    pl.when(i < p_tiles)(functools.partial(run, True))
    pl.when(i >= p_tiles)(functools.partial(run, False))


def _norm_kva_call(xp, xs, nw1, mods3, w_in, nwk, rope):
    tm = TM_NORM
    p_tiles = M_PROMPT // tm
    s_tiles = DEC_SEQ // tm
    assert COL_KVA % (2 * A_KV) == 0
    rope_spec = pl.BlockSpec((tm, HEAD_DIM), lambda i: (jnp.maximum(i - p_tiles, 0) % s_tiles, 0))
    parked = pl.BlockSpec((tm, A_KV_HEADS, HEAD_DIM), lambda i: (jnp.minimum(i, p_tiles - 1), 0, 0))
    return pl.pallas_call(
        _norm_kva_kernel,
        grid=(M_ALL // tm,),
        in_specs=[pl.BlockSpec((tm, D_MODEL), lambda i: (jnp.minimum(i, p_tiles - 1), 0)),
                  pl.BlockSpec((tm, D_MODEL), lambda i: (jnp.maximum(i - p_tiles, 0), 0)),
                  pl.BlockSpec((1, D_MODEL), lambda i: (0, 0)),
                  pl.BlockSpec((None, 1, D_MODEL), lambda i: (_mod_row(i, tm), 0, 0)),
                  pl.BlockSpec((None, 1, D_MODEL), lambda i: (_mod_row(i, tm), 0, 1)),
                  pl.BlockSpec((D_MODEL, 2 * A_KV), lambda i: (0, COL_KVA // (2 * A_KV)),
                               pipeline_mode=pl.Buffered(1)),
                  pl.BlockSpec((1, HEAD_DIM), lambda i: (0, 0)),
                  rope_spec, rope_spec],
        out_specs=[pl.BlockSpec((tm, D_MODEL), lambda i: (i, 0)),
                   pl.BlockSpec((tm, 2 * A_KV), lambda i: (i, 0)),
                   parked, parked],
        out_shape=[jax.ShapeDtypeStruct((M_ALL, D_MODEL), BF16),
                   jax.ShapeDtypeStruct((M_ALL, 2 * A_KV), BF16),
                   jax.ShapeDtypeStruct((M_PROMPT, A_KV_HEADS, HEAD_DIM), F32),
                   jax.ShapeDtypeStruct((M_PROMPT, A_KV_HEADS, HEAD_DIM), F32)],
        scratch_shapes=[pltpu.VMEM((D_MODEL, 2 * A_KV), BF16)],
        compiler_params=_params(("arbitrary",), VMEM_LIMIT),
        name="norm1_proj_kva",
    )(xp, xs, nw1, mods3, mods3, w_in, nwk, *rope)


def _rope_tables():
    n_freq = HEAD_DIM // 4
    pos = np.arange(DEC_SEQ)
    row = (pos // GRID_W).astype(np.float64)
    col = (pos % GRID_W).astype(np.float64)
    inv = ROPE_THETA ** (-np.arange(n_freq, dtype=np.float64) / n_freq)
    ar = row[:, None] * inv
    ac = col[:, None] * inv
    cos = np.concatenate([np.cos(ar), np.cos(ar), np.cos(ac), np.cos(ac)], axis=-1)
    sin = np.concatenate([-np.sin(ar), np.sin(ar), -np.sin(ac), np.sin(ac)], axis=-1)
    return jnp.asarray(cos, F32), jnp.asarray(sin, F32)


PROJ_COL = {"qa": COL_QA, "qb": COL_QB, "kb": COL_KB, "vb": COL_VB}
GATE_PANELS = 2 * D_MODEL // TN
PROJ_KV = ("kb", "vb")
PROJ_QG = ("qa", "qb") + ("gate",) * GATE_PANELS


def _proj_kernel(*refs, kinds):
    names = ["h", "w0", "w1", "nqa", "nqb", "nkb", "cos", "sin", "qkv"]
    names += ["gate"] * ("gate" in kinds) + ["kbp"] * ("kb" in kinds) + ["vbp"] * ("vb" in kinds) + ["wbf"]
    r = dict(zip(names, refs, strict=True))
    n_qkv = sum(k != "gate" for k in kinds)
    j, i = pl.program_id(0), pl.program_id(1)
    is_prompt = i < M_PROMPT // TM

    @pl.when(i == 0)
    def _():
        r["wbf"][:, :W_SUB] = r["w0"][...].astype(BF16)
        r["wbf"][:, W_SUB:] = r["w1"][...].astype(BF16)

    def head_cols(k):
        return slice(k * HEAD_DIM, (k + 1) * HEAD_DIM)

    def run(kind, prompt):
        for rows in _row_chunks(TM):
            acc = _dot(r["h"][rows, :], r["wbf"][...])
            if kind == "gate":
                r["gate"][rows, :] = jax.nn.sigmoid(acc)
            elif kind == "vb":
                if prompt:
                    r["vbp"][rows, :] = acc
                r["qkv"][rows, :] = acc.astype(BF16)
            else:
                nw_ref = r[{"qa": "nqa", "qb": "nqb", "kb": "nkb"}[kind]]
                for k in range(TN // HEAD_DIM):
                    y = _head_norm(acc[:, head_cols(k)], nw_ref[...])
                    if kind == "qa" and not prompt:
                        y = _rope(y, r["cos"][rows, :], r["sin"][rows, :])
                    if kind == "kb" and prompt:
                        r["kbp"][rows, head_cols(k)] = y
                    r["qkv"][rows, head_cols(k)] = y.astype(BF16)

    for p, kind in enumerate(kinds[:n_qkv]):
        if kind == "qb":
            pl.when(j == p)(functools.partial(run, kind, None))
        else:
            pl.when((j == p) & is_prompt)(functools.partial(run, kind, True))
            pl.when((j == p) & jnp.logical_not(is_prompt))(functools.partial(run, kind, False))
    if "gate" in kinds:
        pl.when(j >= n_qkv)(functools.partial(run, "gate", None))


def _proj_call(h, w_in, nqa, nqb, nkb, rope, kinds):
    assert TN == 2 * W_SUB and A_Q == B_QKV == TN
    n_panels = len(kinds)
    n_qkv = sum(k != "gate" for k in kinds)
    assert all(k == "gate" for k in kinds[n_qkv:])
    cols = [PROJ_COL[k] for k in kinds[:n_qkv]] + list(range(COL_G, IN_WIDTH, TN))[:n_panels - n_qkv]
    p_tiles = M_PROMPT // TM
    s_tiles = DEC_SEQ // TM
    n_tiles = M_ALL // TM

    def w_sub(j):
        idx = cols[0] // W_SUB
        for p in range(1, n_panels):
            idx = jnp.where(j == p, cols[p] // W_SUB, idx)
        return idx

    def qkv_idx(j, i):
        return (jnp.minimum(j, n_qkv - 1), jnp.where(j >= n_qkv, n_tiles - 1, i), 0)

    def gate_idx(j, i):
        return (jnp.maximum(j - n_qkv, 0), jnp.where(j < n_qkv, 0, i), 0)

    def cache_idx(panel):
        def idx(j, i):
            return (jnp.where(j < panel, 0, jnp.where(j > panel, p_tiles - 1, jnp.minimum(i, p_tiles - 1))), 0)
        return idx

    out_specs = [pl.BlockSpec((None, TM, TN), qkv_idx)]
    out_shape = [jax.ShapeDtypeStruct((n_qkv, M_ALL, TN), BF16)]
    if "gate" in kinds:
        out_specs.append(pl.BlockSpec((None, TM, TN), gate_idx))
        out_shape.append(jax.ShapeDtypeStruct((n_panels - n_qkv, M_ALL, TN), F32))
    for kind in ("kb", "vb"):
        if kind in kinds:
            out_specs.append(pl.BlockSpec((TM, TN), cache_idx(kinds.index(kind))))
            out_shape.append(jax.ShapeDtypeStruct((M_PROMPT, B_QKV), F32))

    norm_spec = pl.BlockSpec((1, HEAD_DIM), lambda j, i: (0, 0))
    rope_spec = pl.BlockSpec((TM, HEAD_DIM), lambda j, i: (jnp.maximum(i - p_tiles, 0) % s_tiles, 0))
    in_specs = [pl.BlockSpec((TM, D_MODEL), lambda j, i: (i, 0)),
                pl.BlockSpec((D_MODEL, W_SUB), lambda j, i: (0, w_sub(j))),
                pl.BlockSpec((D_MODEL, W_SUB), lambda j, i: (0, w_sub(j) + 1)),
                norm_spec, norm_spec, norm_spec, rope_spec, rope_spec]
    args = [h, w_in, w_in, nqa, nqb, nkb, *rope]
    return pl.pallas_call(
        functools.partial(_proj_kernel, kinds=kinds),
        grid=(n_panels, n_tiles),
        in_specs=in_specs,
        out_specs=out_specs,
        out_shape=out_shape,
        scratch_shapes=[pltpu.VMEM((D_MODEL, TN), BF16)],
        compiler_params=_params(("arbitrary", "arbitrary"), VMEM_LIMIT),
        name="proj_" + "_".join(dict.fromkeys(kinds)),
    )(*args)


LOG2E = 1.4426950408889634
QK_LOG2 = SCALE * LOG2E


def _softmax_pv(parts, sink):
    m = functools.reduce(jnp.maximum, [jnp.max(t, axis=-1, keepdims=True) for t, _ in parts])
    if sink is not None:
        m = jnp.maximum(m, sink)
    acc = None
    for t, v in parts:
        p = jnp.exp2(t - m).astype(BF16)
        y = _dot(p, jnp.concatenate([v, jnp.ones_like(v)], axis=1))
        acc = y if acc is None else acc + y
    o, l = acc[:, :HEAD_DIM], acc[:, HEAD_DIM:]
    if sink is not None:
        l = l + jnp.exp2(sink - m)
    return o / l


def _sink_column(sink_ref, hk, rows_per_head):
    n = A_GROUP * rows_per_head
    g = lax.broadcasted_iota(jnp.int32, (n, 1), 0) // rows_per_head
    col = jnp.full((n, 1), sink_ref[hk * A_GROUP], F32)
    for k in range(1, A_GROUP):
        col = jnp.where(g == k, sink_ref[hk * A_GROUP + k], col)
    return col * LOG2E


def _stack_group(q_ref, hk):
    return jnp.concatenate(
        [q_ref[:, (hk * A_GROUP + g) * HEAD_DIM:(hk * A_GROUP + g + 1) * HEAD_DIM] for g in range(A_GROUP)], axis=0)


def _unstack_group(o_ref, hk, o, rows):
    for g in range(A_GROUP):
        c = (hk * A_GROUP + g) * HEAD_DIM
        o_ref[:, c:c + HEAD_DIM] = o[g * rows:(g + 1) * rows].astype(BF16)


CTX_REQS = 4


def _ctx_attn_kernel(sink_ref, q_ref, kv_ref, kva_ref, s_ref, wada_ref, bada_ref, oa_ref, ob_ref, mods_ref):
    for req in range(CTX_REQS):
        rows = pl.ds(req * SEQ, SEQ)
        qa_ref, qb_ref = q_ref.at[PROJ_QG.index("qa"), rows], q_ref.at[PROJ_QG.index("qb"), rows]
        kb_ref, vb_ref = kv_ref.at[PROJ_KV.index("kb"), rows], kv_ref.at[PROJ_KV.index("vb"), rows]
        ka_ref, oa, ob = kva_ref.at[rows], oa_ref.at[rows], ob_ref.at[rows]
        for hk in range(A_KV_HEADS):
            k = ka_ref[:, hk * HEAD_DIM:(hk + 1) * HEAD_DIM]
            v = ka_ref[:, A_KV + hk * HEAD_DIM:A_KV + (hk + 1) * HEAD_DIM]
            q4 = _stack_group(qa_ref, hk)
            t = _dot_nt(q4, k) * QK_LOG2
            o = _softmax_pv([(t, v)], _sink_column(sink_ref, hk, SEQ))
            _unstack_group(oa, hk, o, SEQ)
        for h in range(B_HEADS):
            sl = slice(h * HEAD_DIM, (h + 1) * HEAD_DIM)
            t = _dot_nt(qb_ref[:, sl], kb_ref[:, sl]) * QK_LOG2
            ob[:, sl] = _softmax_pv([(t, vb_ref[:, sl])], None).astype(BF16)
    _ada_side(s_ref, wada_ref, bada_ref, mods_ref)


def _ctx_attn_call(sink, q, kv, kva, ada):
    rows = CTX_REQS * SEQ
    row = lambda w: pl.BlockSpec((rows, w), lambda b: (b, 0))
    assert (BATCH // CTX_REQS) * ADA_SIDE_COLS == ADA_SIDE
    side_in, side_out, side_shape = _ada_side_specs(ADA_HEAD, lambda b: b)
    return pl.pallas_call(
        _ctx_attn_kernel,
        grid=(BATCH // CTX_REQS,),
        in_specs=[pl.BlockSpec(memory_space=pltpu.SMEM),
                  pl.BlockSpec((2, rows, TN), lambda b: (0, b, 0)),
                  pl.BlockSpec((2, rows, TN), lambda b: (0, b, 0)), row(2 * A_KV)] + side_in,
        out_specs=[row(A_Q), row(B_QKV), side_out],
        out_shape=[jax.ShapeDtypeStruct((M_PROMPT, A_Q), BF16),
                   jax.ShapeDtypeStruct((M_PROMPT, B_QKV), BF16), side_shape],
        compiler_params=_params(("arbitrary",), VMEM_LIMIT),
        name="attn_ctx",
    )(sink, q, kv, kva, *ada)


BAND = 3 * BLOCK


def _cache_to_bf16(n_heads, ck_ref, cv_ref, ckb, cvb):
    for h in range(n_heads):
        sl = slice(h * HEAD_DIM, (h + 1) * HEAD_DIM)
        ckb[:, sl] = ck_ref[:, h, :].astype(BF16)
        cvb[:, sl] = cv_ref[:, h, :].astype(BF16)


WIN_BLOCKS = 4


def _win_attn_kernel(sink_ref, q_ref, kv_ref, ck_ref, cv_ref, s_ref, wada_ref, bada_ref, o_ref, mods_ref, ckb, cvb):
    step = pl.program_id(1)
    pl.when(step == 0)(functools.partial(_cache_to_bf16, A_KV_HEADS, ck_ref, cv_ref, ckb, cvb))
    for blk in range(WIN_BLOCKS):
        n = step * WIN_BLOCKS + blk
        rows = pl.ds(blk * BLOCK, BLOCK)
        start = pl.multiple_of(jnp.clip((n - 1) * BLOCK, 0, DEC_SEQ - BAND), BLOCK)
        qpos = n * BLOCK + lax.broadcasted_iota(jnp.int32, (BLOCK, BAND), 0)
        kpos = start + lax.broadcasted_iota(jnp.int32, (BLOCK, BAND), 1)
        valid = jnp.abs(qpos - kpos) <= A_WINDOW
        valid = jnp.concatenate([valid.astype(jnp.int32)] * A_GROUP, axis=0) > 0
        for hk in range(A_KV_HEADS):
            sl = slice(hk * HEAD_DIM, (hk + 1) * HEAD_DIM)
            slv = slice(A_KV + hk * HEAD_DIM, A_KV + (hk + 1) * HEAD_DIM)
            k_loc = kv_ref[pl.ds(start, BAND), sl]
            v_loc = kv_ref[pl.ds(start, BAND), slv]
            q4 = _stack_group(q_ref.at[rows], hk)
            t_loc = jnp.where(valid, _dot_nt(q4, k_loc) * QK_LOG2, NEG)
            t_ctx = _dot_nt(q4, ckb[:, sl]) * QK_LOG2
            o = _softmax_pv([(t_loc, v_loc), (t_ctx, cvb[:, sl])], _sink_column(sink_ref, hk, BLOCK))
            _unstack_group(o_ref.at[rows], hk, o, BLOCK)
    _ada_side(s_ref, wada_ref, bada_ref, mods_ref)


def _win_attn_call(sink, q, kva, cache_k, cache_v, ada):
    rows = WIN_BLOCKS * BLOCK
    nb = DEC_SEQ // rows
    q0 = M_PROMPT // rows
    b0 = M_PROMPT // DEC_SEQ
    assert DEC_BATCH * nb * ADA_SIDE_COLS == ADA_SIDE and ADA_HEAD + 2 * ADA_SIDE == 6 * D_MODEL
    side_in, side_out, side_shape = _ada_side_specs(ADA_HEAD + ADA_SIDE, lambda b, n: b * nb + n)
    return pl.pallas_call(
        _win_attn_kernel,
        grid=(DEC_BATCH, nb),
        in_specs=[pl.BlockSpec(memory_space=pltpu.SMEM),
                  pl.BlockSpec((None, rows, A_Q), lambda b, n: (PROJ_QG.index("qa"), q0 + b * nb + n, 0)),
                  pl.BlockSpec((DEC_SEQ, 2 * A_KV), lambda b, n: (b0 + b, 0)),
                  pl.BlockSpec((None, None, PAST_LEN, A_KV_HEADS, HEAD_DIM), lambda b, n: (b, 0, 0, 0, 0)),
                  pl.BlockSpec((None, None, PAST_LEN, A_KV_HEADS, HEAD_DIM), lambda b, n: (b, 0, 0, 0, 0))] + side_in,
        out_specs=[pl.BlockSpec((rows, A_Q), lambda b, n: (b * nb + n, 0)), side_out],
        out_shape=[jax.ShapeDtypeStruct((M_SAMPLE, A_Q), BF16), side_shape],
        scratch_shapes=[pltpu.VMEM((PAST_LEN, A_KV), BF16), pltpu.VMEM((PAST_LEN, A_KV), BF16)],
        compiler_params=_params(("arbitrary", "arbitrary"), VMEM_LIMIT),
        name="attn_window",
    )(sink, q, kva, cache_k, cache_v, *ada)


NA_QROWS = 4
NA_BLOCKS = 2
NA_WIN_ROWS = 12
NA_Q = NA_QROWS * GRID_W
NA_KEYS = NA_WIN_ROWS * GRID_W
NA_DR = 2 * NA_ROWS_MAX - 1
NA_PAIRS = NA_DR + 1
RPB_W = 2 * NA_COLS - 1
RPB_PAD = GRID_W - NA_COLS


def _na_row_start(r):
    return jnp.clip(r - NA_ROWS // 2, 0, GRID_ROWS - NA_ROWS)


def _na_window_row0(blk):
    return jnp.clip(blk * NA_QROWS - NA_ROWS // 2, 0, GRID_ROWS - NA_WIN_ROWS)


def _check_na_windows():
    for blk in range(GRID_ROWS // NA_QROWS):
        w0 = int(np.clip(blk * NA_QROWS - NA_ROWS // 2, 0, GRID_ROWS - NA_WIN_ROWS))
        for r in range(blk * NA_QROWS, (blk + 1) * NA_QROWS):
            r0 = int(np.clip(r - NA_ROWS // 2, 0, GRID_ROWS - NA_ROWS))
            assert w0 <= r0 and r0 + NA_ROWS <= w0 + NA_WIN_ROWS, (blk, r)


_check_na_windows()


def _na_attn_kernel(q_ref, k_ref, v_ref, ck_ref, cv_ref, rpb_ref, o_ref, tab_ref, ckb, cvb):
    b, step = pl.program_id(0), pl.program_id(1)
    lane = lax.broadcasted_iota(jnp.int32, (GRID_W, 2 * GRID_W), 1)
    pl.when(step == 0)(functools.partial(_cache_to_bf16, B_HEADS, ck_ref, cv_ref, ckb, cvb))

    @pl.when((b == 0) & (step == 0))
    def _():
        for h in range(B_HEADS):
            for d in range(NA_PAIRS):
                lo = jnp.broadcast_to(rpb_ref[h, d:d + 1, :], (GRID_W, 2 * GRID_W))
                hi = jnp.broadcast_to(rpb_ref[h, d + 1:d + 2, :], (GRID_W, 2 * GRID_W))
                lo = pltpu.roll(lo, GRID_W + 1, 1, stride=1, stride_axis=0)
                hi = pltpu.roll(hi, 1, 1, stride=1, stride_axis=0)
                tab_ref[h, d] = jnp.where(lane < GRID_W, lo, hi) * LOG2E

    for sub in range(NA_BLOCKS):
        blk = step * NA_BLOCKS + sub
        rows = pl.ds(sub * NA_Q, NA_Q)
        row0 = _na_window_row0(blk)
        k0 = pl.multiple_of(row0 * GRID_W, GRID_W)
        qi = lax.broadcasted_iota(jnp.int32, (NA_Q, NA_KEYS), 0)
        ki = lax.broadcasted_iota(jnp.int32, (NA_Q, NA_KEYS), 1)
        qrow, qcol = blk * NA_QROWS + qi // GRID_W, qi % GRID_W
        krow, kcol = row0 + ki // GRID_W, ki % GRID_W
        rstart = _na_row_start(qrow)
        cstart = jnp.clip(qcol - NA_COLS // 2, 0, GRID_W - NA_COLS)
        valid = (krow >= rstart) & (krow < rstart + NA_ROWS) & (kcol >= cstart) & (kcol < cstart + NA_COLS)

        for h in range(B_HEADS):
            sl = slice(h * HEAD_DIM, (h + 1) * HEAD_DIM)
            bias = jnp.concatenate(
                [jnp.concatenate(
                    [tab_ref[h, jnp.clip(row0 + 2 * p - blk * NA_QROWS - q + NA_ROWS_MAX, 0, NA_PAIRS - 1)]
                     for p in range(NA_WIN_ROWS // 2)], axis=1)
                 for q in range(NA_QROWS)], axis=0)
            k_loc = k_ref[pl.ds(k0, NA_KEYS), sl]
            v_loc = v_ref[pl.ds(k0, NA_KEYS), sl]
            t_loc = jnp.where(valid, _dot_nt(q_ref[rows, sl], k_loc) * QK_LOG2 + bias, NEG)
            t_ctx = _dot_nt(q_ref[rows, sl], ckb[:, sl]) * QK_LOG2
            o = _softmax_pv([(t_loc, v_loc), (t_ctx, cvb[:, sl])], None)
            o_ref[rows, sl] = o.astype(BF16)


def _na_bias_rows(rpb):
    rows = jnp.pad(rpb, ((0, 0), (1, 1), (0, 0)))
    left = jnp.broadcast_to(rows[..., :1], rows.shape[:2] + (RPB_PAD,))
    right = jnp.broadcast_to(rows[..., -1:], rows.shape[:2] + (2 * GRID_W - RPB_PAD - RPB_W,))
    return jnp.concatenate([left, rows, right], axis=-1)


def _na_attn_call(q, kv, cache_k, cache_v, rpb):
    nblk = GRID_ROWS // (NA_QROWS * NA_BLOCKS)
    q0 = M_PROMPT // (NA_Q * NA_BLOCKS)
    b0 = M_PROMPT // DEC_SEQ
    return pl.pallas_call(
        _na_attn_kernel,
        grid=(DEC_BATCH, nblk),
        in_specs=[pl.BlockSpec((None, NA_Q * NA_BLOCKS, B_QKV), lambda b, r: (PROJ_QG.index("qb"), q0 + b * nblk + r, 0)),
                  pl.BlockSpec((None, DEC_SEQ, B_QKV), lambda b, r: (PROJ_KV.index("kb"), b0 + b, 0)),
                  pl.BlockSpec((None, DEC_SEQ, B_QKV), lambda b, r: (PROJ_KV.index("vb"), b0 + b, 0)),
                  pl.BlockSpec((None, None, PAST_LEN, B_HEADS, HEAD_DIM), lambda b, r: (b, 0, 0, 0, 0)),
                  pl.BlockSpec((None, None, PAST_LEN, B_HEADS, HEAD_DIM), lambda b, r: (b, 0, 0, 0, 0)),
                  pl.BlockSpec((B_HEADS, NA_DR + 2, 2 * GRID_W), lambda b, r: (0, 0, 0))],
        out_specs=pl.BlockSpec((NA_Q * NA_BLOCKS, B_QKV), lambda b, r: (b * nblk + r, 0)),
        out_shape=jax.ShapeDtypeStruct((M_SAMPLE, B_QKV), BF16),
        scratch_shapes=[pltpu.VMEM((B_HEADS, NA_PAIRS, GRID_W, 2 * GRID_W), F32),
                        pltpu.VMEM((PAST_LEN, B_QKV), BF16), pltpu.VMEM((PAST_LEN, B_QKV), BF16)],
        compiler_params=_params(("arbitrary", "arbitrary"), VMEM_LIMIT),
        name="attn_neighbourhood",
    )(q, kv, kv, cache_k, cache_v, _na_bias_rows(rpb))


def _merge_kernel(oap, oas, obp, obs, wa_ref, wb_ref, ga_ref, gb_ref, z_ref, wa_bf, wb_bf):
    i = pl.program_id(1)
    p_tiles = M_PROMPT // TM

    @pl.when(i == 0)
    def _():
        wa_bf[...] = wa_ref[...].astype(BF16)
        wb_bf[...] = wb_ref[...].astype(BF16)

    def run(oa_ref, ob_ref):
        for rows in _row_chunks(TM):
            ya = _dot(oa_ref[rows, :], wa_bf[...])
            yb = _dot(ob_ref[rows, :], wb_bf[...])
            z_ref[rows, :] = (ga_ref[rows, :] * ya + gb_ref[rows, :] * yb).astype(BF16)

    pl.when(i < p_tiles)(functools.partial(run, oap, obp))
    pl.when(i >= p_tiles)(functools.partial(run, oas, obs))


def _merge_call(oa_p, oa_s, ob_p, ob_s, w_br_a, w_br_b, gates):
    p_tiles = M_PROMPT // TM
    nj = D_MODEL // TN
    pspec = pl.BlockSpec((TM, A_Q), lambda j, i: (jnp.minimum(i, p_tiles - 1), 0))
    sspec = pl.BlockSpec((TM, A_Q), lambda j, i: (jnp.maximum(i - p_tiles, 0), 0))
    once = dict(pipeline_mode=pl.Buffered(1))
    return pl.pallas_call(
        _merge_kernel,
        grid=(nj, M_ALL // TM),
        in_specs=[pspec, sspec, pspec, sspec,
                  pl.BlockSpec((A_Q, TN), lambda j, i: (0, j), **once),
                  pl.BlockSpec((B_QKV, TN), lambda j, i: (0, j), **once),
                  pl.BlockSpec((None, TM, TN), lambda j, i: (j, i, 0)),
                  pl.BlockSpec((None, TM, TN), lambda j, i: (nj + j, i, 0))],
        out_specs=pl.BlockSpec((TM, TN), lambda j, i: (i, j)),
        out_shape=jax.ShapeDtypeStruct((M_ALL, D_MODEL), BF16),
        scratch_shapes=[pltpu.VMEM((A_Q, TN), BF16), pltpu.VMEM((B_QKV, TN), BF16)],
        compiler_params=_params(("arbitrary", "arbitrary"), VMEM_LIMIT),
        name="merge_branches",
    )(oa_p, oa_s, ob_p, ob_s, w_br_a, w_br_b, gates, gates)


def _out_kernel(z_ref, w_ref, xp_ref, xs_ref, g_ref, o_ref, wbf):
    i = pl.program_id(1)
    p_tiles = M_PROMPT // TM

    @pl.when(i == 0)
    def _():
        wbf[...] = w_ref[...].astype(BF16)

    def run(x_ref):
        for rows in _row_chunks(TM):
            o_ref[rows, :] = x_ref[rows, :] + g_ref[...] * _dot(z_ref[rows, :], wbf[...])

    pl.when(i < p_tiles)(functools.partial(run, xp_ref))
    pl.when(i >= p_tiles)(functools.partial(run, xs_ref))


def _out_call(z, w_out, xp, xs, mods_mid):
    p_tiles = M_PROMPT // TM
    nj = D_MODEL // TN
    return pl.pallas_call(
        _out_kernel,
        grid=(nj, M_ALL // TM),
        in_specs=[pl.BlockSpec((TM, D_MODEL), lambda j, i: (i, 0)),
                  pl.BlockSpec((D_MODEL, TN), lambda j, i: (0, j)),
                  pl.BlockSpec((TM, TN), lambda j, i: (jnp.minimum(i, p_tiles - 1), j)),
                  pl.BlockSpec((TM, TN), lambda j, i: (jnp.maximum(i - p_tiles, 0), j)),
                  pl.BlockSpec((None, 1, TN), lambda j, i: (_mod_row(i, TM), 0, j))],
        out_specs=pl.BlockSpec((TM, TN), lambda j, i: (i, j)),
        out_shape=jax.ShapeDtypeStruct((M_ALL, D_MODEL), F32),
        scratch_shapes=[pltpu.VMEM((D_MODEL, TN), BF16)],
        compiler_params=_params(("arbitrary", "arbitrary"), VMEM_LIMIT),
        name="out_proj_residual",
    )(z, w_out, xp, xs, mods_mid)


def _mlp_kernel(x_hbm, nw_ref, sh_ref, sc_ref, g_ref, wu_ref, wd_ref, o_ref, h_ref, x_buf, x_sem, *, tile0):
    i, f = pl.program_id(0), pl.program_id(1)

    def x_copy(tile):
        return pltpu.make_async_copy(x_hbm.at[pl.ds((tile0 + tile) * TM, TM), :], x_buf, x_sem)

    @pl.when((i == 0) & (f == 0))
    def _():
        x_copy(0).start()

    @pl.when(f == 0)
    def _():
        x_copy(i).wait()
        wu = wu_ref[...].astype(BF16)
        wd = wd_ref[...].astype(BF16)
        for r0 in range(0, TM, ROW_CHUNK):
            rows = slice(r0, r0 + ROW_CHUNK)
            x = x_buf[rows, :]
            h = _modnorm(x, nw_ref[...], sc_ref[...], sh_ref[...]).astype(BF16)
            h_ref[rows, :] = h
            u = jnp.square(jnp.maximum(_dot(h, wu), 0.0)).astype(BF16)
            o_ref[rows, :] = x + g_ref[...] * _dot(u, wd)

    @pl.when((f == 1) & (i + 1 < pl.num_programs(0)))
    def _():
        x_copy(i + 1).start()

    @pl.when(f > 0)
    def _():
        u = _dot(h_ref[...], wu_ref[...].astype(BF16))
        u = jnp.square(jnp.maximum(u, 0.0)).astype(BF16)
        for c0 in range(0, D_MODEL, MLP_COLS):
            cols = slice(c0, c0 + MLP_COLS)
            o_ref[:, cols] += g_ref[:, cols] * _dot(u, wd_ref[:, cols].astype(BF16))


def _mlp_call(x1, nw, mods_mid, mods_last, w_up, w_down, tile0, n_tiles):
    mod = lambda k: pl.BlockSpec((None, 1, D_MODEL), lambda i, f: (_mod_row(tile0 + i, TM), 0, k))
    return pl.pallas_call(
        functools.partial(_mlp_kernel, tile0=tile0),
        grid=(n_tiles, D_FF // TF),
        in_specs=[pl.BlockSpec(memory_space=pl.ANY),
                  pl.BlockSpec((1, D_MODEL), lambda i, f: (0, 0)),
                  mod(1), mod(0), mod(1),
                  pl.BlockSpec((D_MODEL, TF), lambda i, f: (0, f)),
                  pl.BlockSpec((TF, D_MODEL), lambda i, f: (f, 0))],
        out_specs=pl.BlockSpec((TM, D_MODEL), lambda i, f: (i, 0)),
        out_shape=jax.ShapeDtypeStruct((n_tiles * TM, D_MODEL), F32),
        scratch_shapes=[pltpu.VMEM((TM, D_MODEL), BF16),
                        pltpu.VMEM((TM, D_MODEL), F32),
                        pltpu.SemaphoreType.DMA(())],
        compiler_params=_params(("arbitrary", "arbitrary"), VMEM_LIMIT),
        name="mlp",
    )(x1, nw, mods_mid, mods_last, mods_last, w_up, w_down)


def kernel(x_prompt, x_sample, cache_a_k, cache_a_v, cache_b_k, cache_b_v, c, c_ctx, norm1_w, norm2_w, w_ada, b_ada, w_in, q_norm_a, k_norm_a, q_norm_b, k_norm_b, sink_a, rpb_b, w_br_a, w_br_b, w_out, w_up, w_down):
    assert w_ada.shape[0] == 1, "one trunk layer"
    xp = x_prompt.reshape(M_PROMPT, D_MODEL)
    xs = x_sample.reshape(M_SAMPLE, D_MODEL)

    mods_head, cond = _ada_call(c_ctx[None, :], c, w_ada[0], b_ada)
    ada = (cond, w_ada[0], b_ada)

    w = w_in[0]
    rope = _rope_tables()
    h, kva, ka_p, va_p = _norm_kva_call(xp, xs, norm1_w, mods_head, w, k_norm_a, rope)
    kv, kb_p, vb_p = _proj_call(h, w, q_norm_a, q_norm_b, k_norm_b, rope, PROJ_KV)
    q, gates = _proj_call(h, w, q_norm_a, q_norm_b, k_norm_b, rope, PROJ_QG)

    sink = sink_a[0]
    oa_p, ob_p, mods_mid = _ctx_attn_call(sink, q, kv, kva, ada)
    oa_s, mods_last = _win_attn_call(sink, q, kva, cache_a_k, cache_a_v, ada)
    ob_s = _na_attn_call(q, kv, cache_b_k, cache_b_v, rpb_b[0])

    z = _merge_call(oa_p, oa_s, ob_p, ob_s, w_br_a[0], w_br_b[0], gates)
    x1 = _out_call(z, w_out[0], xp, xs, mods_mid)

    p_tiles = M_PROMPT // TM
    y_p = _mlp_call(x1, norm2_w, mods_mid, mods_last, w_up[0], w_down[0], 0, p_tiles)
    y_s = _mlp_call(x1, norm2_w, mods_mid, mods_last, w_up[0], w_down[0], p_tiles, M_SAMPLE // TM)

    return (y_p.reshape(BATCH, SEQ, D_MODEL),
            y_s.reshape(DEC_BATCH, DEC_SEQ, D_MODEL),
            ka_p.reshape(BATCH, 1, SEQ, A_KV_HEADS, HEAD_DIM),
            va_p.reshape(BATCH, 1, SEQ, A_KV_HEADS, HEAD_DIM),
            kb_p.reshape(BATCH, 1, SEQ, B_HEADS, HEAD_DIM),
            vb_p.reshape(BATCH, 1, SEQ, B_HEADS, HEAD_DIM))
```

```python
import functools

import numpy as np
import jax
import jax.numpy as jnp
from jax import lax
from jax.experimental import pallas as pl
from jax.experimental.pallas import tpu as pltpu

D_MODEL = 2048
BATCH = 16
SEQ = 256
DEC_BATCH = 2
DEC_SEQ = 1024
PAST_LEN = 256
GRID_W = 64
HEAD_DIM = 128
A_HEADS = 8
A_KV_HEADS = 2
A_GROUP = A_HEADS // A_KV_HEADS
A_WINDOW = 128
BLOCK = 128
B_HEADS = 8
NA_ROWS_MAX = 8
NA_COLS = 16
D_FF = 4 * D_MODEL
ROPE_THETA = 10000.0
EPS = 1e-6
NEG = -1e30
A_Q = A_HEADS * HEAD_DIM
A_KV = A_KV_HEADS * HEAD_DIM
B_QKV = B_HEADS * HEAD_DIM
IN_WIDTH = A_Q + 2 * A_KV + 3 * B_QKV + 2 * D_MODEL
SCALE = HEAD_DIM ** -0.5

M_PROMPT = BATCH * SEQ
M_SAMPLE = DEC_BATCH * DEC_SEQ
M_ALL = M_PROMPT + M_SAMPLE
GRID_ROWS = DEC_SEQ // GRID_W
NA_ROWS = min(NA_ROWS_MAX, GRID_ROWS)

COL_QA = 0
COL_KVA = A_Q
COL_QB = A_Q + 2 * A_KV
COL_KB = COL_QB + B_QKV
COL_VB = COL_KB + B_QKV
COL_G = COL_VB + B_QKV

TM = 1024
TN = 1024
ROW_CHUNK = 256
W_SUB = 512
TM_NORM = 512
TF = 512
MLP_COLS = 512
ADA_HEAD = 2 * D_MODEL
ADA_SIDE_COLS = 1024
ADA_SIDE = 2 * D_MODEL
VMEM_LIMIT = 60 * 1024 * 1024

F32 = jnp.float32
BF16 = jnp.bfloat16


def _mod_row(i, tm):
    p_tiles = M_PROMPT // tm
    return jnp.where(i < p_tiles, 0, 1 + (i - p_tiles) // (DEC_SEQ // tm))


def _row_chunks(tm):
    edges = list(range(0, tm, ROW_CHUNK)) + [tm - ROW_CHUNK // 2, tm]
    edges = sorted(set(edges))
    return [slice(a, b) for a, b in zip(edges[:-1], edges[1:])]


def _dot(a, b):
    return jnp.dot(a, b, preferred_element_type=F32)


def _dot_nt(a, b):
    return lax.dot_general(a, b, (((1,), (1,)), ((), ())), preferred_element_type=F32)


def _params(sem, vmem=None):
    return pltpu.CompilerParams(dimension_semantics=sem, vmem_limit_bytes=vmem)


def _ada_side(s_ref, w_ref, b_ref, o_ref):
    o_ref[:, 0, :] = _dot(s_ref[...], w_ref[...].astype(BF16)) + b_ref[...]


def _ada_kernel(cctx_ref, c_ref, w_ref, b_ref, o_ref, s_ref):
    row = lax.broadcasted_iota(jnp.int32, (8, D_MODEL), 0)
    cv = jnp.where(row == 0, cctx_ref[...], 0.0)
    for b in range(DEC_BATCH):
        cv = jnp.where(row == 1 + b, c_ref[b:b + 1, :], cv)
    s_ref[...] = (cv * jax.nn.sigmoid(cv)).astype(BF16)
    _ada_side(s_ref, w_ref, b_ref, o_ref)


def _ada_call(c_ctx, c, w_ada, b_ada):
    tn = 1024
    n = ADA_HEAD
    return pl.pallas_call(
        _ada_kernel,
        grid=(n // tn,),
        in_specs=[pl.BlockSpec((1, D_MODEL), lambda j: (0, 0)),
                  pl.BlockSpec((DEC_BATCH, D_MODEL), lambda j: (0, 0)),
                  pl.BlockSpec((D_MODEL, tn), lambda j: (0, j)),
                  pl.BlockSpec((1, tn), lambda j: (0, j))],
        out_specs=[pl.BlockSpec((8, 1, tn), lambda j: (0, 0, j)),
                   pl.BlockSpec((8, D_MODEL), lambda j: (0, 0))],
        out_shape=[jax.ShapeDtypeStruct((8, 1, n), F32),
                   jax.ShapeDtypeStruct((8, D_MODEL), BF16)],
        compiler_params=_params(("arbitrary",), VMEM_LIMIT),
        name="ada_mod",
    )(c_ctx, c, w_ada, b_ada)


def _ada_side_specs(first_col, step):
    blk = lambda *g: first_col // ADA_SIDE_COLS + step(*g)
    in_specs = [pl.BlockSpec((8, D_MODEL), lambda *g: (0, 0)),
                pl.BlockSpec((D_MODEL, ADA_SIDE_COLS), lambda *g: (0, blk(*g))),
                pl.BlockSpec((1, ADA_SIDE_COLS), lambda *g: (0, blk(*g)))]
    out_spec = pl.BlockSpec((8, 1, ADA_SIDE_COLS), lambda *g: (0, 0, step(*g)))
    return in_specs, out_spec, jax.ShapeDtypeStruct((8, 1, ADA_SIDE), F32)


def _modnorm(x, nw, sc, sh):
    y = x * lax.rsqrt(jnp.mean(x * x, axis=-1, keepdims=True) + EPS)
    return y * (nw * (1.0 + sc)) + sh


def _head_norm(x, nw):
    return x * lax.rsqrt(jnp.mean(x * x, axis=-1, keepdims=True) + EPS) * nw


def _rope(x, cos, sin_signed):
    lane = lax.broadcasted_iota(jnp.int32, x.shape, 1)
    partner = jnp.where((lane % 64) < 32, pltpu.roll(x, 96, 1), pltpu.roll(x, 32, 1))
    return x * cos + partner * sin_signed


def _norm_kva_kernel(xp_ref, xs_ref, nw1_ref, sh_ref, sc_ref, w_ref, nwk_ref, cos_ref, sin_ref,
                     h_ref, kv_ref, kp_ref, vp_ref, wbf):
    i = pl.program_id(0)
    p_tiles = M_PROMPT // TM_NORM

    @pl.when(i == 0)
    def _():
        wbf[...] = w_ref[...].astype(BF16)

    def head_cols(k, base=0):
        return slice(base + k * HEAD_DIM, base + (k + 1) * HEAD_DIM)

    def run(prompt):
        x_ref = xp_ref if prompt else xs_ref
        for r0 in range(0, TM_NORM, ROW_CHUNK):
            rows = slice(r0, r0 + ROW_CHUNK)
            h = _modnorm(x_ref[rows, :], nw1_ref[...], sc_ref[...], sh_ref[...]).astype(BF16)
            h_ref[rows, :] = h
            acc = _dot(h, wbf[...])
            for k in range(A_KV_HEADS):
                y = _head_norm(acc[:, head_cols(k)], nwk_ref[...])
                v = acc[:, head_cols(k, A_KV)]
                if prompt:
                    kp_ref[rows, k, :] = y
                    vp_ref[rows, k, :] = v
                else:
                    y = _rope(y, cos_ref[rows, :], sin_ref[rows, :])
                kv_ref[rows, head_cols(k)] = y.astype(BF16)
                kv_ref[rows, head_cols(k, A_KV)] = v.astype(BF16)

    pl.when(i < p_tiles)(functools.partial(run, True))
    pl.when(i >= p_tiles)(functools.partial(run, False))


def _norm_kva_call(xp, xs, nw1, mods3, w_in, nwk, rope):
    tm = TM_NORM
    p_tiles = M_PROMPT // tm
    s_tiles = DEC_SEQ // tm
    assert COL_KVA % (2 * A_KV) == 0
    rope_spec = pl.BlockSpec((tm, HEAD_DIM), lambda i: (jnp.maximum(i - p_tiles, 0) % s_tiles, 0))
    parked = pl.BlockSpec((tm, A_KV_HEADS, HEAD_DIM), lambda i: (jnp.minimum(i, p_tiles - 1), 0, 0))
    return pl.pallas_call(
        _norm_kva_kernel,
        grid=(M_ALL // tm,),
        in_specs=[pl.BlockSpec((tm, D_MODEL), lambda i: (jnp.minimum(i, p_tiles - 1), 0)),
                  pl.BlockSpec((tm, D_MODEL), lambda i: (jnp.maximum(i - p_tiles, 0), 0)),
                  pl.BlockSpec((1, D_MODEL), lambda i: (0, 0)),
                  pl.BlockSpec((None, 1, D_MODEL), lambda i: (_mod_row(i, tm), 0, 0)),
                  pl.BlockSpec((None, 1, D_MODEL), lambda i: (_mod_row(i, tm), 0, 1)),
                  pl.BlockSpec((D_MODEL, 2 * A_KV), lambda i: (0, COL_KVA // (2 * A_KV)),
                               pipeline_mode=pl.Buffered(1)),
                  pl.BlockSpec((1, HEAD_DIM), lambda i: (0, 0)),
                  rope_spec, rope_spec],
        out_specs=[pl.BlockSpec((tm, D_MODEL), lambda i: (i, 0)),
                   pl.BlockSpec((tm, 2 * A_KV), lambda i: (i, 0)),
                   parked, parked],
        out_shape=[jax.ShapeDtypeStruct((M_ALL, D_MODEL), BF16),
                   jax.ShapeDtypeStruct((M_ALL, 2 * A_KV), BF16),
                   jax.ShapeDtypeStruct((M_PROMPT, A_KV_HEADS, HEAD_DIM), F32),
                   jax.ShapeDtypeStruct((M_PROMPT, A_KV_HEADS, HEAD_DIM), F32)],
        scratch_shapes=[pltpu.VMEM((D_MODEL, 2 * A_KV), BF16)],
        compiler_params=_params(("arbitrary",), VMEM_LIMIT),
        name="norm1_proj_kva",
    )(xp, xs, nw1, mods3, mods3, w_in, nwk, *rope)


def _rope_tables():
    n_freq = HEAD_DIM // 4
    pos = np.arange(DEC_SEQ)
    row = (pos // GRID_W).astype(np.float64)
    col = (pos % GRID_W).astype(np.float64)
    inv = ROPE_THETA ** (-np.arange(n_freq, dtype=np.float64) / n_freq)
    ar = row[:, None] * inv
    ac = col[:, None] * inv
    cos = np.concatenate([np.cos(ar), np.cos(ar), np.cos(ac), np.cos(ac)], axis=-1)
    sin = np.concatenate([-np.sin(ar), np.sin(ar), -np.sin(ac), np.sin(ac)], axis=-1)
    return jnp.asarray(cos, F32), jnp.asarray(sin, F32)


PROJ_COL = {"qa": COL_QA, "qb": COL_QB, "kb": COL_KB, "vb": COL_VB}
GATE_PANELS = 2 * D_MODEL // TN
PROJ_KV = ("kb", "vb")
PROJ_QG = ("qa", "qb") + ("gate",) * GATE_PANELS


def _proj_kernel(*refs, kinds):
    names = ["h", "w0", "w1", "nqa", "nqb", "nkb", "cos", "sin", "qkv"]
    names += ["gate"] * ("gate" in kinds) + ["kbp"] * ("kb" in kinds) + ["vbp"] * ("vb" in kinds) + ["wbf"]
    r = dict(zip(names, refs, strict=True))
    n_qkv = sum(k != "gate" for k in kinds)
    j, i = pl.program_id(0), pl.program_id(1)
    is_prompt = i < M_PROMPT // TM

    @pl.when(i == 0)
    def _():
        r["wbf"][:, :W_SUB] = r["w0"][...].astype(BF16)
        r["wbf"][:, W_SUB:] = r["w1"][...].astype(BF16)

    def head_cols(k):
        return slice(k * HEAD_DIM, (k + 1) * HEAD_DIM)

    def run(kind, prompt):
        for rows in _row_chunks(TM):
            acc = _dot(r["h"][rows, :], r["wbf"][...])
            if kind == "gate":
                r["gate"][rows, :] = jax.nn.sigmoid(acc)
            elif kind == "vb":
                if prompt:
                    r["vbp"][rows, :] = acc
                r["qkv"][rows, :] = acc.astype(BF16)
            else:
                nw_ref = r[{"qa": "nqa", "qb": "nqb", "kb": "nkb"}[kind]]
                for k in range(TN // HEAD_DIM):
                    y = _head_norm(acc[:, head_cols(k)], nw_ref[...])
                    if kind == "qa" and not prompt:
                        y = _rope(y, r["cos"][rows, :], r["sin"][rows, :])
                    if kind == "kb" and prompt:
                        r["kbp"][rows, head_cols(k)] = y
                    r["qkv"][rows, head_cols(k)] = y.astype(BF16)

    for p, kind in enumerate(kinds[:n_qkv]):
        if kind == "qb":
            pl.when(j == p)(functools.partial(run, kind, None))
        else:
            pl.when((j == p) & is_prompt)(functools.partial(run, kind, True))
            pl.when((j == p) & jnp.logical_not(is_prompt))(functools.partial(run, kind, False))
    if "gate" in kinds:
        pl.when(j >= n_qkv)(functools.partial(run, "gate", None))


def _proj_call(h, w_in, nqa, nqb, nkb, rope, kinds):
    assert TN == 2 * W_SUB and A_Q == B_QKV == TN
    n_panels = len(kinds)
    n_qkv = sum(k != "gate" for k in kinds)
    assert all(k == "gate" for k in kinds[n_qkv:])
    cols = [PROJ_COL[k] for k in kinds[:n_qkv]] + list(range(COL_G, IN_WIDTH, TN))[:n_panels - n_qkv]
    p_tiles = M_PROMPT // TM
    s_tiles = DEC_SEQ // TM
    n_tiles = M_ALL // TM

    def w_sub(j):
        idx = cols[0] // W_SUB
        for p in range(1, n_panels):
            idx = jnp.where(j == p, cols[p] // W_SUB, idx)
        return idx

    def qkv_idx(j, i):
        return (jnp.minimum(j, n_qkv - 1), jnp.where(j >= n_qkv, n_tiles - 1, i), 0)

    def gate_idx(j, i):
        return (jnp.maximum(j - n_qkv, 0), jnp.where(j < n_qkv, 0, i), 0)

    def cache_idx(panel):
        def idx(j, i):
            return (jnp.where(j < panel, 0, jnp.where(j > panel, p_tiles - 1, jnp.minimum(i, p_tiles - 1))), 0)
        return idx

    out_specs = [pl.BlockSpec((None, TM, TN), qkv_idx)]
    out_shape = [jax.ShapeDtypeStruct((n_qkv, M_ALL, TN), BF16)]
    if "gate" in kinds:
        out_specs.append(pl.BlockSpec((None, TM, TN), gate_idx))
        out_shape.append(jax.ShapeDtypeStruct((n_panels - n_qkv, M_ALL, TN), F32))
    for kind in ("kb", "vb"):
        if kind in kinds:
            out_specs.append(pl.BlockSpec((TM, TN), cache_idx(kinds.index(kind))))
            out_shape.append(jax.ShapeDtypeStruct((M_PROMPT, B_QKV), F32))

    norm_spec = pl.BlockSpec((1, HEAD_DIM), lambda j, i: (0, 0))
    rope_spec = pl.BlockSpec((TM, HEAD_DIM), lambda j, i: (jnp.maximum(i - p_tiles, 0) % s_tiles, 0))
    in_specs = [pl.BlockSpec((TM, D_MODEL), lambda j, i: (i, 0)),
                pl.BlockSpec((D_MODEL, W_SUB), lambda j, i: (0, w_sub(j))),
                pl.BlockSpec((D_MODEL, W_SUB), lambda j, i: (0, w_sub(j) + 1)),
                norm_spec, norm_spec, norm_spec, rope_spec, rope_spec]
    args = [h, w_in, w_in, nqa, nqb, nkb, *rope]
    return pl.pallas_call(
        functools.partial(_proj_kernel, kinds=kinds),
        grid=(n_panels, n_tiles),
        in_specs=in_specs,
        out_specs=out_specs,
        out_shape=out_shape,
        scratch_shapes=[pltpu.VMEM((D_MODEL, TN), BF16)],
        compiler_params=_params(("arbitrary", "arbitrary"), VMEM_LIMIT),
        name="proj_" + "_".join(dict.fromkeys(kinds)),
    )(*args)


LOG2E = 1.4426950408889634
QK_LOG2 = SCALE * LOG2E


def _softmax_pv(parts, sink):
    m = functools.reduce(jnp.maximum, [jnp.max(t, axis=-1, keepdims=True) for t, _ in parts])
    if sink is not None:
        m = jnp.maximum(m, sink)
    acc = None
    for t, v in parts:
        p = jnp.exp2(t - m).astype(BF16)
        y = _dot(p, jnp.concatenate([v, jnp.ones_like(v)], axis=1))
        acc = y if acc is None else acc + y
    o, l = acc[:, :HEAD_DIM], acc[:, HEAD_DIM:]
    if sink is not None:
        l = l + jnp.exp2(sink - m)
    return o / l


def _sink_column(sink_ref, hk, rows_per_head):
    n = A_GROUP * rows_per_head
    g = lax.broadcasted_iota(jnp.int32, (n, 1), 0) // rows_per_head
    col = jnp.full((n, 1), sink_ref[hk * A_GROUP], F32)
    for k in range(1, A_GROUP):
        col = jnp.where(g == k, sink_ref[hk * A_GROUP + k], col)
    return col * LOG2E


def _stack_group(q_ref, hk):
    return jnp.concatenate(
        [q_ref[:, (hk * A_GROUP + g) * HEAD_DIM:(hk * A_GROUP + g + 1) * HEAD_DIM] for g in range(A_GROUP)], axis=0)


def _unstack_group(o_ref, hk, o, rows):
    for g in range(A_GROUP):
        c = (hk * A_GROUP + g) * HEAD_DIM
        o_ref[:, c:c + HEAD_DIM] = o[g * rows:(g + 1) * rows].astype(BF16)


CTX_REQS = 4


def _ctx_attn_kernel(sink_ref, q_ref, kv_ref, kva_ref, oa_ref, ob_ref):
    for req in range(CTX_REQS):
        rows = pl.ds(req * SEQ, SEQ)
        qa_ref, qb_ref = q_ref.at[PROJ_QG.index("qa"), rows], q_ref.at[PROJ_QG.index("qb"), rows]
        kb_ref, vb_ref = kv_ref.at[PROJ_KV.index("kb"), rows], kv_ref.at[PROJ_KV.index("vb"), rows]
        ka_ref, oa, ob = kva_ref.at[rows], oa_ref.at[rows], ob_ref.at[rows]
        for hk in range(A_KV_HEADS):
            k = ka_ref[:, hk * HEAD_DIM:(hk + 1) * HEAD_DIM]
            v = ka_ref[:, A_KV + hk * HEAD_DIM:A_KV + (hk + 1) * HEAD_DIM]
            q4 = _stack_group(qa_ref, hk)
            t = _dot_nt(q4, k) * QK_LOG2
            o = _softmax_pv([(t, v)], _sink_column(sink_ref, hk, SEQ))
            _unstack_group(oa, hk, o, SEQ)
        for h in range(B_HEADS):
            sl = slice(h * HEAD_DIM, (h + 1) * HEAD_DIM)
            t = _dot_nt(qb_ref[:, sl], kb_ref[:, sl]) * QK_LOG2
            ob[:, sl] = _softmax_pv([(t, vb_ref[:, sl])], None).astype(BF16)


def _ctx_attn_call(sink, q, kv, kva):
    rows = CTX_REQS * SEQ
    row = lambda w: pl.BlockSpec((rows, w), lambda b: (b, 0))
    return pl.pallas_call(
        _ctx_attn_kernel,
        grid=(BATCH // CTX_REQS,),
        in_specs=[pl.BlockSpec(memory_space=pltpu.SMEM),
                  pl.BlockSpec((2, rows, TN), lambda b: (0, b, 0)),
                  pl.BlockSpec((2, rows, TN), lambda b: (0, b, 0)), row(2 * A_KV)],
        out_specs=[row(A_Q), row(B_QKV)],
        out_shape=[jax.ShapeDtypeStruct((M_PROMPT, A_Q), BF16),
                   jax.ShapeDtypeStruct((M_PROMPT, B_QKV), BF16)],
        compiler_params=_params(("arbitrary",), VMEM_LIMIT),
        name="attn_ctx",
    )(sink, q, kv, kva)


BAND = 3 * BLOCK


def _cache_to_bf16(n_heads, ck_ref, cv_ref, ckb, cvb):
    for h in range(n_heads):
        sl = slice(h * HEAD_DIM, (h + 1) * HEAD_DIM)
        ckb[:, sl] = ck_ref[:, h, :].astype(BF16)
        cvb[:, sl] = cv_ref[:, h, :].astype(BF16)


WIN_BLOCKS = 4


def _win_attn_kernel(sink_ref, q_ref, kv_ref, ck_ref, cv_ref, s_ref, wada_ref, bada_ref, o_ref, mods_ref, ckb, cvb):
    step = pl.program_id(1)
    pl.when(step == 0)(functools.partial(_cache_to_bf16, A_KV_HEADS, ck_ref, cv_ref, ckb, cvb))
    for blk in range(WIN_BLOCKS):
        n = step * WIN_BLOCKS + blk
        rows = pl.ds(blk * BLOCK, BLOCK)
        start = pl.multiple_of(jnp.clip((n - 1) * BLOCK, 0, DEC_SEQ - BAND), BLOCK)
        qpos = n * BLOCK + lax.broadcasted_iota(jnp.int32, (BLOCK, BAND), 0)
        kpos = start + lax.broadcasted_iota(jnp.int32, (BLOCK, BAND), 1)
        valid = jnp.abs(qpos - kpos) <= A_WINDOW
        valid = jnp.concatenate([valid.astype(jnp.int32)] * A_GROUP, axis=0) > 0
        for hk in range(A_KV_HEADS):
            sl = slice(hk * HEAD_DIM, (hk + 1) * HEAD_DIM)
            slv = slice(A_KV + hk * HEAD_DIM, A_KV + (hk + 1) * HEAD_DIM)
            k_loc = kv_ref[pl.ds(start, BAND), sl]
            v_loc = kv_ref[pl.ds(start, BAND), slv]
            q4 = _stack_group(q_ref.at[rows], hk)
            t_loc = jnp.where(valid, _dot_nt(q4, k_loc) * QK_LOG2, NEG)
            t_ctx = _dot_nt(q4, ckb[:, sl]) * QK_LOG2
            o = _softmax_pv([(t_loc, v_loc), (t_ctx, cvb[:, sl])], _sink_column(sink_ref, hk, BLOCK))
            _unstack_group(o_ref.at[rows], hk, o, BLOCK)
    _ada_side(s_ref, wada_ref, bada_ref, mods_ref)


def _win_attn_call(sink, q, kva, cache_k, cache_v, ada):
    rows = WIN_BLOCKS * BLOCK
    nb = DEC_SEQ // rows
    q0 = M_PROMPT // rows
    b0 = M_PROMPT // DEC_SEQ
    assert DEC_BATCH * nb * ADA_SIDE_COLS == ADA_SIDE and ADA_HEAD + 2 * ADA_SIDE == 6 * D_MODEL
    side_in, side_out, side_shape = _ada_side_specs(ADA_HEAD + ADA_SIDE, lambda b, n: b * nb + n)
    return pl.pallas_call(
        _win_attn_kernel,
        grid=(DEC_BATCH, nb),
        in_specs=[pl.BlockSpec(memory_space=pltpu.SMEM),
                  pl.BlockSpec((None, rows, A_Q), lambda b, n: (PROJ_QG.index("qa"), q0 + b * nb + n, 0)),
                  pl.BlockSpec((DEC_SEQ, 2 * A_KV), lambda b, n: (b0 + b, 0)),
                  pl.BlockSpec((None, None, PAST_LEN, A_KV_HEADS, HEAD_DIM), lambda b, n: (b, 0, 0, 0, 0)),
                  pl.BlockSpec((None, None, PAST_LEN, A_KV_HEADS, HEAD_DIM), lambda b, n: (b, 0, 0, 0, 0))] + side_in,
        out_specs=[pl.BlockSpec((rows, A_Q), lambda b, n: (b * nb + n, 0)), side_out],
        out_shape=[jax.ShapeDtypeStruct((M_SAMPLE, A_Q), BF16), side_shape],
        scratch_shapes=[pltpu.VMEM((PAST_LEN, A_KV), BF16), pltpu.VMEM((PAST_LEN, A_KV), BF16)],
        compiler_params=_params(("arbitrary", "arbitrary"), VMEM_LIMIT),
        name="attn_window",
    )(sink, q, kva, cache_k, cache_v, *ada)


NA_QROWS = 4
NA_BLOCKS = 2
NA_WIN_ROWS = 12
NA_Q = NA_QROWS * GRID_W
NA_KEYS = NA_WIN_ROWS * GRID_W
NA_DR = 2 * NA_ROWS_MAX - 1
NA_PAIRS = NA_DR + 1
RPB_W = 2 * NA_COLS - 1
RPB_PAD = GRID_W - NA_COLS


def _na_row_start(r):
    return jnp.clip(r - NA_ROWS // 2, 0, GRID_ROWS - NA_ROWS)


def _na_window_row0(blk):
    return jnp.clip(blk * NA_QROWS - NA_ROWS // 2, 0, GRID_ROWS - NA_WIN_ROWS)


def _check_na_windows():
    for blk in range(GRID_ROWS // NA_QROWS):
        w0 = int(np.clip(blk * NA_QROWS - NA_ROWS // 2, 0, GRID_ROWS - NA_WIN_ROWS))
        for r in range(blk * NA_QROWS, (blk + 1) * NA_QROWS):
            r0 = int(np.clip(r - NA_ROWS // 2, 0, GRID_ROWS - NA_ROWS))
            assert w0 <= r0 and r0 + NA_ROWS <= w0 + NA_WIN_ROWS, (blk, r)


_check_na_windows()


def _na_attn_kernel(q_ref, k_ref, v_ref, ck_ref, cv_ref, rpb_ref, s_ref, wada_ref, bada_ref,
                    o_ref, mods_ref, tab_ref, ckb, cvb):
    b, step = pl.program_id(0), pl.program_id(1)
    lane = lax.broadcasted_iota(jnp.int32, (GRID_W, 2 * GRID_W), 1)
    pl.when(step == 0)(functools.partial(_cache_to_bf16, B_HEADS, ck_ref, cv_ref, ckb, cvb))

    @pl.when((b == 0) & (step == 0))
    def _():
        for h in range(B_HEADS):
            for d in range(NA_PAIRS):
                lo = jnp.broadcast_to(rpb_ref[h, d:d + 1, :], (GRID_W, 2 * GRID_W))
                hi = jnp.broadcast_to(rpb_ref[h, d + 1:d + 2, :], (GRID_W, 2 * GRID_W))
                lo = pltpu.roll(lo, GRID_W + 1, 1, stride=1, stride_axis=0)
                hi = pltpu.roll(hi, 1, 1, stride=1, stride_axis=0)
                tab_ref[h, d] = jnp.where(lane < GRID_W, lo, hi) * LOG2E

    for sub in range(NA_BLOCKS):
        blk = step * NA_BLOCKS + sub
        rows = pl.ds(sub * NA_Q, NA_Q)
        row0 = _na_window_row0(blk)
        k0 = pl.multiple_of(row0 * GRID_W, GRID_W)
        qi = lax.broadcasted_iota(jnp.int32, (NA_Q, NA_KEYS), 0)
        ki = lax.broadcasted_iota(jnp.int32, (NA_Q, NA_KEYS), 1)
        qrow, qcol = blk * NA_QROWS + qi // GRID_W, qi % GRID_W
        krow, kcol = row0 + ki // GRID_W, ki % GRID_W
        rstart = _na_row_start(qrow)
        cstart = jnp.clip(qcol - NA_COLS // 2, 0, GRID_W - NA_COLS)
        valid = (krow >= rstart) & (krow < rstart + NA_ROWS) & (kcol >= cstart) & (kcol < cstart + NA_COLS)

        for h in range(B_HEADS):
            sl = slice(h * HEAD_DIM, (h + 1) * HEAD_DIM)
            bias = jnp.concatenate(
                [jnp.concatenate(
                    [tab_ref[h, jnp.clip(row0 + 2 * p - blk * NA_QROWS - q + NA_ROWS_MAX, 0, NA_PAIRS - 1)]
                     for p in range(NA_WIN_ROWS // 2)], axis=1)
                 for q in range(NA_QROWS)], axis=0)
            k_loc = k_ref[pl.ds(k0, NA_KEYS), sl]
            v_loc = v_ref[pl.ds(k0, NA_KEYS), sl]
            t_loc = jnp.where(valid, _dot_nt(q_ref[rows, sl], k_loc) * QK_LOG2 + bias, NEG)
            t_ctx = _dot_nt(q_ref[rows, sl], ckb[:, sl]) * QK_LOG2
            o = _softmax_pv([(t_loc, v_loc), (t_ctx, cvb[:, sl])], None)
            o_ref[rows, sl] = o.astype(BF16)
    _ada_side(s_ref, wada_ref, bada_ref, mods_ref)


def _na_bias_rows(rpb):
    rows = jnp.pad(rpb, ((0, 0), (1, 1), (0, 0)))
    left = jnp.broadcast_to(rows[..., :1], rows.shape[:2] + (RPB_PAD,))
    right = jnp.broadcast_to(rows[..., -1:], rows.shape[:2] + (2 * GRID_W - RPB_PAD - RPB_W,))
    return jnp.concatenate([left, rows, right], axis=-1)


def _na_attn_call(q, kv, cache_k, cache_v, rpb, ada):
    nblk = GRID_ROWS // (NA_QROWS * NA_BLOCKS)
    assert DEC_BATCH * nblk * ADA_SIDE_COLS == ADA_SIDE
    side_in, side_out, side_shape = _ada_side_specs(ADA_HEAD, lambda b, r: b * nblk + r)
    q0 = M_PROMPT // (NA_Q * NA_BLOCKS)
    b0 = M_PROMPT // DEC_SEQ
    return pl.pallas_call(
        _na_attn_kernel,
        grid=(DEC_BATCH, nblk),
        in_specs=[pl.BlockSpec((None, NA_Q * NA_BLOCKS, B_QKV), lambda b, r: (PROJ_QG.index("qb"), q0 + b * nblk + r, 0)),
                  pl.BlockSpec((None, DEC_SEQ, B_QKV), lambda b, r: (PROJ_KV.index("kb"), b0 + b, 0)),
                  pl.BlockSpec((None, DEC_SEQ, B_QKV), lambda b, r: (PROJ_KV.index("vb"), b0 + b, 0)),
                  pl.BlockSpec((None, None, PAST_LEN, B_HEADS, HEAD_DIM), lambda b, r: (b, 0, 0, 0, 0)),
                  pl.BlockSpec((None, None, PAST_LEN, B_HEADS, HEAD_DIM), lambda b, r: (b, 0, 0, 0, 0)),
                  pl.BlockSpec((B_HEADS, NA_DR + 2, 2 * GRID_W), lambda b, r: (0, 0, 0))] + side_in,
        out_specs=[pl.BlockSpec((NA_Q * NA_BLOCKS, B_QKV), lambda b, r: (b * nblk + r, 0)), side_out],
        out_shape=[jax.ShapeDtypeStruct((M_SAMPLE, B_QKV), BF16), side_shape],
        scratch_shapes=[pltpu.VMEM((B_HEADS, NA_PAIRS, GRID_W, 2 * GRID_W), F32),
                        pltpu.VMEM((PAST_LEN, B_QKV), BF16), pltpu.VMEM((PAST_LEN, B_QKV), BF16)],
        compiler_params=_params(("arbitrary", "arbitrary"), VMEM_LIMIT),
        name="attn_neighbourhood",
    )(q, kv, kv, cache_k, cache_v, _na_bias_rows(rpb), *ada)


def _merge_kernel(oap, oas, obp, obs, wa_ref, wb_ref, ga_ref, gb_ref, z_ref, wa_bf, wb_bf):
    i = pl.program_id(1)
    p_tiles = M_PROMPT // TM

    @pl.when(i == 0)
    def _():
        wa_bf[...] = wa_ref[...].astype(BF16)
        wb_bf[...] = wb_ref[...].astype(BF16)

    def run(oa_ref, ob_ref):
        for rows in _row_chunks(TM):
            ya = _dot(oa_ref[rows, :], wa_bf[...])
            yb = _dot(ob_ref[rows, :], wb_bf[...])
            z_ref[rows, :] = (ga_ref[rows, :] * ya + gb_ref[rows, :] * yb).astype(BF16)

    pl.when(i < p_tiles)(functools.partial(run, oap, obp))
    pl.when(i >= p_tiles)(functools.partial(run, oas, obs))


def _merge_call(oa_p, oa_s, ob_p, ob_s, w_br_a, w_br_b, gates):
    p_tiles = M_PROMPT // TM
    nj = D_MODEL // TN
    pspec = pl.BlockSpec((TM, A_Q), lambda j, i: (jnp.minimum(i, p_tiles - 1), 0))
    sspec = pl.BlockSpec((TM, A_Q), lambda j, i: (jnp.maximum(i - p_tiles, 0), 0))
    once = dict(pipeline_mode=pl.Buffered(1))
    return pl.pallas_call(
        _merge_kernel,
        grid=(nj, M_ALL // TM),
        in_specs=[pspec, sspec, pspec, sspec,
                  pl.BlockSpec((A_Q, TN), lambda j, i: (0, j), **once),
                  pl.BlockSpec((B_QKV, TN), lambda j, i: (0, j), **once),
                  pl.BlockSpec((None, TM, TN), lambda j, i: (j, i, 0)),
                  pl.BlockSpec((None, TM, TN), lambda j, i: (nj + j, i, 0))],
        out_specs=pl.BlockSpec((TM, TN), lambda j, i: (i, j)),
        out_shape=jax.ShapeDtypeStruct((M_ALL, D_MODEL), BF16),
        scratch_shapes=[pltpu.VMEM((A_Q, TN), BF16), pltpu.VMEM((B_QKV, TN), BF16)],
        compiler_params=_params(("arbitrary", "arbitrary"), VMEM_LIMIT),
        name="merge_branches",
    )(oa_p, oa_s, ob_p, ob_s, w_br_a, w_br_b, gates, gates)


def _out_kernel(z_ref, w_ref, xp_ref, xs_ref, g_ref, o_ref, wbf):
    i = pl.program_id(1)
    p_tiles = M_PROMPT // TM

    @pl.when(i == 0)
    def _():
        wbf[...] = w_ref[...].astype(BF16)

    def run(x_ref):
        for rows in _row_chunks(TM):
            o_ref[rows, :] = x_ref[rows, :] + g_ref[...] * _dot(z_ref[rows, :], wbf[...])

    pl.when(i < p_tiles)(functools.partial(run, xp_ref))
    pl.when(i >= p_tiles)(functools.partial(run, xs_ref))


def _out_call(z, w_out, xp, xs, mods_mid):
    p_tiles = M_PROMPT // TM
    nj = D_MODEL // TN
    return pl.pallas_call(
        _out_kernel,
        grid=(nj, M_ALL // TM),
        in_specs=[pl.BlockSpec((TM, D_MODEL), lambda j, i: (i, 0)),
                  pl.BlockSpec((D_MODEL, TN), lambda j, i: (0, j)),
                  pl.BlockSpec((TM, TN), lambda j, i: (jnp.minimum(i, p_tiles - 1), j)),
                  pl.BlockSpec((TM, TN), lambda j, i: (jnp.maximum(i - p_tiles, 0), j)),
                  pl.BlockSpec((None, 1, TN), lambda j, i: (_mod_row(i, TM), 0, j))],
        out_specs=pl.BlockSpec((TM, TN), lambda j, i: (i, j)),
        out_shape=jax.ShapeDtypeStruct((M_ALL, D_MODEL), F32),
        scratch_shapes=[pltpu.VMEM((D_MODEL, TN), BF16)],
        compiler_params=_params(("arbitrary", "arbitrary"), VMEM_LIMIT),
        name="out_proj_residual",
    )(z, w_out, xp, xs, mods_mid)


def _mlp_kernel(x_hbm, nw_ref, sh_ref, sc_ref, g_ref, wu_ref, wd_ref, o_ref, h_ref, x_buf, x_sem, *, tile0):
    i, f = pl.program_id(0), pl.program_id(1)

    def x_copy(tile):
        return pltpu.make_async_copy(x_hbm.at[pl.ds((tile0 + tile) * TM, TM), :], x_buf, x_sem)

    @pl.when((i == 0) & (f == 0))
    def _():
        x_copy(0).start()

    @pl.when(f == 0)
    def _():
        x_copy(i).wait()
        wu = wu_ref[...].astype(BF16)
        wd = wd_ref[...].astype(BF16)
        for r0 in range(0, TM, ROW_CHUNK):
            rows = slice(r0, r0 + ROW_CHUNK)
            x = x_buf[rows, :]
            h = _modnorm(x, nw_ref[...], sc_ref[...], sh_ref[...]).astype(BF16)
            h_ref[rows, :] = h
            u = jnp.square(jnp.maximum(_dot(h, wu), 0.0)).astype(BF16)
            o_ref[rows, :] = x + g_ref[...] * _dot(u, wd)

    @pl.when((f == 1) & (i + 1 < pl.num_programs(0)))
    def _():
        x_copy(i + 1).start()

    @pl.when(f > 0)
    def _():
        u = _dot(h_ref[...], wu_ref[...].astype(BF16))
        u = jnp.square(jnp.maximum(u, 0.0)).astype(BF16)
        for c0 in range(0, D_MODEL, MLP_COLS):
            cols = slice(c0, c0 + MLP_COLS)
            o_ref[:, cols] += g_ref[:, cols] * _dot(u, wd_ref[:, cols].astype(BF16))


def _mlp_call(x1, nw, mods_mid, mods_last, w_up, w_down, tile0, n_tiles):
    mod = lambda k: pl.BlockSpec((None, 1, D_MODEL), lambda i, f: (_mod_row(tile0 + i, TM), 0, k))
    return pl.pallas_call(
        functools.partial(_mlp_kernel, tile0=tile0),
        grid=(n_tiles, D_FF // TF),
        in_specs=[pl.BlockSpec(memory_space=pl.ANY),
                  pl.BlockSpec((1, D_MODEL), lambda i, f: (0, 0)),
                  mod(1), mod(0), mod(1),
                  pl.BlockSpec((D_MODEL, TF), lambda i, f: (0, f)),
                  pl.BlockSpec((TF, D_MODEL), lambda i, f: (f, 0))],
        out_specs=pl.BlockSpec((TM, D_MODEL), lambda i, f: (i, 0)),
        out_shape=jax.ShapeDtypeStruct((n_tiles * TM, D_MODEL), F32),
        scratch_shapes=[pltpu.VMEM((TM, D_MODEL), BF16),
                        pltpu.VMEM((TM, D_MODEL), F32),
                        pltpu.SemaphoreType.DMA(())],
        compiler_params=_params(("arbitrary", "arbitrary"), VMEM_LIMIT),
        name="mlp",
    )(x1, nw, mods_mid, mods_last, mods_last, w_up, w_down)


def kernel(x_prompt, x_sample, cache_a_k, cache_a_v, cache_b_k, cache_b_v, c, c_ctx, norm1_w, norm2_w, w_ada, b_ada, w_in, q_norm_a, k_norm_a, q_norm_b, k_norm_b, sink_a, rpb_b, w_br_a, w_br_b, w_out, w_up, w_down):
    assert w_ada.shape[0] == 1, "one trunk layer"
    xp = x_prompt.reshape(M_PROMPT, D_MODEL)
    xs = x_sample.reshape(M_SAMPLE, D_MODEL)

    mods_head, cond = _ada_call(c_ctx[None, :], c, w_ada[0], b_ada)
    ada = (cond, w_ada[0], b_ada)

    w = w_in[0]
    rope = _rope_tables()
    h, kva, ka_p, va_p = _norm_kva_call(xp, xs, norm1_w, mods_head, w, k_norm_a, rope)
    kv, kb_p, vb_p = _proj_call(h, w, q_norm_a, q_norm_b, k_norm_b, rope, PROJ_KV)
    q, gates = _proj_call(h, w, q_norm_a, q_norm_b, k_norm_b, rope, PROJ_QG)

    sink = sink_a[0]
    oa_p, ob_p = _ctx_attn_call(sink, q, kv, kva)
    oa_s, mods_last = _win_attn_call(sink, q, kva, cache_a_k, cache_a_v, ada)
    ob_s, mods_mid = _na_attn_call(q, kv, cache_b_k, cache_b_v, rpb_b[0], ada)

    z = _merge_call(oa_p, oa_s, ob_p, ob_s, w_br_a[0], w_br_b[0], gates)
    x1 = _out_call(z, w_out[0], xp, xs, mods_mid)

    p_tiles = M_PROMPT // TM
    y_p = _mlp_call(x1, norm2_w, mods_mid, mods_last, w_up[0], w_down[0], 0, p_tiles)
    y_s = _mlp_call(x1, norm2_w, mods_mid, mods_last, w_up[0], w_down[0], p_tiles, M_SAMPLE // TM)

    return (y_p.reshape(BATCH, SEQ, D_MODEL),
            y_s.reshape(DEC_BATCH, DEC_SEQ, D_MODEL),
            ka_p.reshape(BATCH, 1, SEQ, A_KV_HEADS, HEAD_DIM),
            va_p.reshape(BATCH, 1, SEQ, A_KV_HEADS, HEAD_DIM),
            kb_p.reshape(BATCH, 1, SEQ, B_HEADS, HEAD_DIM),
            vb_p.reshape(BATCH, 1, SEQ, B_HEADS, HEAD_DIM))
```

```python
import functools

import numpy as np
import jax
import jax.numpy as jnp
from jax import lax
from jax.experimental import pallas as pl
from jax.experimental.pallas import tpu as pltpu

D_MODEL = 2048
BATCH = 16
SEQ = 256
DEC_BATCH = 2
DEC_SEQ = 1024
PAST_LEN = 256
GRID_W = 64
HEAD_DIM = 128
A_HEADS = 8
A_KV_HEADS = 2
A_GROUP = A_HEADS // A_KV_HEADS
A_WINDOW = 128
BLOCK = 128
B_HEADS = 8
NA_ROWS_MAX = 8
NA_COLS = 16
D_FF = 4 * D_MODEL
ROPE_THETA = 10000.0
EPS = 1e-6
NEG = -1e30
A_Q = A_HEADS * HEAD_DIM
A_KV = A_KV_HEADS * HEAD_DIM
B_QKV = B_HEADS * HEAD_DIM
IN_WIDTH = A_Q + 2 * A_KV + 3 * B_QKV + 2 * D_MODEL
SCALE = HEAD_DIM ** -0.5

M_PROMPT = BATCH * SEQ
M_SAMPLE = DEC_BATCH * DEC_SEQ
M_ALL = M_PROMPT + M_SAMPLE
GRID_ROWS = DEC_SEQ // GRID_W
NA_ROWS = min(NA_ROWS_MAX, GRID_ROWS)

COL_QA = 0
COL_KVA = A_Q
COL_QB = A_Q + 2 * A_KV
COL_KB = COL_QB + B_QKV
COL_VB = COL_KB + B_QKV
COL_G = COL_VB + B_QKV

TM = 1024
TN = 1024
ROW_CHUNK = 256
W_SUB = 512
TM_NORM = 512
TF = 512
MLP_COLS = 512
ADA_HEAD = 2 * D_MODEL
ADA_SIDE_COLS = 1024
ADA_SIDE = 2 * D_MODEL
VMEM_LIMIT = 60 * 1024 * 1024

F32 = jnp.float32
BF16 = jnp.bfloat16


def _mod_row(i, tm):
    p_tiles = M_PROMPT // tm
    return jnp.where(i < p_tiles, 0, 1 + (i - p_tiles) // (DEC_SEQ // tm))


def _row_chunks(tm):
    edges = list(range(0, tm, ROW_CHUNK)) + [tm - ROW_CHUNK // 2, tm]
    edges = sorted(set(edges))
    return [slice(a, b) for a, b in zip(edges[:-1], edges[1:])]


def _dot(a, b):
    return jnp.dot(a, b, preferred_element_type=F32)


def _dot_nt(a, b):
    return lax.dot_general(a, b, (((1,), (1,)), ((), ())), preferred_element_type=F32)


def _params(sem, vmem=None):
    return pltpu.CompilerParams(dimension_semantics=sem, vmem_limit_bytes=vmem)


def _ada_side(s_ref, w_ref, b_ref, o_ref):
    o_ref[:, 0, :] = _dot(s_ref[...], w_ref[...].astype(BF16)) + b_ref[...]


def _ada_kernel(cctx_ref, c_ref, w_ref, b_ref, o_ref, s_ref):
    row = lax.broadcasted_iota(jnp.int32, (8, D_MODEL), 0)
    cv = jnp.where(row == 0, cctx_ref[...], 0.0)
    for b in range(DEC_BATCH):
        cv = jnp.where(row == 1 + b, c_ref[b:b + 1, :], cv)
    s_ref[...] = (cv * jax.nn.sigmoid(cv)).astype(BF16)
    _ada_side(s_ref, w_ref, b_ref, o_ref)


def _ada_call(c_ctx, c, w_ada, b_ada):
    tn = 1024
    n = ADA_HEAD
    return pl.pallas_call(
        _ada_kernel,
        grid=(n // tn,),
        in_specs=[pl.BlockSpec((1, D_MODEL), lambda j: (0, 0)),
                  pl.BlockSpec((DEC_BATCH, D_MODEL), lambda j: (0, 0)),
                  pl.BlockSpec((D_MODEL, tn), lambda j: (0, j)),
                  pl.BlockSpec((1, tn), lambda j: (0, j))],
        out_specs=[pl.BlockSpec((8, 1, tn), lambda j: (0, 0, j)),
                   pl.BlockSpec((8, D_MODEL), lambda j: (0, 0))],
        out_shape=[jax.ShapeDtypeStruct((8, 1, n), F32),
                   jax.ShapeDtypeStruct((8, D_MODEL), BF16)],
        compiler_params=_params(("arbitrary",), VMEM_LIMIT),
        name="ada_mod",
    )(c_ctx, c, w_ada, b_ada)


def _ada_side_specs(first_col, step):
    blk = lambda *g: first_col // ADA_SIDE_COLS + step(*g)
    in_specs = [pl.BlockSpec((8, D_MODEL), lambda *g: (0, 0)),
                pl.BlockSpec((D_MODEL, ADA_SIDE_COLS), lambda *g: (0, blk(*g))),
                pl.BlockSpec((1, ADA_SIDE_COLS), lambda *g: (0, blk(*g)))]
    out_spec = pl.BlockSpec((8, 1, ADA_SIDE_COLS), lambda *g: (0, 0, step(*g)))
    return in_specs, out_spec, jax.ShapeDtypeStruct((8, 1, ADA_SIDE), F32)


def _modnorm(x, nw, sc, sh):
    y = x * lax.rsqrt(jnp.mean(x * x, axis=-1, keepdims=True) + EPS)
    return y * (nw * (1.0 + sc)) + sh


def _head_norm(x, nw):
    return x * lax.rsqrt(jnp.mean(x * x, axis=-1, keepdims=True) + EPS) * nw


def _rope(x, cos, sin_signed):
    lane = lax.broadcasted_iota(jnp.int32, x.shape, 1)
    partner = jnp.where((lane % 64) < 32, pltpu.roll(x, 96, 1), pltpu.roll(x, 32, 1))
    return x * cos + partner * sin_signed


def _norm_kva_kernel(xp_ref, xs_ref, nw1_ref, sh_ref, sc_ref, w_ref, nwk_ref, cos_ref, sin_ref,
                     h_ref, kv_ref, kp_ref, vp_ref, wbf):
    i = pl.program_id(0)
    p_tiles = M_PROMPT // TM_NORM

    @pl.when(i == 0)
    def _():
        wbf[...] = w_ref[...].astype(BF16)

    def head_cols(k, base=0):
        return slice(base + k * HEAD_DIM, base + (k + 1) * HEAD_DIM)

    def run(prompt):
        x_ref = xp_ref if prompt else xs_ref
        for r0 in range(0, TM_NORM, ROW_CHUNK):
            rows = slice(r0, r0 + ROW_CHUNK)
            h = _modnorm(x_ref[rows, :], nw1_ref[...], sc_ref[...], sh_ref[...]).astype(BF16)
            h_ref[rows, :] = h
            acc = _dot(h, wbf[...])
            for k in range(A_KV_HEADS):
                y = _head_norm(acc[:, head_cols(k)], nwk_ref[...])
                v = acc[:, head_cols(k, A_KV)]
                if prompt:
                    kp_ref[rows, k, :] = y
                    vp_ref[rows, k, :] = v
                else:
                    y = _rope(y, cos_ref[rows, :], sin_ref[rows, :])
                kv_ref[rows, head_cols(k)] = y.astype(BF16)
                kv_ref[rows, head_cols(k, A_KV)] = v.astype(BF16)

    pl.when(i < p_tiles)(functools.partial(run, True))
    pl.when(i >= p_tiles)(functools.partial(run, False))


def _norm_kva_call(xp, xs, nw1, mods3, w_in, nwk, rope):
    tm = TM_NORM
    p_tiles = M_PROMPT // tm
    s_tiles = DEC_SEQ // tm
    assert COL_KVA % (2 * A_KV) == 0
    rope_spec = pl.BlockSpec((tm, HEAD_DIM), lambda i: (jnp.maximum(i - p_tiles, 0) % s_tiles, 0))
    parked = pl.BlockSpec((tm, A_KV_HEADS, HEAD_DIM), lambda i: (jnp.minimum(i, p_tiles - 1), 0, 0))
    return pl.pallas_call(
        _norm_kva_kernel,
        grid=(M_ALL // tm,),
        in_specs=[pl.BlockSpec((tm, D_MODEL), lambda i: (jnp.minimum(i, p_tiles - 1), 0)),
                  pl.BlockSpec((tm, D_MODEL), lambda i: (jnp.maximum(i - p_tiles, 0), 0)),
                  pl.BlockSpec((1, D_MODEL), lambda i: (0, 0)),
                  pl.BlockSpec((None, 1, D_MODEL), lambda i: (_mod_row(i, tm), 0, 0)),
                  pl.BlockSpec((None, 1, D_MODEL), lambda i: (_mod_row(i, tm), 0, 1)),
                  pl.BlockSpec((D_MODEL, 2 * A_KV), lambda i: (0, COL_KVA // (2 * A_KV)),
                               pipeline_mode=pl.Buffered(1)),
                  pl.BlockSpec((1, HEAD_DIM), lambda i: (0, 0)),
                  rope_spec, rope_spec],
        out_specs=[pl.BlockSpec((tm, D_MODEL), lambda i: (i, 0)),
                   pl.BlockSpec((tm, 2 * A_KV), lambda i: (i, 0)),
                   parked, parked],
        out_shape=[jax.ShapeDtypeStruct((M_ALL, D_MODEL), BF16),
                   jax.ShapeDtypeStruct((M_ALL, 2 * A_KV), BF16),
                   jax.ShapeDtypeStruct((M_PROMPT, A_KV_HEADS, HEAD_DIM), F32),
                   jax.ShapeDtypeStruct((M_PROMPT, A_KV_HEADS, HEAD_DIM), F32)],
        scratch_shapes=[pltpu.VMEM((D_MODEL, 2 * A_KV), BF16)],
        compiler_params=_params(("arbitrary",), VMEM_LIMIT),
        name="norm1_proj_kva",
    )(xp, xs, nw1, mods3, mods3, w_in, nwk, *rope)


def _rope_tables():
    n_freq = HEAD_DIM // 4
    pos = np.arange(DEC_SEQ)
    row = (pos // GRID_W).astype(np.float64)
    col = (pos % GRID_W).astype(np.float64)
    inv = ROPE_THETA ** (-np.arange(n_freq, dtype=np.float64) / n_freq)
    ar = row[:, None] * inv
    ac = col[:, None] * inv
    cos = np.concatenate([np.cos(ar), np.cos(ar), np.cos(ac), np.cos(ac)], axis=-1)
    sin = np.concatenate([-np.sin(ar), np.sin(ar), -np.sin(ac), np.sin(ac)], axis=-1)
    return jnp.asarray(cos, F32), jnp.asarray(sin, F32)


PROJ_COL = {"qa": COL_QA, "qb": COL_QB, "kb": COL_KB, "vb": COL_VB}
GATE_PANELS = 2 * D_MODEL // TN
PROJ_KV = ("kb", "vb")
PROJ_QG = ("qa", "qb") + ("gate",) * GATE_PANELS


def _proj_kernel(*refs, kinds):
    names = ["h", "w0", "w1", "nqa", "nqb", "nkb", "cos", "sin", "qkv"]
    names += ["gate"] * ("gate" in kinds) + ["kbp"] * ("kb" in kinds) + ["vbp"] * ("vb" in kinds) + ["wbf"]
    r = dict(zip(names, refs, strict=True))
    n_qkv = sum(k != "gate" for k in kinds)
    j, i = pl.program_id(0), pl.program_id(1)
    is_prompt = i < M_PROMPT // TM

    @pl.when(i == 0)
    def _():
        r["wbf"][:, :W_SUB] = r["w0"][...].astype(BF16)
        r["wbf"][:, W_SUB:] = r["w1"][...].astype(BF16)

    def head_cols(k):
        return slice(k * HEAD_DIM, (k + 1) * HEAD_DIM)

    def run(kind, prompt):
        for rows in _row_chunks(TM):
            acc = _dot(r["h"][rows, :], r["wbf"][...])
            if kind == "gate":
                r["gate"][rows, :] = jax.nn.sigmoid(acc)
            elif kind == "vb":
                if prompt:
                    r["vbp"][rows, :] = acc
                r["qkv"][rows, :] = acc.astype(BF16)
            else:
                nw_ref = r[{"qa": "nqa", "qb": "nqb", "kb": "nkb"}[kind]]
                for k in range(TN // HEAD_DIM):
                    y = _head_norm(acc[:, head_cols(k)], nw_ref[...])
                    if kind == "qa" and not prompt:
                        y = _rope(y, r["cos"][rows, :], r["sin"][rows, :])
                    if kind == "kb" and prompt:
                        r["kbp"][rows, head_cols(k)] = y
                    r["qkv"][rows, head_cols(k)] = y.astype(BF16)

    for p, kind in enumerate(kinds[:n_qkv]):
        if kind == "qb":
            pl.when(j == p)(functools.partial(run, kind, None))
        else:
            pl.when((j == p) & is_prompt)(functools.partial(run, kind, True))
            pl.when((j == p) & jnp.logical_not(is_prompt))(functools.partial(run, kind, False))
    if "gate" in kinds:
        pl.when(j >= n_qkv)(functools.partial(run, "gate", None))


def _proj_call(h, w_in, nqa, nqb, nkb, rope, kinds):
    assert TN == 2 * W_SUB and A_Q == B_QKV == TN
    n_panels = len(kinds)
    n_qkv = sum(k != "gate" for k in kinds)
    assert all(k == "gate" for k in kinds[n_qkv:])
    cols = [PROJ_COL[k] for k in kinds[:n_qkv]] + list(range(COL_G, IN_WIDTH, TN))[:n_panels - n_qkv]
    p_tiles = M_PROMPT // TM
    s_tiles = DEC_SEQ // TM
    n_tiles = M_ALL // TM

    def w_sub(j):
        idx = cols[0] // W_SUB
        for p in range(1, n_panels):
            idx = jnp.where(j == p, cols[p] // W_SUB, idx)
        return idx

    def qkv_idx(j, i):
        return (jnp.minimum(j, n_qkv - 1), jnp.where(j >= n_qkv, n_tiles - 1, i), 0)

    def gate_idx(j, i):
        return (jnp.maximum(j - n_qkv, 0), jnp.where(j < n_qkv, 0, i), 0)

    def cache_idx(panel):
        def idx(j, i):
            return (jnp.where(j < panel, 0, jnp.where(j > panel, p_tiles - 1, jnp.minimum(i, p_tiles - 1))), 0)
        return idx

    out_specs = [pl.BlockSpec((None, TM, TN), qkv_idx)]
    out_shape = [jax.ShapeDtypeStruct((n_qkv, M_ALL, TN), BF16)]
    if "gate" in kinds:
        out_specs.append(pl.BlockSpec((None, TM, TN), gate_idx))
        out_shape.append(jax.ShapeDtypeStruct((n_panels - n_qkv, M_ALL, TN), F32))
    for kind in ("kb", "vb"):
        if kind in kinds:
            out_specs.append(pl.BlockSpec((TM, TN), cache_idx(kinds.index(kind))))
            out_shape.append(jax.ShapeDtypeStruct((M_PROMPT, B_QKV), F32))

    norm_spec = pl.BlockSpec((1, HEAD_DIM), lambda j, i: (0, 0))
    rope_spec = pl.BlockSpec((TM, HEAD_DIM), lambda j, i: (jnp.maximum(i - p_tiles, 0) % s_tiles, 0))
    in_specs = [pl.BlockSpec((TM, D_MODEL), lambda j, i: (i, 0)),
                pl.BlockSpec((D_MODEL, W_SUB), lambda j, i: (0, w_sub(j))),
                pl.BlockSpec((D_MODEL, W_SUB), lambda j, i: (0, w_sub(j) + 1)),
                norm_spec, norm_spec, norm_spec, rope_spec, rope_spec]
    args = [h, w_in, w_in, nqa, nqb, nkb, *rope]
    return pl.pallas_call(
        functools.partial(_proj_kernel, kinds=kinds),
        grid=(n_panels, n_tiles),
        in_specs=in_specs,
        out_specs=out_specs,
        out_shape=out_shape,
        scratch_shapes=[pltpu.VMEM((D_MODEL, TN), BF16)],
        compiler_params=_params(("arbitrary", "arbitrary"), VMEM_LIMIT),
        name="proj_" + "_".join(dict.fromkeys(kinds)),
    )(*args)


LOG2E = 1.4426950408889634
QK_LOG2 = SCALE * LOG2E


def _softmax_pv(parts, sink):
    m = functools.reduce(jnp.maximum, [jnp.max(t, axis=-1, keepdims=True) for t, _ in parts])
    if sink is not None:
        m = jnp.maximum(m, sink)
    acc = None
    for t, v in parts:
        p = jnp.exp2(t - m).astype(BF16)
        y = _dot(p, jnp.concatenate([v, jnp.ones_like(v)], axis=1))
        acc = y if acc is None else acc + y
    o, l = acc[:, :HEAD_DIM], acc[:, HEAD_DIM:]
    if sink is not None:
        l = l + jnp.exp2(sink - m)
    return o / l


def _sink_column(sink_ref, hk, rows_per_head):
    n = A_GROUP * rows_per_head
    g = lax.broadcasted_iota(jnp.int32, (n, 1), 0) // rows_per_head
    col = jnp.full((n, 1), sink_ref[hk * A_GROUP], F32)
    for k in range(1, A_GROUP):
        col = jnp.where(g == k, sink_ref[hk * A_GROUP + k], col)
    return col * LOG2E


def _stack_group(q_ref, hk):
    return jnp.concatenate(
        [q_ref[:, (hk * A_GROUP + g) * HEAD_DIM:(hk * A_GROUP + g + 1) * HEAD_DIM] for g in range(A_GROUP)], axis=0)


def _unstack_group(o_ref, hk, o, rows):
    for g in range(A_GROUP):
        c = (hk * A_GROUP + g) * HEAD_DIM
        o_ref[:, c:c + HEAD_DIM] = o[g * rows:(g + 1) * rows].astype(BF16)


CTX_REQS = 4


def _ctx_attn_kernel(sink_ref, q_ref, kv_ref, kva_ref, oa_ref, ob_ref):
    for req in range(CTX_REQS):
        rows = pl.ds(req * SEQ, SEQ)
        qa_ref, qb_ref = q_ref.at[PROJ_QG.index("qa"), rows], q_ref.at[PROJ_QG.index("qb"), rows]
        kb_ref, vb_ref = kv_ref.at[PROJ_KV.index("kb"), rows], kv_ref.at[PROJ_KV.index("vb"), rows]
        ka_ref, oa, ob = kva_ref.at[rows], oa_ref.at[rows], ob_ref.at[rows]
        for hk in range(A_KV_HEADS):
            k = ka_ref[:, hk * HEAD_DIM:(hk + 1) * HEAD_DIM]
            v = ka_ref[:, A_KV + hk * HEAD_DIM:A_KV + (hk + 1) * HEAD_DIM]
            q4 = _stack_group(qa_ref, hk)
            t = _dot_nt(q4, k) * QK_LOG2
            o = _softmax_pv([(t, v)], _sink_column(sink_ref, hk, SEQ))
            _unstack_group(oa, hk, o, SEQ)
        for h in range(B_HEADS):
            sl = slice(h * HEAD_DIM, (h + 1) * HEAD_DIM)
            t = _dot_nt(qb_ref[:, sl], kb_ref[:, sl]) * QK_LOG2
            ob[:, sl] = _softmax_pv([(t, vb_ref[:, sl])], None).astype(BF16)


def _ctx_attn_call(sink, q, kv, kva):
    rows = CTX_REQS * SEQ
    row = lambda w: pl.BlockSpec((rows, w), lambda b: (b, 0))
    return pl.pallas_call(
        _ctx_attn_kernel,
        grid=(BATCH // CTX_REQS,),
        in_specs=[pl.BlockSpec(memory_space=pltpu.SMEM),
                  pl.BlockSpec((2, rows, TN), lambda b: (0, b, 0)),
                  pl.BlockSpec((2, rows, TN), lambda b: (0, b, 0)), row(2 * A_KV)],
        out_specs=[row(A_Q), row(B_QKV)],
        out_shape=[jax.ShapeDtypeStruct((M_PROMPT, A_Q), BF16),
                   jax.ShapeDtypeStruct((M_PROMPT, B_QKV), BF16)],
        compiler_params=_params(("arbitrary",), VMEM_LIMIT),
        name="attn_ctx",
    )(sink, q, kv, kva)


BAND = 3 * BLOCK


def _cache_to_bf16(n_heads, ck_ref, cv_ref, ckb, cvb):
    for h in range(n_heads):
        sl = slice(h * HEAD_DIM, (h + 1) * HEAD_DIM)
        ckb[:, sl] = ck_ref[:, h, :].astype(BF16)
        cvb[:, sl] = cv_ref[:, h, :].astype(BF16)


WIN_BLOCKS = 4


def _win_attn_kernel(sink_ref, q_ref, kv_ref, ck_ref, cv_ref, s_ref, wada_ref, bada_ref, o_ref, mods_ref, ckb, cvb):
    step = pl.program_id(1)
    pl.when(step == 0)(functools.partial(_cache_to_bf16, A_KV_HEADS, ck_ref, cv_ref, ckb, cvb))
    for blk in range(WIN_BLOCKS):
        n = step * WIN_BLOCKS + blk
        rows = pl.ds(blk * BLOCK, BLOCK)
        start = pl.multiple_of(jnp.clip((n - 1) * BLOCK, 0, DEC_SEQ - BAND), BLOCK)
        qpos = n * BLOCK + lax.broadcasted_iota(jnp.int32, (BLOCK, BAND), 0)
        kpos = start + lax.broadcasted_iota(jnp.int32, (BLOCK, BAND), 1)
        valid = jnp.abs(qpos - kpos) <= A_WINDOW
        valid = jnp.concatenate([valid.astype(jnp.int32)] * A_GROUP, axis=0) > 0
        for hk in range(A_KV_HEADS):
            sl = slice(hk * HEAD_DIM, (hk + 1) * HEAD_DIM)
            slv = slice(A_KV + hk * HEAD_DIM, A_KV + (hk + 1) * HEAD_DIM)
            k_loc = kv_ref[pl.ds(start, BAND), sl]
            v_loc = kv_ref[pl.ds(start, BAND), slv]
            q4 = _stack_group(q_ref.at[rows], hk)
            t_loc = jnp.where(valid, _dot_nt(q4, k_loc) * QK_LOG2, NEG)
            t_ctx = _dot_nt(q4, ckb[:, sl]) * QK_LOG2
            o = _softmax_pv([(t_loc, v_loc), (t_ctx, cvb[:, sl])], _sink_column(sink_ref, hk, BLOCK))
            _unstack_group(o_ref.at[rows], hk, o, BLOCK)
    _ada_side(s_ref, wada_ref, bada_ref, mods_ref)


def _win_attn_call(sink, q, kva, cache_k, cache_v, ada):
    rows = WIN_BLOCKS * BLOCK
    nb = DEC_SEQ // rows
    q0 = M_PROMPT // rows
    b0 = M_PROMPT // DEC_SEQ
    assert DEC_BATCH * nb * ADA_SIDE_COLS == ADA_SIDE and ADA_HEAD + 2 * ADA_SIDE == 6 * D_MODEL
    side_in, side_out, side_shape = _ada_side_specs(ADA_HEAD + ADA_SIDE, lambda b, n: b * nb + n)
    return pl.pallas_call(
        _win_attn_kernel,
        grid=(DEC_BATCH, nb),
        in_specs=[pl.BlockSpec(memory_space=pltpu.SMEM),
                  pl.BlockSpec((None, rows, A_Q), lambda b, n: (PROJ_QG.index("qa"), q0 + b * nb + n, 0)),
                  pl.BlockSpec((DEC_SEQ, 2 * A_KV), lambda b, n: (b0 + b, 0)),
                  pl.BlockSpec((None, None, PAST_LEN, A_KV_HEADS, HEAD_DIM), lambda b, n: (b, 0, 0, 0, 0)),
                  pl.BlockSpec((None, None, PAST_LEN, A_KV_HEADS, HEAD_DIM), lambda b, n: (b, 0, 0, 0, 0))] + side_in,
        out_specs=[pl.BlockSpec((rows, A_Q), lambda b, n: (b * nb + n, 0)), side_out],
        out_shape=[jax.ShapeDtypeStruct((M_SAMPLE, A_Q), BF16), side_shape],
        scratch_shapes=[pltpu.VMEM((PAST_LEN, A_KV), BF16), pltpu.VMEM((PAST_LEN, A_KV), BF16)],
        compiler_params=_params(("arbitrary", "arbitrary"), VMEM_LIMIT),
        name="attn_window",
    )(sink, q, kva, cache_k, cache_v, *ada)


NA_QROWS = 4
NA_BLOCKS = 2
NA_WIN_ROWS = 12
NA_Q = NA_QROWS * GRID_W
NA_KEYS = NA_WIN_ROWS * GRID_W
NA_DR = 2 * NA_ROWS_MAX - 1
NA_PAIRS = NA_DR + 1
RPB_W = 2 * NA_COLS - 1
RPB_PAD = GRID_W - NA_COLS


def _na_row_start(r):
    return jnp.clip(r - NA_ROWS // 2, 0, GRID_ROWS - NA_ROWS)


def _na_window_row0(blk):
    return jnp.clip(blk * NA_QROWS - NA_ROWS // 2, 0, GRID_ROWS - NA_WIN_ROWS)


def _check_na_windows():
    for blk in range(GRID_ROWS // NA_QROWS):
        w0 = int(np.clip(blk * NA_QROWS - NA_ROWS // 2, 0, GRID_ROWS - NA_WIN_ROWS))
        for r in range(blk * NA_QROWS, (blk + 1) * NA_QROWS):
            r0 = int(np.clip(r - NA_ROWS // 2, 0, GRID_ROWS - NA_ROWS))
            assert w0 <= r0 and r0 + NA_ROWS <= w0 + NA_WIN_ROWS, (blk, r)


_check_na_windows()


def _na_attn_kernel(q_ref, k_ref, v_ref, ck_ref, cv_ref, rpb_ref, s_ref, wada_ref, bada_ref,
                    o_ref, mods_ref, tab_ref, ckb, cvb):
    b, step = pl.program_id(0), pl.program_id(1)
    lane = lax.broadcasted_iota(jnp.int32, (GRID_W, 2 * GRID_W), 1)
    pl.when(step == 0)(functools.partial(_cache_to_bf16, B_HEADS, ck_ref, cv_ref, ckb, cvb))

    @pl.when((b == 0) & (step == 0))
    def _():
        for h in range(B_HEADS):
            for d in range(NA_PAIRS):
                lo = jnp.broadcast_to(rpb_ref[h, d:d + 1, :], (GRID_W, 2 * GRID_W))
                hi = jnp.broadcast_to(rpb_ref[h, d + 1:d + 2, :], (GRID_W, 2 * GRID_W))
                lo = pltpu.roll(lo, GRID_W + 1, 1, stride=1, stride_axis=0)
                hi = pltpu.roll(hi, 1, 1, stride=1, stride_axis=0)
                tab_ref[h, d] = jnp.where(lane < GRID_W, lo, hi) * LOG2E

    for sub in range(NA_BLOCKS):
        blk = step * NA_BLOCKS + sub
        rows = pl.ds(sub * NA_Q, NA_Q)
        row0 = _na_window_row0(blk)
        k0 = pl.multiple_of(row0 * GRID_W, GRID_W)
        qi = lax.broadcasted_iota(jnp.int32, (NA_Q, NA_KEYS), 0)
        ki = lax.broadcasted_iota(jnp.int32, (NA_Q, NA_KEYS), 1)
        qrow, qcol = blk * NA_QROWS + qi // GRID_W, qi % GRID_W
        krow, kcol = row0 + ki // GRID_W, ki % GRID_W
        rstart = _na_row_start(qrow)
        cstart = jnp.clip(qcol - NA_COLS // 2, 0, GRID_W - NA_COLS)
        valid = (krow >= rstart) & (krow < rstart + NA_ROWS) & (kcol >= cstart) & (kcol < cstart + NA_COLS)

        for h in range(B_HEADS):
            sl = slice(h * HEAD_DIM, (h + 1) * HEAD_DIM)
            bias = jnp.concatenate(
                [jnp.concatenate(
                    [tab_ref[h, jnp.clip(row0 + 2 * p - blk * NA_QROWS - q + NA_ROWS_MAX, 0, NA_PAIRS - 1)]
                     for p in range(NA_WIN_ROWS // 2)], axis=1)
                 for q in range(NA_QROWS)], axis=0)
            k_loc = k_ref[pl.ds(k0, NA_KEYS), sl]
            v_loc = v_ref[pl.ds(k0, NA_KEYS), sl]
            t_loc = jnp.where(valid, _dot_nt(q_ref[rows, sl], k_loc) * QK_LOG2 + bias, NEG)
            t_ctx = _dot_nt(q_ref[rows, sl], ckb[:, sl]) * QK_LOG2
            o = _softmax_pv([(t_loc, v_loc), (t_ctx, cvb[:, sl])], None)
            o_ref[rows, sl] = o.astype(BF16)
    _ada_side(s_ref, wada_ref, bada_ref, mods_ref)


def _na_bias_rows(rpb):
    rows = jnp.pad(rpb, ((0, 0), (1, 1), (0, 0)))
    left = jnp.broadcast_to(rows[..., :1], rows.shape[:2] + (RPB_PAD,))
    right = jnp.broadcast_to(rows[..., -1:], rows.shape[:2] + (2 * GRID_W - RPB_PAD - RPB_W,))
    return jnp.concatenate([left, rows, right], axis=-1)


def _na_attn_call(q, kv, cache_k, cache_v, rpb, ada):
    nblk = GRID_ROWS // (NA_QROWS * NA_BLOCKS)
    assert DEC_BATCH * nblk * ADA_SIDE_COLS == ADA_SIDE
    side_in, side_out, side_shape = _ada_side_specs(ADA_HEAD, lambda b, r: b * nblk + r)
    q0 = M_PROMPT // (NA_Q * NA_BLOCKS)
    b0 = M_PROMPT // DEC_SEQ
    return pl.pallas_call(
        _na_attn_kernel,
        grid=(DEC_BATCH, nblk),
        in_specs=[pl.BlockSpec((None, NA_Q * NA_BLOCKS, B_QKV), lambda b, r: (PROJ_QG.index("qb"), q0 + b * nblk + r, 0)),
                  pl.BlockSpec((None, DEC_SEQ, B_QKV), lambda b, r: (PROJ_KV.index("kb"), b0 + b, 0)),
                  pl.BlockSpec((None, DEC_SEQ, B_QKV), lambda b, r: (PROJ_KV.index("vb"), b0 + b, 0)),
                  pl.BlockSpec((None, None, PAST_LEN, B_HEADS, HEAD_DIM), lambda b, r: (b, 0, 0, 0, 0)),
                  pl.BlockSpec((None, None, PAST_LEN, B_HEADS, HEAD_DIM), lambda b, r: (b, 0, 0, 0, 0)),
                  pl.BlockSpec((B_HEADS, NA_DR + 2, 2 * GRID_W), lambda b, r: (0, 0, 0))] + side_in,
        out_specs=[pl.BlockSpec((NA_Q * NA_BLOCKS, B_QKV), lambda b, r: (b * nblk + r, 0)), side_out],
        out_shape=[jax.ShapeDtypeStruct((M_SAMPLE, B_QKV), BF16), side_shape],
        scratch_shapes=[pltpu.VMEM((B_HEADS, NA_PAIRS, GRID_W, 2 * GRID_W), F32),
                        pltpu.VMEM((PAST_LEN, B_QKV), BF16), pltpu.VMEM((PAST_LEN, B_QKV), BF16)],
        compiler_params=_params(("arbitrary", "arbitrary"), VMEM_LIMIT),
        name="attn_neighbourhood",
    )(q, kv, kv, cache_k, cache_v, _na_bias_rows(rpb), *ada)


def _merge_kernel(oap, oas, obp, obs, wa_ref, wb_ref, ga_ref, gb_ref, z_ref, wa_bf, wb_bf):
    i = pl.program_id(1)
    p_tiles = M_PROMPT // TM

    @pl.when(i == 0)
    def _():
        wa_bf[...] = wa_ref[...].astype(BF16)
        wb_bf[...] = wb_ref[...].astype(BF16)

    def run(oa_ref, ob_ref):
        for rows in _row_chunks(TM):
            ya = _dot(oa_ref[rows, :], wa_bf[...])
            yb = _dot(ob_ref[rows, :], wb_bf[...])
            z_ref[rows, :] = (ga_ref[rows, :] * ya + gb_ref[rows, :] * yb).astype(BF16)

    pl.when(i < p_tiles)(functools.partial(run, oap, obp))
    pl.when(i >= p_tiles)(functools.partial(run, oas, obs))


def _merge_call(oa_p, oa_s, ob_p, ob_s, w_br_a, w_br_b, gates):
    p_tiles = M_PROMPT // TM
    nj = D_MODEL // TN
    pspec = pl.BlockSpec((TM, A_Q), lambda j, i: (jnp.minimum(i, p_tiles - 1), 0))
    sspec = pl.BlockSpec((TM, A_Q), lambda j, i: (jnp.maximum(i - p_tiles, 0), 0))
    return pl.pallas_call(
        _merge_kernel,
        grid=(nj, M_ALL // TM),
        in_specs=[pspec, sspec, pspec, sspec,
                  pl.BlockSpec((A_Q, TN), lambda j, i: (0, j)),
                  pl.BlockSpec((B_QKV, TN), lambda j, i: (0, j)),
                  pl.BlockSpec((None, TM, TN), lambda j, i: (j, i, 0)),
                  pl.BlockSpec((None, TM, TN), lambda j, i: (nj + j, i, 0))],
        out_specs=pl.BlockSpec((TM, TN), lambda j, i: (i, j)),
        out_shape=jax.ShapeDtypeStruct((M_ALL, D_MODEL), BF16),
        scratch_shapes=[pltpu.VMEM((A_Q, TN), BF16), pltpu.VMEM((B_QKV, TN), BF16)],
        compiler_params=_params(("arbitrary", "arbitrary"), VMEM_LIMIT),
        name="merge_branches",
    )(oa_p, oa_s, ob_p, ob_s, w_br_a, w_br_b, gates, gates)


def _out_kernel(z_ref, w_ref, xp_ref, xs_ref, g_ref, o_ref, wbf):
    i = pl.program_id(1)
    p_tiles = M_PROMPT // TM

    @pl.when(i == 0)
    def _():
        wbf[...] = w_ref[...].astype(BF16)

    def run(x_ref):
        for rows in _row_chunks(TM):
            o_ref[rows, :] = x_ref[rows, :] + g_ref[...] * _dot(z_ref[rows, :], wbf[...])

    pl.when(i < p_tiles)(functools.partial(run, xp_ref))
    pl.when(i >= p_tiles)(functools.partial(run, xs_ref))


def _out_call(z, w_out, xp, xs, mods_mid):
    p_tiles = M_PROMPT // TM
    nj = D_MODEL // TN
    return pl.pallas_call(
        _out_kernel,
        grid=(nj, M_ALL // TM),
        in_specs=[pl.BlockSpec((TM, D_MODEL), lambda j, i: (i, 0)),
                  pl.BlockSpec((D_MODEL, TN), lambda j, i: (0, j)),
                  pl.BlockSpec((TM, TN), lambda j, i: (jnp.minimum(i, p_tiles - 1), j)),
                  pl.BlockSpec((TM, TN), lambda j, i: (jnp.maximum(i - p_tiles, 0), j)),
                  pl.BlockSpec((None, 1, TN), lambda j, i: (_mod_row(i, TM), 0, j))],
        out_specs=pl.BlockSpec((TM, TN), lambda j, i: (i, j)),
        out_shape=jax.ShapeDtypeStruct((M_ALL, D_MODEL), F32),
        scratch_shapes=[pltpu.VMEM((D_MODEL, TN), BF16)],
        compiler_params=_params(("arbitrary", "arbitrary"), VMEM_LIMIT),
        name="out_proj_residual",
    )(z, w_out, xp, xs, mods_mid)


def _mlp_kernel(x_hbm, nw_ref, sh_ref, sc_ref, g_ref, wu_ref, wd_ref, o_ref, h_ref, x_buf, x_sem, *, tile0):
    i, f = pl.program_id(0), pl.program_id(1)

    def x_copy(tile):
        return pltpu.make_async_copy(x_hbm.at[pl.ds((tile0 + tile) * TM, TM), :], x_buf, x_sem)

    @pl.when((i == 0) & (f == 0))
    def _():
        x_copy(0).start()

    @pl.when(f == 0)
    def _():
        x_copy(i).wait()
        wu = wu_ref[...].astype(BF16)
        wd = wd_ref[...].astype(BF16)
        for r0 in range(0, TM, ROW_CHUNK):
            rows = slice(r0, r0 + ROW_CHUNK)
            x = x_buf[rows, :]
            h = _modnorm(x, nw_ref[...], sc_ref[...], sh_ref[...]).astype(BF16)
            h_ref[rows, :] = h
            u = jnp.square(jnp.maximum(_dot(h, wu), 0.0)).astype(BF16)
            o_ref[rows, :] = x + g_ref[...] * _dot(u, wd)

    @pl.when((f == 1) & (i + 1 < pl.num_programs(0)))
    def _():
        x_copy(i + 1).start()

    @pl.when(f > 0)
    def _():
        u = _dot(h_ref[...], wu_ref[...].astype(BF16))
        u = jnp.square(jnp.maximum(u, 0.0)).astype(BF16)
        for c0 in range(0, D_MODEL, MLP_COLS):
            cols = slice(c0, c0 + MLP_COLS)
            o_ref[:, cols] += g_ref[:, cols] * _dot(u, wd_ref[:, cols].astype(BF16))


def _mlp_call(x1, nw, mods_mid, mods_last, w_up, w_down, tile0, n_tiles):
    mod = lambda k: pl.BlockSpec((None, 1, D_MODEL), lambda i, f: (_mod_row(tile0 + i, TM), 0, k))
    return pl.pallas_call(
        functools.partial(_mlp_kernel, tile0=tile0),
        grid=(n_tiles, D_FF // TF),
        in_specs=[pl.BlockSpec(memory_space=pl.ANY),
                  pl.BlockSpec((1, D_MODEL), lambda i, f: (0, 0)),
                  mod(1), mod(0), mod(1),
                  pl.BlockSpec((D_MODEL, TF), lambda i, f: (0, f)),
                  pl.BlockSpec((TF, D_MODEL), lambda i, f: (f, 0))],
        out_specs=pl.BlockSpec((TM, D_MODEL), lambda i, f: (i, 0)),
        out_shape=jax.ShapeDtypeStruct((n_tiles * TM, D_MODEL), F32),
        scratch_shapes=[pltpu.VMEM((TM, D_MODEL), BF16),
                        pltpu.VMEM((TM, D_MODEL), F32),
                        pltpu.SemaphoreType.DMA(())],
        compiler_params=_params(("arbitrary", "arbitrary"), VMEM_LIMIT),
        name="mlp",
    )(x1, nw, mods_mid, mods_last, mods_last, w_up, w_down)


def kernel(x_prompt, x_sample, cache_a_k, cache_a_v, cache_b_k, cache_b_v, c, c_ctx, norm1_w, norm2_w, w_ada, b_ada, w_in, q_norm_a, k_norm_a, q_norm_b, k_norm_b, sink_a, rpb_b, w_br_a, w_br_b, w_out, w_up, w_down):
    assert w_ada.shape[0] == 1, "one trunk layer"
    xp = x_prompt.reshape(M_PROMPT, D_MODEL)
    xs = x_sample.reshape(M_SAMPLE, D_MODEL)

    mods_head, cond = _ada_call(c_ctx[None, :], c, w_ada[0], b_ada)
    ada = (cond, w_ada[0], b_ada)

    w = w_in[0]
    rope = _rope_tables()
    h, kva, ka_p, va_p = _norm_kva_call(xp, xs, norm1_w, mods_head, w, k_norm_a, rope)
    kv, kb_p, vb_p = _proj_call(h, w, q_norm_a, q_norm_b, k_norm_b, rope, PROJ_KV)
    q, gates = _proj_call(h, w, q_norm_a, q_norm_b, k_norm_b, rope, PROJ_QG)

    sink = sink_a[0]
    oa_p, ob_p = _ctx_attn_call(sink, q, kv, kva)
    oa_s, mods_last = _win_attn_call(sink, q, kva, cache_a_k, cache_a_v, ada)
    ob_s, mods_mid = _na_attn_call(q, kv, cache_b_k, cache_b_v, rpb_b[0], ada)

    z = _merge_call(oa_p, oa_s, ob_p, ob_s, w_br_a[0], w_br_b[0], gates)
    x1 = _out_call(z, w_out[0], xp, xs, mods_mid)

    p_tiles = M_PROMPT // TM
    y_p = _mlp_call(x1, norm2_w, mods_mid, mods_last, w_up[0], w_down[0], 0, p_tiles)
    y_s = _mlp_call(x1, norm2_w, mods_mid, mods_last, w_up[0], w_down[0], p_tiles, M_SAMPLE // TM)

    return (y_p.reshape(BATCH, SEQ, D_MODEL),
            y_s.reshape(DEC_BATCH, DEC_SEQ, D_MODEL),
            ka_p.reshape(BATCH, 1, SEQ, A_KV_HEADS, HEAD_DIM),
            va_p.reshape(BATCH, 1, SEQ, A_KV_HEADS, HEAD_DIM),
            kb_p.reshape(BATCH, 1, SEQ, B_HEADS, HEAD_DIM),
            vb_p.reshape(BATCH, 1, SEQ, B_HEADS, HEAD_DIM))
```

```python
import functools

import numpy as np
import jax
import jax.numpy as jnp
from jax import lax
from jax.experimental import pallas as pl
from jax.experimental.pallas import tpu as pltpu

D_MODEL = 2048
BATCH = 16
SEQ = 256
DEC_BATCH = 2
DEC_SEQ = 1024
PAST_LEN = 256
GRID_W = 64
HEAD_DIM = 128
A_HEADS = 8
A_KV_HEADS = 2
A_GROUP = A_HEADS // A_KV_HEADS
A_WINDOW = 128
BLOCK = 128
B_HEADS = 8
NA_ROWS_MAX = 8
NA_COLS = 16
D_FF = 4 * D_MODEL
ROPE_THETA = 10000.0
EPS = 1e-6
NEG = -1e30
A_Q = A_HEADS * HEAD_DIM
A_KV = A_KV_HEADS * HEAD_DIM
B_QKV = B_HEADS * HEAD_DIM
IN_WIDTH = A_Q + 2 * A_KV + 3 * B_QKV + 2 * D_MODEL
SCALE = HEAD_DIM ** -0.5

M_PROMPT = BATCH * SEQ
M_SAMPLE = DEC_BATCH * DEC_SEQ
M_ALL = M_PROMPT + M_SAMPLE
GRID_ROWS = DEC_SEQ // GRID_W
NA_ROWS = min(NA_ROWS_MAX, GRID_ROWS)

COL_QA = 0
COL_KVA = A_Q
COL_QB = A_Q + 2 * A_KV
COL_KB = COL_QB + B_QKV
COL_VB = COL_KB + B_QKV
COL_G = COL_VB + B_QKV

TM = 1024
TN = 1024
ROW_CHUNK = 256
W_SUB = 512
TM_NORM = 512
TF = 512
MLP_COLS = 512
ADA_HEAD = 2 * D_MODEL
ADA_SIDE_COLS = 1024
ADA_SIDE = 2 * D_MODEL
VMEM_LIMIT = 60 * 1024 * 1024

F32 = jnp.float32
BF16 = jnp.bfloat16


def _mod_row(i, tm):
    p_tiles = M_PROMPT // tm
    return jnp.where(i < p_tiles, 0, 1 + (i - p_tiles) // (DEC_SEQ // tm))


def _row_chunks(tm):
    edges = list(range(0, tm, ROW_CHUNK)) + [tm - ROW_CHUNK // 2, tm]
    edges = sorted(set(edges))
    return [slice(a, b) for a, b in zip(edges[:-1], edges[1:])]


def _dot(a, b):
    return jnp.dot(a, b, preferred_element_type=F32)


def _dot_nt(a, b):
    return lax.dot_general(a, b, (((1,), (1,)), ((), ())), preferred_element_type=F32)


def _params(sem, vmem=None, fuse_inputs=None):
    return pltpu.CompilerParams(dimension_semantics=sem, vmem_limit_bytes=vmem, allow_input_fusion=fuse_inputs)


def _ada_side(s_ref, w_ref, b_ref, o_ref):
    o_ref[:, 0, :] = _dot(s_ref[...], w_ref[...].astype(BF16)) + b_ref[...]


def _ada_kernel(cctx_ref, c_ref, w_ref, b_ref, o_ref, s_ref):
    row = lax.broadcasted_iota(jnp.int32, (8, D_MODEL), 0)
    cv = jnp.where(row == 0, cctx_ref[...], 0.0)
    for b in range(DEC_BATCH):
        cv = jnp.where(row == 1 + b, c_ref[b:b + 1, :], cv)
    s_ref[...] = (cv * jax.nn.sigmoid(cv)).astype(BF16)
    _ada_side(s_ref, w_ref, b_ref, o_ref)


def _ada_call(c_ctx, c, w_ada, b_ada):
    tn = 1024
    n = ADA_HEAD
    return pl.pallas_call(
        _ada_kernel,
        grid=(n // tn,),
        in_specs=[pl.BlockSpec((1, D_MODEL), lambda j: (0, 0)),
                  pl.BlockSpec((DEC_BATCH, D_MODEL), lambda j: (0, 0)),
                  pl.BlockSpec((D_MODEL, tn), lambda j: (0, j)),
                  pl.BlockSpec((1, tn), lambda j: (0, j))],
        out_specs=[pl.BlockSpec((8, 1, tn), lambda j: (0, 0, j)),
                   pl.BlockSpec((8, D_MODEL), lambda j: (0, 0))],
        out_shape=[jax.ShapeDtypeStruct((8, 1, n), F32),
                   jax.ShapeDtypeStruct((8, D_MODEL), BF16)],
        compiler_params=_params(("arbitrary",), VMEM_LIMIT),
        name="ada_mod",
    )(c_ctx, c, w_ada, b_ada)


def _ada_side_specs(first_col, step):
    blk = lambda *g: first_col // ADA_SIDE_COLS + step(*g)
    in_specs = [pl.BlockSpec((8, D_MODEL), lambda *g: (0, 0)),
                pl.BlockSpec((D_MODEL, ADA_SIDE_COLS), lambda *g: (0, blk(*g))),
                pl.BlockSpec((1, ADA_SIDE_COLS), lambda *g: (0, blk(*g)))]
    out_spec = pl.BlockSpec((8, 1, ADA_SIDE_COLS), lambda *g: (0, 0, step(*g)))
    return in_specs, out_spec, jax.ShapeDtypeStruct((8, 1, ADA_SIDE), F32)


def _modnorm(x, nw, sc, sh):
    y = x * lax.rsqrt(jnp.mean(x * x, axis=-1, keepdims=True) + EPS)
    return y * (nw * (1.0 + sc)) + sh


def _head_norm(x, nw):
    return x * lax.rsqrt(jnp.mean(x * x, axis=-1, keepdims=True) + EPS) * nw


def _rope(x, cos, sin_signed):
    lane = lax.broadcasted_iota(jnp.int32, x.shape, 1)
    partner = jnp.where((lane % 64) < 32, pltpu.roll(x, 96, 1), pltpu.roll(x, 32, 1))
    return x * cos + partner * sin_signed


def _norm_kva_kernel(xp_ref, xs_ref, nw1_ref, sh_ref, sc_ref, w_ref, nwk_ref, cos_ref, sin_ref,
                     h_ref, kv_ref, kp_ref, vp_ref, wbf):
    i = pl.program_id(0)
    p_tiles = M_PROMPT // TM_NORM

    @pl.when(i == 0)
    def _():
        wbf[...] = w_ref[...].astype(BF16)

    def head_cols(k, base=0):
        return slice(base + k * HEAD_DIM, base + (k + 1) * HEAD_DIM)

    def run(prompt):
        x_ref = xp_ref if prompt else xs_ref
        for r0 in range(0, TM_NORM, ROW_CHUNK):
            rows = slice(r0, r0 + ROW_CHUNK)
            h = _modnorm(x_ref[rows, :], nw1_ref[...], sc_ref[...], sh_ref[...]).astype(BF16)
            h_ref[rows, :] = h
            acc = _dot(h, wbf[...])
            for k in range(A_KV_HEADS):
                y = _head_norm(acc[:, head_cols(k)], nwk_ref[...])
                v = acc[:, head_cols(k, A_KV)]
                if prompt:
                    kp_ref[rows, k, :] = y
                    vp_ref[rows, k, :] = v
                else:
                    y = _rope(y, cos_ref[rows, :], sin_ref[rows, :])
                kv_ref[rows, head_cols(k)] = y.astype(BF16)
                kv_ref[rows, head_cols(k, A_KV)] = v.astype(BF16)

    pl.when(i < p_tiles)(functools.partial(run, True))
    pl.when(i >= p_tiles)(functools.partial(run, False))


def _norm_kva_call(xp, xs, nw1, mods3, w_in, nwk, rope):
    tm = TM_NORM
    p_tiles = M_PROMPT // tm
    s_tiles = DEC_SEQ // tm
    assert COL_KVA % (2 * A_KV) == 0
    rope_spec = pl.BlockSpec((tm, HEAD_DIM), lambda i: (jnp.maximum(i - p_tiles, 0) % s_tiles, 0))
    parked = pl.BlockSpec((tm, A_KV_HEADS, HEAD_DIM), lambda i: (jnp.minimum(i, p_tiles - 1), 0, 0))
    return pl.pallas_call(
        _norm_kva_kernel,
        grid=(M_ALL // tm,),
        in_specs=[pl.BlockSpec((tm, D_MODEL), lambda i: (jnp.minimum(i, p_tiles - 1), 0)),
                  pl.BlockSpec((tm, D_MODEL), lambda i: (jnp.maximum(i - p_tiles, 0), 0)),
                  pl.BlockSpec((1, D_MODEL), lambda i: (0, 0)),
                  pl.BlockSpec((None, 1, D_MODEL), lambda i: (_mod_row(i, tm), 0, 0)),
                  pl.BlockSpec((None, 1, D_MODEL), lambda i: (_mod_row(i, tm), 0, 1)),
                  pl.BlockSpec((D_MODEL, 2 * A_KV), lambda i: (0, COL_KVA // (2 * A_KV)),
                               pipeline_mode=pl.Buffered(1)),
                  pl.BlockSpec((1, HEAD_DIM), lambda i: (0, 0)),
                  rope_spec, rope_spec],
        out_specs=[pl.BlockSpec((tm, D_MODEL), lambda i: (i, 0)),
                   pl.BlockSpec((tm, 2 * A_KV), lambda i: (i, 0)),
                   parked, parked],
        out_shape=[jax.ShapeDtypeStruct((M_ALL, D_MODEL), BF16),
                   jax.ShapeDtypeStruct((M_ALL, 2 * A_KV), BF16),
                   jax.ShapeDtypeStruct((M_PROMPT, A_KV_HEADS, HEAD_DIM), F32),
                   jax.ShapeDtypeStruct((M_PROMPT, A_KV_HEADS, HEAD_DIM), F32)],
        scratch_shapes=[pltpu.VMEM((D_MODEL, 2 * A_KV), BF16)],
        compiler_params=_params(("arbitrary",), VMEM_LIMIT),
        name="norm1_proj_kva",
    )(xp, xs, nw1, mods3, mods3, w_in, nwk, *rope)


def _rope_tables():
    n_freq = HEAD_DIM // 4
    pos = np.arange(DEC_SEQ)
    row = (pos // GRID_W).astype(np.float64)
    col = (pos % GRID_W).astype(np.float64)
    inv = ROPE_THETA ** (-np.arange(n_freq, dtype=np.float64) / n_freq)
    ar = row[:, None] * inv
    ac = col[:, None] * inv
    cos = np.concatenate([np.cos(ar), np.cos(ar), np.cos(ac), np.cos(ac)], axis=-1)
    sin = np.concatenate([-np.sin(ar), np.sin(ar), -np.sin(ac), np.sin(ac)], axis=-1)
    return jnp.asarray(cos, F32), jnp.asarray(sin, F32)


PROJ_COL = {"qa": COL_QA, "qb": COL_QB, "kb": COL_KB, "vb": COL_VB}
GATE_PANELS = 2 * D_MODEL // TN
PROJ_KV = ("kb", "vb")
PROJ_QG = ("qa", "qb") + ("gate",) * GATE_PANELS


def _proj_kernel(*refs, kinds):
    names = ["h", "w0", "w1", "nqa", "nqb", "nkb", "cos", "sin", "qkv"]
    names += ["gate"] * ("gate" in kinds) + ["kbp"] * ("kb" in kinds) + ["vbp"] * ("vb" in kinds) + ["wbf"]
    r = dict(zip(names, refs, strict=True))
    n_qkv = sum(k != "gate" for k in kinds)
    j, i = pl.program_id(0), pl.program_id(1)
    is_prompt = i < M_PROMPT // TM

    @pl.when(i == 0)
    def _():
        r["wbf"][:, :W_SUB] = r["w0"][...].astype(BF16)
        r["wbf"][:, W_SUB:] = r["w1"][...].astype(BF16)

    def head_cols(k):
        return slice(k * HEAD_DIM, (k + 1) * HEAD_DIM)

    def run(kind, prompt):
        for rows in _row_chunks(TM):
            acc = _dot(r["h"][rows, :], r["wbf"][...])
            if kind == "gate":
                r["gate"][rows, :] = jax.nn.sigmoid(acc)
            elif kind == "vb":
                if prompt:
                    r["vbp"][rows, :] = acc
                r["qkv"][rows, :] = acc.astype(BF16)
            else:
                nw_ref = r[{"qa": "nqa", "qb": "nqb", "kb": "nkb"}[kind]]
                for k in range(TN // HEAD_DIM):
                    y = _head_norm(acc[:, head_cols(k)], nw_ref[...])
                    if kind == "qa" and not prompt:
                        y = _rope(y, r["cos"][rows, :], r["sin"][rows, :])
                    if kind == "kb" and prompt:
                        r["kbp"][rows, head_cols(k)] = y
                    r["qkv"][rows, head_cols(k)] = y.astype(BF16)

    for p, kind in enumerate(kinds[:n_qkv]):
        if kind == "qb":
            pl.when(j == p)(functools.partial(run, kind, None))
        else:
            pl.when((j == p) & is_prompt)(functools.partial(run, kind, True))
            pl.when((j == p) & jnp.logical_not(is_prompt))(functools.partial(run, kind, False))
    if "gate" in kinds:
        pl.when(j >= n_qkv)(functools.partial(run, "gate", None))


def _proj_call(h, w_in, nqa, nqb, nkb, rope, kinds):
    assert TN == 2 * W_SUB and A_Q == B_QKV == TN
    n_panels = len(kinds)
    n_qkv = sum(k != "gate" for k in kinds)
    assert all(k == "gate" for k in kinds[n_qkv:])
    cols = [PROJ_COL[k] for k in kinds[:n_qkv]] + list(range(COL_G, IN_WIDTH, TN))[:n_panels - n_qkv]
    p_tiles = M_PROMPT // TM
    s_tiles = DEC_SEQ // TM
    n_tiles = M_ALL // TM

    def w_sub(j):
        idx = cols[0] // W_SUB
        for p in range(1, n_panels):
            idx = jnp.where(j == p, cols[p] // W_SUB, idx)
        return idx

    def qkv_idx(j, i):
        return (jnp.minimum(j, n_qkv - 1), jnp.where(j >= n_qkv, n_tiles - 1, i), 0)

    def gate_idx(j, i):
        return (jnp.maximum(j - n_qkv, 0), jnp.where(j < n_qkv, 0, i), 0)

    def cache_idx(panel):
        def idx(j, i):
            return (jnp.where(j < panel, 0, jnp.where(j > panel, p_tiles - 1, jnp.minimum(i, p_tiles - 1))), 0)
        return idx

    out_specs = [pl.BlockSpec((None, TM, TN), qkv_idx)]
    out_shape = [jax.ShapeDtypeStruct((n_qkv, M_ALL, TN), BF16)]
    if "gate" in kinds:
        out_specs.append(pl.BlockSpec((None, TM, TN), gate_idx))
        out_shape.append(jax.ShapeDtypeStruct((n_panels - n_qkv, M_ALL, TN), F32))
    for kind in ("kb", "vb"):
        if kind in kinds:
            out_specs.append(pl.BlockSpec((TM, TN), cache_idx(kinds.index(kind))))
            out_shape.append(jax.ShapeDtypeStruct((M_PROMPT, B_QKV), F32))

    norm_spec = pl.BlockSpec((1, HEAD_DIM), lambda j, i: (0, 0))
    rope_spec = pl.BlockSpec((TM, HEAD_DIM), lambda j, i: (jnp.maximum(i - p_tiles, 0) % s_tiles, 0))
    in_specs = [pl.BlockSpec((TM, D_MODEL), lambda j, i: (i, 0)),
                pl.BlockSpec((D_MODEL, W_SUB), lambda j, i: (0, w_sub(j))),
                pl.BlockSpec((D_MODEL, W_SUB), lambda j, i: (0, w_sub(j) + 1)),
                norm_spec, norm_spec, norm_spec, rope_spec, rope_spec]
    args = [h, w_in, w_in, nqa, nqb, nkb, *rope]
    return pl.pallas_call(
        functools.partial(_proj_kernel, kinds=kinds),
        grid=(n_panels, n_tiles),
        in_specs=in_specs,
        out_specs=out_specs,
        out_shape=out_shape,
        scratch_shapes=[pltpu.VMEM((D_MODEL, TN), BF16)],
        compiler_params=_params(("arbitrary", "arbitrary"), VMEM_LIMIT),
        name="proj_" + "_".join(dict.fromkeys(kinds)),
    )(*args)


LOG2E = 1.4426950408889634
QK_LOG2 = SCALE * LOG2E


def _softmax_pv(parts, sink):
    m = functools.reduce(jnp.maximum, [jnp.max(t, axis=-1, keepdims=True) for t, _ in parts])
    if sink is not None:
        m = jnp.maximum(m, sink)
    acc = None
    for t, v in parts:
        p = jnp.exp2(t - m).astype(BF16)
        y = _dot(p, jnp.concatenate([v, jnp.ones_like(v)], axis=1))
        acc = y if acc is None else acc + y
    o, l = acc[:, :HEAD_DIM], acc[:, HEAD_DIM:]
    if sink is not None:
        l = l + jnp.exp2(sink - m)
    return o / l


def _sink_column(sink_ref, hk, rows_per_head):
    n = A_GROUP * rows_per_head
    g = lax.broadcasted_iota(jnp.int32, (n, 1), 0) // rows_per_head
    col = jnp.full((n, 1), sink_ref[hk * A_GROUP], F32)
    for k in range(1, A_GROUP):
        col = jnp.where(g == k, sink_ref[hk * A_GROUP + k], col)
    return col * LOG2E


def _stack_group(q_ref, hk):
    return jnp.concatenate(
        [q_ref[:, (hk * A_GROUP + g) * HEAD_DIM:(hk * A_GROUP + g + 1) * HEAD_DIM] for g in range(A_GROUP)], axis=0)


def _unstack_group(o_ref, hk, o, rows):
    for g in range(A_GROUP):
        c = (hk * A_GROUP + g) * HEAD_DIM
        o_ref[:, c:c + HEAD_DIM] = o[g * rows:(g + 1) * rows].astype(BF16)


CTX_REQS = 4


def _ctx_attn_kernel(sink_ref, q_ref, kv_ref, kva_ref, oa_ref, ob_ref):
    for req in range(CTX_REQS):
        rows = pl.ds(req * SEQ, SEQ)
        qa_ref, qb_ref = q_ref.at[PROJ_QG.index("qa"), rows], q_ref.at[PROJ_QG.index("qb"), rows]
        kb_ref, vb_ref = kv_ref.at[PROJ_KV.index("kb"), rows], kv_ref.at[PROJ_KV.index("vb"), rows]
        ka_ref, oa, ob = kva_ref.at[rows], oa_ref.at[rows], ob_ref.at[rows]
        for hk in range(A_KV_HEADS):
            k = ka_ref[:, hk * HEAD_DIM:(hk + 1) * HEAD_DIM]
            v = ka_ref[:, A_KV + hk * HEAD_DIM:A_KV + (hk + 1) * HEAD_DIM]
            q4 = _stack_group(qa_ref, hk)
            t = _dot_nt(q4, k) * QK_LOG2
            o = _softmax_pv([(t, v)], _sink_column(sink_ref, hk, SEQ))
            _unstack_group(oa, hk, o, SEQ)
        for h in range(B_HEADS):
            sl = slice(h * HEAD_DIM, (h + 1) * HEAD_DIM)
            t = _dot_nt(qb_ref[:, sl], kb_ref[:, sl]) * QK_LOG2
            ob[:, sl] = _softmax_pv([(t, vb_ref[:, sl])], None).astype(BF16)


def _ctx_attn_call(sink, q, kv, kva):
    rows = CTX_REQS * SEQ
    row = lambda w: pl.BlockSpec((rows, w), lambda b: (b, 0))
    return pl.pallas_call(
        _ctx_attn_kernel,
        grid=(BATCH // CTX_REQS,),
        in_specs=[pl.BlockSpec(memory_space=pltpu.SMEM),
                  pl.BlockSpec((2, rows, TN), lambda b: (0, b, 0)),
                  pl.BlockSpec((2, rows, TN), lambda b: (0, b, 0)), row(2 * A_KV)],
        out_specs=[row(A_Q), row(B_QKV)],
        out_shape=[jax.ShapeDtypeStruct((M_PROMPT, A_Q), BF16),
                   jax.ShapeDtypeStruct((M_PROMPT, B_QKV), BF16)],
        compiler_params=_params(("arbitrary",), VMEM_LIMIT),
        name="attn_ctx",
    )(sink, q, kv, kva)


BAND = 3 * BLOCK


def _cache_to_bf16(n_heads, ck_ref, cv_ref, ckb, cvb):
    for h in range(n_heads):
        sl = slice(h * HEAD_DIM, (h + 1) * HEAD_DIM)
        ckb[:, sl] = ck_ref[:, h, :].astype(BF16)
        cvb[:, sl] = cv_ref[:, h, :].astype(BF16)


WIN_BLOCKS = 4


def _win_attn_kernel(sink_ref, q_ref, kv_ref, ck_ref, cv_ref, s_ref, wada_ref, bada_ref, o_ref, mods_ref, ckb, cvb):
    step = pl.program_id(1)
    pl.when(step == 0)(functools.partial(_cache_to_bf16, A_KV_HEADS, ck_ref, cv_ref, ckb, cvb))
    for blk in range(WIN_BLOCKS):
        n = step * WIN_BLOCKS + blk
        rows = pl.ds(blk * BLOCK, BLOCK)
        start = pl.multiple_of(jnp.clip((n - 1) * BLOCK, 0, DEC_SEQ - BAND), BLOCK)
        qpos = n * BLOCK + lax.broadcasted_iota(jnp.int32, (BLOCK, BAND), 0)
        kpos = start + lax.broadcasted_iota(jnp.int32, (BLOCK, BAND), 1)
        valid = jnp.abs(qpos - kpos) <= A_WINDOW
        valid = jnp.concatenate([valid.astype(jnp.int32)] * A_GROUP, axis=0) > 0
        for hk in range(A_KV_HEADS):
            sl = slice(hk * HEAD_DIM, (hk + 1) * HEAD_DIM)
            slv = slice(A_KV + hk * HEAD_DIM, A_KV + (hk + 1) * HEAD_DIM)
            k_loc = kv_ref[pl.ds(start, BAND), sl]
            v_loc = kv_ref[pl.ds(start, BAND), slv]
            q4 = _stack_group(q_ref.at[rows], hk)
            t_loc = jnp.where(valid, _dot_nt(q4, k_loc) * QK_LOG2, NEG)
            t_ctx = _dot_nt(q4, ckb[:, sl]) * QK_LOG2
            o = _softmax_pv([(t_loc, v_loc), (t_ctx, cvb[:, sl])], _sink_column(sink_ref, hk, BLOCK))
            _unstack_group(o_ref.at[rows], hk, o, BLOCK)
    _ada_side(s_ref, wada_ref, bada_ref, mods_ref)


def _win_attn_call(sink, q, kva, cache_k, cache_v, ada):
    rows = WIN_BLOCKS * BLOCK
    nb = DEC_SEQ // rows
    q0 = M_PROMPT // rows
    b0 = M_PROMPT // DEC_SEQ
    assert DEC_BATCH * nb * ADA_SIDE_COLS == ADA_SIDE and ADA_HEAD + 2 * ADA_SIDE == 6 * D_MODEL
    side_in, side_out, side_shape = _ada_side_specs(ADA_HEAD + ADA_SIDE, lambda b, n: b * nb + n)
    return pl.pallas_call(
        _win_attn_kernel,
        grid=(DEC_BATCH, nb),
        in_specs=[pl.BlockSpec(memory_space=pltpu.SMEM),
                  pl.BlockSpec((None, rows, A_Q), lambda b, n: (PROJ_QG.index("qa"), q0 + b * nb + n, 0)),
                  pl.BlockSpec((DEC_SEQ, 2 * A_KV), lambda b, n: (b0 + b, 0)),
                  pl.BlockSpec((None, None, PAST_LEN, A_KV_HEADS, HEAD_DIM), lambda b, n: (b, 0, 0, 0, 0)),
                  pl.BlockSpec((None, None, PAST_LEN, A_KV_HEADS, HEAD_DIM), lambda b, n: (b, 0, 0, 0, 0))] + side_in,
        out_specs=[pl.BlockSpec((rows, A_Q), lambda b, n: (b * nb + n, 0)), side_out],
        out_shape=[jax.ShapeDtypeStruct((M_SAMPLE, A_Q), BF16), side_shape],
        scratch_shapes=[pltpu.VMEM((PAST_LEN, A_KV), BF16), pltpu.VMEM((PAST_LEN, A_KV), BF16)],
        compiler_params=_params(("arbitrary", "arbitrary"), VMEM_LIMIT),
        name="attn_window",
    )(sink, q, kva, cache_k, cache_v, *ada)


NA_QROWS = 4
NA_BLOCKS = 2
NA_WIN_ROWS = 12
NA_Q = NA_QROWS * GRID_W
NA_KEYS = NA_WIN_ROWS * GRID_W
NA_DR = 2 * NA_ROWS_MAX - 1
NA_PAIRS = NA_DR + 1
RPB_W = 2 * NA_COLS - 1
RPB_PAD = GRID_W - NA_COLS


def _na_row_start(r):
    return jnp.clip(r - NA_ROWS // 2, 0, GRID_ROWS - NA_ROWS)


def _na_window_row0(blk):
    return jnp.clip(blk * NA_QROWS - NA_ROWS // 2, 0, GRID_ROWS - NA_WIN_ROWS)


def _check_na_windows():
    for blk in range(GRID_ROWS // NA_QROWS):
        w0 = int(np.clip(blk * NA_QROWS - NA_ROWS // 2, 0, GRID_ROWS - NA_WIN_ROWS))
        for r in range(blk * NA_QROWS, (blk + 1) * NA_QROWS):
            r0 = int(np.clip(r - NA_ROWS // 2, 0, GRID_ROWS - NA_ROWS))
            assert w0 <= r0 and r0 + NA_ROWS <= w0 + NA_WIN_ROWS, (blk, r)


_check_na_windows()


def _na_attn_kernel(q_ref, k_ref, v_ref, ck_ref, cv_ref, rpb_ref, s_ref, wada_ref, bada_ref,
                    o_ref, mods_ref, tab_ref, ckb, cvb):
    b, step = pl.program_id(0), pl.program_id(1)
    lane = lax.broadcasted_iota(jnp.int32, (GRID_W, 2 * GRID_W), 1)
    pl.when(step == 0)(functools.partial(_cache_to_bf16, B_HEADS, ck_ref, cv_ref, ckb, cvb))

    @pl.when((b == 0) & (step == 0))
    def _():
        for h in range(B_HEADS):
            for d in range(NA_PAIRS):
                lo = jnp.broadcast_to(rpb_ref[h, d:d + 1, :], (GRID_W, 2 * GRID_W))
                hi = jnp.broadcast_to(rpb_ref[h, d + 1:d + 2, :], (GRID_W, 2 * GRID_W))
                lo = pltpu.roll(lo, GRID_W + 1, 1, stride=1, stride_axis=0)
                hi = pltpu.roll(hi, 1, 1, stride=1, stride_axis=0)
                tab_ref[h, d] = jnp.where(lane < GRID_W, lo, hi) * LOG2E

    for sub in range(NA_BLOCKS):
        blk = step * NA_BLOCKS + sub
        rows = pl.ds(sub * NA_Q, NA_Q)
        row0 = _na_window_row0(blk)
        k0 = pl.multiple_of(row0 * GRID_W, GRID_W)
        qi = lax.broadcasted_iota(jnp.int32, (NA_Q, NA_KEYS), 0)
        ki = lax.broadcasted_iota(jnp.int32, (NA_Q, NA_KEYS), 1)
        qrow, qcol = blk * NA_QROWS + qi // GRID_W, qi % GRID_W
        krow, kcol = row0 + ki // GRID_W, ki % GRID_W
        rstart = _na_row_start(qrow)
        cstart = jnp.clip(qcol - NA_COLS // 2, 0, GRID_W - NA_COLS)
        valid = (krow >= rstart) & (krow < rstart + NA_ROWS) & (kcol >= cstart) & (kcol < cstart + NA_COLS)

        for h in range(B_HEADS):
            sl = slice(h * HEAD_DIM, (h + 1) * HEAD_DIM)
            bias = jnp.concatenate(
                [jnp.concatenate(
                    [tab_ref[h, jnp.clip(row0 + 2 * p - blk * NA_QROWS - q + NA_ROWS_MAX, 0, NA_PAIRS - 1)]
                     for p in range(NA_WIN_ROWS // 2)], axis=1)
                 for q in range(NA_QROWS)], axis=0)
            k_loc = k_ref[pl.ds(k0, NA_KEYS), sl]
            v_loc = v_ref[pl.ds(k0, NA_KEYS), sl]
            t_loc = jnp.where(valid, _dot_nt(q_ref[rows, sl], k_loc) * QK_LOG2 + bias, NEG)
            t_ctx = _dot_nt(q_ref[rows, sl], ckb[:, sl]) * QK_LOG2
            o = _softmax_pv([(t_loc, v_loc), (t_ctx, cvb[:, sl])], None)
            o_ref[rows, sl] = o.astype(BF16)
    _ada_side(s_ref, wada_ref, bada_ref, mods_ref)


def _na_bias_rows(rpb):
    rows = jnp.pad(rpb, ((0, 0), (1, 1), (0, 0)))
    left = jnp.broadcast_to(rows[..., :1], rows.shape[:2] + (RPB_PAD,))
    right = jnp.broadcast_to(rows[..., -1:], rows.shape[:2] + (2 * GRID_W - RPB_PAD - RPB_W,))
    return jnp.concatenate([left, rows, right], axis=-1)


def _na_attn_call(q, kv, cache_k, cache_v, rpb, ada):
    nblk = GRID_ROWS // (NA_QROWS * NA_BLOCKS)
    assert DEC_BATCH * nblk * ADA_SIDE_COLS == ADA_SIDE
    side_in, side_out, side_shape = _ada_side_specs(ADA_HEAD, lambda b, r: b * nblk + r)
    q0 = M_PROMPT // (NA_Q * NA_BLOCKS)
    b0 = M_PROMPT // DEC_SEQ
    return pl.pallas_call(
        _na_attn_kernel,
        grid=(DEC_BATCH, nblk),
        in_specs=[pl.BlockSpec((None, NA_Q * NA_BLOCKS, B_QKV), lambda b, r: (PROJ_QG.index("qb"), q0 + b * nblk + r, 0)),
                  pl.BlockSpec((None, DEC_SEQ, B_QKV), lambda b, r: (PROJ_KV.index("kb"), b0 + b, 0)),
                  pl.BlockSpec((None, DEC_SEQ, B_QKV), lambda b, r: (PROJ_KV.index("vb"), b0 + b, 0)),
                  pl.BlockSpec((None, None, PAST_LEN, B_HEADS, HEAD_DIM), lambda b, r: (b, 0, 0, 0, 0)),
                  pl.BlockSpec((None, None, PAST_LEN, B_HEADS, HEAD_DIM), lambda b, r: (b, 0, 0, 0, 0)),
                  pl.BlockSpec((B_HEADS, NA_DR + 2, 2 * GRID_W), lambda b, r: (0, 0, 0))] + side_in,
        out_specs=[pl.BlockSpec((NA_Q * NA_BLOCKS, B_QKV), lambda b, r: (b * nblk + r, 0)), side_out],
        out_shape=[jax.ShapeDtypeStruct((M_SAMPLE, B_QKV), BF16), side_shape],
        scratch_shapes=[pltpu.VMEM((B_HEADS, NA_PAIRS, GRID_W, 2 * GRID_W), F32),
                        pltpu.VMEM((PAST_LEN, B_QKV), BF16), pltpu.VMEM((PAST_LEN, B_QKV), BF16)],
        compiler_params=_params(("arbitrary", "arbitrary"), VMEM_LIMIT, fuse_inputs=[k == 5 for k in range(9)]),
        name="attn_neighbourhood",
    )(q, kv, kv, cache_k, cache_v, _na_bias_rows(rpb), *ada)


def _merge_kernel(oap, oas, obp, obs, wa_ref, wb_ref, ga_ref, gb_ref, z_ref, wa_bf, wb_bf):
    i = pl.program_id(1)
    p_tiles = M_PROMPT // TM

    @pl.when(i == 0)
    def _():
        wa_bf[...] = wa_ref[...].astype(BF16)
        wb_bf[...] = wb_ref[...].astype(BF16)

    def run(oa_ref, ob_ref):
        for rows in _row_chunks(TM):
            ya = _dot(oa_ref[rows, :], wa_bf[...])
            yb = _dot(ob_ref[rows, :], wb_bf[...])
            z_ref[rows, :] = (ga_ref[rows, :] * ya + gb_ref[rows, :] * yb).astype(BF16)

    pl.when(i < p_tiles)(functools.partial(run, oap, obp))
    pl.when(i >= p_tiles)(functools.partial(run, oas, obs))


def _merge_call(oa_p, oa_s, ob_p, ob_s, w_br_a, w_br_b, gates):
    p_tiles = M_PROMPT // TM
    nj = D_MODEL // TN
    pspec = pl.BlockSpec((TM, A_Q), lambda j, i: (jnp.minimum(i, p_tiles - 1), 0))
    sspec = pl.BlockSpec((TM, A_Q), lambda j, i: (jnp.maximum(i - p_tiles, 0), 0))
    return pl.pallas_call(
        _merge_kernel,
        grid=(nj, M_ALL // TM),
        in_specs=[pspec, sspec, pspec, sspec,
                  pl.BlockSpec((A_Q, TN), lambda j, i: (0, j)),
                  pl.BlockSpec((B_QKV, TN), lambda j, i: (0, j)),
                  pl.BlockSpec((None, TM, TN), lambda j, i: (j, i, 0)),
                  pl.BlockSpec((None, TM, TN), lambda j, i: (nj + j, i, 0))],
        out_specs=pl.BlockSpec((TM, TN), lambda j, i: (i, j)),
        out_shape=jax.ShapeDtypeStruct((M_ALL, D_MODEL), BF16),
        scratch_shapes=[pltpu.VMEM((A_Q, TN), BF16), pltpu.VMEM((B_QKV, TN), BF16)],
        compiler_params=_params(("arbitrary", "arbitrary"), VMEM_LIMIT),
        name="merge_branches",
    )(oa_p, oa_s, ob_p, ob_s, w_br_a, w_br_b, gates, gates)


def _out_kernel(z_ref, w_ref, xp_ref, xs_ref, g_ref, o_ref, wbf):
    i = pl.program_id(1)
    p_tiles = M_PROMPT // TM

    @pl.when(i == 0)
    def _():
        wbf[...] = w_ref[...].astype(BF16)

    def run(x_ref):
        for rows in _row_chunks(TM):
            o_ref[rows, :] = x_ref[rows, :] + g_ref[...] * _dot(z_ref[rows, :], wbf[...])

    pl.when(i < p_tiles)(functools.partial(run, xp_ref))
    pl.when(i >= p_tiles)(functools.partial(run, xs_ref))


def _out_call(z, w_out, xp, xs, mods_mid):
    p_tiles = M_PROMPT // TM
    nj = D_MODEL // TN
    return pl.pallas_call(
        _out_kernel,
        grid=(nj, M_ALL // TM),
        in_specs=[pl.BlockSpec((TM, D_MODEL), lambda j, i: (i, 0)),
                  pl.BlockSpec((D_MODEL, TN), lambda j, i: (0, j)),
                  pl.BlockSpec((TM, TN), lambda j, i: (jnp.minimum(i, p_tiles - 1), j)),
                  pl.BlockSpec((TM, TN), lambda j, i: (jnp.maximum(i - p_tiles, 0), j)),
                  pl.BlockSpec((None, 1, TN), lambda j, i: (_mod_row(i, TM), 0, j))],
        out_specs=pl.BlockSpec((TM, TN), lambda j, i: (i, j)),
        out_shape=jax.ShapeDtypeStruct((M_ALL, D_MODEL), F32),
        scratch_shapes=[pltpu.VMEM((D_MODEL, TN), BF16)],
        compiler_params=_params(("arbitrary", "arbitrary"), VMEM_LIMIT),
        name="out_proj_residual",
    )(z, w_out, xp, xs, mods_mid)


def _mlp_kernel(x_hbm, nw_ref, sh_ref, sc_ref, g_ref, wu_ref, wd_ref, o_ref, h_ref, x_buf, x_sem, *, tile0):
    i, f = pl.program_id(0), pl.program_id(1)

    def x_copy(tile):
        return pltpu.make_async_copy(x_hbm.at[pl.ds((tile0 + tile) * TM, TM), :], x_buf, x_sem)

    @pl.when((i == 0) & (f == 0))
    def _():
        x_copy(0).start()

    @pl.when(f == 0)
    def _():
        x_copy(i).wait()
        wu = wu_ref[...].astype(BF16)
        wd = wd_ref[...].astype(BF16)
        for r0 in range(0, TM, ROW_CHUNK):
            rows = slice(r0, r0 + ROW_CHUNK)
            x = x_buf[rows, :]
            h = _modnorm(x, nw_ref[...], sc_ref[...], sh_ref[...]).astype(BF16)
            h_ref[rows, :] = h
            u = jnp.square(jnp.maximum(_dot(h, wu), 0.0)).astype(BF16)
            o_ref[rows, :] = x + g_ref[...] * _dot(u, wd)

    @pl.when((f == 1) & (i + 1 < pl.num_programs(0)))
    def _():
        x_copy(i + 1).start()

    @pl.when(f > 0)
    def _():
        u = _dot(h_ref[...], wu_ref[...].astype(BF16))
        u = jnp.square(jnp.maximum(u, 0.0)).astype(BF16)
        for c0 in range(0, D_MODEL, MLP_COLS):
            cols = slice(c0, c0 + MLP_COLS)
            o_ref[:, cols] += g_ref[:, cols] * _dot(u, wd_ref[:, cols].astype(BF16))


def _mlp_call(x1, nw, mods_mid, mods_last, w_up, w_down, tile0, n_tiles):
    mod = lambda k: pl.BlockSpec((None, 1, D_MODEL), lambda i, f: (_mod_row(tile0 + i, TM), 0, k))
    return pl.pallas_call(
        functools.partial(_mlp_kernel, tile0=tile0),
        grid=(n_tiles, D_FF // TF),
        in_specs=[pl.BlockSpec(memory_space=pl.ANY),
                  pl.BlockSpec((1, D_MODEL), lambda i, f: (0, 0)),
                  mod(1), mod(0), mod(1),
                  pl.BlockSpec((D_MODEL, TF), lambda i, f: (0, f)),
                  pl.BlockSpec((TF, D_MODEL), lambda i, f: (f, 0))],
        out_specs=pl.BlockSpec((TM, D_MODEL), lambda i, f: (i, 0)),
        out_shape=jax.ShapeDtypeStruct((n_tiles * TM, D_MODEL), F32),
        scratch_shapes=[pltpu.VMEM((TM, D_MODEL), BF16),
                        pltpu.VMEM((TM, D_MODEL), F32),
                        pltpu.SemaphoreType.DMA(())],
        compiler_params=_params(("arbitrary", "arbitrary"), VMEM_LIMIT),
        name="mlp",
    )(x1, nw, mods_mid, mods_last, mods_last, w_up, w_down)


def kernel(x_prompt, x_sample, cache_a_k, cache_a_v, cache_b_k, cache_b_v, c, c_ctx, norm1_w, norm2_w, w_ada, b_ada, w_in, q_norm_a, k_norm_a, q_norm_b, k_norm_b, sink_a, rpb_b, w_br_a, w_br_b, w_out, w_up, w_down):
    assert w_ada.shape[0] == 1, "one trunk layer"
    xp = x_prompt.reshape(M_PROMPT, D_MODEL)
    xs = x_sample.reshape(M_SAMPLE, D_MODEL)

    mods_head, cond = _ada_call(c_ctx[None, :], c, w_ada[0], b_ada)
    ada = (cond, w_ada[0], b_ada)

    w = w_in[0]
    rope = _rope_tables()
    h, kva, ka_p, va_p = _norm_kva_call(xp, xs, norm1_w, mods_head, w, k_norm_a, rope)
    kv, kb_p, vb_p = _proj_call(h, w, q_norm_a, q_norm_b, k_norm_b, rope, PROJ_KV)
    q, gates = _proj_call(h, w, q_norm_a, q_norm_b, k_norm_b, rope, PROJ_QG)

    sink = sink_a[0]
    oa_p, ob_p = _ctx_attn_call(sink, q, kv, kva)
    oa_s, mods_last = _win_attn_call(sink, q, kva, cache_a_k, cache_a_v, ada)
    ob_s, mods_mid = _na_attn_call(q, kv, cache_b_k, cache_b_v, rpb_b[0], ada)

    z = _merge_call(oa_p, oa_s, ob_p, ob_s, w_br_a[0], w_br_b[0], gates)
    x1 = _out_call(z, w_out[0], xp, xs, mods_mid)

    p_tiles = M_PROMPT // TM
    y_p = _mlp_call(x1, norm2_w, mods_mid, mods_last, w_up[0], w_down[0], 0, p_tiles)
    y_s = _mlp_call(x1, norm2_w, mods_mid, mods_last, w_up[0], w_down[0], p_tiles, M_SAMPLE // TM)

    return (y_p.reshape(BATCH, SEQ, D_MODEL),
            y_s.reshape(DEC_BATCH, DEC_SEQ, D_MODEL),
            ka_p.reshape(BATCH, 1, SEQ, A_KV_HEADS, HEAD_DIM),
            va_p.reshape(BATCH, 1, SEQ, A_KV_HEADS, HEAD_DIM),
            kb_p.reshape(BATCH, 1, SEQ, B_HEADS, HEAD_DIM),
            vb_p.reshape(BATCH, 1, SEQ, B_HEADS, HEAD_DIM))
```
